```python
import jax
import jax.numpy as jnp
from jax import lax
import numpy as np

D_MODEL = 2048
BATCH = 32
SEQ = 256
DEPTH = 4
DEC_BATCH = 4
DEC_SEQ = 4096
PAST_LEN = 256

GRID_W = 64
N_MIXERS = 3
N_MLSTM = (DEPTH + 2) // N_MIXERS
N_ATTN = (DEPTH + 1) // N_MIXERS
N_RWKV = DEPTH // N_MIXERS
NORM_EPS = 1e-6

M_HEADS = 8
M_DK = D_MODEL // 2 // M_HEADS
M_DV = D_MODEL // M_HEADS
M_CHUNK = 64
M_GATE_CAP = 15.0
M_FGATE_BIAS = 3.0
M_QK = M_HEADS * M_DK
M_V = M_HEADS * M_DV
M_PROJ = 2 * M_QK + 2 * M_V

A_HEADS = 16
A_KV_HEADS = 4
A_HD = 128
A_GROUP = A_HEADS // A_KV_HEADS
A_Q_DIM = A_HEADS * A_HD
A_KV_DIM = A_KV_HEADS * A_HD
ROPE_AXIS_DIM = A_HD // 2
ROPE_THETA = 10000.0
Q_BLOCK = 128

R_HD = 64
R_HEADS = D_MODEL // R_HD
R_DECAY_LORA = 96
R_AAA_LORA = 96
R_GATE_LORA = 256
R_LN_EPS = 64e-5

N_EXPERTS = 16
EC_CAPACITY_FACTOR = 2
D_EXPERT = 1024

kernel_name = 'hybrid_diffusion_mlstm_gqa_rwkv7_ec_step'


def rmsnorm(x, gain, eps=NORM_EPS):
    xf = x.astype(jnp.float32)
    y = xf * lax.rsqrt(jnp.mean(xf * xf, axis=-1, keepdims=True) + eps)
    return (y * gain.astype(jnp.float32)).astype(x.dtype)


def axial_rope_tables(n_tokens):
    n_rows = n_tokens // GRID_W
    row = jnp.repeat(jnp.arange(n_rows, dtype=jnp.float32), GRID_W)
    col = jnp.tile(jnp.arange(GRID_W, dtype=jnp.float32), n_rows)
    inv_freq = ROPE_THETA ** (-jnp.arange(0, ROPE_AXIS_DIM, 2, dtype=jnp.float32) / ROPE_AXIS_DIM)
    ang = jnp.concatenate([row[:, None] * inv_freq, col[:, None] * inv_freq], axis=-1)
    return jnp.cos(ang), jnp.sin(ang)


def apply_rope(x, cos, sin):
    half = x.shape[-1] // 2
    xf = x.astype(jnp.float32)
    x1, x2 = xf[..., :half], xf[..., half:]
    c, s = cos[None, :, None, :], sin[None, :, None, :]
    return jnp.concatenate([x1 * c - x2 * s, x2 * c + x1 * s], axis=-1).astype(x.dtype)


def attn_project(h, w_qkv, q_gain, k_gain):
    B, T, _ = h.shape
    qkv = h @ w_qkv
    q = qkv[..., :A_Q_DIM].reshape(B, T, A_HEADS, A_HD)
    k = qkv[..., A_Q_DIM:A_Q_DIM + A_KV_DIM].reshape(B, T, A_KV_HEADS, A_HD)
    v = qkv[..., A_Q_DIM + A_KV_DIM:].reshape(B, T, A_KV_HEADS, A_HD)
    return rmsnorm(q, q_gain), rmsnorm(k, k_gain), v


def blocked_attention(q, k, v):
    B, T = q.shape[:2]
    nb = T // Q_BLOCK
    qb = q.reshape(B, nb, Q_BLOCK, A_KV_HEADS, A_GROUP, A_HD).transpose(1, 0, 2, 3, 4, 5)
    kf = k.astype(jnp.float32)
    vf = v.astype(jnp.float32)
    scale = A_HD ** -0.5

    def one_block(qblk):
        s = jnp.einsum('bqhgd,bkhd->bhgqk', qblk.astype(jnp.float32), kf) * scale
        p = jax.nn.softmax(s, axis=-1)
        return jnp.einsum('bhgqk,bkhd->bqhgd', p, vf)

    o = lax.map(one_block, qb)
    return o.transpose(1, 0, 2, 3, 4, 5).reshape(B, T, A_Q_DIM).astype(q.dtype)


def mlstm_chunkwise(q, k, v, ig, lf, C0, n0, m0):
    B, T, H, _ = q.shape
    nc = T // M_CHUNK

    def chunks(a):
        a = a.reshape((B, nc, M_CHUNK, H) + a.shape[3:])
        return jnp.moveaxis(jnp.swapaxes(a, 2, 3), 1, 0)

    mask = jnp.tril(jnp.ones((M_CHUNK, M_CHUNK), dtype=bool))

    def step(carry, inp):
        C, n, m = carry
        qc, kc, vc, igc, lfc = inp
        b = jnp.cumsum(lfc, axis=-1)
        dmat = jnp.where(mask, b[..., :, None] - b[..., None, :] + igc[..., None, :], -jnp.inf)
        a = b + m[..., None]
        m_t = jnp.maximum(a, jnp.max(dmat, axis=-1))
        w_inter = jnp.exp(a - m_t)
        s = jnp.einsum('bhtd,bhsd->bhts', qc, kc) * jnp.exp(dmat - m_t[..., None])
        num = w_inter[..., None] * jnp.einsum('bhtd,bhdv->bhtv', qc, C) + jnp.einsum('bhts,bhsv->bhtv', s, vc)
        den = w_inter * jnp.einsum('bhtd,bhd->bht', qc, n) + jnp.sum(s, axis=-1)
        hc = num / jnp.maximum(jnp.abs(den), jnp.exp(-m_t))[..., None]
        b_last = b[..., -1]
        g = b_last[..., None] - b + igc
        m_new = jnp.maximum(b_last + m, jnp.max(g, axis=-1))
        decay = jnp.exp(b_last + m - m_new)
        wk = jnp.exp(g - m_new[..., None])[..., None] * kc
        C_new = decay[..., None, None] * C + jnp.einsum('bhsd,bhsv->bhdv', wk, vc)
        n_new = decay[..., None] * n + jnp.sum(wk, axis=2)
        return (C_new, n_new, m_new), hc

    (C, n, m), hs = lax.scan(step, (C0, n0, m0), tuple(chunks(a) for a in (q, k, v, ig, lf)))
    hs = jnp.swapaxes(jnp.moveaxis(hs, 0, 1), 2, 3).reshape(B, T, H, v.shape[-1])
    return hs, C, n, m


def mlstm_mixer(h, C0, n0, m0, w_qkvo, w_gate, b_gate, hn_gain, w_out):
    B, T, _ = h.shape
    f32 = jnp.float32
    proj = h @ w_qkvo
    q = proj[..., :M_QK].reshape(B, T, M_HEADS, M_DK).astype(f32)
    k = proj[..., M_QK:2 * M_QK].reshape(B, T, M_HEADS, M_DK).astype(f32) * (M_DK ** -0.5)
    v = proj[..., 2 * M_QK:2 * M_QK + M_V].reshape(B, T, M_HEADS, M_DV).astype(f32)
    o = jax.nn.sigmoid(proj[..., 2 * M_QK + M_V:])
    gates = (h @ w_gate + b_gate).astype(f32)
    gates = (M_GATE_CAP * jnp.tanh(gates / M_GATE_CAP)).reshape(B, T, 4, M_HEADS)
    flip = lambda a: jnp.flip(a, axis=1)
    outs, Cs, ns, ms = [], [], [], []
    for d in range(2):
        seq = (q, k, v, gates[:, :, 2 * d], jax.nn.log_sigmoid(gates[:, :, 2 * d + 1]))
        if d == 1:
            seq = tuple(flip(a) for a in seq)
        hd_, Cd, nd, md = mlstm_chunkwise(*seq, C0[:, d].astype(f32), n0[:, d].astype(f32), m0[:, d].astype(f32))
        outs.append(flip(hd_) if d == 1 else hd_)
        Cs.append(Cd)
        ns.append(nd)
        ms.append(md)
    hs = outs[0] + outs[1]
    hs = hs * lax.rsqrt(jnp.mean(hs * hs, axis=-1, keepdims=True) + NORM_EPS) * hn_gain.reshape(M_HEADS, M_DV).astype(f32)
    y = (o * hs.reshape(B, T, M_V).astype(h.dtype)) @ w_out
    dt = h.dtype
    return y, (jnp.stack(Cs, 1).astype(dt), jnp.stack(ns, 1).astype(dt), jnp.stack(ms, 1).astype(dt))


def centred_shift(x):
    xp = jnp.pad(x, ((0, 0), (1, 1), (0, 0)))
    return 0.5 * (xp[:, :-2] + xp[:, 2:])


def rwkv7_scan(r, w, k, v, a, b, S0):
    def step(S, inp):
        rt, wt, kt, vt, at, bt = inp
        sa = jnp.einsum('bhvk,bhk->bhv', S, at)
        S = S * wt[:, :, None, :] + sa[..., None] * bt[:, :, None, :] + vt[..., None] * kt[:, :, None, :]
        return S, jnp.einsum('bhvk,bhk->bhv', S, rt)

    seq = tuple(jnp.moveaxis(t, 1, 0) for t in (r, w, k, v, a, b))
    S_fin, ys = lax.scan(step, S0, seq)
    return jnp.moveaxis(ys, 0, 1), S_fin


def rwkv7_mixer(h, S0, mix, w_rkv, w0, w1, w2, a0, a1, a2, g1, g2, k_k, k_a, r_k, lnx_w, lnx_b, w_out):
    B, T, D = h.shape
    f32 = jnp.float32
    heads = lambda t: t.reshape(B, T, R_HEADS, R_HD).astype(f32)
    flip = lambda t: jnp.flip(t, axis=1)
    xx = centred_shift(h) - h
    xr, xw, xk, xv, xa, xg = [h + xx * mix[i] for i in range(6)]
    rh = heads(xr @ w_rkv[0])
    k = xk @ w_rkv[1]
    vh = heads(xv @ w_rkv[2])
    g = jax.nn.sigmoid(xg @ g1) @ g2
    kk = heads(k * k_k)
    kk = kk / jnp.maximum(jnp.sqrt(jnp.sum(kk * kk, axis=-1, keepdims=True)), 1e-12)
    kh = heads(k)
    k_a_h = k_a.reshape(R_HEADS, R_HD).astype(f32)
    r_k_h = r_k.astype(f32)
    S0f = S0.astype(f32)
    ys, bonuses, finals = [], [], []
    for d in range(2):
        wlog = -jax.nn.softplus(-(w0[d] + jnp.tanh(xw @ w1[d]) @ w2[d])) - 0.5
        decay = jnp.exp(-jnp.exp(heads(wlog)))
        a = heads(jax.nn.sigmoid(a0[d] + (xa @ a1[d]) @ a2[d]))
        kd = kh * (1.0 + (a - 1.0) * k_a_h)
        seq = (rh, decay, kd, vh, -kk, kk * a)
        if d == 1:
            seq = tuple(flip(t) for t in seq)
        y, S_fin = rwkv7_scan(*seq, S0f[:, d])
        ys.append(flip(y) if d == 1 else y)
        bonuses.append(jnp.sum(rh * kd * r_k_h, axis=-1, keepdims=True) * vh)
        finals.append(S_fin)
    y = ys[0] + ys[1]
    mu = jnp.mean(y, axis=-1, keepdims=True)
    var = jnp.mean(jnp.square(y - mu), axis=-1, keepdims=True)
    yn = (y - mu) * lax.rsqrt(var + R_LN_EPS) * lnx_w.reshape(R_HEADS, R_HD).astype(f32) + lnx_b.reshape(R_HEADS, R_HD).astype(f32)
    yn = (yn + bonuses[0] + bonuses[1]).reshape(B, T, D).astype(h.dtype)
    return (yn * g) @ w_out, jnp.stack(finals, 1).astype(h.dtype)


def expert_choice_ffn(h, w_router, w_gate, w_up, w_down):
    B, T, D = h.shape
    n_tok = B * T
    cap = (EC_CAPACITY_FACTOR * n_tok) // N_EXPERTS
    xf = h.reshape(n_tok, D)
    aff = jax.nn.softmax((xf @ w_router).astype(jnp.float32), axis=-1)
    gate, idx = lax.top_k(aff.T, cap)
    xe = xf[idx]
    hid = jax.nn.silu(jnp.einsum('ecd,edf->ecf', xe, w_gate)) * jnp.einsum('ecd,edf->ecf', xe, w_up)
    ye = jnp.einsum('ecf,efd->ecd', hid, w_down) * gate[..., None].astype(h.dtype)
    out = jnp.zeros_like(xf).at[idx.reshape(-1)].add(ye.reshape(-1, D))
    return out.reshape(B, T, D)


def setup_inputs(seed: int = 0) -> dict:
    key = jax.random.key(seed)
    ks = iter(jax.random.split(key, 64))
    f32 = jnp.float32
    D = D_MODEL

    def nrm(shape, scale=1.0):
        return jax.random.normal(next(ks), shape, f32) * scale

    def gain(shape):
        return 1.0 + nrm(shape, 0.02)

    m_gate_base = jnp.tile(jnp.repeat(jnp.array([0.0, M_FGATE_BIAS], f32), M_HEADS), 2)
    return {
        'x_prompt': nrm((BATCH, SEQ, D)),
        'x_sample': nrm((DEC_BATCH, DEC_SEQ, D)),
        'state_l0_C': nrm((DEC_BATCH, 2, M_HEADS, M_DK, M_DV), 0.1),
        'state_l0_n': nrm((DEC_BATCH, 2, M_HEADS, M_DK), 0.1),
        'state_l0_m': nrm((DEC_BATCH, 2, M_HEADS)),
        'cache_l1_k': nrm((DEC_BATCH, PAST_LEN, A_KV_HEADS, A_HD)),
        'cache_l1_v': nrm((DEC_BATCH, PAST_LEN, A_KV_HEADS, A_HD)),
        'state_l2_S': nrm((DEC_BATCH, 2, R_HEADS, R_HD, R_HD), 0.1),
        'state_l3_C': nrm((DEC_BATCH, 2, M_HEADS, M_DK, M_DV), 0.1),
        'state_l3_n': nrm((DEC_BATCH, 2, M_HEADS, M_DK), 0.1),
        'state_l3_m': nrm((DEC_BATCH, 2, M_HEADS)),
        'c': nrm((DEC_BATCH, D)),
        'c_ctx': nrm((D,)),
        'ada_w': nrm((DEPTH, D, 6 * D), 0.5 * D ** -0.5),
        'ada_b': nrm((DEPTH, 6 * D), 0.01),
        'norm1': gain((DEPTH, D)),
        'norm2': gain((DEPTH, D)),
        'router': nrm((DEPTH, D, N_EXPERTS), D ** -0.5),
        'exp_gate': nrm((DEPTH, N_EXPERTS, D, D_EXPERT), D ** -0.5),
        'exp_up': nrm((DEPTH, N_EXPERTS, D, D_EXPERT), D ** -0.5),
        'exp_down': nrm((DEPTH, N_EXPERTS, D_EXPERT, D), D_EXPERT ** -0.5),
        'm_qkvo': nrm((N_MLSTM, D, M_PROJ), D ** -0.5),
        'm_gate_w': nrm((N_MLSTM, D, 4 * M_HEADS), 0.5 * D ** -0.5),
        'm_gate_b': m_gate_base + nrm((N_MLSTM, 4 * M_HEADS), 0.1),
        'm_hnorm': gain((N_MLSTM, M_V)),
        'm_out': nrm((N_MLSTM, M_V, D), M_V ** -0.5),
        'a_qkv': nrm((N_ATTN, D, A_Q_DIM + 2 * A_KV_DIM), D ** -0.5),
        'a_qnorm': gain((N_ATTN, A_HD)),
        'a_knorm': gain((N_ATTN, A_HD)),
        'a_out': nrm((N_ATTN, A_Q_DIM, D), A_Q_DIM ** -0.5),
        'r_mix': jax.random.uniform(next(ks), (N_RWKV, 6, D), f32),
        'r_rkv': nrm((N_RWKV, 3, D, D), D ** -0.5),
        'r_w0': jnp.linspace(-6.0, -1.0, D, dtype=f32) + nrm((N_RWKV, 2, D), 0.1),
        'r_w1': nrm((N_RWKV, 2, D, R_DECAY_LORA), 0.1 * D ** -0.5),
        'r_w2': nrm((N_RWKV, 2, R_DECAY_LORA, D), 0.1 * R_DECAY_LORA ** -0.5),
        'r_a0': nrm((N_RWKV, 2, D), 0.1),
        'r_a1': nrm((N_RWKV, 2, D, R_AAA_LORA), 0.1 * D ** -0.5),
        'r_a2': nrm((N_RWKV, 2, R_AAA_LORA, D), 0.1 * R_AAA_LORA ** -0.5),
        'r_g1': nrm((N_RWKV, D, R_GATE_LORA), D ** -0.5),
        'r_g2': nrm((N_RWKV, R_GATE_LORA, D), R_GATE_LORA ** -0.5),
        'r_kk': 0.85 + nrm((N_RWKV, D), 0.02),
        'r_ka': gain((N_RWKV, D)),
        'r_rk': nrm((N_RWKV, R_HEADS, R_HD), 0.1),
        'r_lnx_w': gain((N_RWKV, D)),
        'r_lnx_b': nrm((N_RWKV, D), 0.01),
        'r_out': nrm((N_RWKV, D, D), D ** -0.5),
        'final_norm': gain((D,)),
    }


def reference(x_prompt, x_sample, state_l0_C, state_l0_n, state_l0_m, cache_l1_k, cache_l1_v, state_l2_S,
              state_l3_C, state_l3_n, state_l3_m, c, c_ctx, ada_w, ada_b, norm1, norm2, router, exp_gate,
              exp_up, exp_down, m_qkvo, m_gate_w, m_gate_b, m_hnorm, m_out, a_qkv, a_qnorm, a_knorm, a_out,
              r_mix, r_rkv, r_w0, r_w1, r_w2, r_a0, r_a1, r_a2, r_g1, r_g2, r_kk, r_ka, r_rk, r_lnx_w,
              r_lnx_b, r_out, final_norm):

    def run_trunk(x, cond, latent_states):
        latent = latent_states is not None
        B, T, _ = x.shape
        dt = x.dtype
        rope = axial_rope_tables(T) if latent else None
        ctx_tensors = []
        for i in range(DEPTH):
            kind, j = i % N_MIXERS, i // N_MIXERS
            mod = jax.nn.silu(cond) @ ada_w[i] + ada_b[i]
            sh1, sc1, g1, sh2, sc2, g2 = jnp.split(mod[:, None, :], 6, axis=-1)
            hmix = rmsnorm(x, norm1[i]) * (1 + sc1) + sh1
            if kind == 0:
                if latent:
                    st = latent_states[i]
                else:
                    st = (jnp.zeros((B, 2, M_HEADS, M_DK, M_DV), dt), jnp.zeros((B, 2, M_HEADS, M_DK), dt),
                          jnp.zeros((B, 2, M_HEADS), dt))
                mix, fin = mlstm_mixer(hmix, *st, m_qkvo[j], m_gate_w[j], m_gate_b[j], m_hnorm[j], m_out[j])
                produced = () if latent else fin
            elif kind == 1:
                q, k, v = attn_project(hmix, a_qkv[j], a_qnorm[j], a_knorm[j])
                if latent:
                    cos, sin = rope
                    q, k = apply_rope(q, cos, sin), apply_rope(k, cos, sin)
                    ck, cv = latent_states[i]
                    k = jnp.concatenate([ck.astype(k.dtype), k], axis=1)
                    v = jnp.concatenate([cv.astype(v.dtype), v], axis=1)
                    produced = ()
                else:
                    produced = (k, v)
                mix = blocked_attention(q, k, v) @ a_out[j]
            else:
                S0 = latent_states[i][0] if latent else jnp.zeros((B, 2, R_HEADS, R_HD, R_HD), dt)
                mix, S_fin = rwkv7_mixer(hmix, S0, r_mix[j], r_rkv[j], r_w0[j], r_w1[j], r_w2[j], r_a0[j],
                                         r_a1[j], r_a2[j], r_g1[j], r_g2[j], r_kk[j], r_ka[j], r_rk[j],
                                         r_lnx_w[j], r_lnx_b[j], r_out[j])
                produced = () if latent else (S_fin,)
            x = x + g1 * mix
            hffn = rmsnorm(x, norm2[i]) * (1 + sc2) + sh2
            x = x + g2 * expert_choice_ffn(hffn, router[i], exp_gate[i], exp_up[i], exp_down[i])
            ctx_tensors.extend(produced)
        return rmsnorm(x, final_norm), ctx_tensors

    y_prompt, new_state = run_trunk(x_prompt, c_ctx[None, :], None)
    lat_states = [(state_l0_C, state_l0_n, state_l0_m), (cache_l1_k, cache_l1_v), (state_l2_S,),
                  (state_l3_C, state_l3_n, state_l3_m)]
    y_sample, _ = run_trunk(x_sample, c, lat_states)
    return (y_prompt, y_sample, *new_state)
```

```python
import functools

import jax
import jax.numpy as jnp
from jax import lax
from jax.experimental import pallas as pl
from jax.experimental.pallas import tpu as pltpu

F32 = jnp.float32
BF16 = jnp.bfloat16
I32 = jnp.int32

D_MODEL = 2048
BATCH = 32
SEQ = 256
DEPTH = 4
DEC_BATCH = 4
DEC_SEQ = 4096
PAST_LEN = 256
GRID_W = 64
N_MIXERS = 3
NORM_EPS = 1e-6
M_HEADS = 8
M_DK = D_MODEL // 2 // M_HEADS
M_DV = D_MODEL // M_HEADS
M_GATE_CAP = 15.0
A_HEADS = 16
A_KV_HEADS = 4
A_HD = 128
ROPE_THETA = 10000.0
R_HD = 64
R_LN_EPS = 64e-5
N_EXPERTS = 16
EC_CAPACITY_FACTOR = 2
D_EXPERT = 1024

LANES = 128
SUBLANES = 8
VMEM_LIMIT_BYTES = 56 * 2 ** 20

MLSTM_CHUNK = 256
RWKV_CHUNK = 64
RWKV_PAIRS_PER_GROUP = 4
RWKV_SOLVE_PASSES = 1
ROUTE_BLOCKS = 128
SLAB_CHUNK = 32


def _params(*sem):
    return pltpu.CompilerParams(dimension_semantics=sem, vmem_limit_bytes=VMEM_LIMIT_BYTES)


def _dot(a, b):
    return jnp.dot(a, b, preferred_element_type=F32)


def _dot_nt(a, b):
    return lax.dot_general(a, b, (((1,), (1,)), ((), ())), preferred_element_type=F32)


def _dot_tn(a, b):
    return lax.dot_general(a, b, (((0,), (0,)), ((), ())), preferred_element_type=F32)


def _split2(x):
    hi = x.astype(BF16)
    lo = (x - hi.astype(F32)).astype(BF16)
    return hi, lo


def _split3(x):
    h1 = x.astype(BF16)
    r1 = x - h1.astype(F32)
    h2 = r1.astype(BF16)
    h3 = (r1 - h2.astype(F32)).astype(BF16)
    return h1, h2, h3


def _dot3(a, w):
    ah, al = _split2(a)
    wh, wl = _split2(w)
    return _dot(ah, wh) + _dot(al, wh) + _dot(ah, wl)


_NN = (((1,), (0,)), ((), ()))
_NT = (((1,), (1,)), ((), ()))
_TN = (((0,), (0,)), ((), ()))


def _mm_passes(a, b, dims, passes):
    def f(x, y):
        return lax.dot_general(x, y, dims, preferred_element_type=F32)
    if passes == 1:
        return f(a.astype(BF16), b.astype(BF16))
    ah, al = _split2(a)
    bh, bl = _split2(b)
    return f(ah, bh) + f(al, bh) + f(ah, bl)


def _iota(shape, dim):
    return lax.broadcasted_iota(I32, shape, dim)


def _log_sigmoid(z):
    return jnp.minimum(z, 0.0) - jnp.log(1.0 + jnp.exp(-jnp.abs(z)))


def _adaln_kernel(c_ref, w_ref, b_ref, o_ref):
    c = c_ref[...]
    s = c * jax.nn.sigmoid(c)
    hi, lo = _split2(s)
    w = w_ref[0].astype(BF16)
    o_ref[0] = _dot(hi, w) + _dot(lo, w) + b_ref[0]


def _adaln(cond8, ada_w, ada_b):
    depth, d, d6 = ada_w.shape
    tn = min(d, 1024)
    return pl.pallas_call(
        _adaln_kernel,
        grid=(depth, d6 // tn),
        in_specs=[pl.BlockSpec((SUBLANES, d), lambda l, j: (0, 0)),
                  pl.BlockSpec((1, d, tn), lambda l, j: (l, 0, j)),
                  pl.BlockSpec((1, 1, tn), lambda l, j: (l, 0, j))],
        out_specs=pl.BlockSpec((1, SUBLANES, tn), lambda l, j: (l, 0, j)),
        out_shape=jax.ShapeDtypeStruct((depth, SUBLANES, d6), F32),
        compiler_params=_params("arbitrary", "arbitrary"),
        name="adaln",
    )(cond8, ada_w, ada_b.reshape(depth, 1, d6))


def _normed(x_ref, g_ref):
    x = x_ref[...]
    return x * lax.rsqrt(jnp.mean(x * x, axis=-1, keepdims=True) + NORM_EPS) * g_ref[...]


def _norm_plain_kernel(sh, sc, x_ref, g_ref, mod_ref, o_ref):
    h = _normed(x_ref, g_ref) * (1.0 + mod_ref[0, sc:sc + 1, :]) + mod_ref[0, sh:sh + 1, :]
    o_ref[...] = h.astype(o_ref.dtype)


def _norm_gate_kernel(sh, sc, x_ref, g_ref, mod_ref, w_ref, b_ref, o_ref, gate_ref):
    h = _normed(x_ref, g_ref) * (1.0 + mod_ref[0, sc:sc + 1, :]) + mod_ref[0, sh:sh + 1, :]
    o_ref[...] = h.astype(o_ref.dtype)
    z = _dot3(h, w_ref[...]) + b_ref[...]
    z = M_GATE_CAP * jnp.tanh(z / M_GATE_CAP)
    is_forget = ((_iota(z.shape, 1) // M_HEADS) % 2) == 1
    gate_ref[...] = jnp.where(is_forget, _log_sigmoid(z), z)


def _norm_router_kernel(sh, sc, x_ref, g_ref, mod_ref, w_ref, o_ref, aff_ref):
    h = _normed(x_ref, g_ref) * (1.0 + mod_ref[0, sc:sc + 1, :]) + mod_ref[0, sh:sh + 1, :]
    o_ref[...] = h
    z = _dot3(h, w_ref[...])
    e = jnp.exp(z - jnp.max(z, axis=-1, keepdims=True))
    aff_ref[...] = e / jnp.sum(e, axis=-1, keepdims=True)


def _norm_final_kernel(x_ref, g_ref, o_ref):
    o_ref[...] = _normed(x_ref, g_ref)


def _row_tile(n, t, want):
    tm = min(want, t)
    assert t % tm == 0 and n % tm == 0
    return tm


def _mod_spec(mod, tm, t):
    d = mod.shape[-1]
    if mod.shape[0] == 1:
        return pl.BlockSpec((1, 6, d), lambda i: (0, 0, 0))
    return pl.BlockSpec((1, 6, d), lambda i: ((i * tm) // t, 0, 0))


def _norm_call(kind, x, gain, mod, t, rows=(0, 1), w=None, b=None, out_dtype=F32):
    n, d = x.shape
    tm = _row_tile(n, t, 256)
    xs = pl.BlockSpec((tm, d), lambda i: (i, 0))
    gs = pl.BlockSpec((1, d), lambda i: (0, 0))
    ms = None if mod is None else _mod_spec(mod, tm, t)
    gain = gain.reshape(1, d)
    if kind == "final":
        return pl.pallas_call(_norm_final_kernel, grid=(n // tm,), in_specs=[xs, gs], out_specs=xs,
                              out_shape=jax.ShapeDtypeStruct((n, d), F32),
                              compiler_params=_params("arbitrary"), name="norm_final")(x, gain)
    sh, sc = rows
    if kind == "plain":
        return pl.pallas_call(functools.partial(_norm_plain_kernel, sh, sc), grid=(n // tm,),
                              in_specs=[xs, gs, ms], out_specs=xs,
                              out_shape=jax.ShapeDtypeStruct((n, d), out_dtype),
                              compiler_params=_params("arbitrary"), name="norm_plain")(x, gain, mod)
    ns = w.shape[1]
    ws = pl.BlockSpec((d, ns), lambda i: (0, 0))
    ps = pl.BlockSpec((tm, ns), lambda i: (i, 0))
    if kind == "gate":
        return pl.pallas_call(
            functools.partial(_norm_gate_kernel, sh, sc), grid=(n // tm,),
            in_specs=[xs, gs, ms, ws, pl.BlockSpec((1, ns), lambda i: (0, 0))],
            out_specs=[xs, ps],
            out_shape=[jax.ShapeDtypeStruct((n, d), out_dtype), jax.ShapeDtypeStruct((n, ns), F32)],
            compiler_params=_params("arbitrary"), name="norm_gate")(x, gain, mod, w, b.reshape(1, ns))
    assert kind == "router"
    return pl.pallas_call(
        functools.partial(_norm_router_kernel, sh, sc), grid=(n // tm,),
        in_specs=[xs, gs, ms, ws], out_specs=[xs, ps],
        out_shape=[jax.ShapeDtypeStruct((n, d), F32), jax.ShapeDtypeStruct((n, ns), F32)],
        compiler_params=_params("arbitrary"), name="norm_router")(x, gain, mod, w)


def _mm_kernel(x_ref, w_ref, o_ref):
    o_ref[...] = _dot(x_ref[...], w_ref[...])


def _mm(x, w, tn):
    n, k = x.shape
    m = w.shape[1]
    tm = min(512, n)
    tn = min(tn, m)
    return pl.pallas_call(
        _mm_kernel, grid=(m // tn, n // tm),
        in_specs=[pl.BlockSpec((tm, k), lambda j, i: (i, 0)), pl.BlockSpec((k, tn), lambda j, i: (0, j))],
        out_specs=pl.BlockSpec((tm, tn), lambda j, i: (i, j)),
        out_shape=jax.ShapeDtypeStruct((n, m), F32),
        compiler_params=_params("arbitrary", "arbitrary"), name="mm")(x, w)


def _mlstm_kernel(has_init, emit_state, *refs):
    q_ref, k_ref, v_ref, g_ref, gt_ref = refs[:5]
    pos = 5
    if has_init:
        c0_ref, n0_ref, m0_ref = refs[pos:pos + 3]
        pos += 3
    h_ref = refs[pos]
    pos += 1
    if emit_state:
        cout_ref, nout_ref, mout_ref = refs[pos:pos + 3]
        pos += 3
    c_scr, n_scr, m_scr = refs[pos:pos + 3]

    d = pl.program_id(1)
    c = pl.program_id(2)
    nc = pl.num_programs(2)
    fwd = d == 0
    heads = M_HEADS
    length = q_ref.shape[0]

    @pl.when(c == 0)
    def _():
        if has_init:
            c_scr[...] = c0_ref[0, 0]
            n_scr[...] = n0_ref[0, 0]
            m_scr[...] = m0_ref[0, 0]
        else:
            c_scr[...] = jnp.zeros_like(c_scr)
            n_scr[...] = jnp.zeros_like(n_scr)
            m_scr[...] = jnp.zeros_like(m_scr)

    sgn = jnp.where(fwd, 1, -1)
    row = _iota((length, length), 0)
    col = _iota((length, length), 1)
    mask = ((row - col) * sgn) >= 0
    mask_t = ((col - row) * sgn) >= 0
    scale = M_DK ** -0.5
    g = g_ref[...]
    gt = gt_ref[0]

    for h in range(heads):
        ig_row = jnp.where(fwd, gt[h:h + 1, :], gt[2 * heads + h:2 * heads + h + 1, :])
        lf_row = jnp.where(fwd, gt[heads + h:heads + h + 1, :], gt[3 * heads + h:3 * heads + h + 1, :])
        ig_col = jnp.where(fwd, g[:, h:h + 1], g[:, 2 * heads + h:2 * heads + h + 1])
        lf_col = jnp.where(fwd, g[:, heads + h:heads + h + 1], g[:, 3 * heads + h:3 * heads + h + 1])
        b_col = jnp.sum(jnp.where(mask, lf_row, 0.0), axis=1, keepdims=True)
        b_row = jnp.sum(jnp.where(mask_t, lf_col, 0.0), axis=0, keepdims=True)
        dmat = jnp.where(mask, b_col - b_row + ig_row, -jnp.inf)
        m_prev = m_scr[h:h + 1, 0:1]
        a = b_col + m_prev
        m_t = jnp.maximum(a, jnp.max(dmat, axis=1, keepdims=True))
        w_inter = jnp.exp(a - m_t)
        qh = q_ref[:, h * M_DK:(h + 1) * M_DK]
        kh = k_ref[:, h * M_DK:(h + 1) * M_DK] * scale
        vh = v_ref[:, h * M_DV:(h + 1) * M_DV]
        qb, kb, vb = qh.astype(BF16), kh.astype(BF16), vh.astype(BF16)
        s = _dot_nt(qb, kb) * jnp.exp(dmat - m_t)
        c_prev = c_scr[h]
        n_prev = n_scr[h:h + 1, :]
        num = w_inter * _dot(qb, c_prev.astype(BF16)) + _dot(s.astype(BF16), vb)
        den = w_inter * jnp.sum(qh * n_prev, axis=1, keepdims=True) + jnp.sum(s, axis=1, keepdims=True)
        h_ref[0, :, h * M_DV:(h + 1) * M_DV] = num / jnp.maximum(jnp.abs(den), jnp.exp(-m_t))

        b_last = jnp.sum(lf_col, axis=0, keepdims=True)
        g_col = b_last - b_col + ig_col
        m_new = jnp.maximum(b_last + m_prev, jnp.max(g_col, axis=0, keepdims=True))
        decay = jnp.exp(b_last + m_prev - m_new)
        wk = jnp.exp(g_col - m_new) * kh
        c_scr[h] = decay * c_prev + _dot_tn(wk.astype(BF16), vb)
        n_scr[h:h + 1, :] = decay * n_prev + jnp.sum(wk, axis=0, keepdims=True)
        m_scr[h:h + 1, :] = jnp.broadcast_to(m_new, (1, LANES))

    if emit_state:
        @pl.when(c == nc - 1)
        def _():
            cout_ref[0, 0] = c_scr[...]
            nout_ref[0, 0] = n_scr[...]
            mout_ref[0, 0] = m_scr[...]


def _mlstm_scan(proj, gates, b, t, init, emit_state):
    n = proj.shape[0]
    heads, dk, dv = M_HEADS, M_DK, M_DV
    qk, mv = heads * dk, heads * dv
    length = min(MLSTM_CHUNK, t)
    nc = t // length
    gates_t = gates.reshape(n // length, length, 4 * heads).transpose(0, 2, 1)

    def tile(bi, di, ci):
        return bi * nc + ci + di * (nc - 1 - 2 * ci)

    in_specs = [pl.BlockSpec((length, qk), lambda bi, di, ci: (tile(bi, di, ci), 0)),
                pl.BlockSpec((length, qk), lambda bi, di, ci: (tile(bi, di, ci), 1)),
                pl.BlockSpec((length, mv), lambda bi, di, ci: (tile(bi, di, ci), 1)),
                pl.BlockSpec((length, 4 * heads), lambda bi, di, ci: (tile(bi, di, ci), 0)),
                pl.BlockSpec((1, 4 * heads, length), lambda bi, di, ci: (tile(bi, di, ci), 0, 0))]
    args = [proj, proj, proj, gates, gates_t]
    state_specs = [pl.BlockSpec((1, 1, heads, dk, dv), lambda bi, di, ci: (bi, di, 0, 0, 0)),
                   pl.BlockSpec((1, 1, heads, dk), lambda bi, di, ci: (bi, di, 0, 0)),
                   pl.BlockSpec((1, 1, heads, LANES), lambda bi, di, ci: (bi, di, 0, 0))]
    if init is not None:
        c0, n0, m0 = init
        in_specs += state_specs
        args += [c0, n0, jnp.broadcast_to(m0[..., None], m0.shape + (LANES,))]
    out_specs = [pl.BlockSpec((1, length, mv), lambda bi, di, ci: (di, tile(bi, di, ci), 0))]
    out_shape = [jax.ShapeDtypeStruct((2, n, mv), F32)]
    if emit_state:
        out_specs += state_specs
        out_shape += [jax.ShapeDtypeStruct((b, 2, heads, dk, dv), F32),
                      jax.ShapeDtypeStruct((b, 2, heads, dk), F32),
                      jax.ShapeDtypeStruct((b, 2, heads, LANES), F32)]
    return pl.pallas_call(
        functools.partial(_mlstm_kernel, init is not None, emit_state),
        grid=(b, 2, nc), in_specs=in_specs, out_specs=out_specs, out_shape=out_shape,
        scratch_shapes=[pltpu.VMEM((heads, dk, dv), F32), pltpu.VMEM((heads, dk), F32),
                        pltpu.VMEM((heads, LANES), F32)],
        compiler_params=_params("arbitrary", "arbitrary", "arbitrary"), name="mlstm_scan")(*args)


def _mlstm_out_kernel(hs_ref, o_ref, gain_ref, w_ref, x_ref, mod_ref, out_ref):
    hs = hs_ref[0] + hs_ref[1]
    parts = []
    for h in range(M_HEADS):
        seg = hs[:, h * M_DV:(h + 1) * M_DV]
        parts.append(seg * lax.rsqrt(jnp.mean(seg * seg, axis=-1, keepdims=True) + NORM_EPS))
    hn = jnp.concatenate(parts, axis=1) * gain_ref[...]
    lhs = (jax.nn.sigmoid(o_ref[...]) * hn).astype(BF16)
    out_ref[...] = x_ref[...] + mod_ref[0, 2:3, :] * _dot(lhs, w_ref[...])


def _mlstm_out(hs2, proj, gain, w_out, x, mod, t):
    n, d = x.shape
    mv = M_HEADS * M_DV
    tm = _row_tile(n, t, 256)
    return pl.pallas_call(
        _mlstm_out_kernel, grid=(n // tm,),
        in_specs=[pl.BlockSpec((2, tm, mv), lambda i: (0, i, 0)),
                  pl.BlockSpec((tm, mv), lambda i: (i, 2)),
                  pl.BlockSpec((1, mv), lambda i: (0, 0)),
                  pl.BlockSpec((mv, d), lambda i: (0, 0)),
                  pl.BlockSpec((tm, d), lambda i: (i, 0)),
                  _mod_spec(mod, tm, t)],
        out_specs=pl.BlockSpec((tm, d), lambda i: (i, 0)),
        out_shape=jax.ShapeDtypeStruct((n, d), F32),
        compiler_params=_params("arbitrary"), name="mlstm_out")(hs2, proj, gain.reshape(1, mv), w_out, x, mod)


def _attn_qkv_kernel(use_rope, *refs):
    if use_rope:
        x_ref, w_ref, qg_ref, kg_ref, cos_ref, sin_ref, q_ref, k_ref, v_ref = refs
    else:
        x_ref, w_ref, qg_ref, kg_ref, q_ref, k_ref, v_ref = refs
    acc = _dot(x_ref[...], w_ref[...])
    qd, kvd = A_HEADS * A_HD, A_KV_HEADS * A_HD

    def head_norm(xh, gain):
        xh = xh * lax.rsqrt(jnp.mean(xh * xh, axis=-1, keepdims=True) + NORM_EPS) * gain
        if use_rope:
            xh = xh * cos_ref[...] + pltpu.roll(xh, A_HD // 2, axis=1) * sin_ref[...]
        return xh

    for h in range(A_HEADS):
        qh = head_norm(acc[:, h * A_HD:(h + 1) * A_HD], qg_ref[...])
        q_ref[:, h * A_HD:(h + 1) * A_HD] = (qh * (A_HD ** -0.5)).astype(BF16)
    for h in range(A_KV_HEADS):
        k_ref[:, h * A_HD:(h + 1) * A_HD] = head_norm(acc[:, qd + h * A_HD:qd + (h + 1) * A_HD], kg_ref[...])
    v_ref[...] = acc[:, qd + kvd:]


def _attn_qkv(hb, w, qgain, kgain, t, rope):
    n, d = hb.shape
    qd, kvd = A_HEADS * A_HD, A_KV_HEADS * A_HD
    tm = _row_tile(n, t, 256)
    in_specs = [pl.BlockSpec((tm, d), lambda i: (i, 0)),
                pl.BlockSpec((d, qd + 2 * kvd), lambda i: (0, 0)),
                pl.BlockSpec((1, A_HD), lambda i: (0, 0)),
                pl.BlockSpec((1, A_HD), lambda i: (0, 0))]
    args = [hb, w, qgain.reshape(1, A_HD), kgain.reshape(1, A_HD)]
    if rope is not None:
        in_specs += [pl.BlockSpec((tm, A_HD), lambda i: (i % (t // tm), 0))] * 2
        args += list(rope)
    return pl.pallas_call(
        functools.partial(_attn_qkv_kernel, rope is not None), grid=(n // tm,),
        in_specs=in_specs,
        out_specs=[pl.BlockSpec((tm, qd), lambda i: (i, 0)),
                   pl.BlockSpec((tm, kvd), lambda i: (i, 0)),
                   pl.BlockSpec((tm, kvd), lambda i: (i, 0))],
        out_shape=[jax.ShapeDtypeStruct((n, qd), BF16), jax.ShapeDtypeStruct((n, kvd), F32),
                   jax.ShapeDtypeStruct((n, kvd), F32)],
        compiler_params=_params("arbitrary"), name="attn_qkv")(*args)


def _attn_kernel(has_cache, *refs):
    if has_cache:
        q_ref, kn_ref, vn_ref, kc_ref, vc_ref, o_ref, k_scr, v_scr = refs
    else:
        q_ref, kn_ref, vn_ref, o_ref, k_scr, v_scr = refs
    group = A_HEADS // A_KV_HEADS
    tq = q_ref.shape[0]
    t_new = kn_ref.shape[0]
    past = k_scr.shape[0] - t_new

    @pl.when(pl.program_id(2) == 0)
    def _():
        if has_cache:
            k_scr[0:past, :] = kc_ref[...].astype(BF16)
            v_scr[0:past, :] = vc_ref[...].astype(BF16)
        k_scr[past:, :] = kn_ref[...].astype(BF16)
        v_scr[past:, :] = vn_ref[...].astype(BF16)

    q4 = jnp.concatenate([q_ref[:, g * A_HD:(g + 1) * A_HD] for g in range(group)], axis=0)
    s = _dot_nt(q4, k_scr[...])
    p = jnp.exp(s - jnp.max(s, axis=-1, keepdims=True))
    o = _dot(p.astype(BF16), v_scr[...]) / jnp.sum(p, axis=-1, keepdims=True)
    for g in range(group):
        o_ref[:, g * A_HD:(g + 1) * A_HD] = o[g * tq:(g + 1) * tq, :].astype(BF16)


def _attention(q, k, v, b, t, cache):
    n = q.shape[0]
    group = A_HEADS // A_KV_HEADS
    tq = min(64, t)
    nq = t // tq
    past = 0 if cache is None else cache[0].shape[0] // b
    in_specs = [pl.BlockSpec((tq, group * A_HD), lambda bi, hi, qi: (bi * nq + qi, hi)),
                pl.BlockSpec((t, A_HD), lambda bi, hi, qi: (bi, hi)),
                pl.BlockSpec((t, A_HD), lambda bi, hi, qi: (bi, hi))]
    args = [q, k, v]
    if cache is not None:
        in_specs += [pl.BlockSpec((past, A_HD), lambda bi, hi, qi: (bi, hi))] * 2
        args += list(cache)
    return pl.pallas_call(
        functools.partial(_attn_kernel, cache is not None), grid=(b, A_KV_HEADS, nq),
        in_specs=in_specs,
        out_specs=pl.BlockSpec((tq, group * A_HD), lambda bi, hi, qi: (bi * nq + qi, hi)),
        out_shape=jax.ShapeDtypeStruct((n, A_HEADS * A_HD), BF16),
        scratch_shapes=[pltpu.VMEM((past + t, A_HD), BF16), pltpu.VMEM((past + t, A_HD), BF16)],
        compiler_params=_params("arbitrary", "arbitrary", "arbitrary"), name="attention")(*args)


def _resid_mm_kernel(lhs_ref, w_ref, x_ref, mod_ref, out_ref):
    out_ref[...] = x_ref[...] + mod_ref[0, 2:3, :] * _dot(lhs_ref[...], w_ref[...])


def _resid_mm(lhs, w, x, mod, t):
    n, d = x.shape
    k = lhs.shape[1]
    tm = _row_tile(n, t, 256)
    return pl.pallas_call(
        _resid_mm_kernel, grid=(n // tm,),
        in_specs=[pl.BlockSpec((tm, k), lambda i: (i, 0)),
                  pl.BlockSpec((k, d), lambda i: (0, 0)),
                  pl.BlockSpec((tm, d), lambda i: (i, 0)),
                  _mod_spec(mod, tm, t)],
        out_specs=pl.BlockSpec((tm, d), lambda i: (i, 0)),
        out_shape=jax.ShapeDtypeStruct((n, d), F32),
        compiler_params=_params("arbitrary"), name="resid_mm")(lhs, w, x, mod)


def _token_shift_delta(h, hp_ref, hn_ref, i, tiles_per_seq):
    tm = h.shape[0]
    rowi = _iota((tm, 1), 0)
    ti = i % tiles_per_seq
    prev_row = jnp.where(ti == 0, 0.0, hp_ref[SUBLANES - 1:SUBLANES, :])
    next_row = jnp.where(ti == tiles_per_seq - 1, 0.0, hn_ref[0:1, :])
    h_prev = jnp.where(rowi == 0, prev_row, pltpu.roll(h, 1, axis=0))
    h_next = jnp.where(rowi == tm - 1, next_row, pltpu.roll(h, tm - 1, axis=0))
    return 0.5 * (h_prev + h_next) - h


def _shift_specs(tm, d, n, grid_rank):
    nb8 = n // SUBLANES
    per = tm // SUBLANES
    if grid_rank == 1:
        return [pl.BlockSpec((tm, d), lambda i: (i, 0)),
                pl.BlockSpec((SUBLANES, d), lambda i: (jnp.maximum(i * per - 1, 0), 0)),
                pl.BlockSpec((SUBLANES, d), lambda i: (jnp.minimum((i + 1) * per, nb8 - 1), 0))]
    return [pl.BlockSpec((tm, d), lambda j, i: (i, 0)),
            pl.BlockSpec((SUBLANES, d), lambda j, i: (jnp.maximum(i * per - 1, 0), 0)),
            pl.BlockSpec((SUBLANES, d), lambda j, i: (jnp.minimum((i + 1) * per, nb8 - 1), 0))]


def _rwkv_rkv_kernel(tiles_per_seq, h_ref, hp_ref, hn_ref, mix_ref, w_ref, o_ref):
    h = h_ref[...]
    xx = _token_shift_delta(h, hp_ref, hn_ref, pl.program_id(1), tiles_per_seq)
    o_ref[0] = _dot((h + xx * mix_ref[0]).astype(BF16), w_ref[0])


def _rwkv_rkv(h, mix3, w_rkv, t):
    n, d = h.shape
    tm = _row_tile(n, t, 256)
    return pl.pallas_call(
        functools.partial(_rwkv_rkv_kernel, t // tm), grid=(3, n // tm),
        in_specs=_shift_specs(tm, d, n, 2) + [pl.BlockSpec((1, 1, d), lambda j, i: (j, 0, 0)),
                                              pl.BlockSpec((1, d, d), lambda j, i: (j, 0, 0))],
        out_specs=pl.BlockSpec((1, tm, d), lambda j, i: (j, i, 0)),
        out_shape=jax.ShapeDtypeStruct((3, n, d), F32),
        compiler_params=_params("arbitrary", "arbitrary"), name="rwkv_rkv")(h, h, h, mix3, w_rkv)


def _rwkv_lora_kernel(tiles_per_seq, lora, h_ref, hp_ref, hn_ref, mix_ref, w1_ref, a1_ref, g1_ref,
                      w2_ref, a2_ref, g2_ref, w0_ref, a0_ref, lw_ref, asig_ref, g_ref):
    h = h_ref[...]
    xx = _token_shift_delta(h, hp_ref, hn_ref, pl.program_id(0), tiles_per_seq)
    xw = (h + xx * mix_ref[0:1, :]).astype(BF16)
    xa = (h + xx * mix_ref[1:2, :]).astype(BF16)
    xg = (h + xx * mix_ref[2:3, :]).astype(BF16)
    tw = jnp.tanh(_dot(xw, w1_ref[...])).astype(BF16)
    ta = _dot(xa, a1_ref[...]).astype(BF16)
    tg = jax.nn.sigmoid(_dot(xg, g1_ref[...])).astype(BF16)
    g_ref[...] = _dot(tg, g2_ref[...])
    for dr in range(2):
        z = -(w0_ref[dr] + _dot(tw[:, dr * lora:(dr + 1) * lora], w2_ref[dr]))
        softplus = jnp.maximum(z, 0.0) + jnp.log(1.0 + jnp.exp(-jnp.abs(z)))
        lw_ref[dr] = -jnp.exp(-softplus - 0.5)
        asig_ref[dr] = jax.nn.sigmoid(a0_ref[dr] + _dot(ta[:, dr * lora:(dr + 1) * lora], a2_ref[dr]))


def _rwkv_lora(h, mix3, w1c, a1c, g1, w2p, a2p, g2, w0, a0, t):
    n, d = h.shape
    tm = _row_tile(n, t, 256)
    lora = w2p.shape[1]
    gl = g1.shape[1]
    full2 = lambda shape: pl.BlockSpec(shape, lambda i: (0, 0))
    full3 = lambda shape: pl.BlockSpec(shape, lambda i: (0, 0, 0))
    return pl.pallas_call(
        functools.partial(_rwkv_lora_kernel, t // tm, lora), grid=(n // tm,),
        in_specs=_shift_specs(tm, d, n, 1) + [
            full2((3, d)), full2((d, 2 * lora)), full2((d, 2 * lora)), full2((d, gl)),
            full3((2, lora, d)), full3((2, lora, d)), full2((gl, d)), full3((2, 1, d)), full3((2, 1, d))],
        out_specs=[pl.BlockSpec((2, tm, d), lambda i: (0, i, 0)),
                   pl.BlockSpec((2, tm, d), lambda i: (0, i, 0)),
                   pl.BlockSpec((tm, d), lambda i: (i, 0))],
        out_shape=[jax.ShapeDtypeStruct((2, n, d), F32), jax.ShapeDtypeStruct((2, n, d), F32),
                   jax.ShapeDtypeStruct((n, d), F32)],
        compiler_params=_params("arbitrary"), name="rwkv_lora")(h, h, h, mix3, w1c, a1c, g1, w2p, a2p, g2, w0, a0)


def _rwkv_scan_kernel(has_init, emit_state, *refs):
    r_ref, k_ref, v_ref, lw_ref, a_ref, kk_ref, ka_ref, rk_ref = refs[:8]
    pos = 8
    if has_init:
        s0_ref = refs[pos]
        pos += 1
    y_ref, bonus_ref = refs[pos:pos + 2]
    pos += 2
    if emit_state:
        sout_ref = refs[pos]
        pos += 1
    s_scr, cum_scr = refs[pos:pos + 2]

    d = pl.program_id(1)
    c = pl.program_id(2)
    nc = pl.num_programs(2)
    fwd = d == 0
    sgn = jnp.where(fwd, 1, -1)
    length = r_ref.shape[1]
    npairs = s_scr.shape[0]
    hd = R_HD

    @pl.when(c == 0)
    def _():
        if has_init:
            s_scr[...] = s0_ref[0, 0]
        else:
            s_scr[...] = jnp.zeros_like(s_scr)

    ti = _iota((length, length), 0)
    si = _iota((length, length), 1)
    tri = (((ti - si) * sgn) >= 0).astype(BF16)
    l1, l2, l3 = _split3(lw_ref[0])
    cum_scr[...] = _dot(tri, l1) + _dot(tri, l2) + _dot(tri, l3)

    gl = 2 * length
    rr = _iota((gl, gl), 0)
    cc = _iota((gl, gl), 1)
    same = (rr // length) == (cc // length)
    e = ((rr % length) - (cc % length)) * sgn
    strict = same & (e > 0)
    incl = same & (e >= 0)
    eye = (rr == cc).astype(F32)
    lane = _iota((1, LANES), 1)
    m0 = (lane < hd).astype(F32)
    m1 = 1.0 - m0
    lr = _iota((LANES, LANES), 0)
    lc = _iota((LANES, LANES), 1)
    ones_blk = ((lr // hd) == (lc // hd)).astype(BF16)

    def segsum(x):
        hi, lo = _split2(x)
        return _dot(hi, ones_blk) + _dot(lo, ones_blk)

    def stack(x):
        return jnp.concatenate([x * m0, x * m1], axis=0)

    def smm(x, y, dims):
        return _mm_passes(x, y, dims, RWKV_SOLVE_PASSES)

    tr = rr % length
    tc = cc % length
    base = 4
    base_mask = (tr // base) == (tc // base)
    level_masks = []
    size = base
    while size < length:
        level_masks.append(((tr // (2 * size)) == (tc // (2 * size))) & ((tr // size) != (tc // size)))
        size *= 2

    def pair(p):
        ds = pl.ds(pl.multiple_of(p * LANES, LANES), LANES)
        r = r_ref[0, :, ds]
        k = k_ref[0, :, ds]
        v = v_ref[0, :, ds]
        lw = lw_ref[0, :, ds]
        a = a_ref[0, :, ds]
        cum = cum_scr[:, ds]
        kk = k * kk_ref[:, ds]
        kk = kk * lax.rsqrt(jnp.maximum(segsum(kk * kk), 1e-24))
        kd = k * (1.0 + (a - 1.0) * ka_ref[:, ds])
        bb = kk * a
        tot = jnp.where(fwd, cum[length - 1:length, :], cum[0:1, :])
        e_r = jnp.exp(cum)
        e_a = jnp.exp(cum - lw)
        e_n = jnp.exp(-cum)
        e_e = jnp.exp(tot - cum)
        sa = stack(-kk * e_a)
        sr_full = r * e_r
        sr = stack(sr_full)
        sb = stack(bb * e_n)
        sk = stack(kd * e_n)
        st = s_scr[p]
        stb = st.astype(BF16)
        vexp = stack(v)
        m_ab = jnp.where(strict, smm(sa, sb, _NT), 0.0)
        m_ak = jnp.where(strict, smm(sa, sk, _NT), 0.0)
        t_rb = jnp.where(incl, _dot_nt(sr.astype(BF16), sb.astype(BF16)), 0.0)
        t_rk = jnp.where(incl, _dot_nt(sr.astype(BF16), sk.astype(BF16)), 0.0)
        rhs = smm(sa, st, _NT) + smm(m_ak, vexp, _NN)
        d0 = jnp.where(base_mask, m_ab, 0.0)
        inv = eye + d0
        inv = inv + smm(inv, smm(d0, d0, _NN), _NN)
        for lm in level_masks:
            inv = inv + smm(smm(inv, jnp.where(lm, m_ab, 0.0), _NN), inv, _NN)
        uexp = smm(inv, rhs, _NN)
        ub = uexp.astype(BF16)
        vb = vexp.astype(BF16)
        t_rb_c = (t_rb[0:length] + t_rb[length:gl]).astype(BF16)
        t_rk_c = (t_rk[0:length] + t_rk[length:gl]).astype(BF16)
        y = _dot_nt(sr_full.astype(BF16), stb) + _dot(t_rb_c, ub) + _dot(t_rk_c, vb)
        y_ref[0, :, ds] = y
        bonus_ref[0, :, ds] = segsum(r * kd * rk_ref[:, ds]) * v
        s_scr[p] = st * jnp.exp(tot) + smm(uexp, stack(bb * e_e), _TN) + smm(vexp, stack(kd * e_e), _TN)

    npg = min(RWKV_PAIRS_PER_GROUP, npairs)

    def group(gi, carry):
        for u in range(npg):
            pair(gi * npg + u)
        return carry

    lax.fori_loop(0, npairs // npg, group, 0)

    if emit_state:
        @pl.when(c == nc - 1)
        def _():
            sout_ref[0, 0] = s_scr[...]


def _rwkv_scan(rkv, lw, asig, kk, ka, rk, b, t, s0, emit_state):
    n, d = rkv.shape[1:]
    length = RWKV_CHUNK
    nc = t // length
    npairs = d // LANES

    def tile(bi, di, ci):
        return bi * nc + ci + di * (nc - 1 - 2 * ci)

    def plane(j):
        return pl.BlockSpec((1, length, d), lambda bi, di, ci: (j, tile(bi, di, ci), 0))

    dirspec = pl.BlockSpec((1, length, d), lambda bi, di, ci: (di, tile(bi, di, ci), 0))
    vec = pl.BlockSpec((1, d), lambda bi, di, ci: (0, 0))
    sspec = pl.BlockSpec((1, 1, npairs, LANES, LANES), lambda bi, di, ci: (bi, di, 0, 0, 0))
    in_specs = [plane(0), plane(1), plane(2), dirspec, dirspec, vec, vec, vec]
    args = [rkv, rkv, rkv, lw, asig, kk.reshape(1, d), ka.reshape(1, d), rk.reshape(1, d)]
    if s0 is not None:
        in_specs.append(sspec)
        args.append(s0)
    out_specs = [dirspec, dirspec]
    out_shape = [jax.ShapeDtypeStruct((2, n, d), F32), jax.ShapeDtypeStruct((2, n, d), F32)]
    if emit_state:
        out_specs.append(sspec)
        out_shape.append(jax.ShapeDtypeStruct((b, 2, npairs, LANES, LANES), F32))
    return pl.pallas_call(
        functools.partial(_rwkv_scan_kernel, s0 is not None, emit_state),
        grid=(b, 2, nc), in_specs=in_specs, out_specs=out_specs, out_shape=out_shape,
        scratch_shapes=[pltpu.VMEM((npairs, LANES, LANES), F32), pltpu.VMEM((length, d), F32)],
        compiler_params=_params("arbitrary", "arbitrary", "arbitrary"), name="rwkv_scan")(*args)


def _rwkv_out_kernel(y_ref, bonus_ref, g_ref, lnw_ref, lnb_ref, w_ref, x_ref, mod_ref, out_ref):
    y = y_ref[0] + y_ref[1]
    lr = _iota((LANES, LANES), 0)
    lc = _iota((LANES, LANES), 1)
    mean_blk = ((lr // R_HD) == (lc // R_HD)).astype(BF16)

    def segmean(x):
        hi, lo = _split2(x)
        return (_dot(hi, mean_blk) + _dot(lo, mean_blk)) * (1.0 / R_HD)

    parts = []
    for j in range(y.shape[1] // LANES):
        yj = y[:, j * LANES:(j + 1) * LANES]
        cen = yj - segmean(yj)
        parts.append(cen * lax.rsqrt(segmean(cen * cen) + R_LN_EPS))
    yn = jnp.concatenate(parts, axis=1) * lnw_ref[...] + lnb_ref[...] + bonus_ref[0] + bonus_ref[1]
    lhs = (yn * g_ref[...]).astype(BF16)
    out_ref[...] = x_ref[...] + mod_ref[0, 2:3, :] * _dot(lhs, w_ref[...])


def _rwkv_out(y2, bonus2, g, lnw, lnb, w_out, x, mod, t):
    n, d = x.shape
    tm = _row_tile(n, t, 256)
    two = pl.BlockSpec((2, tm, d), lambda i: (0, i, 0))
    rows = pl.BlockSpec((tm, d), lambda i: (i, 0))
    vec = pl.BlockSpec((1, d), lambda i: (0, 0))
    return pl.pallas_call(
        _rwkv_out_kernel, grid=(n // tm,),
        in_specs=[two, two, rows, vec, vec, pl.BlockSpec((d, d), lambda i: (0, 0)), rows,
                  _mod_spec(mod, tm, t)],
        out_specs=rows, out_shape=jax.ShapeDtypeStruct((n, d), F32),
        compiler_params=_params("arbitrary"), name="rwkv_out")(
            y2, bonus2, g, lnw.reshape(1, d), lnb.reshape(1, d), w_out, x, mod)


def _route_kernel(cap, aff_ref, idx_ref, gate_ref, pos_ref, tot_ref, off_ref):
    aff = aff_ref[0]
    nb = aff.shape[0]
    bits = lax.bitcast_convert_type(aff, I32)
    tok = _iota(aff.shape, 0) * LANES + _iota(aff.shape, 1)

    def count(m):
        x = jnp.sum(m.astype(F32), axis=1, keepdims=True)
        return jnp.sum(x, axis=0, keepdims=True)

    def value_step(i, cur):
        cand = cur | (jnp.int32(1) << (30 - i))
        return jnp.where(count(bits >= cand) >= cap, cand, cur)

    thr = lax.fori_loop(0, 31, value_step, jnp.zeros((1, 1), I32))
    above = bits > thr
    equal = bits == thr
    need = cap - count(above)

    def index_step(i, cur):
        cand = cur + (jnp.int32(1) << (14 - i))
        return jnp.where(count(equal & (tok < cand)) < need, cand, cur)

    last_tie = lax.fori_loop(0, 15, index_step, jnp.zeros((1, 1), I32))
    sel = above | (equal & (tok <= last_tie))
    self32 = sel.astype(F32)
    selb = self32.astype(BF16)

    li = _iota((LANES, LANES), 0)
    lj = _iota((LANES, LANES), 1)
    upper_incl = (li <= lj).astype(BF16)
    cl = _dot(selb, upper_incl)
    ones8 = jnp.ones((SUBLANES, LANES), BF16)
    tot_row = _dot_nt(ones8, selb)
    bi = _iota((nb, nb), 0)
    bj = _iota((nb, nb), 1)
    offi_row = _dot(tot_row.astype(BF16), (bi <= bj).astype(BF16))
    offx_row = offi_row - tot_row
    tot_col = jnp.broadcast_to(cl[:, LANES - 1:LANES], (nb, LANES)).astype(BF16)
    offx_col = _dot((bj < bi).astype(BF16), tot_col)
    pos_ref[0] = jnp.where(sel, offx_col + cl - 1.0, -1.0).astype(I32)
    tot_ref[0] = tot_row
    off_ref[0] = offx_row

    pcol = _iota((cap, 1), 0).astype(F32)
    before = offi_row[0:1, :] <= pcol
    blk = jnp.sum(before.astype(F32), axis=1, keepdims=True)
    base = jnp.sum(jnp.where(before, tot_row[0:1, :], 0.0), axis=1, keepdims=True)
    onehot = (_iota((cap, nb), 1).astype(F32) == blk).astype(BF16)
    rowcnt = _dot(onehot, cl.astype(BF16))
    rank = pcol - base
    lane = jnp.sum((rowcnt <= rank).astype(F32), axis=1, keepdims=True)
    idx_ref[0] = (blk * LANES + lane).astype(I32)
    a1, a2, a3 = _split3(aff)
    rowaff = _dot(onehot, a1) + _dot(onehot, a2) + _dot(onehot, a3)
    gate_ref[0] = jnp.sum(jnp.where(_iota((cap, LANES), 1).astype(F32) == lane, rowaff, 0.0),
                          axis=1, keepdims=True)


def _route(aff3, cap):
    e, nb, _ = aff3.shape
    return pl.pallas_call(
        functools.partial(_route_kernel, cap), grid=(e,),
        in_specs=[pl.BlockSpec((1, nb, LANES), lambda i: (i, 0, 0))],
        out_specs=[pl.BlockSpec((1, cap, 1), lambda i: (i, 0, 0)),
                   pl.BlockSpec((1, cap, 1), lambda i: (i, 0, 0)),
                   pl.BlockSpec((1, nb, LANES), lambda i: (i, 0, 0)),
                   pl.BlockSpec((1, SUBLANES, nb), lambda i: (i, 0, 0)),
                   pl.BlockSpec((1, SUBLANES, nb), lambda i: (i, 0, 0))],
        out_shape=[jax.ShapeDtypeStruct((e, cap, 1), I32), jax.ShapeDtypeStruct((e, cap, 1), F32),
                   jax.ShapeDtypeStruct((e, nb, LANES), I32),
                   jax.ShapeDtypeStruct((e, SUBLANES, nb), F32), jax.ShapeDtypeStruct((e, SUBLANES, nb), F32)],
        compiler_params=_params("arbitrary"), name="moe_route")(aff3)


def _row_copy(x_hbm, xbuf, sem, tok, r):
    return pltpu.make_async_copy(x_hbm.at[pl.ds(tok, 1)], xbuf.at[pl.ds(r, 1)], sem.at[0])


def _expert_kernel(idx_ref, x_hbm, gate_ref, wg_ref, wu_ref, wd_ref, o_ref, xbuf, sem):
    tm = xbuf.shape[0]

    def start(r, carry):
        _row_copy(x_hbm, xbuf, sem, idx_ref[0, 0, r], r).start()
        return carry

    def wait(r, carry):
        _row_copy(x_hbm, xbuf, sem, 0, r).wait()
        return carry

    lax.fori_loop(0, tm, start, 0)
    lax.fori_loop(0, tm, wait, 0)
    xb = xbuf[...].astype(BF16)
    hg = _dot(xb, wg_ref[0])
    hu = _dot(xb, wu_ref[0])
    hid = (hg * jax.nn.sigmoid(hg) * hu).astype(BF16)
    o_ref[...] = _dot(hid, wd_ref[0]) * gate_ref[...]


def _experts(hffn, idx, gate, wg, wu, wd, cap):
    n, d = hffn.shape
    e, _, f = wg.shape
    tm = min(256, cap)
    nt = cap // tm
    idx3 = idx.reshape(e * nt, 1, tm)
    return pl.pallas_call(
        _expert_kernel, grid=(e, nt),
        in_specs=[pl.BlockSpec((1, 1, tm), lambda ei, ti: (ei * nt + ti, 0, 0), memory_space=pltpu.SMEM),
                  pl.BlockSpec(memory_space=pl.ANY),
                  pl.BlockSpec((tm, 1), lambda ei, ti: (ei * nt + ti, 0)),
                  pl.BlockSpec((1, d, f), lambda ei, ti: (ei, 0, 0)),
                  pl.BlockSpec((1, d, f), lambda ei, ti: (ei, 0, 0)),
                  pl.BlockSpec((1, f, d), lambda ei, ti: (ei, 0, 0))],
        out_specs=pl.BlockSpec((tm, d), lambda ei, ti: (ei * nt + ti, 0)),
        out_shape=jax.ShapeDtypeStruct((e * cap, d), F32),
        scratch_shapes=[pltpu.VMEM((tm, d), F32), pltpu.SemaphoreType.DMA((1,))],
        compiler_params=_params("arbitrary", "arbitrary"), name="moe_experts")(
            idx3, hffn, gate.reshape(e * cap, 1), wg, wu, wd)


def _slab_copy(ye_hbm, slab, sem, e, row, ch):
    return pltpu.make_async_copy(ye_hbm.at[pl.ds(row + ch * SLAB_CHUNK, SLAB_CHUNK)],
                                 slab.at[e, pl.ds(ch * SLAB_CHUNK, SLAB_CHUNK)], sem.at[e])


def _combine_kernel(cap, meta_ref, x_ref, mod_ref, pos_ref, ye_hbm, out_ref, slab, sem):
    n_exp = slab.shape[0]
    rows = slab.shape[1]
    nch = rows // SLAB_CHUNK
    tm = x_ref.shape[0]

    @pl.when(pl.program_id(0) == 0)
    def _():
        slab[...] = jnp.zeros_like(slab)

    for e in range(n_exp):
        row = pl.multiple_of(meta_ref[0, 0, e], SUBLANES)
        for ch in range(nch):
            @pl.when(ch < meta_ref[0, 0, n_exp + e])
            def _():
                _slab_copy(ye_hbm, slab, sem, e, row, ch).start()
    for e in range(n_exp):
        for ch in range(nch):
            @pl.when(ch < meta_ref[0, 0, n_exp + e])
            def _():
                _slab_copy(ye_hbm, slab, sem, e, 0, ch).wait()

    acc = jnp.zeros(x_ref.shape, F32)
    lane = _iota((tm, rows), 1)
    for e in range(n_exp):
        local = pos_ref[:, e:e + 1] - (meta_ref[0, 0, e] - e * cap)
        place = (lane == local).astype(BF16)
        acc = acc + _dot(place, slab[e].astype(BF16))
    out_ref[...] = x_ref[...] + mod_ref[0, 5:6, :] * acc


def _combine(x, mod, pos_t, meta, ye, cap, t, rows):
    n, d = x.shape
    tm = LANES
    n_exp = N_EXPERTS
    return pl.pallas_call(
        functools.partial(_combine_kernel, cap), grid=(n // tm,),
        in_specs=[pl.BlockSpec((1, 1, 2 * n_exp), lambda i: (i, 0, 0), memory_space=pltpu.SMEM),
                  pl.BlockSpec((tm, d), lambda i: (i, 0)),
                  _mod_spec(mod, tm, t),
                  pl.BlockSpec((tm, n_exp), lambda i: (i, 0)),
                  pl.BlockSpec(memory_space=pl.ANY)],
        out_specs=pl.BlockSpec((tm, d), lambda i: (i, 0)),
        out_shape=jax.ShapeDtypeStruct((n, d), F32),
        scratch_shapes=[pltpu.VMEM((n_exp, rows, d), F32), pltpu.SemaphoreType.DMA((n_exp,))],
        compiler_params=_params("arbitrary"), name="moe_combine")(meta, x, mod, pos_t, ye)


def _moe(x, mod, t, norm2, w_router, wg, wu, wd):
    n, d = x.shape
    n_exp = N_EXPERTS
    cap = (EC_CAPACITY_FACTOR * n) // n_exp
    hffn, aff = _norm_call("router", x, norm2, mod, t, rows=(3, 4), w=w_router)
    npad = ROUTE_BLOCKS * LANES
    assert n <= npad and n % LANES == 0 and cap % SLAB_CHUNK == 0
    aff_t = jnp.pad(aff.T, ((0, 0), (0, npad - n)), constant_values=-1.0)
    idx, gate, pos, tot, off = _route(aff_t.reshape(n_exp, ROUTE_BLOCKS, LANES), cap)
    ye = _experts(hffn, idx, gate, wg, wu, wd, cap)
    ntiles = n // LANES
    rows = min(LANES + SUBLANES + SLAB_CHUNK - (LANES + SUBLANES) % SLAB_CHUNK, cap)
    start = off[:, 0, :ntiles].T.astype(I32)
    cnt = tot[:, 0, :ntiles].T.astype(I32)
    start_al = jnp.minimum((start // SUBLANES) * SUBLANES, cap - rows)
    nch = jnp.where(cnt > 0, (start - start_al + cnt + SLAB_CHUNK - 1) // SLAB_CHUNK, 0)
    meta = jnp.concatenate([start_al + jnp.arange(n_exp, dtype=I32)[None, :] * cap, nch], axis=1)
    pos_t = pos.reshape(n_exp, npad)[:, :n].T
    return _combine(x, mod, pos_t, meta.reshape(ntiles, 1, 2 * n_exp), ye, cap, t, rows)


def _rope_tables(t):
    n_rows = t // GRID_W
    axis_dim = A_HD // 2
    row = jnp.repeat(jnp.arange(n_rows, dtype=F32), GRID_W)
    col = jnp.tile(jnp.arange(GRID_W, dtype=F32), n_rows)
    inv_freq = ROPE_THETA ** (-jnp.arange(0, axis_dim, 2, dtype=F32) / axis_dim)
    ang = jnp.concatenate([row[:, None] * inv_freq, col[:, None] * inv_freq], axis=-1)
    cos, sin = jnp.cos(ang), jnp.sin(ang)
    return jnp.concatenate([cos, cos], axis=-1), jnp.concatenate([-sin, sin], axis=-1)


def _pad_lora(w1, w2):
    r = w1.shape[2]
    rp = -(-r // LANES) * LANES
    w1p = jnp.pad(w1, ((0, 0), (0, 0), (0, rp - r)))
    w2p = jnp.pad(w2, ((0, 0), (0, rp - r), (0, 0)))
    return jnp.concatenate([w1p[0], w1p[1]], axis=1).astype(BF16), w2p.astype(BF16)


def _pair_state(s):
    b, _, heads, hd, _ = s.shape
    sp = s.reshape(b, 2, heads // 2, 2, hd, hd)
    out = jnp.zeros((b, 2, heads // 2, 2 * hd, 2 * hd), F32)
    out = out.at[:, :, :, :hd, :hd].set(sp[:, :, :, 0])
    return out.at[:, :, :, hd:, hd:].set(sp[:, :, :, 1])


def _unpair_state(sp):
    b, _, npairs, _, _ = sp.shape
    hd = R_HD
    s = jnp.stack([sp[:, :, :, :hd, :hd], sp[:, :, :, hd:, hd:]], axis=3)
    return s.reshape(b, 2, 2 * npairs, hd, hd)


def _trunk(x, mod_all, b, t, latent_states, p):
    n, d = x.shape
    latent = latent_states is not None
    produced = []
    for i in range(DEPTH):
        kind, j = i % N_MIXERS, i // N_MIXERS
        mod = mod_all[i]
        if kind == 0:
            hb, gates = _norm_call("gate", x, p["norm1"][i], mod, t, w=p["m_gate_w"][j], b=p["m_gate_b"][j],
                                   out_dtype=BF16)
            proj = _mm(hb, p["m_qkvo"][j], 2048)
            res = _mlstm_scan(proj, gates, b, t, latent_states[i] if latent else None, not latent)
            if not latent:
                produced += [res[1], res[2], res[3][..., 0]]
            x = _mlstm_out(res[0], proj, p["m_hnorm"][j], p["m_out"][j], x, mod, t)
        elif kind == 1:
            hb = _norm_call("plain", x, p["norm1"][i], mod, t, out_dtype=BF16)
            q, k, v = _attn_qkv(hb, p["a_qkv"][j], p["a_qnorm"][j], p["a_knorm"][j], t,
                                _rope_tables(t) if latent else None)
            cache = None
            if latent:
                ck, cv = latent_states[i]
                kvd = A_KV_HEADS * A_HD
                cache = (ck.reshape(-1, kvd), cv.reshape(-1, kvd))
            else:
                produced += [k.reshape(b, t, A_KV_HEADS, A_HD), v.reshape(b, t, A_KV_HEADS, A_HD)]
            o = _attention(q, k, v, b, t, cache)
            x = _resid_mm(o, p["a_out"][j], x, mod, t)
        else:
            h = _norm_call("plain", x, p["norm1"][i], mod, t)
            mix = p["r_mix"][j]
            rkv = _rwkv_rkv(h, mix[jnp.array([0, 2, 3])].reshape(3, 1, d), p["r_rkv"][j], t)
            lw, asig, g = _rwkv_lora(h, mix[jnp.array([1, 4, 5])], p["r_w1c"][j], p["r_a1c"][j], p["r_g1"][j],
                                     p["r_w2p"][j], p["r_a2p"][j], p["r_g2"][j],
                                     p["r_w0"][j].reshape(2, 1, d), p["r_a0"][j].reshape(2, 1, d), t)
            s0 = _pair_state(latent_states[i][0]) if latent else None
            res = _rwkv_scan(rkv, lw, asig, p["r_kk"][j], p["r_ka"][j], p["r_rk"][j].reshape(-1), b, t, s0,
                             not latent)
            if not latent:
                produced.append(_unpair_state(res[2]))
            x = _rwkv_out(res[0], res[1], g, p["r_lnx_w"][j], p["r_lnx_b"][j], p["r_out"][j], x, mod, t)
        x = _moe(x, mod, t, p["norm2"][i], p["router"][i], p["exp_gate"][i], p["exp_up"][i], p["exp_down"][i])
    return _norm_call("final", x, p["final_norm"], None, t), produced


def kernel(x_prompt, x_sample, state_l0_C, state_l0_n, state_l0_m, cache_l1_k, cache_l1_v, state_l2_S,
           state_l3_C, state_l3_n, state_l3_m, c, c_ctx, ada_w, ada_b, norm1, norm2, router, exp_gate,
           exp_up, exp_down, m_qkvo, m_gate_w, m_gate_b, m_hnorm, m_out, a_qkv, a_qnorm, a_knorm, a_out,
           r_mix, r_rkv, r_w0, r_w1, r_w2, r_a0, r_a1, r_a2, r_g1, r_g2, r_kk, r_ka, r_rk, r_lnx_w,
           r_lnx_b, r_out, final_norm):
    bc, tc, d = x_prompt.shape
    bl, tl, _ = x_sample.shape
    depth = ada_w.shape[0]
    assert bl + 1 <= SUBLANES

    cond8 = jnp.zeros((SUBLANES, d), F32).at[0].set(c_ctx).at[1:1 + bl].set(c)
    mod = _adaln(cond8, ada_w, ada_b).reshape(depth, SUBLANES, 6, d)
    mod_ctx = mod[:, 0:1]
    mod_lat = mod[:, 1:1 + bl]

    w1c, w2p, a1c, a2p = [], [], [], []
    for j in range(r_w1.shape[0]):
        w1, w2 = _pad_lora(r_w1[j], r_w2[j])
        a1, a2 = _pad_lora(r_a1[j], r_a2[j])
        w1c.append(w1), w2p.append(w2), a1c.append(a1), a2p.append(a2)
    p = dict(norm1=norm1, norm2=norm2, router=router, final_norm=final_norm,
             exp_gate=exp_gate.astype(BF16), exp_up=exp_up.astype(BF16), exp_down=exp_down.astype(BF16),
             m_qkvo=m_qkvo.astype(BF16), m_gate_w=m_gate_w, m_gate_b=m_gate_b, m_hnorm=m_hnorm,
             m_out=m_out.astype(BF16), a_qkv=a_qkv.astype(BF16), a_qnorm=a_qnorm, a_knorm=a_knorm,
             a_out=a_out.astype(BF16), r_mix=r_mix, r_rkv=r_rkv.astype(BF16), r_w0=r_w0, r_a0=r_a0,
             r_w1c=w1c, r_w2p=w2p, r_a1c=a1c, r_a2p=a2p, r_g1=r_g1.astype(BF16), r_g2=r_g2.astype(BF16),
             r_kk=r_kk, r_ka=r_ka, r_rk=r_rk, r_lnx_w=r_lnx_w, r_lnx_b=r_lnx_b, r_out=r_out.astype(BF16))

    y_prompt, new_state = _trunk(x_prompt.reshape(bc * tc, d), mod_ctx, bc, tc, None, p)
    lat_states = [(state_l0_C, state_l0_n, state_l0_m), (cache_l1_k, cache_l1_v), (state_l2_S,),
                  (state_l3_C, state_l3_n, state_l3_m)]
    y_sample, _ = _trunk(x_sample.reshape(bl * tl, d), mod_lat, bl, tl, lat_states, p)
    return (y_prompt.reshape(bc, tc, d), y_sample.reshape(bl, tl, d), *new_state)
```

```python
import functools

import jax
import jax.numpy as jnp
from jax import lax
from jax.experimental import pallas as pl
from jax.experimental.pallas import tpu as pltpu

F32 = jnp.float32
BF16 = jnp.bfloat16
I32 = jnp.int32

D_MODEL = 2048
BATCH = 32
SEQ = 256
DEPTH = 4
DEC_BATCH = 4
DEC_SEQ = 4096
PAST_LEN = 256
GRID_W = 64
N_MIXERS = 3
NORM_EPS = 1e-6
M_HEADS = 8
M_DK = D_MODEL // 2 // M_HEADS
M_DV = D_MODEL // M_HEADS
M_GATE_CAP = 15.0
A_HEADS = 16
A_KV_HEADS = 4
A_HD = 128
ROPE_THETA = 10000.0
R_HD = 64
R_LN_EPS = 64e-5
N_EXPERTS = 16
EC_CAPACITY_FACTOR = 2
D_EXPERT = 1024

LANES = 128
SUBLANES = 8
VMEM_LIMIT_BYTES = 56 * 2 ** 20

MLSTM_CHUNK = 256
RWKV_CHUNK = 64
RWKV_PAIRS_PER_GROUP = 16
RWKV_SOLVE_PASSES = 1
ROUTE_BLOCKS = 128
SLAB_CHUNK = 32


def _params(*sem):
    return pltpu.CompilerParams(dimension_semantics=sem, vmem_limit_bytes=VMEM_LIMIT_BYTES)


def _dot(a, b):
    return jnp.dot(a, b, preferred_element_type=F32)


def _dot_nt(a, b):
    return lax.dot_general(a, b, (((1,), (1,)), ((), ())), preferred_element_type=F32)


def _dot_tn(a, b):
    return lax.dot_general(a, b, (((0,), (0,)), ((), ())), preferred_element_type=F32)


def _split2(x):
    hi = x.astype(BF16)
    lo = (x - hi.astype(F32)).astype(BF16)
    return hi, lo


def _split3(x):
    h1 = x.astype(BF16)
    r1 = x - h1.astype(F32)
    h2 = r1.astype(BF16)
    h3 = (r1 - h2.astype(F32)).astype(BF16)
    return h1, h2, h3


def _dot3(a, w):
    ah, al = _split2(a)
    wh, wl = _split2(w)
    return _dot(ah, wh) + _dot(al, wh) + _dot(ah, wl)


_NN = (((1,), (0,)), ((), ()))
_NT = (((1,), (1,)), ((), ()))
_TN = (((0,), (0,)), ((), ()))


def _mm_passes(a, b, dims, passes):
    def f(x, y):
        return lax.dot_general(x, y, dims, preferred_element_type=F32)
    if passes == 1:
        return f(a.astype(BF16), b.astype(BF16))
    ah, al = _split2(a)
    bh, bl = _split2(b)
    return f(ah, bh) + f(al, bh) + f(ah, bl)


def _iota(shape, dim):
    return lax.broadcasted_iota(I32, shape, dim)


def _log_sigmoid(z):
    return jnp.minimum(z, 0.0) - jnp.log(1.0 + jnp.exp(-jnp.abs(z)))


def _adaln_kernel(c_ref, w_ref, b_ref, o_ref):
    c = c_ref[...]
    s = c * jax.nn.sigmoid(c)
    hi, lo = _split2(s)
    w = w_ref[0].astype(BF16)
    o_ref[0] = _dot(hi, w) + _dot(lo, w) + b_ref[0]


def _adaln(cond8, ada_w, ada_b):
    depth, d, d6 = ada_w.shape
    tn = min(d, 1024)
    return pl.pallas_call(
        _adaln_kernel,
        grid=(depth, d6 // tn),
        in_specs=[pl.BlockSpec((SUBLANES, d), lambda l, j: (0, 0)),
                  pl.BlockSpec((1, d, tn), lambda l, j: (l, 0, j)),
                  pl.BlockSpec((1, 1, tn), lambda l, j: (l, 0, j))],
        out_specs=pl.BlockSpec((1, SUBLANES, tn), lambda l, j: (l, 0, j)),
        out_shape=jax.ShapeDtypeStruct((depth, SUBLANES, d6), F32),
        compiler_params=_params("arbitrary", "arbitrary"),
        name="adaln",
    )(cond8, ada_w, ada_b.reshape(depth, 1, d6))


def _normed(x_ref, g_ref):
    x = x_ref[...]
    return x * lax.rsqrt(jnp.mean(x * x, axis=-1, keepdims=True) + NORM_EPS) * g_ref[...]


def _norm_plain_kernel(sh, sc, x_ref, g_ref, mod_ref, o_ref):
    h = _normed(x_ref, g_ref) * (1.0 + mod_ref[0, sc:sc + 1, :]) + mod_ref[0, sh:sh + 1, :]
    o_ref[...] = h.astype(o_ref.dtype)


def _norm_gate_kernel(sh, sc, x_ref, g_ref, mod_ref, w_ref, b_ref, o_ref, gate_ref):
    h = _normed(x_ref, g_ref) * (1.0 + mod_ref[0, sc:sc + 1, :]) + mod_ref[0, sh:sh + 1, :]
    o_ref[...] = h.astype(o_ref.dtype)
    z = _dot3(h, w_ref[...]) + b_ref[...]
    z = M_GATE_CAP * jnp.tanh(z / M_GATE_CAP)
    is_forget = ((_iota(z.shape, 1) // M_HEADS) % 2) == 1
    gate_ref[...] = jnp.where(is_forget, _log_sigmoid(z), z)


def _norm_router_kernel(sh, sc, x_ref, g_ref, mod_ref, w_ref, o_ref, aff_ref):
    h = _normed(x_ref, g_ref) * (1.0 + mod_ref[0, sc:sc + 1, :]) + mod_ref[0, sh:sh + 1, :]
    o_ref[...] = h
    z = _dot3(h, w_ref[...])
    e = jnp.exp(z - jnp.max(z, axis=-1, keepdims=True))
    aff_ref[...] = e / jnp.sum(e, axis=-1, keepdims=True)


def _norm_final_kernel(x_ref, g_ref, o_ref):
    o_ref[...] = _normed(x_ref, g_ref)


def _row_tile(n, t, want):
    tm = min(want, t)
    assert t % tm == 0 and n % tm == 0
    return tm


def _mod_spec(mod, tm, t):
    d = mod.shape[-1]
    if mod.shape[0] == 1:
        return pl.BlockSpec((1, 6, d), lambda i: (0, 0, 0))
    return pl.BlockSpec((1, 6, d), lambda i: ((i * tm) // t, 0, 0))


def _norm_call(kind, x, gain, mod, t, rows=(0, 1), w=None, b=None, out_dtype=F32):
    n, d = x.shape
    tm = _row_tile(n, t, 256)
    xs = pl.BlockSpec((tm, d), lambda i: (i, 0))
    gs = pl.BlockSpec((1, d), lambda i: (0, 0))
    ms = None if mod is None else _mod_spec(mod, tm, t)
    gain = gain.reshape(1, d)
    if kind == "final":
        return pl.pallas_call(_norm_final_kernel, grid=(n // tm,), in_specs=[xs, gs], out_specs=xs,
                              out_shape=jax.ShapeDtypeStruct((n, d), F32),
                              compiler_params=_params("arbitrary"), name="norm_final")(x, gain)
    sh, sc = rows
    if kind == "plain":
        return pl.pallas_call(functools.partial(_norm_plain_kernel, sh, sc), grid=(n // tm,),
                              in_specs=[xs, gs, ms], out_specs=xs,
                              out_shape=jax.ShapeDtypeStruct((n, d), out_dtype),
                              compiler_params=_params("arbitrary"), name="norm_plain")(x, gain, mod)
    ns = w.shape[1]
    ws = pl.BlockSpec((d, ns), lambda i: (0, 0))
    ps = pl.BlockSpec((tm, ns), lambda i: (i, 0))
    if kind == "gate":
        return pl.pallas_call(
            functools.partial(_norm_gate_kernel, sh, sc), grid=(n // tm,),
            in_specs=[xs, gs, ms, ws, pl.BlockSpec((1, ns), lambda i: (0, 0))],
            out_specs=[xs, ps],
            out_shape=[jax.ShapeDtypeStruct((n, d), out_dtype), jax.ShapeDtypeStruct((n, ns), F32)],
            compiler_params=_params("arbitrary"), name="norm_gate")(x, gain, mod, w, b.reshape(1, ns))
    assert kind == "router"
    return pl.pallas_call(
        functools.partial(_norm_router_kernel, sh, sc), grid=(n // tm,),
        in_specs=[xs, gs, ms, ws], out_specs=[xs, ps],
        out_shape=[jax.ShapeDtypeStruct((n, d), F32), jax.ShapeDtypeStruct((n, ns), F32)],
        compiler_params=_params("arbitrary"), name="norm_router")(x, gain, mod, w)


def _mm_kernel(x_ref, w_ref, o_ref):
    o_ref[...] = _dot(x_ref[...], w_ref[...])


def _mm(x, w, tn):
    n, k = x.shape
    m = w.shape[1]
    tm = min(512, n)
    tn = min(tn, m)
    return pl.pallas_call(
        _mm_kernel, grid=(m // tn, n // tm),
        in_specs=[pl.BlockSpec((tm, k), lambda j, i: (i, 0)), pl.BlockSpec((k, tn), lambda j, i: (0, j))],
        out_specs=pl.BlockSpec((tm, tn), lambda j, i: (i, j)),
        out_shape=jax.ShapeDtypeStruct((n, m), F32),
        compiler_params=_params("arbitrary", "arbitrary"), name="mm")(x, w)


def _mlstm_kernel(has_init, emit_state, *refs):
    q_ref, k_ref, v_ref, g_ref, gt_ref = refs[:5]
    pos = 5
    if has_init:
        c0_ref, n0_ref, m0_ref = refs[pos:pos + 3]
        pos += 3
    h_ref = refs[pos]
    pos += 1
    if emit_state:
        cout_ref, nout_ref, mout_ref = refs[pos:pos + 3]
        pos += 3
    c_scr, n_scr, m_scr = refs[pos:pos + 3]

    d = pl.program_id(1)
    c = pl.program_id(2)
    nc = pl.num_programs(2)
    fwd = d == 0
    heads = M_HEADS
    length = q_ref.shape[0]

    @pl.when(c == 0)
    def _():
        if has_init:
            c_scr[...] = c0_ref[0, 0]
            n_scr[...] = n0_ref[0, 0]
            m_scr[...] = m0_ref[0, 0]
        else:
            c_scr[...] = jnp.zeros_like(c_scr)
            n_scr[...] = jnp.zeros_like(n_scr)
            m_scr[...] = jnp.zeros_like(m_scr)

    sgn = jnp.where(fwd, 1, -1)
    row = _iota((length, length), 0)
    col = _iota((length, length), 1)
    mask = ((row - col) * sgn) >= 0
    mask_t = ((col - row) * sgn) >= 0
    scale = M_DK ** -0.5
    g = g_ref[...]
    gt = gt_ref[0]

    for h in range(heads):
        ig_row = jnp.where(fwd, gt[h:h + 1, :], gt[2 * heads + h:2 * heads + h + 1, :])
        lf_row = jnp.where(fwd, gt[heads + h:heads + h + 1, :], gt[3 * heads + h:3 * heads + h + 1, :])
        ig_col = jnp.where(fwd, g[:, h:h + 1], g[:, 2 * heads + h:2 * heads + h + 1])
        lf_col = jnp.where(fwd, g[:, heads + h:heads + h + 1], g[:, 3 * heads + h:3 * heads + h + 1])
        b_col = jnp.sum(jnp.where(mask, lf_row, 0.0), axis=1, keepdims=True)
        b_row = jnp.sum(jnp.where(mask_t, lf_col, 0.0), axis=0, keepdims=True)
        dmat = jnp.where(mask, b_col - b_row + ig_row, -jnp.inf)
        m_prev = m_scr[h:h + 1, 0:1]
        a = b_col + m_prev
        m_t = jnp.maximum(a, jnp.max(dmat, axis=1, keepdims=True))
        w_inter = jnp.exp(a - m_t)
        qh = q_ref[:, h * M_DK:(h + 1) * M_DK]
        kh = k_ref[:, h * M_DK:(h + 1) * M_DK] * scale
        vh = v_ref[:, h * M_DV:(h + 1) * M_DV]
        qb, kb, vb = qh.astype(BF16), kh.astype(BF16), vh.astype(BF16)
        s = _dot_nt(qb, kb) * jnp.exp(dmat - m_t)
        c_prev = c_scr[h]
        n_prev = n_scr[h:h + 1, :]
        num = w_inter * _dot(qb, c_prev.astype(BF16)) + _dot(s.astype(BF16), vb)
        den = w_inter * jnp.sum(qh * n_prev, axis=1, keepdims=True) + jnp.sum(s, axis=1, keepdims=True)
        h_ref[0, :, h * M_DV:(h + 1) * M_DV] = num / jnp.maximum(jnp.abs(den), jnp.exp(-m_t))

        b_last = jnp.sum(lf_col, axis=0, keepdims=True)
        g_col = b_last - b_col + ig_col
        m_new = jnp.maximum(b_last + m_prev, jnp.max(g_col, axis=0, keepdims=True))
        decay = jnp.exp(b_last + m_prev - m_new)
        wk = jnp.exp(g_col - m_new) * kh
        c_scr[h] = decay * c_prev + _dot_tn(wk.astype(BF16), vb)
        n_scr[h:h + 1, :] = decay * n_prev + jnp.sum(wk, axis=0, keepdims=True)
        m_scr[h:h + 1, :] = jnp.broadcast_to(m_new, (1, LANES))

    if emit_state:
        @pl.when(c == nc - 1)
        def _():
            cout_ref[0, 0] = c_scr[...]
            nout_ref[0, 0] = n_scr[...]
            mout_ref[0, 0] = m_scr[...]


def _mlstm_scan(proj, gates, b, t, init, emit_state):
    n = proj.shape[0]
    heads, dk, dv = M_HEADS, M_DK, M_DV
    qk, mv = heads * dk, heads * dv
    length = min(MLSTM_CHUNK, t)
    nc = t // length
    gates_t = gates.reshape(n // length, length, 4 * heads).transpose(0, 2, 1)

    def tile(bi, di, ci):
        return bi * nc + ci + di * (nc - 1 - 2 * ci)

    in_specs = [pl.BlockSpec((length, qk), lambda bi, di, ci: (tile(bi, di, ci), 0)),
                pl.BlockSpec((length, qk), lambda bi, di, ci: (tile(bi, di, ci), 1)),
                pl.BlockSpec((length, mv), lambda bi, di, ci: (tile(bi, di, ci), 1)),
                pl.BlockSpec((length, 4 * heads), lambda bi, di, ci: (tile(bi, di, ci), 0)),
                pl.BlockSpec((1, 4 * heads, length), lambda bi, di, ci: (tile(bi, di, ci), 0, 0))]
    args = [proj, proj, proj, gates, gates_t]
    state_specs = [pl.BlockSpec((1, 1, heads, dk, dv), lambda bi, di, ci: (bi, di, 0, 0, 0)),
                   pl.BlockSpec((1, 1, heads, dk), lambda bi, di, ci: (bi, di, 0, 0)),
                   pl.BlockSpec((1, 1, heads, LANES), lambda bi, di, ci: (bi, di, 0, 0))]
    if init is not None:
        c0, n0, m0 = init
        in_specs += state_specs
        args += [c0, n0, jnp.broadcast_to(m0[..., None], m0.shape + (LANES,))]
    out_specs = [pl.BlockSpec((1, length, mv), lambda bi, di, ci: (di, tile(bi, di, ci), 0))]
    out_shape = [jax.ShapeDtypeStruct((2, n, mv), F32)]
    if emit_state:
        out_specs += state_specs
        out_shape += [jax.ShapeDtypeStruct((b, 2, heads, dk, dv), F32),
                      jax.ShapeDtypeStruct((b, 2, heads, dk), F32),
                      jax.ShapeDtypeStruct((b, 2, heads, LANES), F32)]
    return pl.pallas_call(
        functools.partial(_mlstm_kernel, init is not None, emit_state),
        grid=(b, 2, nc), in_specs=in_specs, out_specs=out_specs, out_shape=out_shape,
        scratch_shapes=[pltpu.VMEM((heads, dk, dv), F32), pltpu.VMEM((heads, dk), F32),
                        pltpu.VMEM((heads, LANES), F32)],
        compiler_params=_params("arbitrary", "arbitrary", "arbitrary"), name="mlstm_scan")(*args)


def _mlstm_out_kernel(hs_ref, o_ref, gain_ref, w_ref, x_ref, mod_ref, out_ref):
    hs = hs_ref[0] + hs_ref[1]
    parts = []
    for h in range(M_HEADS):
        seg = hs[:, h * M_DV:(h + 1) * M_DV]
        parts.append(seg * lax.rsqrt(jnp.mean(seg * seg, axis=-1, keepdims=True) + NORM_EPS))
    hn = jnp.concatenate(parts, axis=1) * gain_ref[...]
    lhs = (jax.nn.sigmoid(o_ref[...]) * hn).astype(BF16)
    out_ref[...] = x_ref[...] + mod_ref[0, 2:3, :] * _dot(lhs, w_ref[...])


def _mlstm_out(hs2, proj, gain, w_out, x, mod, t):
    n, d = x.shape
    mv = M_HEADS * M_DV
    tm = _row_tile(n, t, 256)
    return pl.pallas_call(
        _mlstm_out_kernel, grid=(n // tm,),
        in_specs=[pl.BlockSpec((2, tm, mv), lambda i: (0, i, 0)),
                  pl.BlockSpec((tm, mv), lambda i: (i, 2)),
                  pl.BlockSpec((1, mv), lambda i: (0, 0)),
                  pl.BlockSpec((mv, d), lambda i: (0, 0)),
                  pl.BlockSpec((tm, d), lambda i: (i, 0)),
                  _mod_spec(mod, tm, t)],
        out_specs=pl.BlockSpec((tm, d), lambda i: (i, 0)),
        out_shape=jax.ShapeDtypeStruct((n, d), F32),
        compiler_params=_params("arbitrary"), name="mlstm_out")(hs2, proj, gain.reshape(1, mv), w_out, x, mod)


def _attn_qkv_kernel(use_rope, *refs):
    if use_rope:
        x_ref, w_ref, qg_ref, kg_ref, cos_ref, sin_ref, q_ref, k_ref, v_ref = refs
    else:
        x_ref, w_ref, qg_ref, kg_ref, q_ref, k_ref, v_ref = refs
    acc = _dot(x_ref[...], w_ref[...])
    qd, kvd = A_HEADS * A_HD, A_KV_HEADS * A_HD

    def head_norm(xh, gain):
        xh = xh * lax.rsqrt(jnp.mean(xh * xh, axis=-1, keepdims=True) + NORM_EPS) * gain
        if use_rope:
            xh = xh * cos_ref[...] + pltpu.roll(xh, A_HD // 2, axis=1) * sin_ref[...]
        return xh

    for h in range(A_HEADS):
        qh = head_norm(acc[:, h * A_HD:(h + 1) * A_HD], qg_ref[...])
        q_ref[:, h * A_HD:(h + 1) * A_HD] = (qh * (A_HD ** -0.5)).astype(BF16)
    for h in range(A_KV_HEADS):
        k_ref[:, h * A_HD:(h + 1) * A_HD] = head_norm(acc[:, qd + h * A_HD:qd + (h + 1) * A_HD], kg_ref[...])
    v_ref[...] = acc[:, qd + kvd:]


def _attn_qkv(hb, w, qgain, kgain, t, rope):
    n, d = hb.shape
    qd, kvd = A_HEADS * A_HD, A_KV_HEADS * A_HD
    tm = _row_tile(n, t, 256)
    in_specs = [pl.BlockSpec((tm, d), lambda i: (i, 0)),
                pl.BlockSpec((d, qd + 2 * kvd), lambda i: (0, 0)),
                pl.BlockSpec((1, A_HD), lambda i: (0, 0)),
                pl.BlockSpec((1, A_HD), lambda i: (0, 0))]
    args = [hb, w, qgain.reshape(1, A_HD), kgain.reshape(1, A_HD)]
    if rope is not None:
        in_specs += [pl.BlockSpec((tm, A_HD), lambda i: (i % (t // tm), 0))] * 2
        args += list(rope)
    return pl.pallas_call(
        functools.partial(_attn_qkv_kernel, rope is not None), grid=(n // tm,),
        in_specs=in_specs,
        out_specs=[pl.BlockSpec((tm, qd), lambda i: (i, 0)),
                   pl.BlockSpec((tm, kvd), lambda i: (i, 0)),
                   pl.BlockSpec((tm, kvd), lambda i: (i, 0))],
        out_shape=[jax.ShapeDtypeStruct((n, qd), BF16), jax.ShapeDtypeStruct((n, kvd), F32),
                   jax.ShapeDtypeStruct((n, kvd), F32)],
        compiler_params=_params("arbitrary"), name="attn_qkv")(*args)


def _attn_kernel(has_cache, *refs):
    if has_cache:
        q_ref, kn_ref, vn_ref, kc_ref, vc_ref, o_ref, k_scr, v_scr = refs
    else:
        q_ref, kn_ref, vn_ref, o_ref, k_scr, v_scr = refs
    group = A_HEADS // A_KV_HEADS
    tq = q_ref.shape[0]
    t_new = kn_ref.shape[0]
    past = k_scr.shape[0] - t_new

    @pl.when(pl.program_id(2) == 0)
    def _():
        if has_cache:
            k_scr[0:past, :] = kc_ref[...].astype(BF16)
            v_scr[0:past, :] = vc_ref[...].astype(BF16)
        k_scr[past:, :] = kn_ref[...].astype(BF16)
        v_scr[past:, :] = vn_ref[...].astype(BF16)

    q4 = jnp.concatenate([q_ref[:, g * A_HD:(g + 1) * A_HD] for g in range(group)], axis=0)
    s = _dot_nt(q4, k_scr[...])
    p = jnp.exp(s - jnp.max(s, axis=-1, keepdims=True))
    o = _dot(p.astype(BF16), v_scr[...]) / jnp.sum(p, axis=-1, keepdims=True)
    for g in range(group):
        o_ref[:, g * A_HD:(g + 1) * A_HD] = o[g * tq:(g + 1) * tq, :].astype(BF16)


def _attention(q, k, v, b, t, cache):
    n = q.shape[0]
    group = A_HEADS // A_KV_HEADS
    tq = min(64, t)
    nq = t // tq
    past = 0 if cache is None else cache[0].shape[0] // b
    in_specs = [pl.BlockSpec((tq, group * A_HD), lambda bi, hi, qi: (bi * nq + qi, hi)),
                pl.BlockSpec((t, A_HD), lambda bi, hi, qi: (bi, hi)),
                pl.BlockSpec((t, A_HD), lambda bi, hi, qi: (bi, hi))]
    args = [q, k, v]
    if cache is not None:
        in_specs += [pl.BlockSpec((past, A_HD), lambda bi, hi, qi: (bi, hi))] * 2
        args += list(cache)
    return pl.pallas_call(
        functools.partial(_attn_kernel, cache is not None), grid=(b, A_KV_HEADS, nq),
        in_specs=in_specs,
        out_specs=pl.BlockSpec((tq, group * A_HD), lambda bi, hi, qi: (bi * nq + qi, hi)),
        out_shape=jax.ShapeDtypeStruct((n, A_HEADS * A_HD), BF16),
        scratch_shapes=[pltpu.VMEM((past + t, A_HD), BF16), pltpu.VMEM((past + t, A_HD), BF16)],
        compiler_params=_params("arbitrary", "arbitrary", "arbitrary"), name="attention")(*args)


def _resid_mm_kernel(lhs_ref, w_ref, x_ref, mod_ref, out_ref):
    out_ref[...] = x_ref[...] + mod_ref[0, 2:3, :] * _dot(lhs_ref[...], w_ref[...])


def _resid_mm(lhs, w, x, mod, t):
    n, d = x.shape
    k = lhs.shape[1]
    tm = _row_tile(n, t, 256)
    return pl.pallas_call(
        _resid_mm_kernel, grid=(n // tm,),
        in_specs=[pl.BlockSpec((tm, k), lambda i: (i, 0)),
                  pl.BlockSpec((k, d), lambda i: (0, 0)),
                  pl.BlockSpec((tm, d), lambda i: (i, 0)),
                  _mod_spec(mod, tm, t)],
        out_specs=pl.BlockSpec((tm, d), lambda i: (i, 0)),
        out_shape=jax.ShapeDtypeStruct((n, d), F32),
        compiler_params=_params("arbitrary"), name="resid_mm")(lhs, w, x, mod)


def _token_shift_delta(h, hp_ref, hn_ref, i, tiles_per_seq):
    tm = h.shape[0]
    rowi = _iota((tm, 1), 0)
    ti = i % tiles_per_seq
    prev_row = jnp.where(ti == 0, 0.0, hp_ref[SUBLANES - 1:SUBLANES, :])
    next_row = jnp.where(ti == tiles_per_seq - 1, 0.0, hn_ref[0:1, :])
    h_prev = jnp.where(rowi == 0, prev_row, pltpu.roll(h, 1, axis=0))
    h_next = jnp.where(rowi == tm - 1, next_row, pltpu.roll(h, tm - 1, axis=0))
    return 0.5 * (h_prev + h_next) - h


def _shift_specs(tm, d, n, grid_rank):
    nb8 = n // SUBLANES
    per = tm // SUBLANES
    if grid_rank == 1:
        return [pl.BlockSpec((tm, d), lambda i: (i, 0)),
                pl.BlockSpec((SUBLANES, d), lambda i: (jnp.maximum(i * per - 1, 0), 0)),
                pl.BlockSpec((SUBLANES, d), lambda i: (jnp.minimum((i + 1) * per, nb8 - 1), 0))]
    return [pl.BlockSpec((tm, d), lambda j, i: (i, 0)),
            pl.BlockSpec((SUBLANES, d), lambda j, i: (jnp.maximum(i * per - 1, 0), 0)),
            pl.BlockSpec((SUBLANES, d), lambda j, i: (jnp.minimum((i + 1) * per, nb8 - 1), 0))]


def _rwkv_rkv_kernel(tiles_per_seq, h_ref, hp_ref, hn_ref, mix_ref, w_ref, o_ref):
    h = h_ref[...]
    xx = _token_shift_delta(h, hp_ref, hn_ref, pl.program_id(1), tiles_per_seq)
    o_ref[0] = _dot((h + xx * mix_ref[0]).astype(BF16), w_ref[0])


def _rwkv_rkv(h, mix3, w_rkv, t):
    n, d = h.shape
    tm = _row_tile(n, t, 256)
    return pl.pallas_call(
        functools.partial(_rwkv_rkv_kernel, t // tm), grid=(3, n // tm),
        in_specs=_shift_specs(tm, d, n, 2) + [pl.BlockSpec((1, 1, d), lambda j, i: (j, 0, 0)),
                                              pl.BlockSpec((1, d, d), lambda j, i: (j, 0, 0))],
        out_specs=pl.BlockSpec((1, tm, d), lambda j, i: (j, i, 0)),
        out_shape=jax.ShapeDtypeStruct((3, n, d), F32),
        compiler_params=_params("arbitrary", "arbitrary"), name="rwkv_rkv")(h, h, h, mix3, w_rkv)


def _rwkv_lora_kernel(tiles_per_seq, lora, h_ref, hp_ref, hn_ref, mix_ref, w1_ref, a1_ref, g1_ref,
                      w2_ref, a2_ref, g2_ref, w0_ref, a0_ref, lw_ref, asig_ref, g_ref):
    h = h_ref[...]
    xx = _token_shift_delta(h, hp_ref, hn_ref, pl.program_id(0), tiles_per_seq)
    xw = (h + xx * mix_ref[0:1, :]).astype(BF16)
    xa = (h + xx * mix_ref[1:2, :]).astype(BF16)
    xg = (h + xx * mix_ref[2:3, :]).astype(BF16)
    tw = jnp.tanh(_dot(xw, w1_ref[...])).astype(BF16)
    ta = _dot(xa, a1_ref[...]).astype(BF16)
    tg = jax.nn.sigmoid(_dot(xg, g1_ref[...])).astype(BF16)
    g_ref[...] = _dot(tg, g2_ref[...])
    for dr in range(2):
        z = -(w0_ref[dr] + _dot(tw[:, dr * lora:(dr + 1) * lora], w2_ref[dr]))
        softplus = jnp.maximum(z, 0.0) + jnp.log(1.0 + jnp.exp(-jnp.abs(z)))
        lw_ref[dr] = -jnp.exp(-softplus - 0.5)
        asig_ref[dr] = jax.nn.sigmoid(a0_ref[dr] + _dot(ta[:, dr * lora:(dr + 1) * lora], a2_ref[dr]))


def _rwkv_lora(h, mix3, w1c, a1c, g1, w2p, a2p, g2, w0, a0, t):
    n, d = h.shape
    tm = _row_tile(n, t, 256)
    lora = w2p.shape[1]
    gl = g1.shape[1]
    full2 = lambda shape: pl.BlockSpec(shape, lambda i: (0, 0))
    full3 = lambda shape: pl.BlockSpec(shape, lambda i: (0, 0, 0))
    return pl.pallas_call(
        functools.partial(_rwkv_lora_kernel, t // tm, lora), grid=(n // tm,),
        in_specs=_shift_specs(tm, d, n, 1) + [
            full2((3, d)), full2((d, 2 * lora)), full2((d, 2 * lora)), full2((d, gl)),
            full3((2, lora, d)), full3((2, lora, d)), full2((gl, d)), full3((2, 1, d)), full3((2, 1, d))],
        out_specs=[pl.BlockSpec((2, tm, d), lambda i: (0, i, 0)),
                   pl.BlockSpec((2, tm, d), lambda i: (0, i, 0)),
                   pl.BlockSpec((tm, d), lambda i: (i, 0))],
        out_shape=[jax.ShapeDtypeStruct((2, n, d), F32), jax.ShapeDtypeStruct((2, n, d), F32),
                   jax.ShapeDtypeStruct((n, d), F32)],
        compiler_params=_params("arbitrary"), name="rwkv_lora")(h, h, h, mix3, w1c, a1c, g1, w2p, a2p, g2, w0, a0)


def _rwkv_scan_kernel(has_init, emit_state, *refs):
    r_ref, k_ref, v_ref, lw_ref, a_ref, kk_ref, ka_ref, rk_ref = refs[:8]
    pos = 8
    if has_init:
        s0_ref = refs[pos]
        pos += 1
    y_ref, bonus_ref = refs[pos:pos + 2]
    pos += 2
    if emit_state:
        sout_ref = refs[pos]
        pos += 1
    s_scr, cum_scr = refs[pos:pos + 2]

    d = pl.program_id(1)
    c = pl.program_id(2)
    nc = pl.num_programs(2)
    fwd = d == 0
    sgn = jnp.where(fwd, 1, -1)
    length = r_ref.shape[1]
    npairs = s_scr.shape[0]
    hd = R_HD

    @pl.when(c == 0)
    def _():
        if has_init:
            s_scr[...] = s0_ref[0, 0]
        else:
            s_scr[...] = jnp.zeros_like(s_scr)

    ti = _iota((length, length), 0)
    si = _iota((length, length), 1)
    tri = (((ti - si) * sgn) >= 0).astype(BF16)
    l1, l2, l3 = _split3(lw_ref[0])
    cum_scr[...] = _dot(tri, l1) + _dot(tri, l2) + _dot(tri, l3)

    gl = 2 * length
    rr = _iota((gl, gl), 0)
    cc = _iota((gl, gl), 1)
    same = (rr // length) == (cc // length)
    e = ((rr % length) - (cc % length)) * sgn
    strict = same & (e > 0)
    incl = same & (e >= 0)
    eye = (rr == cc).astype(F32)
    lane = _iota((1, LANES), 1)
    m0 = (lane < hd).astype(F32)
    m1 = 1.0 - m0
    lr = _iota((LANES, LANES), 0)
    lc = _iota((LANES, LANES), 1)
    ones_blk = ((lr // hd) == (lc // hd)).astype(BF16)

    def segsum(x):
        hi, lo = _split2(x)
        return _dot(hi, ones_blk) + _dot(lo, ones_blk)

    def stack(x):
        return jnp.concatenate([x * m0, x * m1], axis=0)

    def smm(x, y, dims):
        return _mm_passes(x, y, dims, RWKV_SOLVE_PASSES)

    tr = rr % length
    tc = cc % length
    base = 4
    base_mask = (tr // base) == (tc // base)
    level_masks = []
    size = base
    while size < length:
        level_masks.append(((tr // (2 * size)) == (tc // (2 * size))) & ((tr // size) != (tc // size)))
        size *= 2

    def load_pair(p):
        ds = pl.ds(pl.multiple_of(p * LANES, LANES), LANES)
        return dict(ds=ds, p=p, r=r_ref[0, :, ds], k=k_ref[0, :, ds], v=v_ref[0, :, ds], lw=lw_ref[0, :, ds],
                    a=a_ref[0, :, ds], cum=cum_scr[:, ds], kkp=kk_ref[:, ds], kap=ka_ref[:, ds],
                    rkp=rk_ref[:, ds], st=s_scr[p])

    def each(fn, *lists):
        return [fn(*args) for args in zip(*lists)]

    def solve_group(xs):
        get = lambda name: [x[name] for x in xs]
        r, k, v, lw, a, cum, st = (get(n) for n in ("r", "k", "v", "lw", "a", "cum", "st"))
        kk = each(lambda k_, p_: k_ * p_, k, get("kkp"))
        ss = each(lambda q: segsum(q * q), kk)
        kk = each(lambda q, s_: q * lax.rsqrt(jnp.maximum(s_, 1e-24)), kk, ss)
        kd = each(lambda k_, a_, p_: k_ * (1.0 + (a_ - 1.0) * p_), k, a, get("kap"))
        bb = each(lambda q, a_: q * a_, kk, a)
        tot = each(lambda c_: jnp.where(fwd, c_[length - 1:length, :], c_[0:1, :]), cum)
        e_n = each(lambda c_: jnp.exp(-c_), cum)
        sa = each(lambda q, c_, l_: stack(-q * jnp.exp(c_ - l_)), kk, cum, lw)
        sr_full = each(lambda r_, c_: r_ * jnp.exp(c_), r, cum)
        sr = each(stack, sr_full)
        sb = each(lambda b_, e_: stack(b_ * e_), bb, e_n)
        sk = each(lambda k_, e_: stack(k_ * e_), kd, e_n)
        vexp = each(stack, v)
        m_ab = each(lambda x_, y_: jnp.where(strict, smm(x_, y_, _NT), 0.0), sa, sb)
        m_ak = each(lambda x_, y_: jnp.where(strict, smm(x_, y_, _NT), 0.0), sa, sk)
        rhs1 = each(lambda x_, s_: smm(x_, s_, _NT), sa, st)
        rhs2 = each(lambda m_, v_: smm(m_, v_, _NN), m_ak, vexp)
        rhs = each(lambda x_, y_: x_ + y_, rhs1, rhs2)
        d0 = each(lambda m_: jnp.where(base_mask, m_, 0.0), m_ab)
        sq = each(lambda d_: smm(d_, d_, _NN), d0)
        inv = each(lambda d_: eye + d_, d0)
        inv = each(lambda i_, q_: i_ + smm(i_, q_, _NN), inv, sq)
        for lm in level_masks:
            half = each(lambda i_, m_: smm(i_, jnp.where(lm, m_, 0.0), _NN), inv, m_ab)
            inv = each(lambda i_, h_: i_ + smm(h_, i_, _NN), inv, half)
        uexp = each(lambda i_, r_: smm(i_, r_, _NN), inv, rhs)
        t_rb = each(lambda x_, y_: jnp.where(incl, smm(x_, y_, _NT), 0.0), sr, sb)
        t_rk = each(lambda x_, y_: jnp.where(incl, smm(x_, y_, _NT), 0.0), sr, sk)
        fold = lambda t_: t_[0:length] + t_[length:gl]
        y1 = each(lambda r_, s_: smm(r_, s_, _NT), sr_full, st)
        y2 = each(lambda t_, u_: smm(fold(t_), u_, _NN), t_rb, uexp)
        y3 = each(lambda t_, v_: smm(fold(t_), v_, _NN), t_rk, vexp)
        y = each(lambda p_, q_, w_: p_ + q_ + w_, y1, y2, y3)
        bs = each(lambda r_, k_, p_: segsum(r_ * k_ * p_), r, kd, get("rkp"))
        bonus = each(lambda s_, v_: s_ * v_, bs, v)
        e_e = each(lambda t_, c_: jnp.exp(t_ - c_), tot, cum)
        s1 = each(lambda u_, b_, e_: smm(u_, stack(b_ * e_), _TN), uexp, bb, e_e)
        s2 = each(lambda v_, k_, e_: smm(v_, stack(k_ * e_), _TN), vexp, kd, e_e)
        s_new = each(lambda s_, t_, p_, q_: s_ * jnp.exp(t_) + p_ + q_, st, tot, s1, s2)
        return y, bonus, s_new

    npg = min(RWKV_PAIRS_PER_GROUP, npairs)

    def group(gi, carry):
        xs = [load_pair(gi * npg + u) for u in range(npg)]
        ys, bonuses, states = solve_group(xs)
        for x, y, bonus, s_new in zip(xs, ys, bonuses, states):
            y_ref[0, :, x["ds"]] = y
            bonus_ref[0, :, x["ds"]] = bonus
            s_scr[x["p"]] = s_new
        return carry

    lax.fori_loop(0, npairs // npg, group, 0)

    if emit_state:
        @pl.when(c == nc - 1)
        def _():
            sout_ref[0, 0] = s_scr[...]


def _rwkv_scan(rkv, lw, asig, kk, ka, rk, b, t, s0, emit_state):
    n, d = rkv.shape[1:]
    length = RWKV_CHUNK
    nc = t // length
    npairs = d // LANES

    def tile(bi, di, ci):
        return bi * nc + ci + di * (nc - 1 - 2 * ci)

    def plane(j):
        return pl.BlockSpec((1, length, d), lambda bi, di, ci: (j, tile(bi, di, ci), 0))

    dirspec = pl.BlockSpec((1, length, d), lambda bi, di, ci: (di, tile(bi, di, ci), 0))
    vec = pl.BlockSpec((1, d), lambda bi, di, ci: (0, 0))
    sspec = pl.BlockSpec((1, 1, npairs, LANES, LANES), lambda bi, di, ci: (bi, di, 0, 0, 0))
    in_specs = [plane(0), plane(1), plane(2), dirspec, dirspec, vec, vec, vec]
    args = [rkv, rkv, rkv, lw, asig, kk.reshape(1, d), ka.reshape(1, d), rk.reshape(1, d)]
    if s0 is not None:
        in_specs.append(sspec)
        args.append(s0)
    out_specs = [dirspec, dirspec]
    out_shape = [jax.ShapeDtypeStruct((2, n, d), F32), jax.ShapeDtypeStruct((2, n, d), F32)]
    if emit_state:
        out_specs.append(sspec)
        out_shape.append(jax.ShapeDtypeStruct((b, 2, npairs, LANES, LANES), F32))
    return pl.pallas_call(
        functools.partial(_rwkv_scan_kernel, s0 is not None, emit_state),
        grid=(b, 2, nc), in_specs=in_specs, out_specs=out_specs, out_shape=out_shape,
        scratch_shapes=[pltpu.VMEM((npairs, LANES, LANES), F32), pltpu.VMEM((length, d), F32)],
        compiler_params=_params("arbitrary", "arbitrary", "arbitrary"), name="rwkv_scan")(*args)


def _rwkv_out_kernel(y_ref, bonus_ref, g_ref, lnw_ref, lnb_ref, w_ref, x_ref, mod_ref, out_ref):
    y = y_ref[0] + y_ref[1]
    lr = _iota((LANES, LANES), 0)
    lc = _iota((LANES, LANES), 1)
    mean_blk = ((lr // R_HD) == (lc // R_HD)).astype(BF16)

    def segmean(x):
        hi, lo = _split2(x)
        return (_dot(hi, mean_blk) + _dot(lo, mean_blk)) * (1.0 / R_HD)

    parts = []
    for j in range(y.shape[1] // LANES):
        yj = y[:, j * LANES:(j + 1) * LANES]
        cen = yj - segmean(yj)
        parts.append(cen * lax.rsqrt(segmean(cen * cen) + R_LN_EPS))
    yn = jnp.concatenate(parts, axis=1) * lnw_ref[...] + lnb_ref[...] + bonus_ref[0] + bonus_ref[1]
    lhs = (yn * g_ref[...]).astype(BF16)
    out_ref[...] = x_ref[...] + mod_ref[0, 2:3, :] * _dot(lhs, w_ref[...])


def _rwkv_out(y2, bonus2, g, lnw, lnb, w_out, x, mod, t):
    n, d = x.shape
    tm = _row_tile(n, t, 256)
    two = pl.BlockSpec((2, tm, d), lambda i: (0, i, 0))
    rows = pl.BlockSpec((tm, d), lambda i: (i, 0))
    vec = pl.BlockSpec((1, d), lambda i: (0, 0))
    return pl.pallas_call(
        _rwkv_out_kernel, grid=(n // tm,),
        in_specs=[two, two, rows, vec, vec, pl.BlockSpec((d, d), lambda i: (0, 0)), rows,
                  _mod_spec(mod, tm, t)],
        out_specs=rows, out_shape=jax.ShapeDtypeStruct((n, d), F32),
        compiler_params=_params("arbitrary"), name="rwkv_out")(
            y2, bonus2, g, lnw.reshape(1, d), lnb.reshape(1, d), w_out, x, mod)


def _route_kernel(cap, aff_ref, idx_ref, gate_ref, pos_ref, tot_ref, off_ref):
    aff = aff_ref[0]
    nb = aff.shape[0]
    bits = lax.bitcast_convert_type(aff, I32)
    tok = _iota(aff.shape, 0) * LANES + _iota(aff.shape, 1)

    def count(m):
        x = jnp.sum(m.astype(F32), axis=1, keepdims=True)
        return jnp.sum(x, axis=0, keepdims=True)

    def value_step(i, cur):
        cand = cur | (jnp.int32(1) << (30 - i))
        return jnp.where(count(bits >= cand) >= cap, cand, cur)

    thr = lax.fori_loop(0, 31, value_step, jnp.zeros((1, 1), I32))
    above = bits > thr
    equal = bits == thr
    need = cap - count(above)

    def index_step(i, cur):
        cand = cur + (jnp.int32(1) << (14 - i))
        return jnp.where(count(equal & (tok < cand)) < need, cand, cur)

    last_tie = lax.fori_loop(0, 15, index_step, jnp.zeros((1, 1), I32))
    sel = above | (equal & (tok <= last_tie))
    self32 = sel.astype(F32)
    selb = self32.astype(BF16)

    li = _iota((LANES, LANES), 0)
    lj = _iota((LANES, LANES), 1)
    upper_incl = (li <= lj).astype(BF16)
    cl = _dot(selb, upper_incl)
    ones8 = jnp.ones((SUBLANES, LANES), BF16)
    tot_row = _dot_nt(ones8, selb)
    bi = _iota((nb, nb), 0)
    bj = _iota((nb, nb), 1)
    offi_row = _dot(tot_row.astype(BF16), (bi <= bj).astype(BF16))
    offx_row = offi_row - tot_row
    tot_col = jnp.broadcast_to(cl[:, LANES - 1:LANES], (nb, LANES)).astype(BF16)
    offx_col = _dot((bj < bi).astype(BF16), tot_col)
    pos_ref[0] = jnp.where(sel, offx_col + cl - 1.0, -1.0).astype(I32)
    tot_ref[0] = tot_row
    off_ref[0] = offx_row

    pcol = _iota((cap, 1), 0).astype(F32)
    before = offi_row[0:1, :] <= pcol
    blk = jnp.sum(before.astype(F32), axis=1, keepdims=True)
    base = jnp.sum(jnp.where(before, tot_row[0:1, :], 0.0), axis=1, keepdims=True)
    onehot = (_iota((cap, nb), 1).astype(F32) == blk).astype(BF16)
    rowcnt = _dot(onehot, cl.astype(BF16))
    rank = pcol - base
    lane = jnp.sum((rowcnt <= rank).astype(F32), axis=1, keepdims=True)
    idx_ref[0] = (blk * LANES + lane).astype(I32)
    a1, a2, a3 = _split3(aff)
    rowaff = _dot(onehot, a1) + _dot(onehot, a2) + _dot(onehot, a3)
    gate_ref[0] = jnp.sum(jnp.where(_iota((cap, LANES), 1).astype(F32) == lane, rowaff, 0.0),
                          axis=1, keepdims=True)


def _route(aff3, cap):
    e, nb, _ = aff3.shape
    return pl.pallas_call(
        functools.partial(_route_kernel, cap), grid=(e,),
        in_specs=[pl.BlockSpec((1, nb, LANES), lambda i: (i, 0, 0))],
        out_specs=[pl.BlockSpec((1, cap, 1), lambda i: (i, 0, 0)),
                   pl.BlockSpec((1, cap, 1), lambda i: (i, 0, 0)),
                   pl.BlockSpec((1, nb, LANES), lambda i: (i, 0, 0)),
                   pl.BlockSpec((1, SUBLANES, nb), lambda i: (i, 0, 0)),
                   pl.BlockSpec((1, SUBLANES, nb), lambda i: (i, 0, 0))],
        out_shape=[jax.ShapeDtypeStruct((e, cap, 1), I32), jax.ShapeDtypeStruct((e, cap, 1), F32),
                   jax.ShapeDtypeStruct((e, nb, LANES), I32),
                   jax.ShapeDtypeStruct((e, SUBLANES, nb), F32), jax.ShapeDtypeStruct((e, SUBLANES, nb), F32)],
        compiler_params=_params("arbitrary"), name="moe_route")(aff3)


def _row_copy(x_hbm, xbuf, sem, slot, tok, r):
    return pltpu.make_async_copy(x_hbm.at[pl.ds(tok, 1)], xbuf.at[pl.ds(r, 1)], sem.at[slot])


def _expert_kernel(idx_ref, idx_next_ref, x_hbm, gate_ref, wg_ref, wu_ref, wd_ref, o_ref, xbuf0, xbuf1, sem):
    tm = xbuf0.shape[0]
    bufs = (xbuf0, xbuf1)
    step = pl.program_id(0) * pl.num_programs(1) + pl.program_id(1)
    last = pl.num_programs(0) * pl.num_programs(1) - 1

    def start_rows(ids_ref, slot):
        for r in range(tm):
            _row_copy(x_hbm, bufs[slot], sem, slot, ids_ref[0, 0, r], r).start()

    def wait_rows(slot):
        for r in range(tm):
            _row_copy(x_hbm, bufs[slot], sem, slot, 0, r).wait()

    @pl.when(step == 0)
    def _():
        start_rows(idx_ref, 0)

    def run(slot):
        start_rows(idx_next_ref, 1 - slot)
        wait_rows(slot)
        xb = bufs[slot][...].astype(BF16)
        hg = _dot(xb, wg_ref[0])
        hu = _dot(xb, wu_ref[0])
        hid = (hg * jax.nn.sigmoid(hg) * hu).astype(BF16)
        o_ref[...] = _dot(hid, wd_ref[0]) * gate_ref[...]

        @pl.when(step == last)
        def _():
            wait_rows(1 - slot)

    for slot in range(2):
        @pl.when(step % 2 == slot)
        def _():
            run(slot)


def _experts(hffn, idx, gate, wg, wu, wd, cap):
    n, d = hffn.shape
    e, _, f = wg.shape
    tm = min(256, cap)
    nt = cap // tm
    idx3 = idx.reshape(e * nt, 1, tm)
    last = e * nt - 1
    return pl.pallas_call(
        _expert_kernel, grid=(e, nt),
        in_specs=[pl.BlockSpec((1, 1, tm), lambda ei, ti: (ei * nt + ti, 0, 0), memory_space=pltpu.SMEM),
                  pl.BlockSpec((1, 1, tm), lambda ei, ti: (jnp.minimum(ei * nt + ti + 1, last), 0, 0),
                               memory_space=pltpu.SMEM),
                  pl.BlockSpec(memory_space=pl.ANY),
                  pl.BlockSpec((tm, 1), lambda ei, ti: (ei * nt + ti, 0)),
                  pl.BlockSpec((1, d, f), lambda ei, ti: (ei, 0, 0)),
                  pl.BlockSpec((1, d, f), lambda ei, ti: (ei, 0, 0)),
                  pl.BlockSpec((1, f, d), lambda ei, ti: (ei, 0, 0))],
        out_specs=pl.BlockSpec((tm, d), lambda ei, ti: (ei * nt + ti, 0)),
        out_shape=jax.ShapeDtypeStruct((e * cap, d), F32),
        scratch_shapes=[pltpu.VMEM((tm, d), F32), pltpu.VMEM((tm, d), F32), pltpu.SemaphoreType.DMA((2,))],
        compiler_params=_params("arbitrary", "arbitrary"), name="moe_experts")(
            idx3, idx3, hffn, gate.reshape(e * cap, 1), wg, wu, wd)


def _slab_copy(ye_hbm, slab, sem, e, row, ch):
    return pltpu.make_async_copy(ye_hbm.at[pl.ds(row + ch * SLAB_CHUNK, SLAB_CHUNK)],
                                 slab.at[e, pl.ds(ch * SLAB_CHUNK, SLAB_CHUNK)], sem.at[e])


def _combine_kernel(cap, meta_ref, x_ref, mod_ref, pos_ref, ye_hbm, out_ref, slab, sem):
    n_exp = slab.shape[0]
    rows = slab.shape[1]
    nch = rows // SLAB_CHUNK
    tm = x_ref.shape[0]

    @pl.when(pl.program_id(0) == 0)
    def _():
        slab[...] = jnp.zeros_like(slab)

    for e in range(n_exp):
        row = pl.multiple_of(meta_ref[0, 0, e], SUBLANES)
        for ch in range(nch):
            @pl.when(ch < meta_ref[0, 0, n_exp + e])
            def _():
                _slab_copy(ye_hbm, slab, sem, e, row, ch).start()
    for e in range(n_exp):
        for ch in range(nch):
            @pl.when(ch < meta_ref[0, 0, n_exp + e])
            def _():
                _slab_copy(ye_hbm, slab, sem, e, 0, ch).wait()

    acc = jnp.zeros(x_ref.shape, F32)
    lane = _iota((tm, rows), 1)
    for e in range(n_exp):
        local = pos_ref[:, e:e + 1] - (meta_ref[0, 0, e] - e * cap)
        place = (lane == local).astype(BF16)
        acc = acc + _dot(place, slab[e].astype(BF16))
    out_ref[...] = x_ref[...] + mod_ref[0, 5:6, :] * acc


def _combine(x, mod, pos_t, meta, ye, cap, t, rows):
    n, d = x.shape
    tm = LANES
    n_exp = N_EXPERTS
    return pl.pallas_call(
        functools.partial(_combine_kernel, cap), grid=(n // tm,),
        in_specs=[pl.BlockSpec((1, 1, 2 * n_exp), lambda i: (i, 0, 0), memory_space=pltpu.SMEM),
                  pl.BlockSpec((tm, d), lambda i: (i, 0)),
                  _mod_spec(mod, tm, t),
                  pl.BlockSpec((tm, n_exp), lambda i: (i, 0)),
                  pl.BlockSpec(memory_space=pl.ANY)],
        out_specs=pl.BlockSpec((tm, d), lambda i: (i, 0)),
        out_shape=jax.ShapeDtypeStruct((n, d), F32),
        scratch_shapes=[pltpu.VMEM((n_exp, rows, d), F32), pltpu.SemaphoreType.DMA((n_exp,))],
        compiler_params=_params("arbitrary"), name="moe_combine")(meta, x, mod, pos_t, ye)


def _moe(x, mod, t, norm2, w_router, wg, wu, wd):
    n, d = x.shape
    n_exp = N_EXPERTS
    cap = (EC_CAPACITY_FACTOR * n) // n_exp
    hffn, aff = _norm_call("router", x, norm2, mod, t, rows=(3, 4), w=w_router)
    npad = ROUTE_BLOCKS * LANES
    assert n <= npad and n % LANES == 0 and cap % SLAB_CHUNK == 0
    aff_t = jnp.pad(aff.T, ((0, 0), (0, npad - n)), constant_values=-1.0)
    idx, gate, pos, tot, off = _route(aff_t.reshape(n_exp, ROUTE_BLOCKS, LANES), cap)
    ye = _experts(hffn, idx, gate, wg, wu, wd, cap)
    ntiles = n // LANES
    rows = min(LANES + SUBLANES + SLAB_CHUNK - (LANES + SUBLANES) % SLAB_CHUNK, cap)
    start = off[:, 0, :ntiles].T.astype(I32)
    cnt = tot[:, 0, :ntiles].T.astype(I32)
    start_al = jnp.minimum((start // SUBLANES) * SUBLANES, cap - rows)
    nch = jnp.where(cnt > 0, (start - start_al + cnt + SLAB_CHUNK - 1) // SLAB_CHUNK, 0)
    meta = jnp.concatenate([start_al + jnp.arange(n_exp, dtype=I32)[None, :] * cap, nch], axis=1)
    pos_t = pos.reshape(n_exp, npad)[:, :n].T
    return _combine(x, mod, pos_t, meta.reshape(ntiles, 1, 2 * n_exp), ye, cap, t, rows)


def _rope_tables(t):
    n_rows = t // GRID_W
    axis_dim = A_HD // 2
    row = jnp.repeat(jnp.arange(n_rows, dtype=F32), GRID_W)
    col = jnp.tile(jnp.arange(GRID_W, dtype=F32), n_rows)
    inv_freq = ROPE_THETA ** (-jnp.arange(0, axis_dim, 2, dtype=F32) / axis_dim)
    ang = jnp.concatenate([row[:, None] * inv_freq, col[:, None] * inv_freq], axis=-1)
    cos, sin = jnp.cos(ang), jnp.sin(ang)
    return jnp.concatenate([cos, cos], axis=-1), jnp.concatenate([-sin, sin], axis=-1)


def _pad_lora(w1, w2):
    r = w1.shape[2]
    rp = -(-r // LANES) * LANES
    w1p = jnp.pad(w1, ((0, 0), (0, 0), (0, rp - r)))
    w2p = jnp.pad(w2, ((0, 0), (0, rp - r), (0, 0)))
    return jnp.concatenate([w1p[0], w1p[1]], axis=1).astype(BF16), w2p.astype(BF16)


def _pair_state(s):
    b, _, heads, hd, _ = s.shape
    sp = s.reshape(b, 2, heads // 2, 2, hd, hd)
    out = jnp.zeros((b, 2, heads // 2, 2 * hd, 2 * hd), F32)
    out = out.at[:, :, :, :hd, :hd].set(sp[:, :, :, 0])
    return out.at[:, :, :, hd:, hd:].set(sp[:, :, :, 1])


def _unpair_state(sp):
    b, _, npairs, _, _ = sp.shape
    hd = R_HD
    s = jnp.stack([sp[:, :, :, :hd, :hd], sp[:, :, :, hd:, hd:]], axis=3)
    return s.reshape(b, 2, 2 * npairs, hd, hd)


def _trunk(x, mod_all, b, t, latent_states, p):
    n, d = x.shape
    latent = latent_states is not None
    produced = []
    for i in range(DEPTH):
        kind, j = i % N_MIXERS, i // N_MIXERS
        mod = mod_all[i]
        if kind == 0:
            hb, gates = _norm_call("gate", x, p["norm1"][i], mod, t, w=p["m_gate_w"][j], b=p["m_gate_b"][j],
                                   out_dtype=BF16)
            proj = _mm(hb, p["m_qkvo"][j], 2048)
            res = _mlstm_scan(proj, gates, b, t, latent_states[i] if latent else None, not latent)
            if not latent:
                produced += [res[1], res[2], res[3][..., 0]]
            x = _mlstm_out(res[0], proj, p["m_hnorm"][j], p["m_out"][j], x, mod, t)
        elif kind == 1:
            hb = _norm_call("plain", x, p["norm1"][i], mod, t, out_dtype=BF16)
            q, k, v = _attn_qkv(hb, p["a_qkv"][j], p["a_qnorm"][j], p["a_knorm"][j], t,
                                _rope_tables(t) if latent else None)
            cache = None
            if latent:
                ck, cv = latent_states[i]
                kvd = A_KV_HEADS * A_HD
                cache = (ck.reshape(-1, kvd), cv.reshape(-1, kvd))
            else:
                produced += [k.reshape(b, t, A_KV_HEADS, A_HD), v.reshape(b, t, A_KV_HEADS, A_HD)]
            o = _attention(q, k, v, b, t, cache)
            x = _resid_mm(o, p["a_out"][j], x, mod, t)
        else:
            h = _norm_call("plain", x, p["norm1"][i], mod, t)
            mix = p["r_mix"][j]
            rkv = _rwkv_rkv(h, mix[jnp.array([0, 2, 3])].reshape(3, 1, d), p["r_rkv"][j], t)
            lw, asig, g = _rwkv_lora(h, mix[jnp.array([1, 4, 5])], p["r_w1c"][j], p["r_a1c"][j], p["r_g1"][j],
                                     p["r_w2p"][j], p["r_a2p"][j], p["r_g2"][j],
                                     p["r_w0"][j].reshape(2, 1, d), p["r_a0"][j].reshape(2, 1, d), t)
            s0 = _pair_state(latent_states[i][0]) if latent else None
            res = _rwkv_scan(rkv, lw, asig, p["r_kk"][j], p["r_ka"][j], p["r_rk"][j].reshape(-1), b, t, s0,
                             not latent)
            if not latent:
                produced.append(_unpair_state(res[2]))
            x = _rwkv_out(res[0], res[1], g, p["r_lnx_w"][j], p["r_lnx_b"][j], p["r_out"][j], x, mod, t)
        x = _moe(x, mod, t, p["norm2"][i], p["router"][i], p["exp_gate"][i], p["exp_up"][i], p["exp_down"][i])
    return _norm_call("final", x, p["final_norm"], None, t), produced


def kernel(x_prompt, x_sample, state_l0_C, state_l0_n, state_l0_m, cache_l1_k, cache_l1_v, state_l2_S,
           state_l3_C, state_l3_n, state_l3_m, c, c_ctx, ada_w, ada_b, norm1, norm2, router, exp_gate,
           exp_up, exp_down, m_qkvo, m_gate_w, m_gate_b, m_hnorm, m_out, a_qkv, a_qnorm, a_knorm, a_out,
           r_mix, r_rkv, r_w0, r_w1, r_w2, r_a0, r_a1, r_a2, r_g1, r_g2, r_kk, r_ka, r_rk, r_lnx_w,
           r_lnx_b, r_out, final_norm):
    bc, tc, d = x_prompt.shape
    bl, tl, _ = x_sample.shape
    depth = ada_w.shape[0]
    assert bl + 1 <= SUBLANES

    cond8 = jnp.zeros((SUBLANES, d), F32).at[0].set(c_ctx).at[1:1 + bl].set(c)
    mod = _adaln(cond8, ada_w, ada_b).reshape(depth, SUBLANES, 6, d)
    mod_ctx = mod[:, 0:1]
    mod_lat = mod[:, 1:1 + bl]

    w1c, w2p, a1c, a2p = [], [], [], []
    for j in range(r_w1.shape[0]):
        w1, w2 = _pad_lora(r_w1[j], r_w2[j])
        a1, a2 = _pad_lora(r_a1[j], r_a2[j])
        w1c.append(w1), w2p.append(w2), a1c.append(a1), a2p.append(a2)
    p = dict(norm1=norm1, norm2=norm2, router=router, final_norm=final_norm,
             exp_gate=exp_gate.astype(BF16), exp_up=exp_up.astype(BF16), exp_down=exp_down.astype(BF16),
             m_qkvo=m_qkvo.astype(BF16), m_gate_w=m_gate_w, m_gate_b=m_gate_b, m_hnorm=m_hnorm,
             m_out=m_out.astype(BF16), a_qkv=a_qkv.astype(BF16), a_qnorm=a_qnorm, a_knorm=a_knorm,
             a_out=a_out.astype(BF16), r_mix=r_mix, r_rkv=r_rkv.astype(BF16), r_w0=r_w0, r_a0=r_a0,
             r_w1c=w1c, r_w2p=w2p, r_a1c=a1c, r_a2p=a2p, r_g1=r_g1.astype(BF16), r_g2=r_g2.astype(BF16),
             r_kk=r_kk, r_ka=r_ka, r_rk=r_rk, r_lnx_w=r_lnx_w, r_lnx_b=r_lnx_b, r_out=r_out.astype(BF16))

    y_prompt, new_state = _trunk(x_prompt.reshape(bc * tc, d), mod_ctx, bc, tc, None, p)
    lat_states = [(state_l0_C, state_l0_n, state_l0_m), (cache_l1_k, cache_l1_v), (state_l2_S,),
                  (state_l3_C, state_l3_n, state_l3_m)]
    y_sample, _ = _trunk(x_sample.reshape(bl * tl, d), mod_lat, bl, tl, lat_states, p)
    return (y_prompt.reshape(bc, tc, d), y_sample.reshape(bl, tl, d), *new_state)
```

```python
import functools

import jax
import jax.numpy as jnp
from jax import lax
from jax.experimental import pallas as pl
from jax.experimental.pallas import tpu as pltpu

F32 = jnp.float32
BF16 = jnp.bfloat16
I32 = jnp.int32

D_MODEL = 2048
BATCH = 32
SEQ = 256
DEPTH = 4
DEC_BATCH = 4
DEC_SEQ = 4096
PAST_LEN = 256
GRID_W = 64
N_MIXERS = 3
NORM_EPS = 1e-6
M_HEADS = 8
M_DK = D_MODEL // 2 // M_HEADS
M_DV = D_MODEL // M_HEADS
M_GATE_CAP = 15.0
A_HEADS = 16
A_KV_HEADS = 4
A_HD = 128
ROPE_THETA = 10000.0
R_HD = 64
R_LN_EPS = 64e-5
N_EXPERTS = 16
EC_CAPACITY_FACTOR = 2
D_EXPERT = 1024

LANES = 128
SUBLANES = 8
VMEM_LIMIT_BYTES = 56 * 2 ** 20

MLSTM_CHUNK = 256
RWKV_CHUNK = 64
RWKV_PAIRS_PER_GROUP = 16
RWKV_SOLVE_PASSES = 1
ROUTE_BLOCKS = 128
SLAB_CHUNK = 32


def _params(*sem):
    return pltpu.CompilerParams(dimension_semantics=sem, vmem_limit_bytes=VMEM_LIMIT_BYTES)


def _dot(a, b):
    return jnp.dot(a, b, preferred_element_type=F32)


def _dot_nt(a, b):
    return lax.dot_general(a, b, (((1,), (1,)), ((), ())), preferred_element_type=F32)


def _dot_tn(a, b):
    return lax.dot_general(a, b, (((0,), (0,)), ((), ())), preferred_element_type=F32)


def _split2(x):
    hi = x.astype(BF16)
    lo = (x - hi.astype(F32)).astype(BF16)
    return hi, lo


def _split3(x):
    h1 = x.astype(BF16)
    r1 = x - h1.astype(F32)
    h2 = r1.astype(BF16)
    h3 = (r1 - h2.astype(F32)).astype(BF16)
    return h1, h2, h3


def _dot3(a, w):
    ah, al = _split2(a)
    wh, wl = _split2(w)
    return _dot(ah, wh) + _dot(al, wh) + _dot(ah, wl)


_NN = (((1,), (0,)), ((), ()))
_NT = (((1,), (1,)), ((), ()))
_TN = (((0,), (0,)), ((), ()))


def _mm_passes(a, b, dims, passes):
    def f(x, y):
        return lax.dot_general(x, y, dims, preferred_element_type=F32)
    if passes == 1:
        return f(a.astype(BF16), b.astype(BF16))
    ah, al = _split2(a)
    bh, bl = _split2(b)
    return f(ah, bh) + f(al, bh) + f(ah, bl)


def _iota(shape, dim):
    return lax.broadcasted_iota(I32, shape, dim)


def _log_sigmoid(z):
    return jnp.minimum(z, 0.0) - jnp.log(1.0 + jnp.exp(-jnp.abs(z)))


def _adaln_kernel(c_ref, w_ref, b_ref, o_ref):
    c = c_ref[...]
    s = c * jax.nn.sigmoid(c)
    hi, lo = _split2(s)
    w = w_ref[0].astype(BF16)
    o_ref[0] = _dot(hi, w) + _dot(lo, w) + b_ref[0]


def _adaln(cond8, ada_w, ada_b):
    depth, d, d6 = ada_w.shape
    tn = min(d, 1024)
    return pl.pallas_call(
        _adaln_kernel,
        grid=(depth, d6 // tn),
        in_specs=[pl.BlockSpec((SUBLANES, d), lambda l, j: (0, 0)),
                  pl.BlockSpec((1, d, tn), lambda l, j: (l, 0, j)),
                  pl.BlockSpec((1, 1, tn), lambda l, j: (l, 0, j))],
        out_specs=pl.BlockSpec((1, SUBLANES, tn), lambda l, j: (l, 0, j)),
        out_shape=jax.ShapeDtypeStruct((depth, SUBLANES, d6), F32),
        compiler_params=_params("arbitrary", "arbitrary"),
        name="adaln",
    )(cond8, ada_w, ada_b.reshape(depth, 1, d6))


def _normed(x_ref, g_ref):
    x = x_ref[...]
    return x * lax.rsqrt(jnp.mean(x * x, axis=-1, keepdims=True) + NORM_EPS) * g_ref[...]


def _norm_plain_kernel(sh, sc, x_ref, g_ref, mod_ref, o_ref):
    h = _normed(x_ref, g_ref) * (1.0 + mod_ref[0, sc:sc + 1, :]) + mod_ref[0, sh:sh + 1, :]
    o_ref[...] = h.astype(o_ref.dtype)


def _norm_gate_kernel(sh, sc, x_ref, g_ref, mod_ref, w_ref, b_ref, o_ref, gate_ref):
    h = _normed(x_ref, g_ref) * (1.0 + mod_ref[0, sc:sc + 1, :]) + mod_ref[0, sh:sh + 1, :]
    o_ref[...] = h.astype(o_ref.dtype)
    z = _dot3(h, w_ref[...]) + b_ref[...]
    z = M_GATE_CAP * jnp.tanh(z / M_GATE_CAP)
    is_forget = ((_iota(z.shape, 1) // M_HEADS) % 2) == 1
    gate_ref[...] = jnp.where(is_forget, _log_sigmoid(z), z)


def _norm_router_kernel(sh, sc, x_ref, g_ref, mod_ref, w_ref, o_ref, aff_ref):
    h = _normed(x_ref, g_ref) * (1.0 + mod_ref[0, sc:sc + 1, :]) + mod_ref[0, sh:sh + 1, :]
    o_ref[...] = h
    z = _dot3(h, w_ref[...])
    e = jnp.exp(z - jnp.max(z, axis=-1, keepdims=True))
    aff_ref[...] = e / jnp.sum(e, axis=-1, keepdims=True)


def _norm_final_kernel(x_ref, g_ref, o_ref):
    o_ref[...] = _normed(x_ref, g_ref)


def _row_tile(n, t, want):
    tm = min(want, t)
    assert t % tm == 0 and n % tm == 0
    return tm


def _mod_spec(mod, tm, t):
    d = mod.shape[-1]
    if mod.shape[0] == 1:
        return pl.BlockSpec((1, 6, d), lambda i: (0, 0, 0))
    return pl.BlockSpec((1, 6, d), lambda i: ((i * tm) // t, 0, 0))


def _norm_call(kind, x, gain, mod, t, rows=(0, 1), w=None, b=None, out_dtype=F32):
    n, d = x.shape
    tm = _row_tile(n, t, 256)
    xs = pl.BlockSpec((tm, d), lambda i: (i, 0))
    gs = pl.BlockSpec((1, d), lambda i: (0, 0))
    ms = None if mod is None else _mod_spec(mod, tm, t)
    gain = gain.reshape(1, d)
    if kind == "final":
        return pl.pallas_call(_norm_final_kernel, grid=(n // tm,), in_specs=[xs, gs], out_specs=xs,
                              out_shape=jax.ShapeDtypeStruct((n, d), F32),
                              compiler_params=_params("arbitrary"), name="norm_final")(x, gain)
    sh, sc = rows
    if kind == "plain":
        return pl.pallas_call(functools.partial(_norm_plain_kernel, sh, sc), grid=(n // tm,),
                              in_specs=[xs, gs, ms], out_specs=xs,
                              out_shape=jax.ShapeDtypeStruct((n, d), out_dtype),
                              compiler_params=_params("arbitrary"), name="norm_plain")(x, gain, mod)
    ns = w.shape[1]
    ws = pl.BlockSpec((d, ns), lambda i: (0, 0))
    ps = pl.BlockSpec((tm, ns), lambda i: (i, 0))
    if kind == "gate":
        return pl.pallas_call(
            functools.partial(_norm_gate_kernel, sh, sc), grid=(n // tm,),
            in_specs=[xs, gs, ms, ws, pl.BlockSpec((1, ns), lambda i: (0, 0))],
            out_specs=[xs, ps],
            out_shape=[jax.ShapeDtypeStruct((n, d), out_dtype), jax.ShapeDtypeStruct((n, ns), F32)],
            compiler_params=_params("arbitrary"), name="norm_gate")(x, gain, mod, w, b.reshape(1, ns))
    assert kind == "router"
    return pl.pallas_call(
        functools.partial(_norm_router_kernel, sh, sc), grid=(n // tm,),
        in_specs=[xs, gs, ms, ws], out_specs=[xs, ps],
        out_shape=[jax.ShapeDtypeStruct((n, d), F32), jax.ShapeDtypeStruct((n, ns), F32)],
        compiler_params=_params("arbitrary"), name="norm_router")(x, gain, mod, w)


def _mm_kernel(x_ref, w_ref, o_ref):
    o_ref[...] = _dot(x_ref[...], w_ref[...])


def _mm(x, w, tn):
    n, k = x.shape
    m = w.shape[1]
    tm = min(512, n)
    tn = min(tn, m)
    return pl.pallas_call(
        _mm_kernel, grid=(m // tn, n // tm),
        in_specs=[pl.BlockSpec((tm, k), lambda j, i: (i, 0)), pl.BlockSpec((k, tn), lambda j, i: (0, j))],
        out_specs=pl.BlockSpec((tm, tn), lambda j, i: (i, j)),
        out_shape=jax.ShapeDtypeStruct((n, m), F32),
        compiler_params=_params("arbitrary", "arbitrary"), name="mm")(x, w)


def _mlstm_kernel(has_init, emit_state, *refs):
    q_ref, k_ref, v_ref, g_ref, gt_ref = refs[:5]
    pos = 5
    if has_init:
        c0_ref, n0_ref, m0_ref = refs[pos:pos + 3]
        pos += 3
    h_ref = refs[pos]
    pos += 1
    if emit_state:
        cout_ref, nout_ref, mout_ref = refs[pos:pos + 3]
        pos += 3
    c_scr, n_scr, m_scr = refs[pos:pos + 3]

    d = pl.program_id(1)
    c = pl.program_id(2)
    nc = pl.num_programs(2)
    fwd = d == 0
    heads = M_HEADS
    length = q_ref.shape[0]

    @pl.when(c == 0)
    def _():
        if has_init:
            c_scr[...] = c0_ref[0, 0]
            n_scr[...] = n0_ref[0, 0]
            m_scr[...] = m0_ref[0, 0]
        else:
            c_scr[...] = jnp.zeros_like(c_scr)
            n_scr[...] = jnp.zeros_like(n_scr)
            m_scr[...] = jnp.zeros_like(m_scr)

    sgn = jnp.where(fwd, 1, -1)
    row = _iota((length, length), 0)
    col = _iota((length, length), 1)
    mask = ((row - col) * sgn) >= 0
    mask_t = ((col - row) * sgn) >= 0
    scale = M_DK ** -0.5
    g = g_ref[...]
    gt = gt_ref[0]

    for h in range(heads):
        ig_row = jnp.where(fwd, gt[h:h + 1, :], gt[2 * heads + h:2 * heads + h + 1, :])
        lf_row = jnp.where(fwd, gt[heads + h:heads + h + 1, :], gt[3 * heads + h:3 * heads + h + 1, :])
        ig_col = jnp.where(fwd, g[:, h:h + 1], g[:, 2 * heads + h:2 * heads + h + 1])
        lf_col = jnp.where(fwd, g[:, heads + h:heads + h + 1], g[:, 3 * heads + h:3 * heads + h + 1])
        b_col = jnp.sum(jnp.where(mask, lf_row, 0.0), axis=1, keepdims=True)
        b_row = jnp.sum(jnp.where(mask_t, lf_col, 0.0), axis=0, keepdims=True)
        dmat = jnp.where(mask, b_col - b_row + ig_row, -jnp.inf)
        m_prev = m_scr[h:h + 1, 0:1]
        a = b_col + m_prev
        m_t = jnp.maximum(a, jnp.max(dmat, axis=1, keepdims=True))
        w_inter = jnp.exp(a - m_t)
        qh = q_ref[:, h * M_DK:(h + 1) * M_DK]
        kh = k_ref[:, h * M_DK:(h + 1) * M_DK] * scale
        vh = v_ref[:, h * M_DV:(h + 1) * M_DV]
        qb, kb, vb = qh.astype(BF16), kh.astype(BF16), vh.astype(BF16)
        s = _dot_nt(qb, kb) * jnp.exp(dmat - m_t)
        c_prev = c_scr[h]
        n_prev = n_scr[h:h + 1, :]
        num = w_inter * _dot(qb, c_prev.astype(BF16)) + _dot(s.astype(BF16), vb)
        den = w_inter * jnp.sum(qh * n_prev, axis=1, keepdims=True) + jnp.sum(s, axis=1, keepdims=True)
        h_ref[0, :, h * M_DV:(h + 1) * M_DV] = num / jnp.maximum(jnp.abs(den), jnp.exp(-m_t))

        b_last = jnp.sum(lf_col, axis=0, keepdims=True)
        g_col = b_last - b_col + ig_col
        m_new = jnp.maximum(b_last + m_prev, jnp.max(g_col, axis=0, keepdims=True))
        decay = jnp.exp(b_last + m_prev - m_new)
        wk = jnp.exp(g_col - m_new) * kh
        c_scr[h] = decay * c_prev + _dot_tn(wk.astype(BF16), vb)
        n_scr[h:h + 1, :] = decay * n_prev + jnp.sum(wk, axis=0, keepdims=True)
        m_scr[h:h + 1, :] = jnp.broadcast_to(m_new, (1, LANES))

    if emit_state:
        @pl.when(c == nc - 1)
        def _():
            cout_ref[0, 0] = c_scr[...]
            nout_ref[0, 0] = n_scr[...]
            mout_ref[0, 0] = m_scr[...]


def _mlstm_scan(proj, gates, b, t, init, emit_state):
    n = proj.shape[0]
    heads, dk, dv = M_HEADS, M_DK, M_DV
    qk, mv = heads * dk, heads * dv
    length = min(MLSTM_CHUNK, t)
    nc = t // length
    gates_t = gates.reshape(n // length, length, 4 * heads).transpose(0, 2, 1)

    def tile(bi, di, ci):
        return bi * nc + ci + di * (nc - 1 - 2 * ci)

    in_specs = [pl.BlockSpec((length, qk), lambda bi, di, ci: (tile(bi, di, ci), 0)),
                pl.BlockSpec((length, qk), lambda bi, di, ci: (tile(bi, di, ci), 1)),
                pl.BlockSpec((length, mv), lambda bi, di, ci: (tile(bi, di, ci), 1)),
                pl.BlockSpec((length, 4 * heads), lambda bi, di, ci: (tile(bi, di, ci), 0)),
                pl.BlockSpec((1, 4 * heads, length), lambda bi, di, ci: (tile(bi, di, ci), 0, 0))]
    args = [proj, proj, proj, gates, gates_t]
    state_specs = [pl.BlockSpec((1, 1, heads, dk, dv), lambda bi, di, ci: (bi, di, 0, 0, 0)),
                   pl.BlockSpec((1, 1, heads, dk), lambda bi, di, ci: (bi, di, 0, 0)),
                   pl.BlockSpec((1, 1, heads, LANES), lambda bi, di, ci: (bi, di, 0, 0))]
    if init is not None:
        c0, n0, m0 = init
        in_specs += state_specs
        args += [c0, n0, jnp.broadcast_to(m0[..., None], m0.shape + (LANES,))]
    out_specs = [pl.BlockSpec((1, length, mv), lambda bi, di, ci: (di, tile(bi, di, ci), 0))]
    out_shape = [jax.ShapeDtypeStruct((2, n, mv), F32)]
    if emit_state:
        out_specs += state_specs
        out_shape += [jax.ShapeDtypeStruct((b, 2, heads, dk, dv), F32),
                      jax.ShapeDtypeStruct((b, 2, heads, dk), F32),
                      jax.ShapeDtypeStruct((b, 2, heads, LANES), F32)]
    return pl.pallas_call(
        functools.partial(_mlstm_kernel, init is not None, emit_state),
        grid=(b, 2, nc), in_specs=in_specs, out_specs=out_specs, out_shape=out_shape,
        scratch_shapes=[pltpu.VMEM((heads, dk, dv), F32), pltpu.VMEM((heads, dk), F32),
                        pltpu.VMEM((heads, LANES), F32)],
        compiler_params=_params("arbitrary", "arbitrary", "arbitrary"), name="mlstm_scan")(*args)


def _mlstm_out_kernel(hs_ref, o_ref, gain_ref, w_ref, x_ref, mod_ref, out_ref):
    hs = hs_ref[0] + hs_ref[1]
    parts = []
    for h in range(M_HEADS):
        seg = hs[:, h * M_DV:(h + 1) * M_DV]
        parts.append(seg * lax.rsqrt(jnp.mean(seg * seg, axis=-1, keepdims=True) + NORM_EPS))
    hn = jnp.concatenate(parts, axis=1) * gain_ref[...]
    lhs = (jax.nn.sigmoid(o_ref[...]) * hn).astype(BF16)
    out_ref[...] = x_ref[...] + mod_ref[0, 2:3, :] * _dot(lhs, w_ref[...])


def _mlstm_out(hs2, proj, gain, w_out, x, mod, t):
    n, d = x.shape
    mv = M_HEADS * M_DV
    tm = _row_tile(n, t, 256)
    return pl.pallas_call(
        _mlstm_out_kernel, grid=(n // tm,),
        in_specs=[pl.BlockSpec((2, tm, mv), lambda i: (0, i, 0)),
                  pl.BlockSpec((tm, mv), lambda i: (i, 2)),
                  pl.BlockSpec((1, mv), lambda i: (0, 0)),
                  pl.BlockSpec((mv, d), lambda i: (0, 0)),
                  pl.BlockSpec((tm, d), lambda i: (i, 0)),
                  _mod_spec(mod, tm, t)],
        out_specs=pl.BlockSpec((tm, d), lambda i: (i, 0)),
        out_shape=jax.ShapeDtypeStruct((n, d), F32),
        compiler_params=_params("arbitrary"), name="mlstm_out")(hs2, proj, gain.reshape(1, mv), w_out, x, mod)


def _attn_qkv_kernel(use_rope, *refs):
    if use_rope:
        x_ref, w_ref, qg_ref, kg_ref, cos_ref, sin_ref, q_ref, k_ref, v_ref = refs
    else:
        x_ref, w_ref, qg_ref, kg_ref, q_ref, k_ref, v_ref = refs
    acc = _dot(x_ref[...], w_ref[...])
    qd, kvd = A_HEADS * A_HD, A_KV_HEADS * A_HD

    def head_norm(xh, gain):
        xh = xh * lax.rsqrt(jnp.mean(xh * xh, axis=-1, keepdims=True) + NORM_EPS) * gain
        if use_rope:
            xh = xh * cos_ref[...] + pltpu.roll(xh, A_HD // 2, axis=1) * sin_ref[...]
        return xh

    for h in range(A_HEADS):
        qh = head_norm(acc[:, h * A_HD:(h + 1) * A_HD], qg_ref[...])
        q_ref[:, h * A_HD:(h + 1) * A_HD] = (qh * (A_HD ** -0.5)).astype(BF16)
    for h in range(A_KV_HEADS):
        k_ref[:, h * A_HD:(h + 1) * A_HD] = head_norm(acc[:, qd + h * A_HD:qd + (h + 1) * A_HD], kg_ref[...])
    v_ref[...] = acc[:, qd + kvd:]


def _attn_qkv(hb, w, qgain, kgain, t, rope):
    n, d = hb.shape
    qd, kvd = A_HEADS * A_HD, A_KV_HEADS * A_HD
    tm = _row_tile(n, t, 256)
    in_specs = [pl.BlockSpec((tm, d), lambda i: (i, 0)),
                pl.BlockSpec((d, qd + 2 * kvd), lambda i: (0, 0)),
                pl.BlockSpec((1, A_HD), lambda i: (0, 0)),
                pl.BlockSpec((1, A_HD), lambda i: (0, 0))]
    args = [hb, w, qgain.reshape(1, A_HD), kgain.reshape(1, A_HD)]
    if rope is not None:
        in_specs += [pl.BlockSpec((tm, A_HD), lambda i: (i % (t // tm), 0))] * 2
        args += list(rope)
    return pl.pallas_call(
        functools.partial(_attn_qkv_kernel, rope is not None), grid=(n // tm,),
        in_specs=in_specs,
        out_specs=[pl.BlockSpec((tm, qd), lambda i: (i, 0)),
                   pl.BlockSpec((tm, kvd), lambda i: (i, 0)),
                   pl.BlockSpec((tm, kvd), lambda i: (i, 0))],
        out_shape=[jax.ShapeDtypeStruct((n, qd), BF16), jax.ShapeDtypeStruct((n, kvd), F32),
                   jax.ShapeDtypeStruct((n, kvd), F32)],
        compiler_params=_params("arbitrary"), name="attn_qkv")(*args)


def _attn_kernel(has_cache, *refs):
    if has_cache:
        q_ref, kn_ref, vn_ref, kc_ref, vc_ref, o_ref, k_scr, v_scr = refs
    else:
        q_ref, kn_ref, vn_ref, o_ref, k_scr, v_scr = refs
    group = A_HEADS // A_KV_HEADS
    tq = q_ref.shape[0]
    t_new = kn_ref.shape[0]
    past = k_scr.shape[0] - t_new

    @pl.when(pl.program_id(2) == 0)
    def _():
        if has_cache:
            k_scr[0:past, :] = kc_ref[...].astype(BF16)
            v_scr[0:past, :] = vc_ref[...].astype(BF16)
        k_scr[past:, :] = kn_ref[...].astype(BF16)
        v_scr[past:, :] = vn_ref[...].astype(BF16)

    q4 = jnp.concatenate([q_ref[:, g * A_HD:(g + 1) * A_HD] for g in range(group)], axis=0)
    s = _dot_nt(q4, k_scr[...])
    p = jnp.exp(s - jnp.max(s, axis=-1, keepdims=True))
    o = _dot(p.astype(BF16), v_scr[...]) / jnp.sum(p, axis=-1, keepdims=True)
    for g in range(group):
        o_ref[:, g * A_HD:(g + 1) * A_HD] = o[g * tq:(g + 1) * tq, :].astype(BF16)


def _attention(q, k, v, b, t, cache):
    n = q.shape[0]
    group = A_HEADS // A_KV_HEADS
    tq = min(64, t)
    nq = t // tq
    past = 0 if cache is None else cache[0].shape[0] // b
    in_specs = [pl.BlockSpec((tq, group * A_HD), lambda bi, hi, qi: (bi * nq + qi, hi)),
                pl.BlockSpec((t, A_HD), lambda bi, hi, qi: (bi, hi)),
                pl.BlockSpec((t, A_HD), lambda bi, hi, qi: (bi, hi))]
    args = [q, k, v]
    if cache is not None:
        in_specs += [pl.BlockSpec((past, A_HD), lambda bi, hi, qi: (bi, hi))] * 2
        args += list(cache)
    return pl.pallas_call(
        functools.partial(_attn_kernel, cache is not None), grid=(b, A_KV_HEADS, nq),
        in_specs=in_specs,
        out_specs=pl.BlockSpec((tq, group * A_HD), lambda bi, hi, qi: (bi * nq + qi, hi)),
        out_shape=jax.ShapeDtypeStruct((n, A_HEADS * A_HD), BF16),
        scratch_shapes=[pltpu.VMEM((past + t, A_HD), BF16), pltpu.VMEM((past + t, A_HD), BF16)],
        compiler_params=_params("arbitrary", "arbitrary", "arbitrary"), name="attention")(*args)


def _resid_mm_kernel(lhs_ref, w_ref, x_ref, mod_ref, out_ref):
    out_ref[...] = x_ref[...] + mod_ref[0, 2:3, :] * _dot(lhs_ref[...], w_ref[...])


def _resid_mm(lhs, w, x, mod, t):
    n, d = x.shape
    k = lhs.shape[1]
    tm = _row_tile(n, t, 256)
    return pl.pallas_call(
        _resid_mm_kernel, grid=(n // tm,),
        in_specs=[pl.BlockSpec((tm, k), lambda i: (i, 0)),
                  pl.BlockSpec((k, d), lambda i: (0, 0)),
                  pl.BlockSpec((tm, d), lambda i: (i, 0)),
                  _mod_spec(mod, tm, t)],
        out_specs=pl.BlockSpec((tm, d), lambda i: (i, 0)),
        out_shape=jax.ShapeDtypeStruct((n, d), F32),
        compiler_params=_params("arbitrary"), name="resid_mm")(lhs, w, x, mod)


def _token_shift_delta(h, hp_ref, hn_ref, i, tiles_per_seq):
    tm = h.shape[0]
    rowi = _iota((tm, 1), 0)
    ti = i % tiles_per_seq
    prev_row = jnp.where(ti == 0, 0.0, hp_ref[SUBLANES - 1:SUBLANES, :])
    next_row = jnp.where(ti == tiles_per_seq - 1, 0.0, hn_ref[0:1, :])
    h_prev = jnp.where(rowi == 0, prev_row, pltpu.roll(h, 1, axis=0))
    h_next = jnp.where(rowi == tm - 1, next_row, pltpu.roll(h, tm - 1, axis=0))
    return 0.5 * (h_prev + h_next) - h


def _shift_specs(tm, d, n, grid_rank):
    nb8 = n // SUBLANES
    per = tm // SUBLANES
    if grid_rank == 1:
        return [pl.BlockSpec((tm, d), lambda i: (i, 0)),
                pl.BlockSpec((SUBLANES, d), lambda i: (jnp.maximum(i * per - 1, 0), 0)),
                pl.BlockSpec((SUBLANES, d), lambda i: (jnp.minimum((i + 1) * per, nb8 - 1), 0))]
    return [pl.BlockSpec((tm, d), lambda j, i: (i, 0)),
            pl.BlockSpec((SUBLANES, d), lambda j, i: (jnp.maximum(i * per - 1, 0), 0)),
            pl.BlockSpec((SUBLANES, d), lambda j, i: (jnp.minimum((i + 1) * per, nb8 - 1), 0))]


def _rwkv_rkv_kernel(tiles_per_seq, h_ref, hp_ref, hn_ref, mix_ref, w_ref, o_ref):
    h = h_ref[...]
    xx = _token_shift_delta(h, hp_ref, hn_ref, pl.program_id(1), tiles_per_seq)
    o_ref[0] = _dot((h + xx * mix_ref[0]).astype(BF16), w_ref[0])


def _rwkv_rkv(h, mix3, w_rkv, t):
    n, d = h.shape
    tm = _row_tile(n, t, 256)
    return pl.pallas_call(
        functools.partial(_rwkv_rkv_kernel, t // tm), grid=(3, n // tm),
        in_specs=_shift_specs(tm, d, n, 2) + [pl.BlockSpec((1, 1, d), lambda j, i: (j, 0, 0)),
                                              pl.BlockSpec((1, d, d), lambda j, i: (j, 0, 0))],
        out_specs=pl.BlockSpec((1, tm, d), lambda j, i: (j, i, 0)),
        out_shape=jax.ShapeDtypeStruct((3, n, d), F32),
        compiler_params=_params("arbitrary", "arbitrary"), name="rwkv_rkv")(h, h, h, mix3, w_rkv)


def _rwkv_lora_kernel(tiles_per_seq, lora, h_ref, hp_ref, hn_ref, mix_ref, w1_ref, a1_ref, g1_ref,
                      w2_ref, a2_ref, g2_ref, w0_ref, a0_ref, lw_ref, asig_ref, g_ref):
    h = h_ref[...]
    xx = _token_shift_delta(h, hp_ref, hn_ref, pl.program_id(0), tiles_per_seq)
    xw = (h + xx * mix_ref[0:1, :]).astype(BF16)
    xa = (h + xx * mix_ref[1:2, :]).astype(BF16)
    xg = (h + xx * mix_ref[2:3, :]).astype(BF16)
    tw = jnp.tanh(_dot(xw, w1_ref[...])).astype(BF16)
    ta = _dot(xa, a1_ref[...]).astype(BF16)
    tg = jax.nn.sigmoid(_dot(xg, g1_ref[...])).astype(BF16)
    g_ref[...] = _dot(tg, g2_ref[...])
    for dr in range(2):
        z = -(w0_ref[dr] + _dot(tw[:, dr * lora:(dr + 1) * lora], w2_ref[dr]))
        softplus = jnp.maximum(z, 0.0) + jnp.log(1.0 + jnp.exp(-jnp.abs(z)))
        lw_ref[dr] = -jnp.exp(-softplus - 0.5)
        asig_ref[dr] = jax.nn.sigmoid(a0_ref[dr] + _dot(ta[:, dr * lora:(dr + 1) * lora], a2_ref[dr]))


def _rwkv_lora(h, mix3, w1c, a1c, g1, w2p, a2p, g2, w0, a0, t):
    n, d = h.shape
    tm = _row_tile(n, t, 256)
    lora = w2p.shape[1]
    gl = g1.shape[1]
    full2 = lambda shape: pl.BlockSpec(shape, lambda i: (0, 0))
    full3 = lambda shape: pl.BlockSpec(shape, lambda i: (0, 0, 0))
    return pl.pallas_call(
        functools.partial(_rwkv_lora_kernel, t // tm, lora), grid=(n // tm,),
        in_specs=_shift_specs(tm, d, n, 1) + [
            full2((3, d)), full2((d, 2 * lora)), full2((d, 2 * lora)), full2((d, gl)),
            full3((2, lora, d)), full3((2, lora, d)), full2((gl, d)), full3((2, 1, d)), full3((2, 1, d))],
        out_specs=[pl.BlockSpec((2, tm, d), lambda i: (0, i, 0)),
                   pl.BlockSpec((2, tm, d), lambda i: (0, i, 0)),
                   pl.BlockSpec((tm, d), lambda i: (i, 0))],
        out_shape=[jax.ShapeDtypeStruct((2, n, d), F32), jax.ShapeDtypeStruct((2, n, d), F32),
                   jax.ShapeDtypeStruct((n, d), F32)],
        compiler_params=_params("arbitrary"), name="rwkv_lora")(h, h, h, mix3, w1c, a1c, g1, w2p, a2p, g2, w0, a0)


def _rwkv_scan_kernel(has_init, emit_state, *refs):
    r_ref, k_ref, v_ref, lw_ref, a_ref, kk_ref, ka_ref, rk_ref = refs[:8]
    pos = 8
    if has_init:
        s0_ref = refs[pos]
        pos += 1
    y_ref, bonus_ref = refs[pos:pos + 2]
    pos += 2
    if emit_state:
        sout_ref = refs[pos]
        pos += 1
    s_scr, cum_scr = refs[pos:pos + 2]

    d = pl.program_id(1)
    c = pl.program_id(2)
    nc = pl.num_programs(2)
    fwd = d == 0
    sgn = jnp.where(fwd, 1, -1)
    length = r_ref.shape[1]
    npairs = s_scr.shape[0]
    hd = R_HD

    @pl.when(c == 0)
    def _():
        if has_init:
            s_scr[...] = s0_ref[0, 0]
        else:
            s_scr[...] = jnp.zeros_like(s_scr)

    ti = _iota((length, length), 0)
    si = _iota((length, length), 1)
    tri = (((ti - si) * sgn) >= 0).astype(BF16)
    l1, l2, l3 = _split3(lw_ref[0])
    cum_scr[...] = _dot(tri, l1) + _dot(tri, l2) + _dot(tri, l3)

    gl = 2 * length
    rr = _iota((gl, gl), 0)
    cc = _iota((gl, gl), 1)
    same = (rr // length) == (cc // length)
    e = ((rr % length) - (cc % length)) * sgn
    strict = same & (e > 0)
    incl = same & (e >= 0)
    eye = (rr == cc).astype(F32)
    lane = _iota((1, LANES), 1)
    m0 = (lane < hd).astype(F32)
    m1 = 1.0 - m0
    lr = _iota((LANES, LANES), 0)
    lc = _iota((LANES, LANES), 1)
    ones_blk = ((lr // hd) == (lc // hd)).astype(BF16)

    def segsum(x):
        hi, lo = _split2(x)
        return _dot(hi, ones_blk) + _dot(lo, ones_blk)

    def stack(x):
        return jnp.concatenate([x * m0, x * m1], axis=0)

    def smm(x, y, dims):
        return _mm_passes(x, y, dims, RWKV_SOLVE_PASSES)

    tr = rr % length
    tc = cc % length
    base = 4
    base_mask = (tr // base) == (tc // base)
    level_masks = []
    size = base
    while size < length:
        level_masks.append(((tr // (2 * size)) == (tc // (2 * size))) & ((tr // size) != (tc // size)))
        size *= 2

    def load_pair(p):
        ds = pl.ds(pl.multiple_of(p * LANES, LANES), LANES)
        return dict(ds=ds, p=p, r=r_ref[0, :, ds], k=k_ref[0, :, ds], v=v_ref[0, :, ds], lw=lw_ref[0, :, ds],
                    a=a_ref[0, :, ds], cum=cum_scr[:, ds], kkp=kk_ref[:, ds], kap=ka_ref[:, ds],
                    rkp=rk_ref[:, ds], st=s_scr[p])

    def each(fn, *lists):
        return [fn(*args) for args in zip(*lists)]

    def solve_group(xs):
        get = lambda name: [x[name] for x in xs]
        r, k, v, lw, a, cum, st = (get(n) for n in ("r", "k", "v", "lw", "a", "cum", "st"))
        kk = each(lambda k_, p_: k_ * p_, k, get("kkp"))
        ss = each(lambda q: segsum(q * q), kk)
        kk = each(lambda q, s_: q * lax.rsqrt(jnp.maximum(s_, 1e-24)), kk, ss)
        kd = each(lambda k_, a_, p_: k_ * (1.0 + (a_ - 1.0) * p_), k, a, get("kap"))
        bb = each(lambda q, a_: q * a_, kk, a)
        tot = each(lambda c_: jnp.where(fwd, c_[length - 1:length, :], c_[0:1, :]), cum)
        e_n = each(lambda c_: jnp.exp(-c_), cum)
        sa = each(lambda q, c_, l_: stack(-q * jnp.exp(c_ - l_)), kk, cum, lw)
        sr_full = each(lambda r_, c_: r_ * jnp.exp(c_), r, cum)
        sr = each(stack, sr_full)
        sb = each(lambda b_, e_: stack(b_ * e_), bb, e_n)
        sk = each(lambda k_, e_: stack(k_ * e_), kd, e_n)
        vexp = each(stack, v)
        m_ab = each(lambda x_, y_: jnp.where(strict, smm(x_, y_, _NT), 0.0), sa, sb)
        m_ak = each(lambda x_, y_: jnp.where(strict, smm(x_, y_, _NT), 0.0), sa, sk)
        rhs1 = each(lambda x_, s_: smm(x_, s_, _NT), sa, st)
        rhs2 = each(lambda m_, v_: smm(m_, v_, _NN), m_ak, vexp)
        rhs = each(lambda x_, y_: x_ + y_, rhs1, rhs2)
        d0 = each(lambda m_: jnp.where(base_mask, m_, 0.0), m_ab)
        sq = each(lambda d_: smm(d_, d_, _NN), d0)
        inv = each(lambda d_: eye + d_, d0)
        inv = each(lambda i_, q_: i_ + smm(i_, q_, _NN), inv, sq)
        for lm in level_masks:
            half = each(lambda i_, m_: smm(i_, jnp.where(lm, m_, 0.0), _NN), inv, m_ab)
            inv = each(lambda i_, h_: i_ + smm(h_, i_, _NN), inv, half)
        uexp = each(lambda i_, r_: smm(i_, r_, _NN), inv, rhs)
        t_rb = each(lambda x_, y_: jnp.where(incl, smm(x_, y_, _NT), 0.0), sr, sb)
        t_rk = each(lambda x_, y_: jnp.where(incl, smm(x_, y_, _NT), 0.0), sr, sk)
        fold = lambda t_: t_[0:length] + t_[length:gl]
        y1 = each(lambda r_, s_: smm(r_, s_, _NT), sr_full, st)
        y2 = each(lambda t_, u_: smm(fold(t_), u_, _NN), t_rb, uexp)
        y3 = each(lambda t_, v_: smm(fold(t_), v_, _NN), t_rk, vexp)
        y = each(lambda p_, q_, w_: p_ + q_ + w_, y1, y2, y3)
        bs = each(lambda r_, k_, p_: segsum(r_ * k_ * p_), r, kd, get("rkp"))
        bonus = each(lambda s_, v_: s_ * v_, bs, v)
        e_e = each(lambda t_, c_: jnp.exp(t_ - c_), tot, cum)
        s1 = each(lambda u_, b_, e_: smm(u_, stack(b_ * e_), _TN), uexp, bb, e_e)
        s2 = each(lambda v_, k_, e_: smm(v_, stack(k_ * e_), _TN), vexp, kd, e_e)
        s_new = each(lambda s_, t_, p_, q_: s_ * jnp.exp(t_) + p_ + q_, st, tot, s1, s2)
        return y, bonus, s_new

    npg = min(RWKV_PAIRS_PER_GROUP, npairs)

    def group(gi, carry):
        xs = [load_pair(gi * npg + u) for u in range(npg)]
        ys, bonuses, states = solve_group(xs)
        for x, y, bonus, s_new in zip(xs, ys, bonuses, states):
            y_ref[0, :, x["ds"]] = y
            bonus_ref[0, :, x["ds"]] = bonus
            s_scr[x["p"]] = s_new
        return carry

    lax.fori_loop(0, npairs // npg, group, 0)

    if emit_state:
        @pl.when(c == nc - 1)
        def _():
            sout_ref[0, 0] = s_scr[...]


def _rwkv_scan(rkv, lw, asig, kk, ka, rk, b, t, s0, emit_state):
    n, d = rkv.shape[1:]
    length = RWKV_CHUNK
    nc = t // length
    npairs = d // LANES

    def tile(bi, di, ci):
        return bi * nc + ci + di * (nc - 1 - 2 * ci)

    def plane(j):
        return pl.BlockSpec((1, length, d), lambda bi, di, ci: (j, tile(bi, di, ci), 0))

    dirspec = pl.BlockSpec((1, length, d), lambda bi, di, ci: (di, tile(bi, di, ci), 0))
    vec = pl.BlockSpec((1, d), lambda bi, di, ci: (0, 0))
    sspec = pl.BlockSpec((1, 1, npairs, LANES, LANES), lambda bi, di, ci: (bi, di, 0, 0, 0))
    in_specs = [plane(0), plane(1), plane(2), dirspec, dirspec, vec, vec, vec]
    args = [rkv, rkv, rkv, lw, asig, kk.reshape(1, d), ka.reshape(1, d), rk.reshape(1, d)]
    if s0 is not None:
        in_specs.append(sspec)
        args.append(s0)
    out_specs = [dirspec, dirspec]
    out_shape = [jax.ShapeDtypeStruct((2, n, d), F32), jax.ShapeDtypeStruct((2, n, d), F32)]
    if emit_state:
        out_specs.append(sspec)
        out_shape.append(jax.ShapeDtypeStruct((b, 2, npairs, LANES, LANES), F32))
    return pl.pallas_call(
        functools.partial(_rwkv_scan_kernel, s0 is not None, emit_state),
        grid=(b, 2, nc), in_specs=in_specs, out_specs=out_specs, out_shape=out_shape,
        scratch_shapes=[pltpu.VMEM((npairs, LANES, LANES), F32), pltpu.VMEM((length, d), F32)],
        compiler_params=_params("arbitrary", "arbitrary", "arbitrary"), name="rwkv_scan")(*args)


def _rwkv_out_kernel(y_ref, bonus_ref, g_ref, lnw_ref, lnb_ref, w_ref, x_ref, mod_ref, out_ref):
    y = y_ref[0] + y_ref[1]
    lr = _iota((LANES, LANES), 0)
    lc = _iota((LANES, LANES), 1)
    mean_blk = ((lr // R_HD) == (lc // R_HD)).astype(BF16)

    def segmean(x):
        hi, lo = _split2(x)
        return (_dot(hi, mean_blk) + _dot(lo, mean_blk)) * (1.0 / R_HD)

    parts = []
    for j in range(y.shape[1] // LANES):
        yj = y[:, j * LANES:(j + 1) * LANES]
        cen = yj - segmean(yj)
        parts.append(cen * lax.rsqrt(segmean(cen * cen) + R_LN_EPS))
    yn = jnp.concatenate(parts, axis=1) * lnw_ref[...] + lnb_ref[...] + bonus_ref[0] + bonus_ref[1]
    lhs = (yn * g_ref[...]).astype(BF16)
    out_ref[...] = x_ref[...] + mod_ref[0, 2:3, :] * _dot(lhs, w_ref[...])


def _rwkv_out(y2, bonus2, g, lnw, lnb, w_out, x, mod, t):
    n, d = x.shape
    tm = _row_tile(n, t, 256)
    two = pl.BlockSpec((2, tm, d), lambda i: (0, i, 0))
    rows = pl.BlockSpec((tm, d), lambda i: (i, 0))
    vec = pl.BlockSpec((1, d), lambda i: (0, 0))
    return pl.pallas_call(
        _rwkv_out_kernel, grid=(n // tm,),
        in_specs=[two, two, rows, vec, vec, pl.BlockSpec((d, d), lambda i: (0, 0)), rows,
                  _mod_spec(mod, tm, t)],
        out_specs=rows, out_shape=jax.ShapeDtypeStruct((n, d), F32),
        compiler_params=_params("arbitrary"), name="rwkv_out")(
            y2, bonus2, g, lnw.reshape(1, d), lnb.reshape(1, d), w_out, x, mod)


def _route_kernel(cap, aff_ref, idx_ref, gate_ref, pos_ref, tot_ref, off_ref):
    aff = aff_ref[0]
    nb = aff.shape[0]
    bits = lax.bitcast_convert_type(aff, I32)
    tok = _iota(aff.shape, 0) * LANES + _iota(aff.shape, 1)

    def count(m):
        x = jnp.sum(m.astype(F32), axis=1, keepdims=True)
        return jnp.sum(x, axis=0, keepdims=True)

    def value_step(i, cur):
        cand = cur | (jnp.int32(1) << (30 - i))
        return jnp.where(count(bits >= cand) >= cap, cand, cur)

    thr = lax.fori_loop(0, 31, value_step, jnp.zeros((1, 1), I32))
    above = bits > thr
    equal = bits == thr
    need = cap - count(above)

    def index_step(i, cur):
        cand = cur + (jnp.int32(1) << (14 - i))
        return jnp.where(count(equal & (tok < cand)) < need, cand, cur)

    last_tie = lax.fori_loop(0, 15, index_step, jnp.zeros((1, 1), I32))
    sel = above | (equal & (tok <= last_tie))
    self32 = sel.astype(F32)
    selb = self32.astype(BF16)

    li = _iota((LANES, LANES), 0)
    lj = _iota((LANES, LANES), 1)
    upper_incl = (li <= lj).astype(BF16)
    cl = _dot(selb, upper_incl)
    ones8 = jnp.ones((SUBLANES, LANES), BF16)
    tot_row = _dot_nt(ones8, selb)
    bi = _iota((nb, nb), 0)
    bj = _iota((nb, nb), 1)
    offi_row = _dot(tot_row.astype(BF16), (bi <= bj).astype(BF16))
    offx_row = offi_row - tot_row
    tot_col = jnp.broadcast_to(cl[:, LANES - 1:LANES], (nb, LANES)).astype(BF16)
    offx_col = _dot((bj < bi).astype(BF16), tot_col)
    pos_ref[0] = jnp.where(sel, offx_col + cl - 1.0, -1.0).astype(I32)
    tot_ref[0] = tot_row
    off_ref[0] = offx_row

    pcol = _iota((cap, 1), 0).astype(F32)
    before = offi_row[0:1, :] <= pcol
    blk = jnp.sum(before.astype(F32), axis=1, keepdims=True)
    base = jnp.sum(jnp.where(before, tot_row[0:1, :], 0.0), axis=1, keepdims=True)
    onehot = (_iota((cap, nb), 1).astype(F32) == blk).astype(BF16)
    rowcnt = _dot(onehot, cl.astype(BF16))
    rank = pcol - base
    lane = jnp.sum((rowcnt <= rank).astype(F32), axis=1, keepdims=True)
    idx_ref[0] = (blk * LANES + lane).astype(I32)
    a1, a2, a3 = _split3(aff)
    rowaff = _dot(onehot, a1) + _dot(onehot, a2) + _dot(onehot, a3)
    gate_ref[0] = jnp.sum(jnp.where(_iota((cap, LANES), 1).astype(F32) == lane, rowaff, 0.0),
                          axis=1, keepdims=True)


def _route(aff3, cap):
    e, nb, _ = aff3.shape
    return pl.pallas_call(
        functools.partial(_route_kernel, cap), grid=(e,),
        in_specs=[pl.BlockSpec((1, nb, LANES), lambda i: (i, 0, 0))],
        out_specs=[pl.BlockSpec((1, cap, 1), lambda i: (i, 0, 0)),
                   pl.BlockSpec((1, cap, 1), lambda i: (i, 0, 0)),
                   pl.BlockSpec((1, nb, LANES), lambda i: (i, 0, 0)),
                   pl.BlockSpec((1, SUBLANES, nb), lambda i: (i, 0, 0)),
                   pl.BlockSpec((1, SUBLANES, nb), lambda i: (i, 0, 0))],
        out_shape=[jax.ShapeDtypeStruct((e, cap, 1), I32), jax.ShapeDtypeStruct((e, cap, 1), F32),
                   jax.ShapeDtypeStruct((e, nb, LANES), I32),
                   jax.ShapeDtypeStruct((e, SUBLANES, nb), F32), jax.ShapeDtypeStruct((e, SUBLANES, nb), F32)],
        compiler_params=_params("arbitrary"), name="moe_route")(aff3)


def _row_copy(x_hbm, xbuf, sem, slot, tok, r):
    return pltpu.make_async_copy(x_hbm.at[pl.ds(tok, 1)], xbuf.at[pl.ds(r, 1)], sem.at[slot])


def _expert_kernel(idx_ref, idx_next_ref, x_hbm, gate_ref, wg_ref, wu_ref, wd_ref, o_ref, xbuf0, xbuf1, sem):
    tm = xbuf0.shape[0]
    bufs = (xbuf0, xbuf1)
    step = pl.program_id(0) * pl.num_programs(1) + pl.program_id(1)
    last = pl.num_programs(0) * pl.num_programs(1) - 1

    def start_rows(ids_ref, slot):
        for r in range(tm):
            _row_copy(x_hbm, bufs[slot], sem, slot, ids_ref[0, 0, r], r).start()

    def wait_rows(slot):
        for r in range(tm):
            _row_copy(x_hbm, bufs[slot], sem, slot, 0, r).wait()

    @pl.when(step == 0)
    def _():
        start_rows(idx_ref, 0)

    def run(slot):
        start_rows(idx_next_ref, 1 - slot)
        wait_rows(slot)
        xb = bufs[slot][...].astype(BF16)
        hg = _dot(xb, wg_ref[0, 0])
        hu = _dot(xb, wu_ref[0, 0])
        hid = (hg * jax.nn.sigmoid(hg) * hu).astype(BF16)
        o_ref[...] = _dot(hid, wd_ref[0, 0]) * gate_ref[...]

        @pl.when(step == last)
        def _():
            wait_rows(1 - slot)

    for slot in range(2):
        @pl.when(step % 2 == slot)
        def _():
            run(slot)


def _experts(hffn, idx, gate, wg, wu, wd, layer, cap):
    n, d = hffn.shape
    _, e, _, f = wg.shape
    tm = min(256, cap)
    nt = cap // tm
    idx3 = idx.reshape(e * nt, 1, tm)
    last = e * nt - 1
    return pl.pallas_call(
        _expert_kernel, grid=(e, nt),
        in_specs=[pl.BlockSpec((1, 1, tm), lambda ei, ti: (ei * nt + ti, 0, 0), memory_space=pltpu.SMEM),
                  pl.BlockSpec((1, 1, tm), lambda ei, ti: (jnp.minimum(ei * nt + ti + 1, last), 0, 0),
                               memory_space=pltpu.SMEM),
                  pl.BlockSpec(memory_space=pl.ANY),
                  pl.BlockSpec((tm, 1), lambda ei, ti: (ei * nt + ti, 0)),
                  pl.BlockSpec((1, 1, d, f), lambda ei, ti: (layer, ei, 0, 0)),
                  pl.BlockSpec((1, 1, d, f), lambda ei, ti: (layer, ei, 0, 0)),
                  pl.BlockSpec((1, 1, f, d), lambda ei, ti: (layer, ei, 0, 0))],
        out_specs=pl.BlockSpec((tm, d), lambda ei, ti: (ei * nt + ti, 0)),
        out_shape=jax.ShapeDtypeStruct((e * cap, d), F32),
        scratch_shapes=[pltpu.VMEM((tm, d), F32), pltpu.VMEM((tm, d), F32), pltpu.SemaphoreType.DMA((2,))],
        compiler_params=_params("arbitrary", "arbitrary"), name="moe_experts")(
            idx3, idx3, hffn, gate.reshape(e * cap, 1), wg, wu, wd)


def _slab_copy(ye_hbm, slab, sem, src_row, slot):
    return pltpu.make_async_copy(ye_hbm.at[pl.ds(pl.multiple_of(src_row, SUBLANES), SLAB_CHUNK)],
                                 slab.at[pl.ds(pl.multiple_of(slot * SLAB_CHUNK, SLAB_CHUNK), SLAB_CHUNK)],
                                 sem.at[0])


def _combine_kernel(n_exp, meta_ref, x_ref, mod_ref, pos_ref, ye_hbm, out_ref, slab, acc, sem):
    tm = x_ref.shape[0]
    total = meta_ref[0, 0, 0]

    @pl.when(pl.program_id(0) == 0)
    def _():
        slab[...] = jnp.zeros_like(slab)

    def start(g, carry):
        _slab_copy(ye_hbm, slab, sem, meta_ref[0, 0, 1 + n_exp + g], g).start()
        return carry

    def wait(g, carry):
        _slab_copy(ye_hbm, slab, sem, 0, g).wait()
        return carry

    lax.fori_loop(0, total, start, 0)
    pos = pos_ref[...]
    sub_e = _iota((n_exp, 1), 0)
    shift = jnp.zeros((n_exp, 1), I32)
    for e in range(n_exp):
        shift = jnp.where(sub_e == e, meta_ref[0, 0, 1 + e], shift)
    target = jnp.where(pos >= 0, pos + shift, -1)
    acc[...] = jnp.zeros_like(acc)
    lax.fori_loop(0, total, wait, 0)

    kc_rows = 2 * LANES
    chunks_per_kc = kc_rows // SLAB_CHUNK

    def kbody(kc, carry):
        base = pl.multiple_of(kc * kc_rows, kc_rows)
        rows = slab[pl.ds(base, kc_rows), :].astype(BF16)
        slab_row = _iota((kc_rows, tm), 0) + base
        place_t = target[0:1, :] == slab_row
        for e in range(1, n_exp):
            place_t = place_t | (target[e:e + 1, :] == slab_row)
        acc[...] += _dot_tn(place_t.astype(F32).astype(BF16), rows)
        return carry

    lax.fori_loop(0, (total + chunks_per_kc - 1) // chunks_per_kc, kbody, 0)
    out_ref[...] = x_ref[...] + mod_ref[0, 5:6, :] * acc[...]


def _combine(x, mod, pos_t, meta, ye, t, max_chunks):
    n, d = x.shape
    tm = LANES
    n_exp = N_EXPERTS
    kc_rows = 2 * LANES
    slab_rows = -(-(max_chunks * SLAB_CHUNK) // kc_rows) * kc_rows
    return pl.pallas_call(
        functools.partial(_combine_kernel, n_exp), grid=(n // tm,),
        in_specs=[pl.BlockSpec((1, 1, meta.shape[2]), lambda i: (i, 0, 0), memory_space=pltpu.SMEM),
                  pl.BlockSpec((tm, d), lambda i: (i, 0)),
                  _mod_spec(mod, tm, t),
                  pl.BlockSpec((n_exp, tm), lambda i: (0, i)),
                  pl.BlockSpec(memory_space=pl.ANY)],
        out_specs=pl.BlockSpec((tm, d), lambda i: (i, 0)),
        out_shape=jax.ShapeDtypeStruct((n, d), F32),
        scratch_shapes=[pltpu.VMEM((slab_rows, d), F32), pltpu.VMEM((tm, d), F32),
                        pltpu.SemaphoreType.DMA((1,))],
        compiler_params=_params("arbitrary"), name="moe_combine")(meta, x, mod, pos_t, ye)


def _moe(x, mod, t, norm2, w_router, wg, wu, wd, layer):
    n, d = x.shape
    n_exp = N_EXPERTS
    cap = (EC_CAPACITY_FACTOR * n) // n_exp
    hffn, aff = _norm_call("router", x, norm2, mod, t, rows=(3, 4), w=w_router)
    npad = ROUTE_BLOCKS * LANES
    assert n <= npad and n % LANES == 0 and cap % SLAB_CHUNK == 0
    aff_t = jnp.pad(aff.T, ((0, 0), (0, npad - n)), constant_values=-1.0)
    idx, gate, pos, tot, off = _route(aff_t.reshape(n_exp, ROUTE_BLOCKS, LANES), cap)
    ye = _experts(hffn, idx, gate, wg, wu, wd, layer, cap)
    ntiles = n // LANES
    span = min(LANES + SUBLANES + SLAB_CHUNK - (LANES + SUBLANES) % SLAB_CHUNK, cap)
    per_expert = span // SLAB_CHUNK
    max_chunks = n_exp * per_expert
    start = off[:, 0, :ntiles].T.astype(I32)
    cnt = tot[:, 0, :ntiles].T.astype(I32)
    start_al = jnp.minimum((start // SUBLANES) * SUBLANES, cap - span)
    nch = jnp.where(cnt > 0, (start - start_al + cnt + SLAB_CHUNK - 1) // SLAB_CHUNK, 0)
    cend = jnp.cumsum(nch, axis=1)
    cstart = cend - nch
    slots = jnp.arange(max_chunks, dtype=I32)
    owner = jnp.minimum(jnp.sum((cend[:, None, :] <= slots[None, :, None]).astype(I32), axis=2), n_exp - 1)
    base = start_al + jnp.arange(n_exp, dtype=I32)[None, :] * cap
    src = (jnp.take_along_axis(base, owner, axis=1)
           + (slots[None, :] - jnp.take_along_axis(cstart, owner, axis=1)) * SLAB_CHUNK)
    src = jnp.where(slots[None, :] < cend[:, -1:], src, 0)
    meta = jnp.concatenate([cend[:, -1:], cstart * SLAB_CHUNK - start_al, src], axis=1)
    return _combine(x, mod, pos.reshape(n_exp, npad), meta.reshape(ntiles, 1, 1 + n_exp + max_chunks), ye, t,
                    max_chunks)


def _rope_tables(t):
    n_rows = t // GRID_W
    axis_dim = A_HD // 2
    row = jnp.repeat(jnp.arange(n_rows, dtype=F32), GRID_W)
    col = jnp.tile(jnp.arange(GRID_W, dtype=F32), n_rows)
    inv_freq = ROPE_THETA ** (-jnp.arange(0, axis_dim, 2, dtype=F32) / axis_dim)
    ang = jnp.concatenate([row[:, None] * inv_freq, col[:, None] * inv_freq], axis=-1)
    cos, sin = jnp.cos(ang), jnp.sin(ang)
    return jnp.concatenate([cos, cos], axis=-1), jnp.concatenate([-sin, sin], axis=-1)


def _pad_lora(w1, w2):
    r = w1.shape[2]
    rp = -(-r // LANES) * LANES
    w1p = jnp.pad(w1, ((0, 0), (0, 0), (0, rp - r)))
    w2p = jnp.pad(w2, ((0, 0), (0, rp - r), (0, 0)))
    return jnp.concatenate([w1p[0], w1p[1]], axis=1).astype(BF16), w2p.astype(BF16)


def _pair_state(s):
    b, _, heads, hd, _ = s.shape
    sp = s.reshape(b, 2, heads // 2, 2, hd, hd)
    out = jnp.zeros((b, 2, heads // 2, 2 * hd, 2 * hd), F32)
    out = out.at[:, :, :, :hd, :hd].set(sp[:, :, :, 0])
    return out.at[:, :, :, hd:, hd:].set(sp[:, :, :, 1])


def _unpair_state(sp):
    b, _, npairs, _, _ = sp.shape
    hd = R_HD
    s = jnp.stack([sp[:, :, :, :hd, :hd], sp[:, :, :, hd:, hd:]], axis=3)
    return s.reshape(b, 2, 2 * npairs, hd, hd)


def _trunk(x, mod_all, b, t, latent_states, p):
    n, d = x.shape
    latent = latent_states is not None
    produced = []
    for i in range(DEPTH):
        kind, j = i % N_MIXERS, i // N_MIXERS
        mod = mod_all[i]
        if kind == 0:
            hb, gates = _norm_call("gate", x, p["norm1"][i], mod, t, w=p["m_gate_w"][j], b=p["m_gate_b"][j],
                                   out_dtype=BF16)
            proj = _mm(hb, p["m_qkvo"][j], 2048)
            res = _mlstm_scan(proj, gates, b, t, latent_states[i] if latent else None, not latent)
            if not latent:
                produced += [res[1], res[2], res[3][..., 0]]
            x = _mlstm_out(res[0], proj, p["m_hnorm"][j], p["m_out"][j], x, mod, t)
        elif kind == 1:
            hb = _norm_call("plain", x, p["norm1"][i], mod, t, out_dtype=BF16)
            q, k, v = _attn_qkv(hb, p["a_qkv"][j], p["a_qnorm"][j], p["a_knorm"][j], t,
                                _rope_tables(t) if latent else None)
            cache = None
            if latent:
                ck, cv = latent_states[i]
                kvd = A_KV_HEADS * A_HD
                cache = (ck.reshape(-1, kvd), cv.reshape(-1, kvd))
            else:
                produced += [k.reshape(b, t, A_KV_HEADS, A_HD), v.reshape(b, t, A_KV_HEADS, A_HD)]
            o = _attention(q, k, v, b, t, cache)
            x = _resid_mm(o, p["a_out"][j], x, mod, t)
        else:
            h = _norm_call("plain", x, p["norm1"][i], mod, t)
            mix = p["r_mix"][j]
            rkv = _rwkv_rkv(h, mix[jnp.array([0, 2, 3])].reshape(3, 1, d), p["r_rkv"][j], t)
            lw, asig, g = _rwkv_lora(h, mix[jnp.array([1, 4, 5])], p["r_w1c"][j], p["r_a1c"][j], p["r_g1"][j],
                                     p["r_w2p"][j], p["r_a2p"][j], p["r_g2"][j],
                                     p["r_w0"][j].reshape(2, 1, d), p["r_a0"][j].reshape(2, 1, d), t)
            s0 = _pair_state(latent_states[i][0]) if latent else None
            res = _rwkv_scan(rkv, lw, asig, p["r_kk"][j], p["r_ka"][j], p["r_rk"][j].reshape(-1), b, t, s0,
                             not latent)
            if not latent:
                produced.append(_unpair_state(res[2]))
            x = _rwkv_out(res[0], res[1], g, p["r_lnx_w"][j], p["r_lnx_b"][j], p["r_out"][j], x, mod, t)
        x = _moe(x, mod, t, p["norm2"][i], p["router"][i], p["exp_gate"], p["exp_up"], p["exp_down"], i)
    return _norm_call("final", x, p["final_norm"], None, t), produced


def kernel(x_prompt, x_sample, state_l0_C, state_l0_n, state_l0_m, cache_l1_k, cache_l1_v, state_l2_S,
           state_l3_C, state_l3_n, state_l3_m, c, c_ctx, ada_w, ada_b, norm1, norm2, router, exp_gate,
           exp_up, exp_down, m_qkvo, m_gate_w, m_gate_b, m_hnorm, m_out, a_qkv, a_qnorm, a_knorm, a_out,
           r_mix, r_rkv, r_w0, r_w1, r_w2, r_a0, r_a1, r_a2, r_g1, r_g2, r_kk, r_ka, r_rk, r_lnx_w,
           r_lnx_b, r_out, final_norm):
    bc, tc, d = x_prompt.shape
    bl, tl, _ = x_sample.shape
    depth = ada_w.shape[0]
    assert bl + 1 <= SUBLANES

    cond8 = jnp.zeros((SUBLANES, d), F32).at[0].set(c_ctx).at[1:1 + bl].set(c)
    mod = _adaln(cond8, ada_w, ada_b).reshape(depth, SUBLANES, 6, d)
    mod_ctx = mod[:, 0:1]
    mod_lat = mod[:, 1:1 + bl]

    w1c, w2p, a1c, a2p = [], [], [], []
    for j in range(r_w1.shape[0]):
        w1, w2 = _pad_lora(r_w1[j], r_w2[j])
        a1, a2 = _pad_lora(r_a1[j], r_a2[j])
        w1c.append(w1), w2p.append(w2), a1c.append(a1), a2p.append(a2)
    p = dict(norm1=norm1, norm2=norm2, router=router, final_norm=final_norm,
             exp_gate=exp_gate.astype(BF16), exp_up=exp_up.astype(BF16), exp_down=exp_down.astype(BF16),
             m_qkvo=m_qkvo.astype(BF16), m_gate_w=m_gate_w, m_gate_b=m_gate_b, m_hnorm=m_hnorm,
             m_out=m_out.astype(BF16), a_qkv=a_qkv.astype(BF16), a_qnorm=a_qnorm, a_knorm=a_knorm,
             a_out=a_out.astype(BF16), r_mix=r_mix, r_rkv=r_rkv.astype(BF16), r_w0=r_w0, r_a0=r_a0,
             r_w1c=w1c, r_w2p=w2p, r_a1c=a1c, r_a2p=a2p, r_g1=r_g1.astype(BF16), r_g2=r_g2.astype(BF16),
             r_kk=r_kk, r_ka=r_ka, r_rk=r_rk, r_lnx_w=r_lnx_w, r_lnx_b=r_lnx_b, r_out=r_out.astype(BF16))

    y_prompt, new_state = _trunk(x_prompt.reshape(bc * tc, d), mod_ctx, bc, tc, None, p)
    lat_states = [(state_l0_C, state_l0_n, state_l0_m), (cache_l1_k, cache_l1_v), (state_l2_S,),
                  (state_l3_C, state_l3_n, state_l3_m)]
    y_sample, _ = _trunk(x_sample.reshape(bl * tl, d), mod_lat, bl, tl, lat_states, p)
    return (y_prompt.reshape(bc, tc, d), y_sample.reshape(bl, tl, d), *new_state)
```

```python
import functools

import jax
import jax.numpy as jnp
from jax import lax
from jax.experimental import pallas as pl
from jax.experimental.pallas import tpu as pltpu

F32 = jnp.float32
BF16 = jnp.bfloat16
I32 = jnp.int32

D_MODEL = 2048
BATCH = 32
SEQ = 256
DEPTH = 4
DEC_BATCH = 4
DEC_SEQ = 4096
PAST_LEN = 256
GRID_W = 64
N_MIXERS = 3
NORM_EPS = 1e-6
M_HEADS = 8
M_DK = D_MODEL // 2 // M_HEADS
M_DV = D_MODEL // M_HEADS
M_GATE_CAP = 15.0
A_HEADS = 16
A_KV_HEADS = 4
A_HD = 128
ROPE_THETA = 10000.0
R_HD = 64
R_LN_EPS = 64e-5
N_EXPERTS = 16
EC_CAPACITY_FACTOR = 2
D_EXPERT = 1024

LANES = 128
SUBLANES = 8
PACKED_ROWS = 16
VMEM_LIMIT_BYTES = 56 * 2 ** 20

MLSTM_CHUNK = 256
RWKV_CHUNK = 64
RWKV_PAIRS_PER_GROUP = 16
RWKV_SOLVE_PASSES = 1
ROUTE_BLOCKS = 128
SLAB_CHUNK = 32


def _params(*sem):
    return pltpu.CompilerParams(dimension_semantics=sem, vmem_limit_bytes=VMEM_LIMIT_BYTES)


def _dot(a, b):
    return jnp.dot(a, b, preferred_element_type=F32)


def _dot_nt(a, b):
    return lax.dot_general(a, b, (((1,), (1,)), ((), ())), preferred_element_type=F32)


def _dot_tn(a, b):
    return lax.dot_general(a, b, (((0,), (0,)), ((), ())), preferred_element_type=F32)


def _split2(x):
    hi = x.astype(BF16)
    lo = (x - hi.astype(F32)).astype(BF16)
    return hi, lo


def _split3(x):
    h1 = x.astype(BF16)
    r1 = x - h1.astype(F32)
    h2 = r1.astype(BF16)
    h3 = (r1 - h2.astype(F32)).astype(BF16)
    return h1, h2, h3


def _dot3(a, w):
    ah, al = _split2(a)
    wh, wl = _split2(w)
    return _dot(ah, wh) + _dot(al, wh) + _dot(ah, wl)


_NN = (((1,), (0,)), ((), ()))
_NT = (((1,), (1,)), ((), ()))
_TN = (((0,), (0,)), ((), ()))


def _mm_passes(a, b, dims, passes):
    def f(x, y):
        return lax.dot_general(x, y, dims, preferred_element_type=F32)
    if passes == 1:
        return f(a.astype(BF16), b.astype(BF16))
    ah, al = _split2(a)
    bh, bl = _split2(b)
    return f(ah, bh) + f(al, bh) + f(ah, bl)


def _iota(shape, dim):
    return lax.broadcasted_iota(I32, shape, dim)


def _log_sigmoid(z):
    return jnp.minimum(z, 0.0) - jnp.log(1.0 + jnp.exp(-jnp.abs(z)))


def _adaln_kernel(c_ref, w_ref, b_ref, o_ref):
    c = c_ref[...]
    s = c * jax.nn.sigmoid(c)
    hi, lo = _split2(s)
    w = w_ref[0].astype(BF16)
    o_ref[0] = _dot(hi, w) + _dot(lo, w) + b_ref[0]


def _adaln(cond8, ada_w, ada_b):
    depth, d, d6 = ada_w.shape
    tn = min(d, 1024)
    return pl.pallas_call(
        _adaln_kernel,
        grid=(depth, d6 // tn),
        in_specs=[pl.BlockSpec((SUBLANES, d), lambda l, j: (0, 0)),
                  pl.BlockSpec((1, d, tn), lambda l, j: (l, 0, j)),
                  pl.BlockSpec((1, 1, tn), lambda l, j: (l, 0, j))],
        out_specs=pl.BlockSpec((1, SUBLANES, tn), lambda l, j: (l, 0, j)),
        out_shape=jax.ShapeDtypeStruct((depth, SUBLANES, d6), F32),
        compiler_params=_params("arbitrary", "arbitrary"),
        name="adaln",
    )(cond8, ada_w, ada_b.reshape(depth, 1, d6))


def _normed(x_ref, g_ref):
    x = x_ref[...]
    return x * lax.rsqrt(jnp.mean(x * x, axis=-1, keepdims=True) + NORM_EPS) * g_ref[...]


def _norm_plain_kernel(sh, sc, x_ref, g_ref, mod_ref, o_ref):
    h = _normed(x_ref, g_ref) * (1.0 + mod_ref[0, sc:sc + 1, :]) + mod_ref[0, sh:sh + 1, :]
    o_ref[...] = h.astype(o_ref.dtype)


def _norm_gate_kernel(sh, sc, x_ref, g_ref, mod_ref, w_ref, b_ref, o_ref, gate_ref):
    h = _normed(x_ref, g_ref) * (1.0 + mod_ref[0, sc:sc + 1, :]) + mod_ref[0, sh:sh + 1, :]
    o_ref[...] = h.astype(o_ref.dtype)
    z = _dot3(h, w_ref[...]) + b_ref[...]
    z = M_GATE_CAP * jnp.tanh(z / M_GATE_CAP)
    is_forget = ((_iota(z.shape, 1) // M_HEADS) % 2) == 1
    gate_ref[...] = jnp.where(is_forget, _log_sigmoid(z), z)


def _norm_router_kernel(sh, sc, x_ref, g_ref, mod_ref, w_ref, o_ref, aff_ref):
    h = _normed(x_ref, g_ref) * (1.0 + mod_ref[0, sc:sc + 1, :]) + mod_ref[0, sh:sh + 1, :]
    o_ref[...] = h
    z = _dot3(h, w_ref[...])
    e = jnp.exp(z - jnp.max(z, axis=-1, keepdims=True))
    aff_ref[...] = e / jnp.sum(e, axis=-1, keepdims=True)


def _norm_final_kernel(x_ref, g_ref, o_ref):
    o_ref[...] = _normed(x_ref, g_ref)


def _row_tile(n, t, want):
    tm = min(want, t)
    assert t % tm == 0 and n % tm == 0
    return tm


def _mod_spec(mod, tm, t):
    d = mod.shape[-1]
    if mod.shape[0] == 1:
        return pl.BlockSpec((1, 6, d), lambda i: (0, 0, 0))
    return pl.BlockSpec((1, 6, d), lambda i: ((i * tm) // t, 0, 0))


def _norm_call(kind, x, gain, mod, t, rows=(0, 1), w=None, b=None, out_dtype=F32):
    n, d = x.shape
    tm = _row_tile(n, t, 256)
    xs = pl.BlockSpec((tm, d), lambda i: (i, 0))
    gs = pl.BlockSpec((1, d), lambda i: (0, 0))
    ms = None if mod is None else _mod_spec(mod, tm, t)
    gain = gain.reshape(1, d)
    if kind == "final":
        return pl.pallas_call(_norm_final_kernel, grid=(n // tm,), in_specs=[xs, gs], out_specs=xs,
                              out_shape=jax.ShapeDtypeStruct((n, d), F32),
                              compiler_params=_params("arbitrary"), name="norm_final")(x, gain)
    sh, sc = rows
    if kind == "plain":
        return pl.pallas_call(functools.partial(_norm_plain_kernel, sh, sc), grid=(n // tm,),
                              in_specs=[xs, gs, ms], out_specs=xs,
                              out_shape=jax.ShapeDtypeStruct((n, d), out_dtype),
                              compiler_params=_params("arbitrary"), name="norm_plain")(x, gain, mod)
    ns = w.shape[1]
    ws = pl.BlockSpec((d, ns), lambda i: (0, 0))
    ps = pl.BlockSpec((tm, ns), lambda i: (i, 0))
    if kind == "gate":
        return pl.pallas_call(
            functools.partial(_norm_gate_kernel, sh, sc), grid=(n // tm,),
            in_specs=[xs, gs, ms, ws, pl.BlockSpec((1, ns), lambda i: (0, 0))],
            out_specs=[xs, ps],
            out_shape=[jax.ShapeDtypeStruct((n, d), out_dtype), jax.ShapeDtypeStruct((n, ns), F32)],
            compiler_params=_params("arbitrary"), name="norm_gate")(x, gain, mod, w, b.reshape(1, ns))
    assert kind == "router"
    return pl.pallas_call(
        functools.partial(_norm_router_kernel, sh, sc), grid=(n // tm,),
        in_specs=[xs, gs, ms, ws], out_specs=[xs, ps],
        out_shape=[jax.ShapeDtypeStruct((n, d), F32), jax.ShapeDtypeStruct((n, ns), F32)],
        compiler_params=_params("arbitrary"), name="norm_router")(x, gain, mod, w)


def _mm_kernel(x_ref, w_ref, o_ref):
    o_ref[...] = _dot(x_ref[...], w_ref[...])


def _mm(x, w, tn):
    n, k = x.shape
    m = w.shape[1]
    tm = min(512, n)
    tn = min(tn, m)
    return pl.pallas_call(
        _mm_kernel, grid=(m // tn, n // tm),
        in_specs=[pl.BlockSpec((tm, k), lambda j, i: (i, 0)), pl.BlockSpec((k, tn), lambda j, i: (0, j))],
        out_specs=pl.BlockSpec((tm, tn), lambda j, i: (i, j)),
        out_shape=jax.ShapeDtypeStruct((n, m), F32),
        compiler_params=_params("arbitrary", "arbitrary"), name="mm")(x, w)


def _mlstm_kernel(has_init, emit_state, *refs):
    q_ref, k_ref, v_ref, g_ref, gt_ref = refs[:5]
    pos = 5
    if has_init:
        c0_ref, n0_ref, m0_ref = refs[pos:pos + 3]
        pos += 3
    h_ref = refs[pos]
    pos += 1
    if emit_state:
        cout_ref, nout_ref, mout_ref = refs[pos:pos + 3]
        pos += 3
    c_scr, n_scr, m_scr = refs[pos:pos + 3]

    d = pl.program_id(1)
    c = pl.program_id(2)
    nc = pl.num_programs(2)
    fwd = d == 0
    heads = M_HEADS
    length = q_ref.shape[0]

    @pl.when(c == 0)
    def _():
        if has_init:
            c_scr[...] = c0_ref[0, 0]
            n_scr[...] = n0_ref[0, 0]
            m_scr[...] = m0_ref[0, 0]
        else:
            c_scr[...] = jnp.zeros_like(c_scr)
            n_scr[...] = jnp.zeros_like(n_scr)
            m_scr[...] = jnp.zeros_like(m_scr)

    sgn = jnp.where(fwd, 1, -1)
    row = _iota((length, length), 0)
    col = _iota((length, length), 1)
    mask = ((row - col) * sgn) >= 0
    mask_t = ((col - row) * sgn) >= 0
    scale = M_DK ** -0.5
    g = g_ref[...]
    gt = gt_ref[0]

    for h in range(heads):
        ig_row = jnp.where(fwd, gt[h:h + 1, :], gt[2 * heads + h:2 * heads + h + 1, :])
        lf_row = jnp.where(fwd, gt[heads + h:heads + h + 1, :], gt[3 * heads + h:3 * heads + h + 1, :])
        ig_col = jnp.where(fwd, g[:, h:h + 1], g[:, 2 * heads + h:2 * heads + h + 1])
        lf_col = jnp.where(fwd, g[:, heads + h:heads + h + 1], g[:, 3 * heads + h:3 * heads + h + 1])
        b_col = jnp.sum(jnp.where(mask, lf_row, 0.0), axis=1, keepdims=True)
        b_row = jnp.sum(jnp.where(mask_t, lf_col, 0.0), axis=0, keepdims=True)
        dmat = jnp.where(mask, b_col - b_row + ig_row, -jnp.inf)
        m_prev = m_scr[h:h + 1, 0:1]
        a = b_col + m_prev
        m_t = jnp.maximum(a, jnp.max(dmat, axis=1, keepdims=True))
        w_inter = jnp.exp(a - m_t)
        qh = q_ref[:, h * M_DK:(h + 1) * M_DK]
        kh = k_ref[:, h * M_DK:(h + 1) * M_DK] * scale
        vh = v_ref[:, h * M_DV:(h + 1) * M_DV]
        qb, kb, vb = qh.astype(BF16), kh.astype(BF16), vh.astype(BF16)
        s = _dot_nt(qb, kb) * jnp.exp(dmat - m_t)
        c_prev = c_scr[h]
        n_prev = n_scr[h:h + 1, :]
        num = w_inter * _dot(qb, c_prev.astype(BF16)) + _dot(s.astype(BF16), vb)
        den = w_inter * jnp.sum(qh * n_prev, axis=1, keepdims=True) + jnp.sum(s, axis=1, keepdims=True)
        h_ref[0, :, h * M_DV:(h + 1) * M_DV] = num / jnp.maximum(jnp.abs(den), jnp.exp(-m_t))

        b_last = jnp.sum(lf_col, axis=0, keepdims=True)
        g_col = b_last - b_col + ig_col
        m_new = jnp.maximum(b_last + m_prev, jnp.max(g_col, axis=0, keepdims=True))
        decay = jnp.exp(b_last + m_prev - m_new)
        wk = jnp.exp(g_col - m_new) * kh
        c_scr[h] = decay * c_prev + _dot_tn(wk.astype(BF16), vb)
        n_scr[h:h + 1, :] = decay * n_prev + jnp.sum(wk, axis=0, keepdims=True)
        m_scr[h:h + 1, :] = jnp.broadcast_to(m_new, (1, LANES))

    if emit_state:
        @pl.when(c == nc - 1)
        def _():
            cout_ref[0, 0] = c_scr[...]
            nout_ref[0, 0] = n_scr[...]
            mout_ref[0, 0] = m_scr[...]


def _mlstm_scan(proj, gates, b, t, init, emit_state):
    n = proj.shape[0]
    heads, dk, dv = M_HEADS, M_DK, M_DV
    qk, mv = heads * dk, heads * dv
    length = min(MLSTM_CHUNK, t)
    nc = t // length
    gates_t = gates.reshape(n // length, length, 4 * heads).transpose(0, 2, 1)

    def tile(bi, di, ci):
        return bi * nc + ci + di * (nc - 1 - 2 * ci)

    in_specs = [pl.BlockSpec((length, qk), lambda bi, di, ci: (tile(bi, di, ci), 0)),
                pl.BlockSpec((length, qk), lambda bi, di, ci: (tile(bi, di, ci), 1)),
                pl.BlockSpec((length, mv), lambda bi, di, ci: (tile(bi, di, ci), 1)),
                pl.BlockSpec((length, 4 * heads), lambda bi, di, ci: (tile(bi, di, ci), 0)),
                pl.BlockSpec((1, 4 * heads, length), lambda bi, di, ci: (tile(bi, di, ci), 0, 0))]
    args = [proj, proj, proj, gates, gates_t]
    state_specs = [pl.BlockSpec((1, 1, heads, dk, dv), lambda bi, di, ci: (bi, di, 0, 0, 0)),
                   pl.BlockSpec((1, 1, heads, dk), lambda bi, di, ci: (bi, di, 0, 0)),
                   pl.BlockSpec((1, 1, heads, LANES), lambda bi, di, ci: (bi, di, 0, 0))]
    if init is not None:
        c0, n0, m0 = init
        in_specs += state_specs
        args += [c0, n0, jnp.broadcast_to(m0[..., None], m0.shape + (LANES,))]
    out_specs = [pl.BlockSpec((1, length, mv), lambda bi, di, ci: (di, tile(bi, di, ci), 0))]
    out_shape = [jax.ShapeDtypeStruct((2, n, mv), F32)]
    if emit_state:
        out_specs += state_specs
        out_shape += [jax.ShapeDtypeStruct((b, 2, heads, dk, dv), F32),
                      jax.ShapeDtypeStruct((b, 2, heads, dk), F32),
                      jax.ShapeDtypeStruct((b, 2, heads, LANES), F32)]
    return pl.pallas_call(
        functools.partial(_mlstm_kernel, init is not None, emit_state),
        grid=(b, 2, nc), in_specs=in_specs, out_specs=out_specs, out_shape=out_shape,
        scratch_shapes=[pltpu.VMEM((heads, dk, dv), F32), pltpu.VMEM((heads, dk), F32),
                        pltpu.VMEM((heads, LANES), F32)],
        compiler_params=_params("arbitrary", "arbitrary", "arbitrary"), name="mlstm_scan")(*args)


def _mlstm_out_kernel(hs_ref, o_ref, gain_ref, w_ref, x_ref, mod_ref, out_ref):
    hs = hs_ref[0] + hs_ref[1]
    parts = []
    for h in range(M_HEADS):
        seg = hs[:, h * M_DV:(h + 1) * M_DV]
        parts.append(seg * lax.rsqrt(jnp.mean(seg * seg, axis=-1, keepdims=True) + NORM_EPS))
    hn = jnp.concatenate(parts, axis=1) * gain_ref[...]
    lhs = (jax.nn.sigmoid(o_ref[...]) * hn).astype(BF16)
    out_ref[...] = x_ref[...] + mod_ref[0, 2:3, :] * _dot(lhs, w_ref[...])


def _mlstm_out(hs2, proj, gain, w_out, x, mod, t):
    n, d = x.shape
    mv = M_HEADS * M_DV
    tm = _row_tile(n, t, 256)
    return pl.pallas_call(
        _mlstm_out_kernel, grid=(n // tm,),
        in_specs=[pl.BlockSpec((2, tm, mv), lambda i: (0, i, 0)),
                  pl.BlockSpec((tm, mv), lambda i: (i, 2)),
                  pl.BlockSpec((1, mv), lambda i: (0, 0)),
                  pl.BlockSpec((mv, d), lambda i: (0, 0)),
                  pl.BlockSpec((tm, d), lambda i: (i, 0)),
                  _mod_spec(mod, tm, t)],
        out_specs=pl.BlockSpec((tm, d), lambda i: (i, 0)),
        out_shape=jax.ShapeDtypeStruct((n, d), F32),
        compiler_params=_params("arbitrary"), name="mlstm_out")(hs2, proj, gain.reshape(1, mv), w_out, x, mod)


def _attn_qkv_kernel(use_rope, *refs):
    if use_rope:
        x_ref, w_ref, qg_ref, kg_ref, cos_ref, sin_ref, q_ref, k_ref, v_ref = refs
    else:
        x_ref, w_ref, qg_ref, kg_ref, q_ref, k_ref, v_ref = refs
    acc = _dot(x_ref[...], w_ref[...])
    qd, kvd = A_HEADS * A_HD, A_KV_HEADS * A_HD

    def head_norm(xh, gain):
        xh = xh * lax.rsqrt(jnp.mean(xh * xh, axis=-1, keepdims=True) + NORM_EPS) * gain
        if use_rope:
            xh = xh * cos_ref[...] + pltpu.roll(xh, A_HD // 2, axis=1) * sin_ref[...]
        return xh

    for h in range(A_HEADS):
        qh = head_norm(acc[:, h * A_HD:(h + 1) * A_HD], qg_ref[...])
        q_ref[:, h * A_HD:(h + 1) * A_HD] = (qh * (A_HD ** -0.5)).astype(BF16)
    for h in range(A_KV_HEADS):
        k_ref[:, h * A_HD:(h + 1) * A_HD] = head_norm(acc[:, qd + h * A_HD:qd + (h + 1) * A_HD], kg_ref[...])
    v_ref[...] = acc[:, qd + kvd:]


def _attn_qkv(hb, w, qgain, kgain, t, rope):
    n, d = hb.shape
    qd, kvd = A_HEADS * A_HD, A_KV_HEADS * A_HD
    tm = _row_tile(n, t, 256)
    in_specs = [pl.BlockSpec((tm, d), lambda i: (i, 0)),
                pl.BlockSpec((d, qd + 2 * kvd), lambda i: (0, 0)),
                pl.BlockSpec((1, A_HD), lambda i: (0, 0)),
                pl.BlockSpec((1, A_HD), lambda i: (0, 0))]
    args = [hb, w, qgain.reshape(1, A_HD), kgain.reshape(1, A_HD)]
    if rope is not None:
        in_specs += [pl.BlockSpec((tm, A_HD), lambda i: (i % (t // tm), 0))] * 2
        args += list(rope)
    return pl.pallas_call(
        functools.partial(_attn_qkv_kernel, rope is not None), grid=(n // tm,),
        in_specs=in_specs,
        out_specs=[pl.BlockSpec((tm, qd), lambda i: (i, 0)),
                   pl.BlockSpec((tm, kvd), lambda i: (i, 0)),
                   pl.BlockSpec((tm, kvd), lambda i: (i, 0))],
        out_shape=[jax.ShapeDtypeStruct((n, qd), BF16), jax.ShapeDtypeStruct((n, kvd), F32),
                   jax.ShapeDtypeStruct((n, kvd), F32)],
        compiler_params=_params("arbitrary"), name="attn_qkv")(*args)


def _attn_kernel(has_cache, *refs):
    if has_cache:
        q_ref, kn_ref, vn_ref, kc_ref, vc_ref, o_ref, k_scr, v_scr = refs
    else:
        q_ref, kn_ref, vn_ref, o_ref, k_scr, v_scr = refs
    group = A_HEADS // A_KV_HEADS
    tq = q_ref.shape[0]
    t_new = kn_ref.shape[0]
    past = k_scr.shape[0] - t_new

    @pl.when(pl.program_id(2) == 0)
    def _():
        if has_cache:
            k_scr[0:past, :] = kc_ref[...].astype(BF16)
            v_scr[0:past, :] = vc_ref[...].astype(BF16)
        k_scr[past:, :] = kn_ref[...].astype(BF16)
        v_scr[past:, :] = vn_ref[...].astype(BF16)

    q4 = jnp.concatenate([q_ref[:, g * A_HD:(g + 1) * A_HD] for g in range(group)], axis=0)
    s = _dot_nt(q4, k_scr[...])
    p = jnp.exp(s - jnp.max(s, axis=-1, keepdims=True))
    o = _dot(p.astype(BF16), v_scr[...]) / jnp.sum(p, axis=-1, keepdims=True)
    for g in range(group):
        o_ref[:, g * A_HD:(g + 1) * A_HD] = o[g * tq:(g + 1) * tq, :].astype(BF16)


def _attention(q, k, v, b, t, cache):
    n = q.shape[0]
    group = A_HEADS // A_KV_HEADS
    tq = min(64, t)
    nq = t // tq
    past = 0 if cache is None else cache[0].shape[0] // b
    in_specs = [pl.BlockSpec((tq, group * A_HD), lambda bi, hi, qi: (bi * nq + qi, hi)),
                pl.BlockSpec((t, A_HD), lambda bi, hi, qi: (bi, hi)),
                pl.BlockSpec((t, A_HD), lambda bi, hi, qi: (bi, hi))]
    args = [q, k, v]
    if cache is not None:
        in_specs += [pl.BlockSpec((past, A_HD), lambda bi, hi, qi: (bi, hi))] * 2
        args += list(cache)
    return pl.pallas_call(
        functools.partial(_attn_kernel, cache is not None), grid=(b, A_KV_HEADS, nq),
        in_specs=in_specs,
        out_specs=pl.BlockSpec((tq, group * A_HD), lambda bi, hi, qi: (bi * nq + qi, hi)),
        out_shape=jax.ShapeDtypeStruct((n, A_HEADS * A_HD), BF16),
        scratch_shapes=[pltpu.VMEM((past + t, A_HD), BF16), pltpu.VMEM((past + t, A_HD), BF16)],
        compiler_params=_params("arbitrary", "arbitrary", "arbitrary"), name="attention")(*args)


def _resid_mm_kernel(lhs_ref, w_ref, x_ref, mod_ref, out_ref):
    out_ref[...] = x_ref[...] + mod_ref[0, 2:3, :] * _dot(lhs_ref[...], w_ref[...])


def _resid_mm(lhs, w, x, mod, t):
    n, d = x.shape
    k = lhs.shape[1]
    tm = _row_tile(n, t, 256)
    return pl.pallas_call(
        _resid_mm_kernel, grid=(n // tm,),
        in_specs=[pl.BlockSpec((tm, k), lambda i: (i, 0)),
                  pl.BlockSpec((k, d), lambda i: (0, 0)),
                  pl.BlockSpec((tm, d), lambda i: (i, 0)),
                  _mod_spec(mod, tm, t)],
        out_specs=pl.BlockSpec((tm, d), lambda i: (i, 0)),
        out_shape=jax.ShapeDtypeStruct((n, d), F32),
        compiler_params=_params("arbitrary"), name="resid_mm")(lhs, w, x, mod)


def _token_shift_delta(h, hp_ref, hn_ref, i, tiles_per_seq):
    tm = h.shape[0]
    rowi = _iota((tm, 1), 0)
    ti = i % tiles_per_seq
    prev_row = jnp.where(ti == 0, 0.0, hp_ref[SUBLANES - 1:SUBLANES, :])
    next_row = jnp.where(ti == tiles_per_seq - 1, 0.0, hn_ref[0:1, :])
    h_prev = jnp.where(rowi == 0, prev_row, pltpu.roll(h, 1, axis=0))
    h_next = jnp.where(rowi == tm - 1, next_row, pltpu.roll(h, tm - 1, axis=0))
    return 0.5 * (h_prev + h_next) - h


def _shift_specs(tm, d, n, grid_rank):
    nb8 = n // SUBLANES
    per = tm // SUBLANES
    if grid_rank == 1:
        return [pl.BlockSpec((tm, d), lambda i: (i, 0)),
                pl.BlockSpec((SUBLANES, d), lambda i: (jnp.maximum(i * per - 1, 0), 0)),
                pl.BlockSpec((SUBLANES, d), lambda i: (jnp.minimum((i + 1) * per, nb8 - 1), 0))]
    return [pl.BlockSpec((tm, d), lambda j, i: (i, 0)),
            pl.BlockSpec((SUBLANES, d), lambda j, i: (jnp.maximum(i * per - 1, 0), 0)),
            pl.BlockSpec((SUBLANES, d), lambda j, i: (jnp.minimum((i + 1) * per, nb8 - 1), 0))]


def _rwkv_rkv_kernel(tiles_per_seq, h_ref, hp_ref, hn_ref, mix_ref, w_ref, o_ref):
    h = h_ref[...]
    xx = _token_shift_delta(h, hp_ref, hn_ref, pl.program_id(1), tiles_per_seq)
    o_ref[0] = _dot((h + xx * mix_ref[0]).astype(BF16), w_ref[0])


def _rwkv_rkv(h, mix3, w_rkv, t):
    n, d = h.shape
    tm = _row_tile(n, t, 256)
    return pl.pallas_call(
        functools.partial(_rwkv_rkv_kernel, t // tm), grid=(3, n // tm),
        in_specs=_shift_specs(tm, d, n, 2) + [pl.BlockSpec((1, 1, d), lambda j, i: (j, 0, 0)),
                                              pl.BlockSpec((1, d, d), lambda j, i: (j, 0, 0))],
        out_specs=pl.BlockSpec((1, tm, d), lambda j, i: (j, i, 0)),
        out_shape=jax.ShapeDtypeStruct((3, n, d), F32),
        compiler_params=_params("arbitrary", "arbitrary"), name="rwkv_rkv")(h, h, h, mix3, w_rkv)


def _rwkv_lora_kernel(tiles_per_seq, lora, h_ref, hp_ref, hn_ref, mix_ref, w1_ref, a1_ref, g1_ref,
                      w2_ref, a2_ref, g2_ref, w0_ref, a0_ref, lw_ref, asig_ref, g_ref):
    h = h_ref[...]
    xx = _token_shift_delta(h, hp_ref, hn_ref, pl.program_id(0), tiles_per_seq)
    xw = (h + xx * mix_ref[0:1, :]).astype(BF16)
    xa = (h + xx * mix_ref[1:2, :]).astype(BF16)
    xg = (h + xx * mix_ref[2:3, :]).astype(BF16)
    tw = jnp.tanh(_dot(xw, w1_ref[...])).astype(BF16)
    ta = _dot(xa, a1_ref[...]).astype(BF16)
    tg = jax.nn.sigmoid(_dot(xg, g1_ref[...])).astype(BF16)
    g_ref[...] = _dot(tg, g2_ref[...])
    for dr in range(2):
        z = -(w0_ref[dr] + _dot(tw[:, dr * lora:(dr + 1) * lora], w2_ref[dr]))
        softplus = jnp.maximum(z, 0.0) + jnp.log(1.0 + jnp.exp(-jnp.abs(z)))
        lw_ref[dr] = -jnp.exp(-softplus - 0.5)
        asig_ref[dr] = jax.nn.sigmoid(a0_ref[dr] + _dot(ta[:, dr * lora:(dr + 1) * lora], a2_ref[dr]))


def _rwkv_lora(h, mix3, w1c, a1c, g1, w2p, a2p, g2, w0, a0, t):
    n, d = h.shape
    tm = _row_tile(n, t, 256)
    lora = w2p.shape[1]
    gl = g1.shape[1]
    full2 = lambda shape: pl.BlockSpec(shape, lambda i: (0, 0))
    full3 = lambda shape: pl.BlockSpec(shape, lambda i: (0, 0, 0))
    return pl.pallas_call(
        functools.partial(_rwkv_lora_kernel, t // tm, lora), grid=(n // tm,),
        in_specs=_shift_specs(tm, d, n, 1) + [
            full2((3, d)), full2((d, 2 * lora)), full2((d, 2 * lora)), full2((d, gl)),
            full3((2, lora, d)), full3((2, lora, d)), full2((gl, d)), full3((2, 1, d)), full3((2, 1, d))],
        out_specs=[pl.BlockSpec((2, tm, d), lambda i: (0, i, 0)),
                   pl.BlockSpec((2, tm, d), lambda i: (0, i, 0)),
                   pl.BlockSpec((tm, d), lambda i: (i, 0))],
        out_shape=[jax.ShapeDtypeStruct((2, n, d), F32), jax.ShapeDtypeStruct((2, n, d), F32),
                   jax.ShapeDtypeStruct((n, d), F32)],
        compiler_params=_params("arbitrary"), name="rwkv_lora")(h, h, h, mix3, w1c, a1c, g1, w2p, a2p, g2, w0, a0)


def _rwkv_scan_kernel(has_init, emit_state, *refs):
    r_ref, k_ref, v_ref, lw_ref, a_ref, kk_ref, ka_ref, rk_ref = refs[:8]
    pos = 8
    if has_init:
        s0_ref = refs[pos]
        pos += 1
    y_ref, bonus_ref = refs[pos:pos + 2]
    pos += 2
    if emit_state:
        sout_ref = refs[pos]
        pos += 1
    s_scr, cum_scr = refs[pos:pos + 2]

    d = pl.program_id(1)
    c = pl.program_id(2)
    nc = pl.num_programs(2)
    fwd = d == 0
    sgn = jnp.where(fwd, 1, -1)
    length = r_ref.shape[1]
    npairs = s_scr.shape[0]
    hd = R_HD

    @pl.when(c == 0)
    def _():
        if has_init:
            s_scr[...] = s0_ref[0, 0]
        else:
            s_scr[...] = jnp.zeros_like(s_scr)

    ti = _iota((length, length), 0)
    si = _iota((length, length), 1)
    tri = (((ti - si) * sgn) >= 0).astype(BF16)
    l1, l2, l3 = _split3(lw_ref[0])
    cum_scr[...] = _dot(tri, l1) + _dot(tri, l2) + _dot(tri, l3)

    gl = 2 * length
    rr = _iota((gl, gl), 0)
    cc = _iota((gl, gl), 1)
    same = (rr // length) == (cc // length)
    e = ((rr % length) - (cc % length)) * sgn
    strict = same & (e > 0)
    incl = same & (e >= 0)
    eye = (rr == cc).astype(F32)
    lane = _iota((1, LANES), 1)
    m0 = (lane < hd).astype(F32)
    m1 = 1.0 - m0
    lr = _iota((LANES, LANES), 0)
    lc = _iota((LANES, LANES), 1)
    ones_blk = ((lr // hd) == (lc // hd)).astype(BF16)

    def segsum(x):
        hi, lo = _split2(x)
        return _dot(hi, ones_blk) + _dot(lo, ones_blk)

    def stack(x):
        return jnp.concatenate([x * m0, x * m1], axis=0)

    def smm(x, y, dims):
        return _mm_passes(x, y, dims, RWKV_SOLVE_PASSES)

    tr = rr % length
    tc = cc % length
    base = 4
    base_mask = (tr // base) == (tc // base)
    level_masks = []
    size = base
    while size < length:
        level_masks.append(((tr // (2 * size)) == (tc // (2 * size))) & ((tr // size) != (tc // size)))
        size *= 2

    def load_pair(p):
        ds = pl.ds(pl.multiple_of(p * LANES, LANES), LANES)
        return dict(ds=ds, p=p, r=r_ref[0, :, ds], k=k_ref[0, :, ds], v=v_ref[0, :, ds], lw=lw_ref[0, :, ds],
                    a=a_ref[0, :, ds], cum=cum_scr[:, ds], kkp=kk_ref[:, ds], kap=ka_ref[:, ds],
                    rkp=rk_ref[:, ds], st=s_scr[p])

    def each(fn, *lists):
        return [fn(*args) for args in zip(*lists)]

    def solve_group(xs):
        get = lambda name: [x[name] for x in xs]
        r, k, v, lw, a, cum, st = (get(n) for n in ("r", "k", "v", "lw", "a", "cum", "st"))
        kk = each(lambda k_, p_: k_ * p_, k, get("kkp"))
        ss = each(lambda q: segsum(q * q), kk)
        kk = each(lambda q, s_: q * lax.rsqrt(jnp.maximum(s_, 1e-24)), kk, ss)
        kd = each(lambda k_, a_, p_: k_ * (1.0 + (a_ - 1.0) * p_), k, a, get("kap"))
        bb = each(lambda q, a_: q * a_, kk, a)
        tot = each(lambda c_: jnp.where(fwd, c_[length - 1:length, :], c_[0:1, :]), cum)
        e_n = each(lambda c_: jnp.exp(-c_), cum)
        sa = each(lambda q, c_, l_: stack(-q * jnp.exp(c_ - l_)), kk, cum, lw)
        sr_full = each(lambda r_, c_: r_ * jnp.exp(c_), r, cum)
        sr = each(stack, sr_full)
        sb = each(lambda b_, e_: stack(b_ * e_), bb, e_n)
        sk = each(lambda k_, e_: stack(k_ * e_), kd, e_n)
        vexp = each(stack, v)
        m_ab = each(lambda x_, y_: jnp.where(strict, smm(x_, y_, _NT), 0.0), sa, sb)
        m_ak = each(lambda x_, y_: jnp.where(strict, smm(x_, y_, _NT), 0.0), sa, sk)
        rhs1 = each(lambda x_, s_: smm(x_, s_, _NT), sa, st)
        rhs2 = each(lambda m_, v_: smm(m_, v_, _NN), m_ak, vexp)
        rhs = each(lambda x_, y_: x_ + y_, rhs1, rhs2)
        d0 = each(lambda m_: jnp.where(base_mask, m_, 0.0), m_ab)
        sq = each(lambda d_: smm(d_, d_, _NN), d0)
        inv = each(lambda d_: eye + d_, d0)
        inv = each(lambda i_, q_: i_ + smm(i_, q_, _NN), inv, sq)
        for lm in level_masks:
            half = each(lambda i_, m_: smm(i_, jnp.where(lm, m_, 0.0), _NN), inv, m_ab)
            inv = each(lambda i_, h_: i_ + smm(h_, i_, _NN), inv, half)
        uexp = each(lambda i_, r_: smm(i_, r_, _NN), inv, rhs)
        t_rb = each(lambda x_, y_: jnp.where(incl, smm(x_, y_, _NT), 0.0), sr, sb)
        t_rk = each(lambda x_, y_: jnp.where(incl, smm(x_, y_, _NT), 0.0), sr, sk)
        fold = lambda t_: t_[0:length] + t_[length:gl]
        y1 = each(lambda r_, s_: smm(r_, s_, _NT), sr_full, st)
        y2 = each(lambda t_, u_: smm(fold(t_), u_, _NN), t_rb, uexp)
        y3 = each(lambda t_, v_: smm(fold(t_), v_, _NN), t_rk, vexp)
        y = each(lambda p_, q_, w_: p_ + q_ + w_, y1, y2, y3)
        bs = each(lambda r_, k_, p_: segsum(r_ * k_ * p_), r, kd, get("rkp"))
        bonus = each(lambda s_, v_: s_ * v_, bs, v)
        e_e = each(lambda t_, c_: jnp.exp(t_ - c_), tot, cum)
        s1 = each(lambda u_, b_, e_: smm(u_, stack(b_ * e_), _TN), uexp, bb, e_e)
        s2 = each(lambda v_, k_, e_: smm(v_, stack(k_ * e_), _TN), vexp, kd, e_e)
        s_new = each(lambda s_, t_, p_, q_: s_ * jnp.exp(t_) + p_ + q_, st, tot, s1, s2)
        return y, bonus, s_new

    npg = min(RWKV_PAIRS_PER_GROUP, npairs)

    def group(gi, carry):
        xs = [load_pair(gi * npg + u) for u in range(npg)]
        ys, bonuses, states = solve_group(xs)
        for x, y, bonus, s_new in zip(xs, ys, bonuses, states):
            y_ref[0, :, x["ds"]] = y
            bonus_ref[0, :, x["ds"]] = bonus
            s_scr[x["p"]] = s_new
        return carry

    lax.fori_loop(0, npairs // npg, group, 0)

    if emit_state:
        @pl.when(c == nc - 1)
        def _():
            sout_ref[0, 0] = s_scr[...]


def _rwkv_scan(rkv, lw, asig, kk, ka, rk, b, t, s0, emit_state):
    n, d = rkv.shape[1:]
    length = RWKV_CHUNK
    nc = t // length
    npairs = d // LANES

    def tile(bi, di, ci):
        return bi * nc + ci + di * (nc - 1 - 2 * ci)

    def plane(j):
        return pl.BlockSpec((1, length, d), lambda bi, di, ci: (j, tile(bi, di, ci), 0))

    dirspec = pl.BlockSpec((1, length, d), lambda bi, di, ci: (di, tile(bi, di, ci), 0))
    vec = pl.BlockSpec((1, d), lambda bi, di, ci: (0, 0))
    sspec = pl.BlockSpec((1, 1, npairs, LANES, LANES), lambda bi, di, ci: (bi, di, 0, 0, 0))
    in_specs = [plane(0), plane(1), plane(2), dirspec, dirspec, vec, vec, vec]
    args = [rkv, rkv, rkv, lw, asig, kk.reshape(1, d), ka.reshape(1, d), rk.reshape(1, d)]
    if s0 is not None:
        in_specs.append(sspec)
        args.append(s0)
    out_specs = [dirspec, dirspec]
    out_shape = [jax.ShapeDtypeStruct((2, n, d), F32), jax.ShapeDtypeStruct((2, n, d), F32)]
    if emit_state:
        out_specs.append(sspec)
        out_shape.append(jax.ShapeDtypeStruct((b, 2, npairs, LANES, LANES), F32))
    return pl.pallas_call(
        functools.partial(_rwkv_scan_kernel, s0 is not None, emit_state),
        grid=(b, 2, nc), in_specs=in_specs, out_specs=out_specs, out_shape=out_shape,
        scratch_shapes=[pltpu.VMEM((npairs, LANES, LANES), F32), pltpu.VMEM((length, d), F32)],
        compiler_params=_params("arbitrary", "arbitrary", "arbitrary"), name="rwkv_scan")(*args)


def _rwkv_out_kernel(y_ref, bonus_ref, g_ref, lnw_ref, lnb_ref, w_ref, x_ref, mod_ref, out_ref):
    y = y_ref[0] + y_ref[1]
    lr = _iota((LANES, LANES), 0)
    lc = _iota((LANES, LANES), 1)
    mean_blk = ((lr // R_HD) == (lc // R_HD)).astype(BF16)

    def segmean(x):
        hi, lo = _split2(x)
        return (_dot(hi, mean_blk) + _dot(lo, mean_blk)) * (1.0 / R_HD)

    parts = []
    for j in range(y.shape[1] // LANES):
        yj = y[:, j * LANES:(j + 1) * LANES]
        cen = yj - segmean(yj)
        parts.append(cen * lax.rsqrt(segmean(cen * cen) + R_LN_EPS))
    yn = jnp.concatenate(parts, axis=1) * lnw_ref[...] + lnb_ref[...] + bonus_ref[0] + bonus_ref[1]
    lhs = (yn * g_ref[...]).astype(BF16)
    out_ref[...] = x_ref[...] + mod_ref[0, 2:3, :] * _dot(lhs, w_ref[...])


def _rwkv_out(y2, bonus2, g, lnw, lnb, w_out, x, mod, t):
    n, d = x.shape
    tm = _row_tile(n, t, 256)
    two = pl.BlockSpec((2, tm, d), lambda i: (0, i, 0))
    rows = pl.BlockSpec((tm, d), lambda i: (i, 0))
    vec = pl.BlockSpec((1, d), lambda i: (0, 0))
    return pl.pallas_call(
        _rwkv_out_kernel, grid=(n // tm,),
        in_specs=[two, two, rows, vec, vec, pl.BlockSpec((d, d), lambda i: (0, 0)), rows,
                  _mod_spec(mod, tm, t)],
        out_specs=rows, out_shape=jax.ShapeDtypeStruct((n, d), F32),
        compiler_params=_params("arbitrary"), name="rwkv_out")(
            y2, bonus2, g, lnw.reshape(1, d), lnb.reshape(1, d), w_out, x, mod)


def _route_kernel(cap, aff_ref, idx_ref, gate_ref, pos_ref, tot_ref, off_ref):
    aff = aff_ref[0]
    nb = aff.shape[0]
    bits = lax.bitcast_convert_type(aff, I32)
    tok = _iota(aff.shape, 0) * LANES + _iota(aff.shape, 1)

    def count(m):
        x = jnp.sum(m.astype(F32), axis=1, keepdims=True)
        return jnp.sum(x, axis=0, keepdims=True)

    def value_step(i, cur):
        cand = cur | (jnp.int32(1) << (30 - i))
        return jnp.where(count(bits >= cand) >= cap, cand, cur)

    thr = lax.fori_loop(0, 31, value_step, jnp.zeros((1, 1), I32))
    above = bits > thr
    equal = bits == thr
    need = cap - count(above)

    def index_step(i, cur):
        cand = cur + (jnp.int32(1) << (14 - i))
        return jnp.where(count(equal & (tok < cand)) < need, cand, cur)

    last_tie = lax.fori_loop(0, 15, index_step, jnp.zeros((1, 1), I32))
    sel = above | (equal & (tok <= last_tie))
    self32 = sel.astype(F32)
    selb = self32.astype(BF16)

    li = _iota((LANES, LANES), 0)
    lj = _iota((LANES, LANES), 1)
    upper_incl = (li <= lj).astype(BF16)
    cl = _dot(selb, upper_incl)
    ones8 = jnp.ones((SUBLANES, LANES), BF16)
    tot_row = _dot_nt(ones8, selb)
    bi = _iota((nb, nb), 0)
    bj = _iota((nb, nb), 1)
    offi_row = _dot(tot_row.astype(BF16), (bi <= bj).astype(BF16))
    offx_row = offi_row - tot_row
    tot_col = jnp.broadcast_to(cl[:, LANES - 1:LANES], (nb, LANES)).astype(BF16)
    offx_col = _dot((bj < bi).astype(BF16), tot_col)
    pos_ref[0] = jnp.where(sel, offx_col + cl - 1.0, -1.0).astype(I32)
    tot_ref[0] = tot_row
    off_ref[0] = offx_row

    pcol = _iota((cap, 1), 0).astype(F32)
    before = offi_row[0:1, :] <= pcol
    blk = jnp.sum(before.astype(F32), axis=1, keepdims=True)
    base = jnp.sum(jnp.where(before, tot_row[0:1, :], 0.0), axis=1, keepdims=True)
    onehot = (_iota((cap, nb), 1).astype(F32) == blk).astype(BF16)
    rowcnt = _dot(onehot, cl.astype(BF16))
    rank = pcol - base
    lane = jnp.sum((rowcnt <= rank).astype(F32), axis=1, keepdims=True)
    idx_ref[0] = (blk * LANES + lane).astype(I32)
    a1, a2, a3 = _split3(aff)
    rowaff = _dot(onehot, a1) + _dot(onehot, a2) + _dot(onehot, a3)
    gate_ref[0] = jnp.sum(jnp.where(_iota((cap, LANES), 1).astype(F32) == lane, rowaff, 0.0),
                          axis=1, keepdims=True)


def _route(aff3, cap):
    e, nb, _ = aff3.shape
    return pl.pallas_call(
        functools.partial(_route_kernel, cap), grid=(e,),
        in_specs=[pl.BlockSpec((1, nb, LANES), lambda i: (i, 0, 0))],
        out_specs=[pl.BlockSpec((1, cap, 1), lambda i: (i, 0, 0)),
                   pl.BlockSpec((1, cap, 1), lambda i: (i, 0, 0)),
                   pl.BlockSpec((1, nb, LANES), lambda i: (i, 0, 0)),
                   pl.BlockSpec((1, SUBLANES, nb), lambda i: (i, 0, 0)),
                   pl.BlockSpec((1, SUBLANES, nb), lambda i: (i, 0, 0))],
        out_shape=[jax.ShapeDtypeStruct((e, cap, 1), I32), jax.ShapeDtypeStruct((e, cap, 1), F32),
                   jax.ShapeDtypeStruct((e, nb, LANES), I32),
                   jax.ShapeDtypeStruct((e, SUBLANES, nb), F32), jax.ShapeDtypeStruct((e, SUBLANES, nb), F32)],
        compiler_params=_params("arbitrary"), name="moe_route")(aff3)


def _row_copy(x_hbm, xbuf, sem, slot, tok, r):
    return pltpu.make_async_copy(x_hbm.at[pl.ds(tok, 1)], xbuf.at[pl.ds(r, 1)], sem.at[slot])


def _expert_kernel(idx_ref, idx_next_ref, x_hbm, gate_ref, wg_ref, wu_ref, wd_ref, o_ref, xbuf0, xbuf1, sem):
    tm = xbuf0.shape[0]
    bufs = (xbuf0, xbuf1)
    step = pl.program_id(0) * pl.num_programs(1) + pl.program_id(1)
    last = pl.num_programs(0) * pl.num_programs(1) - 1

    def start_rows(ids_ref, slot):
        for r in range(tm):
            _row_copy(x_hbm, bufs[slot], sem, slot, ids_ref[0, 0, r], r).start()

    def wait_rows(slot):
        for r in range(tm):
            _row_copy(x_hbm, bufs[slot], sem, slot, 0, r).wait()

    @pl.when(step == 0)
    def _():
        start_rows(idx_ref, 0)

    def run(slot):
        start_rows(idx_next_ref, 1 - slot)
        wait_rows(slot)
        xb = bufs[slot][...].astype(BF16)
        hg = _dot(xb, wg_ref[0, 0])
        hu = _dot(xb, wu_ref[0, 0])
        hid = (hg * jax.nn.sigmoid(hg) * hu).astype(BF16)
        o_ref[...] = (_dot(hid, wd_ref[0, 0]) * gate_ref[...]).astype(o_ref.dtype)

        @pl.when(step == last)
        def _():
            wait_rows(1 - slot)

    for slot in range(2):
        @pl.when(step % 2 == slot)
        def _():
            run(slot)


def _experts(hffn, idx, gate, wg, wu, wd, layer, cap):
    n, d = hffn.shape
    _, e, _, f = wg.shape
    tm = min(256, cap)
    nt = cap // tm
    idx3 = idx.reshape(e * nt, 1, tm)
    last = e * nt - 1
    return pl.pallas_call(
        _expert_kernel, grid=(e, nt),
        in_specs=[pl.BlockSpec((1, 1, tm), lambda ei, ti: (ei * nt + ti, 0, 0), memory_space=pltpu.SMEM),
                  pl.BlockSpec((1, 1, tm), lambda ei, ti: (jnp.minimum(ei * nt + ti + 1, last), 0, 0),
                               memory_space=pltpu.SMEM),
                  pl.BlockSpec(memory_space=pl.ANY),
                  pl.BlockSpec((tm, 1), lambda ei, ti: (ei * nt + ti, 0)),
                  pl.BlockSpec((1, 1, d, f), lambda ei, ti: (layer, ei, 0, 0)),
                  pl.BlockSpec((1, 1, d, f), lambda ei, ti: (layer, ei, 0, 0)),
                  pl.BlockSpec((1, 1, f, d), lambda ei, ti: (layer, ei, 0, 0))],
        out_specs=pl.BlockSpec((tm, d), lambda ei, ti: (ei * nt + ti, 0)),
        out_shape=jax.ShapeDtypeStruct((e * cap, d), BF16),
        scratch_shapes=[pltpu.VMEM((tm, d), F32), pltpu.VMEM((tm, d), F32), pltpu.SemaphoreType.DMA((2,))],
        compiler_params=_params("arbitrary", "arbitrary"), name="moe_experts")(
            idx3, idx3, hffn, gate.reshape(e * cap, 1), wg, wu, wd)


def _slab_copy(ye_hbm, slab, sem, buf, src_row, slot):
    return pltpu.make_async_copy(ye_hbm.at[pl.ds(pl.multiple_of(src_row, PACKED_ROWS), SLAB_CHUNK)],
                                 slab.at[buf, pl.ds(pl.multiple_of(slot * SLAB_CHUNK, SLAB_CHUNK), SLAB_CHUNK)],
                                 sem.at[buf])


def _combine_kernel(n_exp, meta_ref, meta_next_ref, x_ref, mod_ref, pos_ref, ye_hbm, out_ref, slab, acc, sem):
    tm = x_ref.shape[0]
    i = pl.program_id(0)
    buf = i % 2
    total = meta_ref[0, 0, 0]
    total_next = jnp.where(i + 1 < pl.num_programs(0), meta_next_ref[0, 0, 0], 0)

    def start_chunks(ref, count, b):
        def body(g, carry):
            _slab_copy(ye_hbm, slab, sem, b, ref[0, 0, 1 + n_exp + g], g).start()
            return carry
        lax.fori_loop(0, count, body, 0)

    @pl.when(i == 0)
    def _():
        slab[...] = jnp.zeros_like(slab)
        start_chunks(meta_ref, total, 0)

    start_chunks(meta_next_ref, total_next, 1 - buf)

    def wait(g, carry):
        _slab_copy(ye_hbm, slab, sem, buf, 0, g).wait()
        return carry

    pos = pos_ref[...]
    sub_e = _iota((n_exp, 1), 0)
    shift = jnp.zeros((n_exp, 1), I32)
    for e in range(n_exp):
        shift = jnp.where(sub_e == e, meta_ref[0, 0, 1 + e], shift)
    target = jnp.where(pos >= 0, pos + shift, -1)
    acc[...] = jnp.zeros_like(acc)
    lax.fori_loop(0, total, wait, 0)

    kc_rows = 2 * LANES
    chunks_per_kc = kc_rows // SLAB_CHUNK

    def kbody(kc, carry):
        base = pl.multiple_of(kc * kc_rows, kc_rows)
        rows = slab[buf, pl.ds(base, kc_rows), :]
        slab_row = _iota((kc_rows, tm), 0) + base
        place_t = target[0:1, :] == slab_row
        for e in range(1, n_exp):
            place_t = place_t | (target[e:e + 1, :] == slab_row)
        acc[...] += _dot_tn(place_t.astype(F32).astype(BF16), rows)
        return carry

    lax.fori_loop(0, (total + chunks_per_kc - 1) // chunks_per_kc, kbody, 0)
    out_ref[...] = x_ref[...] + mod_ref[0, 5:6, :] * acc[...]


def _combine(x, mod, pos_t, meta, ye, t, max_chunks):
    n, d = x.shape
    tm = LANES
    n_exp = N_EXPERTS
    ntiles = n // tm
    kc_rows = 2 * LANES
    slab_rows = -(-(max_chunks * SLAB_CHUNK) // kc_rows) * kc_rows
    return pl.pallas_call(
        functools.partial(_combine_kernel, n_exp), grid=(ntiles,),
        in_specs=[pl.BlockSpec((1, 1, meta.shape[2]), lambda i: (i, 0, 0), memory_space=pltpu.SMEM),
                  pl.BlockSpec((1, 1, meta.shape[2]), lambda i: (jnp.minimum(i + 1, ntiles - 1), 0, 0),
                               memory_space=pltpu.SMEM),
                  pl.BlockSpec((tm, d), lambda i: (i, 0)),
                  _mod_spec(mod, tm, t),
                  pl.BlockSpec((n_exp, tm), lambda i: (0, i)),
                  pl.BlockSpec(memory_space=pl.ANY)],
        out_specs=pl.BlockSpec((tm, d), lambda i: (i, 0)),
        out_shape=jax.ShapeDtypeStruct((n, d), F32),
        scratch_shapes=[pltpu.VMEM((2, slab_rows, d), BF16), pltpu.VMEM((tm, d), F32),
                        pltpu.SemaphoreType.DMA((2,))],
        compiler_params=_params("arbitrary"), name="moe_combine")(meta, meta, x, mod, pos_t, ye)


def _moe(x, mod, t, norm2, w_router, wg, wu, wd, layer):
    n, d = x.shape
    n_exp = N_EXPERTS
    cap = (EC_CAPACITY_FACTOR * n) // n_exp
    hffn, aff = _norm_call("router", x, norm2, mod, t, rows=(3, 4), w=w_router)
    npad = ROUTE_BLOCKS * LANES
    assert n <= npad and n % LANES == 0 and cap % SLAB_CHUNK == 0
    aff_t = jnp.pad(aff.T, ((0, 0), (0, npad - n)), constant_values=-1.0)
    idx, gate, pos, tot, off = _route(aff_t.reshape(n_exp, ROUTE_BLOCKS, LANES), cap)
    ye = _experts(hffn, idx, gate, wg, wu, wd, layer, cap)
    ntiles = n // LANES
    span = min(LANES + PACKED_ROWS + SLAB_CHUNK - (LANES + PACKED_ROWS) % SLAB_CHUNK, cap)
    per_expert = span // SLAB_CHUNK
    max_chunks = n_exp * per_expert
    start = off[:, 0, :ntiles].T.astype(I32)
    cnt = tot[:, 0, :ntiles].T.astype(I32)
    start_al = jnp.minimum((start // PACKED_ROWS) * PACKED_ROWS, cap - span)
    nch = jnp.where(cnt > 0, (start - start_al + cnt + SLAB_CHUNK - 1) // SLAB_CHUNK, 0)
    cend = jnp.cumsum(nch, axis=1)
    cstart = cend - nch
    slots = jnp.arange(max_chunks, dtype=I32)
    owns = (cstart[:, None, :] <= slots[None, :, None]) & (slots[None, :, None] < cend[:, None, :])
    base = start_al + jnp.arange(n_exp, dtype=I32)[None, :] * cap
    src = jnp.sum(jnp.where(owns, (base - cstart * SLAB_CHUNK)[:, None, :], 0), axis=2) + jnp.where(
        jnp.any(owns, axis=2), slots[None, :] * SLAB_CHUNK, 0)
    meta = jnp.concatenate([cend[:, -1:], cstart * SLAB_CHUNK - start_al, src], axis=1)
    return _combine(x, mod, pos.reshape(n_exp, npad), meta.reshape(ntiles, 1, 1 + n_exp + max_chunks), ye, t,
                    max_chunks)


def _rope_tables(t):
    n_rows = t // GRID_W
    axis_dim = A_HD // 2
    row = jnp.repeat(jnp.arange(n_rows, dtype=F32), GRID_W)
    col = jnp.tile(jnp.arange(GRID_W, dtype=F32), n_rows)
    inv_freq = ROPE_THETA ** (-jnp.arange(0, axis_dim, 2, dtype=F32) / axis_dim)
    ang = jnp.concatenate([row[:, None] * inv_freq, col[:, None] * inv_freq], axis=-1)
    cos, sin = jnp.cos(ang), jnp.sin(ang)
    return jnp.concatenate([cos, cos], axis=-1), jnp.concatenate([-sin, sin], axis=-1)


def _pad_lora(w1, w2):
    r = w1.shape[2]
    rp = -(-r // LANES) * LANES
    w1p = jnp.pad(w1, ((0, 0), (0, 0), (0, rp - r)))
    w2p = jnp.pad(w2, ((0, 0), (0, rp - r), (0, 0)))
    return jnp.concatenate([w1p[0], w1p[1]], axis=1).astype(BF16), w2p.astype(BF16)


def _pair_state(s):
    b, _, heads, hd, _ = s.shape
    sp = s.reshape(b, 2, heads // 2, 2, hd, hd)
    out = jnp.zeros((b, 2, heads // 2, 2 * hd, 2 * hd), F32)
    out = out.at[:, :, :, :hd, :hd].set(sp[:, :, :, 0])
    return out.at[:, :, :, hd:, hd:].set(sp[:, :, :, 1])


def _unpair_state(sp):
    b, _, npairs, _, _ = sp.shape
    hd = R_HD
    s = jnp.stack([sp[:, :, :, :hd, :hd], sp[:, :, :, hd:, hd:]], axis=3)
    return s.reshape(b, 2, 2 * npairs, hd, hd)


def _trunk(x, mod_all, b, t, latent_states, p):
    n, d = x.shape
    latent = latent_states is not None
    produced = []
    for i in range(DEPTH):
        kind, j = i % N_MIXERS, i // N_MIXERS
        mod = mod_all[i]
        if kind == 0:
            hb, gates = _norm_call("gate", x, p["norm1"][i], mod, t, w=p["m_gate_w"][j], b=p["m_gate_b"][j],
                                   out_dtype=BF16)
            proj = _mm(hb, p["m_qkvo"][j], 2048)
            res = _mlstm_scan(proj, gates, b, t, latent_states[i] if latent else None, not latent)
            if not latent:
                produced += [res[1], res[2], res[3][..., 0]]
            x = _mlstm_out(res[0], proj, p["m_hnorm"][j], p["m_out"][j], x, mod, t)
        elif kind == 1:
            hb = _norm_call("plain", x, p["norm1"][i], mod, t, out_dtype=BF16)
            q, k, v = _attn_qkv(hb, p["a_qkv"][j], p["a_qnorm"][j], p["a_knorm"][j], t,
                                _rope_tables(t) if latent else None)
            cache = None
            if latent:
                ck, cv = latent_states[i]
                kvd = A_KV_HEADS * A_HD
                cache = (ck.reshape(-1, kvd), cv.reshape(-1, kvd))
            else:
                produced += [k.reshape(b, t, A_KV_HEADS, A_HD), v.reshape(b, t, A_KV_HEADS, A_HD)]
            o = _attention(q, k, v, b, t, cache)
            x = _resid_mm(o, p["a_out"][j], x, mod, t)
        else:
            h = _norm_call("plain", x, p["norm1"][i], mod, t)
            mix = p["r_mix"][j]
            rkv = _rwkv_rkv(h, mix[jnp.array([0, 2, 3])].reshape(3, 1, d), p["r_rkv"][j], t)
            lw, asig, g = _rwkv_lora(h, mix[jnp.array([1, 4, 5])], p["r_w1c"][j], p["r_a1c"][j], p["r_g1"][j],
                                     p["r_w2p"][j], p["r_a2p"][j], p["r_g2"][j],
                                     p["r_w0"][j].reshape(2, 1, d), p["r_a0"][j].reshape(2, 1, d), t)
            s0 = _pair_state(latent_states[i][0]) if latent else None
            res = _rwkv_scan(rkv, lw, asig, p["r_kk"][j], p["r_ka"][j], p["r_rk"][j].reshape(-1), b, t, s0,
                             not latent)
            if not latent:
                produced.append(_unpair_state(res[2]))
            x = _rwkv_out(res[0], res[1], g, p["r_lnx_w"][j], p["r_lnx_b"][j], p["r_out"][j], x, mod, t)
        x = _moe(x, mod, t, p["norm2"][i], p["router"][i], p["exp_gate"], p["exp_up"], p["exp_down"], i)
    return _norm_call("final", x, p["final_norm"], None, t), produced


def kernel(x_prompt, x_sample, state_l0_C, state_l0_n, state_l0_m, cache_l1_k, cache_l1_v, state_l2_S,
           state_l3_C, state_l3_n, state_l3_m, c, c_ctx, ada_w, ada_b, norm1, norm2, router, exp_gate,
           exp_up, exp_down, m_qkvo, m_gate_w, m_gate_b, m_hnorm, m_out, a_qkv, a_qnorm, a_knorm, a_out,
           r_mix, r_rkv, r_w0, r_w1, r_w2, r_a0, r_a1, r_a2, r_g1, r_g2, r_kk, r_ka, r_rk, r_lnx_w,
           r_lnx_b, r_out, final_norm):
    bc, tc, d = x_prompt.shape
    bl, tl, _ = x_sample.shape
    depth = ada_w.shape[0]
    assert bl + 1 <= SUBLANES

    cond8 = jnp.zeros((SUBLANES, d), F32).at[0].set(c_ctx).at[1:1 + bl].set(c)
    mod = _adaln(cond8, ada_w, ada_b).reshape(depth, SUBLANES, 6, d)
    mod_ctx = mod[:, 0:1]
    mod_lat = mod[:, 1:1 + bl]

    w1c, w2p, a1c, a2p = [], [], [], []
    for j in range(r_w1.shape[0]):
        w1, w2 = _pad_lora(r_w1[j], r_w2[j])
        a1, a2 = _pad_lora(r_a1[j], r_a2[j])
        w1c.append(w1), w2p.append(w2), a1c.append(a1), a2p.append(a2)
    p = dict(norm1=norm1, norm2=norm2, router=router, final_norm=final_norm,
             exp_gate=exp_gate.astype(BF16), exp_up=exp_up.astype(BF16), exp_down=exp_down.astype(BF16),
             m_qkvo=m_qkvo.astype(BF16), m_gate_w=m_gate_w, m_gate_b=m_gate_b, m_hnorm=m_hnorm,
             m_out=m_out.astype(BF16), a_qkv=a_qkv.astype(BF16), a_qnorm=a_qnorm, a_knorm=a_knorm,
             a_out=a_out.astype(BF16), r_mix=r_mix, r_rkv=r_rkv.astype(BF16), r_w0=r_w0, r_a0=r_a0,
             r_w1c=w1c, r_w2p=w2p, r_a1c=a1c, r_a2p=a2p, r_g1=r_g1.astype(BF16), r_g2=r_g2.astype(BF16),
             r_kk=r_kk, r_ka=r_ka, r_rk=r_rk, r_lnx_w=r_lnx_w, r_lnx_b=r_lnx_b, r_out=r_out.astype(BF16))

    y_prompt, new_state = _trunk(x_prompt.reshape(bc * tc, d), mod_ctx, bc, tc, None, p)
    lat_states = [(state_l0_C, state_l0_n, state_l0_m), (cache_l1_k, cache_l1_v), (state_l2_S,),
                  (state_l3_C, state_l3_n, state_l3_m)]
    y_sample, _ = _trunk(x_sample.reshape(bl * tl, d), mod_lat, bl, tl, lat_states, p)
    return (y_prompt.reshape(bc, tc, d), y_sample.reshape(bl, tl, d), *new_state)
```

```python
import functools

import jax
import jax.numpy as jnp
from jax import lax
from jax.experimental import pallas as pl
from jax.experimental.pallas import tpu as pltpu

F32 = jnp.float32
BF16 = jnp.bfloat16
I32 = jnp.int32

D_MODEL = 2048
BATCH = 32
SEQ = 256
DEPTH = 4
DEC_BATCH = 4
DEC_SEQ = 4096
PAST_LEN = 256
GRID_W = 64
N_MIXERS = 3
NORM_EPS = 1e-6
M_HEADS = 8
M_DK = D_MODEL // 2 // M_HEADS
M_DV = D_MODEL // M_HEADS
M_GATE_CAP = 15.0
A_HEADS = 16
A_KV_HEADS = 4
A_HD = 128
ROPE_THETA = 10000.0
R_HD = 64
R_LN_EPS = 64e-5
N_EXPERTS = 16
EC_CAPACITY_FACTOR = 2
D_EXPERT = 1024

LANES = 128
SUBLANES = 8
PACKED_ROWS = 16
VMEM_LIMIT_BYTES = 56 * 2 ** 20

MLSTM_CHUNK = 256
RWKV_CHUNK = 64
RWKV_PAIRS_PER_GROUP = 16
RWKV_SOLVE_PASSES = 1
ROUTE_BLOCKS = 128
SLAB_CHUNK = 32


def _params(*sem):
    return pltpu.CompilerParams(dimension_semantics=sem, vmem_limit_bytes=VMEM_LIMIT_BYTES)


def _dot(a, b):
    return jnp.dot(a, b, preferred_element_type=F32)


def _dot_nt(a, b):
    return lax.dot_general(a, b, (((1,), (1,)), ((), ())), preferred_element_type=F32)


def _dot_tn(a, b):
    return lax.dot_general(a, b, (((0,), (0,)), ((), ())), preferred_element_type=F32)


def _split2(x):
    hi = x.astype(BF16)
    lo = (x - hi.astype(F32)).astype(BF16)
    return hi, lo


def _split3(x):
    h1 = x.astype(BF16)
    r1 = x - h1.astype(F32)
    h2 = r1.astype(BF16)
    h3 = (r1 - h2.astype(F32)).astype(BF16)
    return h1, h2, h3


def _dot3(a, w):
    ah, al = _split2(a)
    wh, wl = _split2(w)
    return _dot(ah, wh) + _dot(al, wh) + _dot(ah, wl)


_NN = (((1,), (0,)), ((), ()))
_NT = (((1,), (1,)), ((), ()))
_TN = (((0,), (0,)), ((), ()))


def _mm_passes(a, b, dims, passes):
    def f(x, y):
        return lax.dot_general(x, y, dims, preferred_element_type=F32)
    if passes == 1:
        return f(a.astype(BF16), b.astype(BF16))
    ah, al = _split2(a)
    bh, bl = _split2(b)
    return f(ah, bh) + f(al, bh) + f(ah, bl)


def _iota(shape, dim):
    return lax.broadcasted_iota(I32, shape, dim)


def _log_sigmoid(z):
    return jnp.minimum(z, 0.0) - jnp.log(1.0 + jnp.exp(-jnp.abs(z)))


def _adaln_kernel(c_ref, w_ref, b_ref, o_ref):
    c = c_ref[...]
    s = c * jax.nn.sigmoid(c)
    hi, lo = _split2(s)
    w = w_ref[0].astype(BF16)
    o_ref[0] = _dot(hi, w) + _dot(lo, w) + b_ref[0]


def _adaln(cond8, ada_w, ada_b):
    depth, d, d6 = ada_w.shape
    tn = min(d, 1024)
    return pl.pallas_call(
        _adaln_kernel,
        grid=(depth, d6 // tn),
        in_specs=[pl.BlockSpec((SUBLANES, d), lambda l, j: (0, 0)),
                  pl.BlockSpec((1, d, tn), lambda l, j: (l, 0, j)),
                  pl.BlockSpec((1, 1, tn), lambda l, j: (l, 0, j))],
        out_specs=pl.BlockSpec((1, SUBLANES, tn), lambda l, j: (l, 0, j)),
        out_shape=jax.ShapeDtypeStruct((depth, SUBLANES, d6), F32),
        compiler_params=_params("arbitrary", "arbitrary"),
        name="adaln",
    )(cond8, ada_w, ada_b.reshape(depth, 1, d6))


def _normed(x_ref, g_ref):
    x = x_ref[...]
    return x * lax.rsqrt(jnp.mean(x * x, axis=-1, keepdims=True) + NORM_EPS) * g_ref[...]


def _norm_plain_kernel(sh, sc, x_ref, g_ref, mod_ref, o_ref):
    h = _normed(x_ref, g_ref) * (1.0 + mod_ref[0, sc:sc + 1, :]) + mod_ref[0, sh:sh + 1, :]
    o_ref[...] = h.astype(o_ref.dtype)


def _norm_gate_kernel(sh, sc, x_ref, g_ref, mod_ref, w_ref, b_ref, o_ref, gate_ref):
    h = _normed(x_ref, g_ref) * (1.0 + mod_ref[0, sc:sc + 1, :]) + mod_ref[0, sh:sh + 1, :]
    o_ref[...] = h.astype(o_ref.dtype)
    z = _dot3(h, w_ref[...]) + b_ref[...]
    z = M_GATE_CAP * jnp.tanh(z / M_GATE_CAP)
    is_forget = ((_iota(z.shape, 1) // M_HEADS) % 2) == 1
    gate_ref[...] = jnp.where(is_forget, _log_sigmoid(z), z)


def _norm_router_kernel(sh, sc, x_ref, g_ref, mod_ref, w_ref, o_ref, aff_ref):
    h = _normed(x_ref, g_ref) * (1.0 + mod_ref[0, sc:sc + 1, :]) + mod_ref[0, sh:sh + 1, :]
    o_ref[...] = h
    z = _dot3(h, w_ref[...])
    e = jnp.exp(z - jnp.max(z, axis=-1, keepdims=True))
    aff_ref[...] = e / jnp.sum(e, axis=-1, keepdims=True)


def _norm_final_kernel(x_ref, g_ref, o_ref):
    o_ref[...] = _normed(x_ref, g_ref)


def _row_tile(n, t, want):
    tm = min(want, t)
    assert t % tm == 0 and n % tm == 0
    return tm


def _mod_spec(mod, tm, t):
    d = mod.shape[-1]
    if mod.shape[0] == 1:
        return pl.BlockSpec((1, 6, d), lambda i: (0, 0, 0))
    return pl.BlockSpec((1, 6, d), lambda i: ((i * tm) // t, 0, 0))


def _norm_call(kind, x, gain, mod, t, rows=(0, 1), w=None, b=None, out_dtype=F32):
    n, d = x.shape
    tm = _row_tile(n, t, 256)
    xs = pl.BlockSpec((tm, d), lambda i: (i, 0))
    gs = pl.BlockSpec((1, d), lambda i: (0, 0))
    ms = None if mod is None else _mod_spec(mod, tm, t)
    gain = gain.reshape(1, d)
    if kind == "final":
        return pl.pallas_call(_norm_final_kernel, grid=(n // tm,), in_specs=[xs, gs], out_specs=xs,
                              out_shape=jax.ShapeDtypeStruct((n, d), F32),
                              compiler_params=_params("arbitrary"), name="norm_final")(x, gain)
    sh, sc = rows
    if kind == "plain":
        return pl.pallas_call(functools.partial(_norm_plain_kernel, sh, sc), grid=(n // tm,),
                              in_specs=[xs, gs, ms], out_specs=xs,
                              out_shape=jax.ShapeDtypeStruct((n, d), out_dtype),
                              compiler_params=_params("arbitrary"), name="norm_plain")(x, gain, mod)
    ns = w.shape[1]
    ws = pl.BlockSpec((d, ns), lambda i: (0, 0))
    ps = pl.BlockSpec((tm, ns), lambda i: (i, 0))
    if kind == "gate":
        return pl.pallas_call(
            functools.partial(_norm_gate_kernel, sh, sc), grid=(n // tm,),
            in_specs=[xs, gs, ms, ws, pl.BlockSpec((1, ns), lambda i: (0, 0))],
            out_specs=[xs, ps],
            out_shape=[jax.ShapeDtypeStruct((n, d), out_dtype), jax.ShapeDtypeStruct((n, ns), F32)],
            compiler_params=_params("arbitrary"), name="norm_gate")(x, gain, mod, w, b.reshape(1, ns))
    assert kind == "router"
    return pl.pallas_call(
        functools.partial(_norm_router_kernel, sh, sc), grid=(n // tm,),
        in_specs=[xs, gs, ms, ws], out_specs=[xs, ps],
        out_shape=[jax.ShapeDtypeStruct((n, d), F32), jax.ShapeDtypeStruct((n, ns), F32)],
        compiler_params=_params("arbitrary"), name="norm_router")(x, gain, mod, w)


def _mm_kernel(x_ref, w_ref, o_ref):
    o_ref[...] = _dot(x_ref[...], w_ref[...])


def _mm(x, w, tn):
    n, k = x.shape
    m = w.shape[1]
    tm = min(512, n)
    tn = min(tn, m)
    return pl.pallas_call(
        _mm_kernel, grid=(m // tn, n // tm),
        in_specs=[pl.BlockSpec((tm, k), lambda j, i: (i, 0)), pl.BlockSpec((k, tn), lambda j, i: (0, j))],
        out_specs=pl.BlockSpec((tm, tn), lambda j, i: (i, j)),
        out_shape=jax.ShapeDtypeStruct((n, m), F32),
        compiler_params=_params("arbitrary", "arbitrary"), name="mm")(x, w)


def _mlstm_kernel(has_init, emit_state, *refs):
    q_ref, k_ref, v_ref, g_ref, gt_ref = refs[:5]
    pos = 5
    if has_init:
        c0_ref, n0_ref, m0_ref = refs[pos:pos + 3]
        pos += 3
    h_ref = refs[pos]
    pos += 1
    if emit_state:
        cout_ref, nout_ref, mout_ref = refs[pos:pos + 3]
        pos += 3
    c_scr, n_scr, m_scr = refs[pos:pos + 3]

    d = pl.program_id(1)
    c = pl.program_id(2)
    nc = pl.num_programs(2)
    fwd = d == 0
    heads = M_HEADS
    length = q_ref.shape[0]

    @pl.when(c == 0)
    def _():
        if has_init:
            c_scr[...] = c0_ref[0, 0]
            n_scr[...] = n0_ref[0, 0]
            m_scr[...] = m0_ref[0, 0]
        else:
            c_scr[...] = jnp.zeros_like(c_scr)
            n_scr[...] = jnp.zeros_like(n_scr)
            m_scr[...] = jnp.zeros_like(m_scr)

    sgn = jnp.where(fwd, 1, -1)
    row = _iota((length, length), 0)
    col = _iota((length, length), 1)
    mask = ((row - col) * sgn) >= 0
    mask_t = ((col - row) * sgn) >= 0
    scale = M_DK ** -0.5
    g = g_ref[...]
    gt = gt_ref[0]

    tri = mask.astype(F32).astype(BF16)
    tri_t = mask_t.astype(F32).astype(BF16)
    g1, g2, g3 = _split3(g)
    cum_cols = _dot(tri, g1) + _dot(tri, g2) + _dot(tri, g3)
    t1, t2, t3 = _split3(gt)
    cum_rows = _dot(t1, tri_t) + _dot(t2, tri_t) + _dot(t3, tri_t)

    def pick_col(x, kind, h):
        i, j = kind * heads + h, (kind + 2) * heads + h
        return jnp.where(fwd, x[:, i:i + 1], x[:, j:j + 1])

    def pick_row(x, kind, h):
        i, j = kind * heads + h, (kind + 2) * heads + h
        return jnp.where(fwd, x[i:i + 1, :], x[j:j + 1, :])

    def each(fn, *lists):
        return [fn(*args) for args in zip(*lists)]

    hs = list(range(heads))
    ig_row = [pick_row(gt, 0, h) for h in hs]
    ig_col = [pick_col(g, 0, h) for h in hs]
    b_col = [pick_col(cum_cols, 1, h) for h in hs]
    b_row = [pick_row(cum_rows, 1, h) for h in hs]
    m_prev = [m_scr[h:h + 1, 0:1] for h in hs]
    c_prev = [c_scr[h] for h in hs]
    n_prev = [n_scr[h:h + 1, :] for h in hs]
    qh = [q_ref[:, h * M_DK:(h + 1) * M_DK] for h in hs]
    kh = [k_ref[:, h * M_DK:(h + 1) * M_DK] * scale for h in hs]
    qb = each(lambda x: x.astype(BF16), qh)
    kb = each(lambda x: x.astype(BF16), kh)
    vb = [v_ref[:, h * M_DV:(h + 1) * M_DV].astype(BF16) for h in hs]
    qk = each(_dot_nt, qb, kb)
    qc = each(lambda q_, c_: _dot(q_, c_.astype(BF16)), qb, c_prev)
    dmat = each(lambda bc, br, ir: jnp.where(mask, bc - br + ir, -jnp.inf), b_col, b_row, ig_row)
    a = each(lambda bc, mp: bc + mp, b_col, m_prev)
    m_t = each(lambda a_, d_: jnp.maximum(a_, jnp.max(d_, axis=1, keepdims=True)), a, dmat)
    w_inter = each(lambda a_, m_: jnp.exp(a_ - m_), a, m_t)
    s = each(lambda x_, d_, m_: x_ * jnp.exp(d_ - m_), qk, dmat, m_t)
    sv = each(lambda s_, v_: _dot(s_.astype(BF16), v_), s, vb)
    num = each(lambda w_, x_, y_: w_ * x_ + y_, w_inter, qc, sv)
    den = each(lambda w_, q_, n_, s_: w_ * jnp.sum(q_ * n_, axis=1, keepdims=True)
               + jnp.sum(s_, axis=1, keepdims=True), w_inter, qh, n_prev, s)
    for h in hs:
        h_ref[0, :, h * M_DV:(h + 1) * M_DV] = num[h] / jnp.maximum(jnp.abs(den[h]), jnp.exp(-m_t[h]))

    b_last = each(lambda bc: jnp.where(fwd, bc[length - 1:length, :], bc[0:1, :]), b_col)
    g_col = each(lambda bl, bc, ic: bl - bc + ic, b_last, b_col, ig_col)
    m_new = each(lambda bl, mp, gc: jnp.maximum(bl + mp, jnp.max(gc, axis=0, keepdims=True)), b_last, m_prev, g_col)
    decay = each(lambda bl, mp, mn: jnp.exp(bl + mp - mn), b_last, m_prev, m_new)
    wk = each(lambda gc, mn, k_: jnp.exp(gc - mn) * k_, g_col, m_new, kh)
    kv = each(lambda w_, v_: _dot_tn(w_.astype(BF16), v_), wk, vb)
    for h in hs:
        c_scr[h] = decay[h] * c_prev[h] + kv[h]
        n_scr[h:h + 1, :] = decay[h] * n_prev[h] + jnp.sum(wk[h], axis=0, keepdims=True)
        m_scr[h:h + 1, :] = jnp.broadcast_to(m_new[h], (1, LANES))

    if emit_state:
        @pl.when(c == nc - 1)
        def _():
            cout_ref[0, 0] = c_scr[...]
            nout_ref[0, 0] = n_scr[...]
            mout_ref[0, 0] = m_scr[...]


def _mlstm_scan(proj, gates, b, t, init, emit_state):
    n = proj.shape[0]
    heads, dk, dv = M_HEADS, M_DK, M_DV
    qk, mv = heads * dk, heads * dv
    length = min(MLSTM_CHUNK, t)
    nc = t // length
    gates_t = gates.reshape(n // length, length, 4 * heads).transpose(0, 2, 1)

    def tile(bi, di, ci):
        return bi * nc + ci + di * (nc - 1 - 2 * ci)

    in_specs = [pl.BlockSpec((length, qk), lambda bi, di, ci: (tile(bi, di, ci), 0)),
                pl.BlockSpec((length, qk), lambda bi, di, ci: (tile(bi, di, ci), 1)),
                pl.BlockSpec((length, mv), lambda bi, di, ci: (tile(bi, di, ci), 1)),
                pl.BlockSpec((length, 4 * heads), lambda bi, di, ci: (tile(bi, di, ci), 0)),
                pl.BlockSpec((1, 4 * heads, length), lambda bi, di, ci: (tile(bi, di, ci), 0, 0))]
    args = [proj, proj, proj, gates, gates_t]
    state_specs = [pl.BlockSpec((1, 1, heads, dk, dv), lambda bi, di, ci: (bi, di, 0, 0, 0)),
                   pl.BlockSpec((1, 1, heads, dk), lambda bi, di, ci: (bi, di, 0, 0)),
                   pl.BlockSpec((1, 1, heads, LANES), lambda bi, di, ci: (bi, di, 0, 0))]
    if init is not None:
        c0, n0, m0 = init
        in_specs += state_specs
        args += [c0, n0, jnp.broadcast_to(m0[..., None], m0.shape + (LANES,))]
    out_specs = [pl.BlockSpec((1, length, mv), lambda bi, di, ci: (di, tile(bi, di, ci), 0))]
    out_shape = [jax.ShapeDtypeStruct((2, n, mv), F32)]
    if emit_state:
        out_specs += state_specs
        out_shape += [jax.ShapeDtypeStruct((b, 2, heads, dk, dv), F32),
                      jax.ShapeDtypeStruct((b, 2, heads, dk), F32),
                      jax.ShapeDtypeStruct((b, 2, heads, LANES), F32)]
    return pl.pallas_call(
        functools.partial(_mlstm_kernel, init is not None, emit_state),
        grid=(b, 2, nc), in_specs=in_specs, out_specs=out_specs, out_shape=out_shape,
        scratch_shapes=[pltpu.VMEM((heads, dk, dv), F32), pltpu.VMEM((heads, dk), F32),
                        pltpu.VMEM((heads, LANES), F32)],
        compiler_params=_params("arbitrary", "arbitrary", "arbitrary"), name="mlstm_scan")(*args)


def _mlstm_out_kernel(hs_ref, o_ref, gain_ref, w_ref, x_ref, mod_ref, out_ref):
    hs = hs_ref[0] + hs_ref[1]
    parts = []
    for h in range(M_HEADS):
        seg = hs[:, h * M_DV:(h + 1) * M_DV]
        parts.append(seg * lax.rsqrt(jnp.mean(seg * seg, axis=-1, keepdims=True) + NORM_EPS))
    hn = jnp.concatenate(parts, axis=1) * gain_ref[...]
    lhs = (jax.nn.sigmoid(o_ref[...]) * hn).astype(BF16)
    out_ref[...] = x_ref[...] + mod_ref[0, 2:3, :] * _dot(lhs, w_ref[...])


def _mlstm_out(hs2, proj, gain, w_out, x, mod, t):
    n, d = x.shape
    mv = M_HEADS * M_DV
    tm = _row_tile(n, t, 256)
    return pl.pallas_call(
        _mlstm_out_kernel, grid=(n // tm,),
        in_specs=[pl.BlockSpec((2, tm, mv), lambda i: (0, i, 0)),
                  pl.BlockSpec((tm, mv), lambda i: (i, 2)),
                  pl.BlockSpec((1, mv), lambda i: (0, 0)),
                  pl.BlockSpec((mv, d), lambda i: (0, 0)),
                  pl.BlockSpec((tm, d), lambda i: (i, 0)),
                  _mod_spec(mod, tm, t)],
        out_specs=pl.BlockSpec((tm, d), lambda i: (i, 0)),
        out_shape=jax.ShapeDtypeStruct((n, d), F32),
        compiler_params=_params("arbitrary"), name="mlstm_out")(hs2, proj, gain.reshape(1, mv), w_out, x, mod)


def _attn_qkv_kernel(use_rope, *refs):
    if use_rope:
        x_ref, w_ref, qg_ref, kg_ref, cos_ref, sin_ref, q_ref, k_ref, v_ref = refs
    else:
        x_ref, w_ref, qg_ref, kg_ref, q_ref, k_ref, v_ref = refs
    acc = _dot(x_ref[...], w_ref[...])
    qd, kvd = A_HEADS * A_HD, A_KV_HEADS * A_HD

    def head_norm(xh, gain):
        xh = xh * lax.rsqrt(jnp.mean(xh * xh, axis=-1, keepdims=True) + NORM_EPS) * gain
        if use_rope:
            xh = xh * cos_ref[...] + pltpu.roll(xh, A_HD // 2, axis=1) * sin_ref[...]
        return xh

    for h in range(A_HEADS):
        qh = head_norm(acc[:, h * A_HD:(h + 1) * A_HD], qg_ref[...])
        q_ref[:, h * A_HD:(h + 1) * A_HD] = (qh * (A_HD ** -0.5)).astype(BF16)
    for h in range(A_KV_HEADS):
        k_ref[:, h * A_HD:(h + 1) * A_HD] = head_norm(acc[:, qd + h * A_HD:qd + (h + 1) * A_HD], kg_ref[...])
    v_ref[...] = acc[:, qd + kvd:]


def _attn_qkv(hb, w, qgain, kgain, t, rope):
    n, d = hb.shape
    qd, kvd = A_HEADS * A_HD, A_KV_HEADS * A_HD
    tm = _row_tile(n, t, 256)
    in_specs = [pl.BlockSpec((tm, d), lambda i: (i, 0)),
                pl.BlockSpec((d, qd + 2 * kvd), lambda i: (0, 0)),
                pl.BlockSpec((1, A_HD), lambda i: (0, 0)),
                pl.BlockSpec((1, A_HD), lambda i: (0, 0))]
    args = [hb, w, qgain.reshape(1, A_HD), kgain.reshape(1, A_HD)]
    if rope is not None:
        in_specs += [pl.BlockSpec((tm, A_HD), lambda i: (i % (t // tm), 0))] * 2
        args += list(rope)
    return pl.pallas_call(
        functools.partial(_attn_qkv_kernel, rope is not None), grid=(n // tm,),
        in_specs=in_specs,
        out_specs=[pl.BlockSpec((tm, qd), lambda i: (i, 0)),
                   pl.BlockSpec((tm, kvd), lambda i: (i, 0)),
                   pl.BlockSpec((tm, kvd), lambda i: (i, 0))],
        out_shape=[jax.ShapeDtypeStruct((n, qd), BF16), jax.ShapeDtypeStruct((n, kvd), F32),
                   jax.ShapeDtypeStruct((n, kvd), F32)],
        compiler_params=_params("arbitrary"), name="attn_qkv")(*args)


def _attn_kernel(has_cache, *refs):
    if has_cache:
        q_ref, kn_ref, vn_ref, kc_ref, vc_ref, o_ref, k_scr, v_scr = refs
    else:
        q_ref, kn_ref, vn_ref, o_ref, k_scr, v_scr = refs
    group = A_HEADS // A_KV_HEADS
    tq = q_ref.shape[0]
    t_new = kn_ref.shape[0]
    past = k_scr.shape[0] - t_new

    @pl.when(pl.program_id(2) == 0)
    def _():
        if has_cache:
            k_scr[0:past, :] = kc_ref[...].astype(BF16)
            v_scr[0:past, :] = vc_ref[...].astype(BF16)
        k_scr[past:, :] = kn_ref[...].astype(BF16)
        v_scr[past:, :] = vn_ref[...].astype(BF16)

    q4 = jnp.concatenate([q_ref[:, g * A_HD:(g + 1) * A_HD] for g in range(group)], axis=0)
    s = _dot_nt(q4, k_scr[...])
    p = jnp.exp(s - jnp.max(s, axis=-1, keepdims=True))
    o = _dot(p.astype(BF16), v_scr[...]) / jnp.sum(p, axis=-1, keepdims=True)
    for g in range(group):
        o_ref[:, g * A_HD:(g + 1) * A_HD] = o[g * tq:(g + 1) * tq, :].astype(BF16)


def _attention(q, k, v, b, t, cache):
    n = q.shape[0]
    group = A_HEADS // A_KV_HEADS
    tq = min(64, t)
    nq = t // tq
    past = 0 if cache is None else cache[0].shape[0] // b
    in_specs = [pl.BlockSpec((tq, group * A_HD), lambda bi, hi, qi: (bi * nq + qi, hi)),
                pl.BlockSpec((t, A_HD), lambda bi, hi, qi: (bi, hi)),
                pl.BlockSpec((t, A_HD), lambda bi, hi, qi: (bi, hi))]
    args = [q, k, v]
    if cache is not None:
        in_specs += [pl.BlockSpec((past, A_HD), lambda bi, hi, qi: (bi, hi))] * 2
        args += list(cache)
    return pl.pallas_call(
        functools.partial(_attn_kernel, cache is not None), grid=(b, A_KV_HEADS, nq),
        in_specs=in_specs,
        out_specs=pl.BlockSpec((tq, group * A_HD), lambda bi, hi, qi: (bi * nq + qi, hi)),
        out_shape=jax.ShapeDtypeStruct((n, A_HEADS * A_HD), BF16),
        scratch_shapes=[pltpu.VMEM((past + t, A_HD), BF16), pltpu.VMEM((past + t, A_HD), BF16)],
        compiler_params=_params("arbitrary", "arbitrary", "arbitrary"), name="attention")(*args)


def _resid_mm_kernel(lhs_ref, w_ref, x_ref, mod_ref, out_ref):
    out_ref[...] = x_ref[...] + mod_ref[0, 2:3, :] * _dot(lhs_ref[...], w_ref[...])


def _resid_mm(lhs, w, x, mod, t):
    n, d = x.shape
    k = lhs.shape[1]
    tm = _row_tile(n, t, 256)
    return pl.pallas_call(
        _resid_mm_kernel, grid=(n // tm,),
        in_specs=[pl.BlockSpec((tm, k), lambda i: (i, 0)),
                  pl.BlockSpec((k, d), lambda i: (0, 0)),
                  pl.BlockSpec((tm, d), lambda i: (i, 0)),
                  _mod_spec(mod, tm, t)],
        out_specs=pl.BlockSpec((tm, d), lambda i: (i, 0)),
        out_shape=jax.ShapeDtypeStruct((n, d), F32),
        compiler_params=_params("arbitrary"), name="resid_mm")(lhs, w, x, mod)


def _token_shift_delta(h, hp_ref, hn_ref, i, tiles_per_seq):
    tm = h.shape[0]
    rowi = _iota((tm, 1), 0)
    ti = i % tiles_per_seq
    prev_row = jnp.where(ti == 0, 0.0, hp_ref[SUBLANES - 1:SUBLANES, :])
    next_row = jnp.where(ti == tiles_per_seq - 1, 0.0, hn_ref[0:1, :])
    h_prev = jnp.where(rowi == 0, prev_row, pltpu.roll(h, 1, axis=0))
    h_next = jnp.where(rowi == tm - 1, next_row, pltpu.roll(h, tm - 1, axis=0))
    return 0.5 * (h_prev + h_next) - h


def _shift_specs(tm, d, n, grid_rank):
    nb8 = n // SUBLANES
    per = tm // SUBLANES
    if grid_rank == 1:
        return [pl.BlockSpec((tm, d), lambda i: (i, 0)),
                pl.BlockSpec((SUBLANES, d), lambda i: (jnp.maximum(i * per - 1, 0), 0)),
                pl.BlockSpec((SUBLANES, d), lambda i: (jnp.minimum((i + 1) * per, nb8 - 1), 0))]
    return [pl.BlockSpec((tm, d), lambda j, i: (i, 0)),
            pl.BlockSpec((SUBLANES, d), lambda j, i: (jnp.maximum(i * per - 1, 0), 0)),
            pl.BlockSpec((SUBLANES, d), lambda j, i: (jnp.minimum((i + 1) * per, nb8 - 1), 0))]


def _rwkv_rkv_kernel(tiles_per_seq, h_ref, hp_ref, hn_ref, mix_ref, w_ref, o_ref):
    h = h_ref[...]
    xx = _token_shift_delta(h, hp_ref, hn_ref, pl.program_id(1), tiles_per_seq)
    o_ref[0] = _dot((h + xx * mix_ref[0]).astype(BF16), w_ref[0])


def _rwkv_rkv(h, mix3, w_rkv, t):
    n, d = h.shape
    tm = _row_tile(n, t, 256)
    return pl.pallas_call(
        functools.partial(_rwkv_rkv_kernel, t // tm), grid=(3, n // tm),
        in_specs=_shift_specs(tm, d, n, 2) + [pl.BlockSpec((1, 1, d), lambda j, i: (j, 0, 0)),
                                              pl.BlockSpec((1, d, d), lambda j, i: (j, 0, 0))],
        out_specs=pl.BlockSpec((1, tm, d), lambda j, i: (j, i, 0)),
        out_shape=jax.ShapeDtypeStruct((3, n, d), F32),
        compiler_params=_params("arbitrary", "arbitrary"), name="rwkv_rkv")(h, h, h, mix3, w_rkv)


def _rwkv_lora_kernel(tiles_per_seq, lora, h_ref, hp_ref, hn_ref, mix_ref, w1_ref, a1_ref, g1_ref,
                      w2_ref, a2_ref, g2_ref, w0_ref, a0_ref, lw_ref, asig_ref, g_ref):
    h = h_ref[...]
    xx = _token_shift_delta(h, hp_ref, hn_ref, pl.program_id(0), tiles_per_seq)
    xw = (h + xx * mix_ref[0:1, :]).astype(BF16)
    xa = (h + xx * mix_ref[1:2, :]).astype(BF16)
    xg = (h + xx * mix_ref[2:3, :]).astype(BF16)
    tw = jnp.tanh(_dot(xw, w1_ref[...])).astype(BF16)
    ta = _dot(xa, a1_ref[...]).astype(BF16)
    tg = jax.nn.sigmoid(_dot(xg, g1_ref[...])).astype(BF16)
    g_ref[...] = _dot(tg, g2_ref[...])
    for dr in range(2):
        z = -(w0_ref[dr] + _dot(tw[:, dr * lora:(dr + 1) * lora], w2_ref[dr]))
        softplus = jnp.maximum(z, 0.0) + jnp.log(1.0 + jnp.exp(-jnp.abs(z)))
        lw_ref[dr] = -jnp.exp(-softplus - 0.5)
        asig_ref[dr] = jax.nn.sigmoid(a0_ref[dr] + _dot(ta[:, dr * lora:(dr + 1) * lora], a2_ref[dr]))


def _rwkv_lora(h, mix3, w1c, a1c, g1, w2p, a2p, g2, w0, a0, t):
    n, d = h.shape
    tm = _row_tile(n, t, 256)
    lora = w2p.shape[1]
    gl = g1.shape[1]
    full2 = lambda shape: pl.BlockSpec(shape, lambda i: (0, 0))
    full3 = lambda shape: pl.BlockSpec(shape, lambda i: (0, 0, 0))
    return pl.pallas_call(
        functools.partial(_rwkv_lora_kernel, t // tm, lora), grid=(n // tm,),
        in_specs=_shift_specs(tm, d, n, 1) + [
            full2((3, d)), full2((d, 2 * lora)), full2((d, 2 * lora)), full2((d, gl)),
            full3((2, lora, d)), full3((2, lora, d)), full2((gl, d)), full3((2, 1, d)), full3((2, 1, d))],
        out_specs=[pl.BlockSpec((2, tm, d), lambda i: (0, i, 0)),
                   pl.BlockSpec((2, tm, d), lambda i: (0, i, 0)),
                   pl.BlockSpec((tm, d), lambda i: (i, 0))],
        out_shape=[jax.ShapeDtypeStruct((2, n, d), F32), jax.ShapeDtypeStruct((2, n, d), F32),
                   jax.ShapeDtypeStruct((n, d), F32)],
        compiler_params=_params("arbitrary"), name="rwkv_lora")(h, h, h, mix3, w1c, a1c, g1, w2p, a2p, g2, w0, a0)


def _rwkv_scan_kernel(has_init, emit_state, *refs):
    r_ref, k_ref, v_ref, lw_ref, a_ref, kk_ref, ka_ref, rk_ref = refs[:8]
    pos = 8
    if has_init:
        s0_ref = refs[pos]
        pos += 1
    y_ref, bonus_ref = refs[pos:pos + 2]
    pos += 2
    if emit_state:
        sout_ref = refs[pos]
        pos += 1
    s_scr, cum_scr = refs[pos:pos + 2]

    d = pl.program_id(1)
    c = pl.program_id(2)
    nc = pl.num_programs(2)
    fwd = d == 0
    sgn = jnp.where(fwd, 1, -1)
    length = r_ref.shape[1]
    npairs = s_scr.shape[0]
    hd = R_HD

    @pl.when(c == 0)
    def _():
        if has_init:
            s_scr[...] = s0_ref[0, 0]
        else:
            s_scr[...] = jnp.zeros_like(s_scr)

    ti = _iota((length, length), 0)
    si = _iota((length, length), 1)
    tri = (((ti - si) * sgn) >= 0).astype(BF16)
    l1, l2, l3 = _split3(lw_ref[0])
    cum_scr[...] = _dot(tri, l1) + _dot(tri, l2) + _dot(tri, l3)

    gl = 2 * length
    rr = _iota((gl, gl), 0)
    cc = _iota((gl, gl), 1)
    same = (rr // length) == (cc // length)
    e = ((rr % length) - (cc % length)) * sgn
    strict = same & (e > 0)
    incl = same & (e >= 0)
    eye = (rr == cc).astype(F32)
    lane = _iota((1, LANES), 1)
    m0 = (lane < hd).astype(F32)
    m1 = 1.0 - m0
    lr = _iota((LANES, LANES), 0)
    lc = _iota((LANES, LANES), 1)
    ones_blk = ((lr // hd) == (lc // hd)).astype(BF16)

    def segsum(x):
        hi, lo = _split2(x)
        return _dot(hi, ones_blk) + _dot(lo, ones_blk)

    def stack(x):
        return jnp.concatenate([x * m0, x * m1], axis=0)

    def smm(x, y, dims):
        return _mm_passes(x, y, dims, RWKV_SOLVE_PASSES)

    tr = rr % length
    tc = cc % length
    base = 4
    base_mask = (tr // base) == (tc // base)
    level_masks = []
    size = base
    while size < length:
        level_masks.append(((tr // (2 * size)) == (tc // (2 * size))) & ((tr // size) != (tc // size)))
        size *= 2

    def load_pair(p):
        ds = pl.ds(pl.multiple_of(p * LANES, LANES), LANES)
        return dict(ds=ds, p=p, r=r_ref[0, :, ds], k=k_ref[0, :, ds], v=v_ref[0, :, ds], lw=lw_ref[0, :, ds],
                    a=a_ref[0, :, ds], cum=cum_scr[:, ds], kkp=kk_ref[:, ds], kap=ka_ref[:, ds],
                    rkp=rk_ref[:, ds], st=s_scr[p])

    def each(fn, *lists):
        return [fn(*args) for args in zip(*lists)]

    def solve_group(xs):
        get = lambda name: [x[name] for x in xs]
        r, k, v, lw, a, cum, st = (get(n) for n in ("r", "k", "v", "lw", "a", "cum", "st"))
        kk = each(lambda k_, p_: k_ * p_, k, get("kkp"))
        ss = each(lambda q: segsum(q * q), kk)
        kk = each(lambda q, s_: q * lax.rsqrt(jnp.maximum(s_, 1e-24)), kk, ss)
        kd = each(lambda k_, a_, p_: k_ * (1.0 + (a_ - 1.0) * p_), k, a, get("kap"))
        bb = each(lambda q, a_: q * a_, kk, a)
        tot = each(lambda c_: jnp.where(fwd, c_[length - 1:length, :], c_[0:1, :]), cum)
        e_n = each(lambda c_: jnp.exp(-c_), cum)
        sa = each(lambda q, c_, l_: stack(-q * jnp.exp(c_ - l_)), kk, cum, lw)
        sr_full = each(lambda r_, c_: r_ * jnp.exp(c_), r, cum)
        sr = each(stack, sr_full)
        sb = each(lambda b_, e_: stack(b_ * e_), bb, e_n)
        sk = each(lambda k_, e_: stack(k_ * e_), kd, e_n)
        vexp = each(stack, v)
        m_ab = each(lambda x_, y_: jnp.where(strict, smm(x_, y_, _NT), 0.0), sa, sb)
        m_ak = each(lambda x_, y_: jnp.where(strict, smm(x_, y_, _NT), 0.0), sa, sk)
        rhs1 = each(lambda x_, s_: smm(x_, s_, _NT), sa, st)
        rhs2 = each(lambda m_, v_: smm(m_, v_, _NN), m_ak, vexp)
        rhs = each(lambda x_, y_: x_ + y_, rhs1, rhs2)
        d0 = each(lambda m_: jnp.where(base_mask, m_, 0.0), m_ab)
        sq = each(lambda d_: smm(d_, d_, _NN), d0)
        inv = each(lambda d_: eye + d_, d0)
        inv = each(lambda i_, q_: i_ + smm(i_, q_, _NN), inv, sq)
        for lm in level_masks:
            half = each(lambda i_, m_: smm(i_, jnp.where(lm, m_, 0.0), _NN), inv, m_ab)
            inv = each(lambda i_, h_: i_ + smm(h_, i_, _NN), inv, half)
        uexp = each(lambda i_, r_: smm(i_, r_, _NN), inv, rhs)
        t_rb = each(lambda x_, y_: jnp.where(incl, smm(x_, y_, _NT), 0.0), sr, sb)
        t_rk = each(lambda x_, y_: jnp.where(incl, smm(x_, y_, _NT), 0.0), sr, sk)
        fold = lambda t_: t_[0:length] + t_[length:gl]
        y1 = each(lambda r_, s_: smm(r_, s_, _NT), sr_full, st)
        uv = each(lambda u_, v_: jnp.concatenate([u_, v_], axis=0), uexp, vexp)
        y2 = each(lambda b_, k_, w_: smm(jnp.concatenate([fold(b_), fold(k_)], axis=1), w_, _NN), t_rb, t_rk, uv)
        y = each(lambda p_, q_: p_ + q_, y1, y2)
        bs = each(lambda r_, k_, p_: segsum(r_ * k_ * p_), r, kd, get("rkp"))
        bonus = each(lambda s_, v_: s_ * v_, bs, v)
        e_e = each(lambda t_, c_: jnp.exp(t_ - c_), tot, cum)
        bk = each(lambda b_, k_, e_: jnp.concatenate([stack(b_ * e_), stack(k_ * e_)], axis=0), bb, kd, e_e)
        s1 = each(lambda w_, x_: smm(w_, x_, _TN), uv, bk)
        s_new = each(lambda s_, t_, p_: s_ * jnp.exp(t_) + p_, st, tot, s1)
        return y, bonus, s_new

    npg = min(RWKV_PAIRS_PER_GROUP, npairs)

    def group(gi, carry):
        xs = [load_pair(gi * npg + u) for u in range(npg)]
        ys, bonuses, states = solve_group(xs)
        for x, y, bonus, s_new in zip(xs, ys, bonuses, states):
            y_ref[0, :, x["ds"]] = y
            bonus_ref[0, :, x["ds"]] = bonus
            s_scr[x["p"]] = s_new
        return carry

    lax.fori_loop(0, npairs // npg, group, 0)

    if emit_state:
        @pl.when(c == nc - 1)
        def _():
            sout_ref[0, 0] = s_scr[...]


def _rwkv_scan(rkv, lw, asig, kk, ka, rk, b, t, s0, emit_state):
    n, d = rkv.shape[1:]
    length = RWKV_CHUNK
    nc = t // length
    npairs = d // LANES

    def tile(bi, di, ci):
        return bi * nc + ci + di * (nc - 1 - 2 * ci)

    def plane(j):
        return pl.BlockSpec((1, length, d), lambda bi, di, ci: (j, tile(bi, di, ci), 0))

    dirspec = pl.BlockSpec((1, length, d), lambda bi, di, ci: (di, tile(bi, di, ci), 0))
    vec = pl.BlockSpec((1, d), lambda bi, di, ci: (0, 0))
    sspec = pl.BlockSpec((1, 1, npairs, LANES, LANES), lambda bi, di, ci: (bi, di, 0, 0, 0))
    in_specs = [plane(0), plane(1), plane(2), dirspec, dirspec, vec, vec, vec]
    args = [rkv, rkv, rkv, lw, asig, kk.reshape(1, d), ka.reshape(1, d), rk.reshape(1, d)]
    if s0 is not None:
        in_specs.append(sspec)
        args.append(s0)
    out_specs = [dirspec, dirspec]
    out_shape = [jax.ShapeDtypeStruct((2, n, d), F32), jax.ShapeDtypeStruct((2, n, d), F32)]
    if emit_state:
        out_specs.append(sspec)
        out_shape.append(jax.ShapeDtypeStruct((b, 2, npairs, LANES, LANES), F32))
    return pl.pallas_call(
        functools.partial(_rwkv_scan_kernel, s0 is not None, emit_state),
        grid=(b, 2, nc), in_specs=in_specs, out_specs=out_specs, out_shape=out_shape,
        scratch_shapes=[pltpu.VMEM((npairs, LANES, LANES), F32), pltpu.VMEM((length, d), F32)],
        compiler_params=_params("arbitrary", "arbitrary", "arbitrary"), name="rwkv_scan")(*args)


def _rwkv_out_kernel(y_ref, bonus_ref, g_ref, lnw_ref, lnb_ref, w_ref, x_ref, mod_ref, out_ref):
    y = y_ref[0] + y_ref[1]
    lr = _iota((LANES, LANES), 0)
    lc = _iota((LANES, LANES), 1)
    mean_blk = ((lr // R_HD) == (lc // R_HD)).astype(BF16)

    def segmean(x):
        hi, lo = _split2(x)
        return (_dot(hi, mean_blk) + _dot(lo, mean_blk)) * (1.0 / R_HD)

    tiles = [y[:, j * LANES:(j + 1) * LANES] for j in range(y.shape[1] // LANES)]
    means = [segmean(yj) for yj in tiles]
    cens = [yj - mj for yj, mj in zip(tiles, means)]
    variances = [segmean(cj * cj) for cj in cens]
    parts = [cj * lax.rsqrt(vj + R_LN_EPS) for cj, vj in zip(cens, variances)]
    yn = jnp.concatenate(parts, axis=1) * lnw_ref[...] + lnb_ref[...] + bonus_ref[0] + bonus_ref[1]
    lhs = (yn * g_ref[...]).astype(BF16)
    out_ref[...] = x_ref[...] + mod_ref[0, 2:3, :] * _dot(lhs, w_ref[...])


def _rwkv_out(y2, bonus2, g, lnw, lnb, w_out, x, mod, t):
    n, d = x.shape
    tm = _row_tile(n, t, 256)
    two = pl.BlockSpec((2, tm, d), lambda i: (0, i, 0))
    rows = pl.BlockSpec((tm, d), lambda i: (i, 0))
    vec = pl.BlockSpec((1, d), lambda i: (0, 0))
    return pl.pallas_call(
        _rwkv_out_kernel, grid=(n // tm,),
        in_specs=[two, two, rows, vec, vec, pl.BlockSpec((d, d), lambda i: (0, 0)), rows,
                  _mod_spec(mod, tm, t)],
        out_specs=rows, out_shape=jax.ShapeDtypeStruct((n, d), F32),
        compiler_params=_params("arbitrary"), name="rwkv_out")(
            y2, bonus2, g, lnw.reshape(1, d), lnb.reshape(1, d), w_out, x, mod)


def _count(m):
    x = jnp.sum(m.astype(F32), axis=1, keepdims=True)
    return jnp.sum(x, axis=0, keepdims=True)


def _route_search_kernel(cap, aff_ref, thr_ref, tie_ref):
    n_exp, nb, _ = aff_ref.shape
    experts = list(range(n_exp))
    bits = [lax.bitcast_convert_type(aff_ref[e], I32) for e in experts]
    tok = _iota((nb, LANES), 0) * LANES + _iota((nb, LANES), 1)

    def value_step(i, cur):
        bit = jnp.int32(1) << (30 - i)
        return tuple(jnp.where(_count(b >= (c | bit)) >= cap, c | bit, c) for b, c in zip(bits, cur))

    zeros = tuple(jnp.zeros((1, 1), I32) for _ in experts)
    thr = lax.fori_loop(0, 31, value_step, zeros)
    equal = [b == t for b, t in zip(bits, thr)]
    need = [cap - _count(b > t) for b, t in zip(bits, thr)]

    def index_step(i, cur):
        bit = jnp.int32(1) << (14 - i)
        return tuple(jnp.where(_count(q & (tok < (c + bit))) < n, c + bit, c) for q, n, c in zip(equal, need, cur))

    tie = lax.fori_loop(0, 15, index_step, zeros)
    for e in experts:
        thr_ref[e] = jnp.broadcast_to(thr[e], (SUBLANES, LANES))
        tie_ref[e] = jnp.broadcast_to(tie[e], (SUBLANES, LANES))


def _route_kernel(cap, aff_ref, thr_ref, tie_ref, idx_ref, gate_ref, pos_ref, tot_ref, off_ref):
    aff = aff_ref[0]
    nb = aff.shape[0]
    bits = lax.bitcast_convert_type(aff, I32)
    tok = _iota(aff.shape, 0) * LANES + _iota(aff.shape, 1)
    thr = thr_ref[0, 0:1, 0:1]
    sel = (bits > thr) | ((bits == thr) & (tok <= tie_ref[0, 0:1, 0:1]))
    self32 = sel.astype(F32)
    selb = self32.astype(BF16)

    li = _iota((LANES, LANES), 0)
    lj = _iota((LANES, LANES), 1)
    upper_incl = (li <= lj).astype(BF16)
    cl = _dot(selb, upper_incl)
    ones8 = jnp.ones((SUBLANES, LANES), BF16)
    tot_row = _dot_nt(ones8, selb)
    bi = _iota((nb, nb), 0)
    bj = _iota((nb, nb), 1)
    offi_row = _dot(tot_row.astype(BF16), (bi <= bj).astype(BF16))
    offx_row = offi_row - tot_row
    tot_col = jnp.broadcast_to(cl[:, LANES - 1:LANES], (nb, LANES)).astype(BF16)
    offx_col = _dot((bj < bi).astype(BF16), tot_col)
    pos_ref[0] = jnp.where(sel, offx_col + cl - 1.0, -1.0).astype(I32)
    tot_ref[0] = tot_row
    off_ref[0] = offx_row

    pcol = _iota((cap, 1), 0).astype(F32)
    before = offi_row[0:1, :] <= pcol
    blk = jnp.sum(before.astype(F32), axis=1, keepdims=True)
    base = jnp.sum(jnp.where(before, tot_row[0:1, :], 0.0), axis=1, keepdims=True)
    onehot = (_iota((cap, nb), 1).astype(F32) == blk).astype(BF16)
    rowcnt = _dot(onehot, cl.astype(BF16))
    rank = pcol - base
    lane = jnp.sum((rowcnt <= rank).astype(F32), axis=1, keepdims=True)
    idx_ref[0] = (blk * LANES + lane).astype(I32)
    a1, a2, a3 = _split3(aff)
    rowaff = _dot(onehot, a1) + _dot(onehot, a2) + _dot(onehot, a3)
    gate_ref[0] = jnp.sum(jnp.where(_iota((cap, LANES), 1).astype(F32) == lane, rowaff, 0.0),
                          axis=1, keepdims=True)


def _route(aff3, cap):
    e, nb, _ = aff3.shape
    word = pl.BlockSpec((1, SUBLANES, LANES), lambda i: (i, 0, 0))
    thr, tie = pl.pallas_call(
        functools.partial(_route_search_kernel, cap), grid=(1,),
        in_specs=[pl.BlockSpec((e, nb, LANES), lambda i: (0, 0, 0))],
        out_specs=[pl.BlockSpec((e, SUBLANES, LANES), lambda i: (0, 0, 0))] * 2,
        out_shape=[jax.ShapeDtypeStruct((e, SUBLANES, LANES), I32)] * 2,
        compiler_params=_params("arbitrary"), name="moe_route_search")(aff3)
    return pl.pallas_call(
        functools.partial(_route_kernel, cap), grid=(e,),
        in_specs=[pl.BlockSpec((1, nb, LANES), lambda i: (i, 0, 0)), word, word],
        out_specs=[pl.BlockSpec((1, cap, 1), lambda i: (i, 0, 0)),
                   pl.BlockSpec((1, cap, 1), lambda i: (i, 0, 0)),
                   pl.BlockSpec((1, nb, LANES), lambda i: (i, 0, 0)),
                   pl.BlockSpec((1, SUBLANES, nb), lambda i: (i, 0, 0)),
                   pl.BlockSpec((1, SUBLANES, nb), lambda i: (i, 0, 0))],
        out_shape=[jax.ShapeDtypeStruct((e, cap, 1), I32), jax.ShapeDtypeStruct((e, cap, 1), F32),
                   jax.ShapeDtypeStruct((e, nb, LANES), I32),
                   jax.ShapeDtypeStruct((e, SUBLANES, nb), F32), jax.ShapeDtypeStruct((e, SUBLANES, nb), F32)],
        compiler_params=_params("arbitrary"), name="moe_route")(aff3, thr, tie)


def _row_copy(x_hbm, xbuf, sem, slot, tok, r):
    return pltpu.make_async_copy(x_hbm.at[pl.ds(tok, 1)], xbuf.at[pl.ds(r, 1)], sem.at[slot])


def _expert_kernel(idx_ref, idx_next_ref, x_hbm, gate_ref, wg_ref, wu_ref, wd_ref, o_ref, xbuf0, xbuf1, sem):
    tm = xbuf0.shape[0]
    bufs = (xbuf0, xbuf1)
    step = pl.program_id(0) * pl.num_programs(1) + pl.program_id(1)
    last = pl.num_programs(0) * pl.num_programs(1) - 1

    def start_rows(ids_ref, slot):
        for r in range(tm):
            _row_copy(x_hbm, bufs[slot], sem, slot, ids_ref[0, 0, r], r).start()

    def wait_rows(slot):
        for r in range(tm):
            _row_copy(x_hbm, bufs[slot], sem, slot, 0, r).wait()

    @pl.when(step == 0)
    def _():
        start_rows(idx_ref, 0)

    def run(slot):
        start_rows(idx_next_ref, 1 - slot)
        wait_rows(slot)
        xb = bufs[slot][...].astype(BF16)
        hg = _dot(xb, wg_ref[0, 0])
        hu = _dot(xb, wu_ref[0, 0])
        hid = (hg * jax.nn.sigmoid(hg) * hu).astype(BF16)
        o_ref[...] = (_dot(hid, wd_ref[0, 0]) * gate_ref[...]).astype(o_ref.dtype)

        @pl.when(step == last)
        def _():
            wait_rows(1 - slot)

    for slot in range(2):
        @pl.when(step % 2 == slot)
        def _():
            run(slot)


def _experts(hffn, idx, gate, wg, wu, wd, layer, cap):
    n, d = hffn.shape
    _, e, _, f = wg.shape
    tm = min(256, cap)
    nt = cap // tm
    idx3 = idx.reshape(e * nt, 1, tm)
    last = e * nt - 1
    return pl.pallas_call(
        _expert_kernel, grid=(e, nt),
        in_specs=[pl.BlockSpec((1, 1, tm), lambda ei, ti: (ei * nt + ti, 0, 0), memory_space=pltpu.SMEM),
                  pl.BlockSpec((1, 1, tm), lambda ei, ti: (jnp.minimum(ei * nt + ti + 1, last), 0, 0),
                               memory_space=pltpu.SMEM),
                  pl.BlockSpec(memory_space=pl.ANY),
                  pl.BlockSpec((tm, 1), lambda ei, ti: (ei * nt + ti, 0)),
                  pl.BlockSpec((1, 1, d, f), lambda ei, ti: (layer, ei, 0, 0)),
                  pl.BlockSpec((1, 1, d, f), lambda ei, ti: (layer, ei, 0, 0)),
                  pl.BlockSpec((1, 1, f, d), lambda ei, ti: (layer, ei, 0, 0))],
        out_specs=pl.BlockSpec((tm, d), lambda ei, ti: (ei * nt + ti, 0)),
        out_shape=jax.ShapeDtypeStruct((e * cap, d), BF16),
        scratch_shapes=[pltpu.VMEM((tm, d), F32), pltpu.VMEM((tm, d), F32), pltpu.SemaphoreType.DMA((2,))],
        compiler_params=_params("arbitrary", "arbitrary"), name="moe_experts")(
            idx3, idx3, hffn, gate.reshape(e * cap, 1), wg, wu, wd)


def _slab_copy(ye_hbm, slab, sem, buf, src_row, slot):
    return pltpu.make_async_copy(ye_hbm.at[pl.ds(pl.multiple_of(src_row, PACKED_ROWS), SLAB_CHUNK)],
                                 slab.at[buf, pl.ds(pl.multiple_of(slot * SLAB_CHUNK, SLAB_CHUNK), SLAB_CHUNK)],
                                 sem.at[buf])


def _combine_kernel(n_exp, meta_ref, meta_next_ref, x_ref, mod_ref, pos_ref, ye_hbm, out_ref, slab, acc, sem):
    tm = x_ref.shape[0]
    i = pl.program_id(0)
    buf = i % 2
    total = meta_ref[0, 0, 0]
    total_next = jnp.where(i + 1 < pl.num_programs(0), meta_next_ref[0, 0, 0], 0)

    def start_chunks(ref, count, b):
        def body(g, carry):
            _slab_copy(ye_hbm, slab, sem, b, ref[0, 0, 1 + n_exp + g], g).start()
            return carry
        lax.fori_loop(0, count, body, 0)

    @pl.when(i == 0)
    def _():
        slab[...] = jnp.zeros_like(slab)
        start_chunks(meta_ref, total, 0)

    start_chunks(meta_next_ref, total_next, 1 - buf)

    def wait(g, carry):
        _slab_copy(ye_hbm, slab, sem, buf, 0, g).wait()
        return carry

    pos = pos_ref[...]
    sub_e = _iota((n_exp, 1), 0)
    shift = jnp.zeros((n_exp, 1), I32)
    for e in range(n_exp):
        shift = jnp.where(sub_e == e, meta_ref[0, 0, 1 + e], shift)
    target = jnp.where(pos >= 0, pos + shift, -1)
    acc[...] = jnp.zeros_like(acc)
    lax.fori_loop(0, total, wait, 0)

    kc_rows = 2 * LANES
    chunks_per_kc = kc_rows // SLAB_CHUNK

    def kbody(kc, carry):
        base = pl.multiple_of(kc * kc_rows, kc_rows)
        rows = slab[buf, pl.ds(base, kc_rows), :]
        slab_row = _iota((kc_rows, tm), 0) + base
        place_t = target[0:1, :] == slab_row
        for e in range(1, n_exp):
            place_t = place_t | (target[e:e + 1, :] == slab_row)
        acc[...] += _dot_tn(place_t.astype(F32).astype(BF16), rows)
        return carry

    lax.fori_loop(0, (total + chunks_per_kc - 1) // chunks_per_kc, kbody, 0)
    out_ref[...] = x_ref[...] + mod_ref[0, 5:6, :] * acc[...]


def _combine(x, mod, pos_t, meta, ye, t, max_chunks):
    n, d = x.shape
    tm = LANES
    n_exp = N_EXPERTS
    ntiles = n // tm
    kc_rows = 2 * LANES
    slab_rows = -(-(max_chunks * SLAB_CHUNK) // kc_rows) * kc_rows
    return pl.pallas_call(
        functools.partial(_combine_kernel, n_exp), grid=(ntiles,),
        in_specs=[pl.BlockSpec((1, 1, meta.shape[2]), lambda i: (i, 0, 0), memory_space=pltpu.SMEM),
                  pl.BlockSpec((1, 1, meta.shape[2]), lambda i: (jnp.minimum(i + 1, ntiles - 1), 0, 0),
                               memory_space=pltpu.SMEM),
                  pl.BlockSpec((tm, d), lambda i: (i, 0)),
                  _mod_spec(mod, tm, t),
                  pl.BlockSpec((n_exp, tm), lambda i: (0, i)),
                  pl.BlockSpec(memory_space=pl.ANY)],
        out_specs=pl.BlockSpec((tm, d), lambda i: (i, 0)),
        out_shape=jax.ShapeDtypeStruct((n, d), F32),
        scratch_shapes=[pltpu.VMEM((2, slab_rows, d), BF16), pltpu.VMEM((tm, d), F32),
                        pltpu.SemaphoreType.DMA((2,))],
        compiler_params=_params("arbitrary"), name="moe_combine")(meta, meta, x, mod, pos_t, ye)


def _moe(x, mod, t, norm2, w_router, wg, wu, wd, layer):
    n, d = x.shape
    n_exp = N_EXPERTS
    cap = (EC_CAPACITY_FACTOR * n) // n_exp
    hffn, aff = _norm_call("router", x, norm2, mod, t, rows=(3, 4), w=w_router)
    npad = ROUTE_BLOCKS * LANES
    assert n <= npad and n % LANES == 0 and cap % SLAB_CHUNK == 0
    aff_t = jnp.pad(aff.T, ((0, 0), (0, npad - n)), constant_values=-1.0)
    idx, gate, pos, tot, off = _route(aff_t.reshape(n_exp, ROUTE_BLOCKS, LANES), cap)
    ye = _experts(hffn, idx, gate, wg, wu, wd, layer, cap)
    ntiles = n // LANES
    span = min(LANES + PACKED_ROWS + SLAB_CHUNK - (LANES + PACKED_ROWS) % SLAB_CHUNK, cap)
    per_expert = span // SLAB_CHUNK
    max_chunks = n_exp * per_expert
    start = off[:, 0, :ntiles].T.astype(I32)
    cnt = tot[:, 0, :ntiles].T.astype(I32)
    start_al = jnp.minimum((start // PACKED_ROWS) * PACKED_ROWS, cap - span)
    nch = jnp.where(cnt > 0, (start - start_al + cnt + SLAB_CHUNK - 1) // SLAB_CHUNK, 0)
    cend = jnp.cumsum(nch, axis=1)
    cstart = cend - nch
    slots = jnp.arange(max_chunks, dtype=I32)
    owns = (cstart[:, None, :] <= slots[None, :, None]) & (slots[None, :, None] < cend[:, None, :])
    base = start_al + jnp.arange(n_exp, dtype=I32)[None, :] * cap
    src = jnp.sum(jnp.where(owns, (base - cstart * SLAB_CHUNK)[:, None, :], 0), axis=2) + jnp.where(
        jnp.any(owns, axis=2), slots[None, :] * SLAB_CHUNK, 0)
    meta = jnp.concatenate([cend[:, -1:], cstart * SLAB_CHUNK - start_al, src], axis=1)
    return _combine(x, mod, pos.reshape(n_exp, npad), meta.reshape(ntiles, 1, 1 + n_exp + max_chunks), ye, t,
                    max_chunks)


def _rope_tables(t):
    n_rows = t // GRID_W
    axis_dim = A_HD // 2
    row = jnp.repeat(jnp.arange(n_rows, dtype=F32), GRID_W)
    col = jnp.tile(jnp.arange(GRID_W, dtype=F32), n_rows)
    inv_freq = ROPE_THETA ** (-jnp.arange(0, axis_dim, 2, dtype=F32) / axis_dim)
    ang = jnp.concatenate([row[:, None] * inv_freq, col[:, None] * inv_freq], axis=-1)
    cos, sin = jnp.cos(ang), jnp.sin(ang)
    return jnp.concatenate([cos, cos], axis=-1), jnp.concatenate([-sin, sin], axis=-1)


def _pad_lora(w1, w2):
    r = w1.shape[2]
    rp = -(-r // LANES) * LANES
    w1p = jnp.pad(w1, ((0, 0), (0, 0), (0, rp - r)))
    w2p = jnp.pad(w2, ((0, 0), (0, rp - r), (0, 0)))
    return jnp.concatenate([w1p[0], w1p[1]], axis=1).astype(BF16), w2p.astype(BF16)


def _pair_state(s):
    b, _, heads, hd, _ = s.shape
    sp = s.reshape(b, 2, heads // 2, 2, hd, hd)
    out = jnp.zeros((b, 2, heads // 2, 2 * hd, 2 * hd), F32)
    out = out.at[:, :, :, :hd, :hd].set(sp[:, :, :, 0])
    return out.at[:, :, :, hd:, hd:].set(sp[:, :, :, 1])


def _unpair_state(sp):
    b, _, npairs, _, _ = sp.shape
    hd = R_HD
    s = jnp.stack([sp[:, :, :, :hd, :hd], sp[:, :, :, hd:, hd:]], axis=3)
    return s.reshape(b, 2, 2 * npairs, hd, hd)


def _trunk(x, mod_all, b, t, latent_states, p):
    n, d = x.shape
    latent = latent_states is not None
    produced = []
    for i in range(DEPTH):
        kind, j = i % N_MIXERS, i // N_MIXERS
        mod = mod_all[i]
        if kind == 0:
            hb, gates = _norm_call("gate", x, p["norm1"][i], mod, t, w=p["m_gate_w"][j], b=p["m_gate_b"][j],
                                   out_dtype=BF16)
            proj = _mm(hb, p["m_qkvo"][j], 2048)
            res = _mlstm_scan(proj, gates, b, t, latent_states[i] if latent else None, not latent)
            if not latent:
                produced += [res[1], res[2], res[3][..., 0]]
            x = _mlstm_out(res[0], proj, p["m_hnorm"][j], p["m_out"][j], x, mod, t)
        elif kind == 1:
            hb = _norm_call("plain", x, p["norm1"][i], mod, t, out_dtype=BF16)
            q, k, v = _attn_qkv(hb, p["a_qkv"][j], p["a_qnorm"][j], p["a_knorm"][j], t,
                                _rope_tables(t) if latent else None)
            cache = None
            if latent:
                ck, cv = latent_states[i]
                kvd = A_KV_HEADS * A_HD
                cache = (ck.reshape(-1, kvd), cv.reshape(-1, kvd))
            else:
                produced += [k.reshape(b, t, A_KV_HEADS, A_HD), v.reshape(b, t, A_KV_HEADS, A_HD)]
            o = _attention(q, k, v, b, t, cache)
            x = _resid_mm(o, p["a_out"][j], x, mod, t)
        else:
            h = _norm_call("plain", x, p["norm1"][i], mod, t)
            mix = p["r_mix"][j]
            rkv = _rwkv_rkv(h, mix[jnp.array([0, 2, 3])].reshape(3, 1, d), p["r_rkv"][j], t)
            lw, asig, g = _rwkv_lora(h, mix[jnp.array([1, 4, 5])], p["r_w1c"][j], p["r_a1c"][j], p["r_g1"][j],
                                     p["r_w2p"][j], p["r_a2p"][j], p["r_g2"][j],
                                     p["r_w0"][j].reshape(2, 1, d), p["r_a0"][j].reshape(2, 1, d), t)
            s0 = _pair_state(latent_states[i][0]) if latent else None
            res = _rwkv_scan(rkv, lw, asig, p["r_kk"][j], p["r_ka"][j], p["r_rk"][j].reshape(-1), b, t, s0,
                             not latent)
            if not latent:
                produced.append(_unpair_state(res[2]))
            x = _rwkv_out(res[0], res[1], g, p["r_lnx_w"][j], p["r_lnx_b"][j], p["r_out"][j], x, mod, t)
        x = _moe(x, mod, t, p["norm2"][i], p["router"][i], p["exp_gate"], p["exp_up"], p["exp_down"], i)
    return _norm_call("final", x, p["final_norm"], None, t), produced


def kernel(x_prompt, x_sample, state_l0_C, state_l0_n, state_l0_m, cache_l1_k, cache_l1_v, state_l2_S,
           state_l3_C, state_l3_n, state_l3_m, c, c_ctx, ada_w, ada_b, norm1, norm2, router, exp_gate,
           exp_up, exp_down, m_qkvo, m_gate_w, m_gate_b, m_hnorm, m_out, a_qkv, a_qnorm, a_knorm, a_out,
           r_mix, r_rkv, r_w0, r_w1, r_w2, r_a0, r_a1, r_a2, r_g1, r_g2, r_kk, r_ka, r_rk, r_lnx_w,
           r_lnx_b, r_out, final_norm):
    bc, tc, d = x_prompt.shape
    bl, tl, _ = x_sample.shape
    depth = ada_w.shape[0]
    assert bl + 1 <= SUBLANES

    cond8 = jnp.zeros((SUBLANES, d), F32).at[0].set(c_ctx).at[1:1 + bl].set(c)
    mod = _adaln(cond8, ada_w, ada_b).reshape(depth, SUBLANES, 6, d)
    mod_ctx = mod[:, 0:1]
    mod_lat = mod[:, 1:1 + bl]

    w1c, w2p, a1c, a2p = [], [], [], []
    for j in range(r_w1.shape[0]):
        w1, w2 = _pad_lora(r_w1[j], r_w2[j])
        a1, a2 = _pad_lora(r_a1[j], r_a2[j])
        w1c.append(w1), w2p.append(w2), a1c.append(a1), a2p.append(a2)
    p = dict(norm1=norm1, norm2=norm2, router=router, final_norm=final_norm,
             exp_gate=exp_gate.astype(BF16), exp_up=exp_up.astype(BF16), exp_down=exp_down.astype(BF16),
             m_qkvo=m_qkvo.astype(BF16), m_gate_w=m_gate_w, m_gate_b=m_gate_b, m_hnorm=m_hnorm,
             m_out=m_out.astype(BF16), a_qkv=a_qkv.astype(BF16), a_qnorm=a_qnorm, a_knorm=a_knorm,
             a_out=a_out.astype(BF16), r_mix=r_mix, r_rkv=r_rkv.astype(BF16), r_w0=r_w0, r_a0=r_a0,
             r_w1c=w1c, r_w2p=w2p, r_a1c=a1c, r_a2p=a2p, r_g1=r_g1.astype(BF16), r_g2=r_g2.astype(BF16),
             r_kk=r_kk, r_ka=r_ka, r_rk=r_rk, r_lnx_w=r_lnx_w, r_lnx_b=r_lnx_b, r_out=r_out.astype(BF16))

    y_prompt, new_state = _trunk(x_prompt.reshape(bc * tc, d), mod_ctx, bc, tc, None, p)
    lat_states = [(state_l0_C, state_l0_n, state_l0_m), (cache_l1_k, cache_l1_v), (state_l2_S,),
                  (state_l3_C, state_l3_n, state_l3_m)]
    y_sample, _ = _trunk(x_sample.reshape(bl * tl, d), mod_lat, bl, tl, lat_states, p)
    return (y_prompt.reshape(bc, tc, d), y_sample.reshape(bl, tl, d), *new_state)
```

```python
import functools

import jax
import jax.numpy as jnp
from jax import lax
from jax.experimental import pallas as pl
from jax.experimental.pallas import tpu as pltpu

F32 = jnp.float32
BF16 = jnp.bfloat16
I32 = jnp.int32

D_MODEL = 2048
BATCH = 32
SEQ = 256
DEPTH = 4
DEC_BATCH = 4
DEC_SEQ = 4096
PAST_LEN = 256
GRID_W = 64
N_MIXERS = 3
NORM_EPS = 1e-6
M_HEADS = 8
M_DK = D_MODEL // 2 // M_HEADS
M_DV = D_MODEL // M_HEADS
M_GATE_CAP = 15.0
A_HEADS = 16
A_KV_HEADS = 4
A_HD = 128
ROPE_THETA = 10000.0
R_HD = 64
R_LN_EPS = 64e-5
N_EXPERTS = 16
EC_CAPACITY_FACTOR = 2
D_EXPERT = 1024

LANES = 128
SUBLANES = 8
PACKED_ROWS = 16
VMEM_LIMIT_BYTES = 56 * 2 ** 20

MLSTM_CHUNK = 256
ATTN_Q_ROWS = 128
ATTN_SUB_ROWS = 64
RWKV_CHUNK = 64
RWKV_PAIRS_PER_GROUP = 16
RWKV_SOLVE_PASSES = 1
ROUTE_BLOCKS = 128
SCAN_OUT_DTYPE = jnp.float32
SLAB_CHUNK = 32


def _params(*sem):
    return pltpu.CompilerParams(dimension_semantics=sem, vmem_limit_bytes=VMEM_LIMIT_BYTES)


def _dot(a, b):
    return jnp.dot(a, b, preferred_element_type=F32)


def _dot_nt(a, b):
    return lax.dot_general(a, b, (((1,), (1,)), ((), ())), preferred_element_type=F32)


def _dot_tn(a, b):
    return lax.dot_general(a, b, (((0,), (0,)), ((), ())), preferred_element_type=F32)


def _split2(x):
    hi = x.astype(BF16)
    lo = (x - hi.astype(F32)).astype(BF16)
    return hi, lo


def _split3(x):
    h1 = x.astype(BF16)
    r1 = x - h1.astype(F32)
    h2 = r1.astype(BF16)
    h3 = (r1 - h2.astype(F32)).astype(BF16)
    return h1, h2, h3


def _dot3(a, w):
    ah, al = _split2(a)
    wh, wl = _split2(w)
    return _dot(ah, wh) + _dot(al, wh) + _dot(ah, wl)


_NN = (((1,), (0,)), ((), ()))
_NT = (((1,), (1,)), ((), ()))
_TN = (((0,), (0,)), ((), ()))


def _mm_passes(a, b, dims, passes):
    def f(x, y):
        return lax.dot_general(x, y, dims, preferred_element_type=F32)
    if passes == 1:
        return f(a.astype(BF16), b.astype(BF16))
    ah, al = _split2(a)
    bh, bl = _split2(b)
    return f(ah, bh) + f(al, bh) + f(ah, bl)


def _iota(shape, dim):
    return lax.broadcasted_iota(I32, shape, dim)


def _log_sigmoid(z):
    return jnp.minimum(z, 0.0) - jnp.log(1.0 + jnp.exp(-jnp.abs(z)))


def _adaln_kernel(c_ref, w_ref, b_ref, o_ref):
    c = c_ref[...]
    s = c * jax.nn.sigmoid(c)
    hi, lo = _split2(s)
    w = w_ref[0].astype(BF16)
    o_ref[0] = _dot(hi, w) + _dot(lo, w) + b_ref[0]


def _adaln(cond8, ada_w, ada_b):
    depth, d, d6 = ada_w.shape
    tn = min(d, 1024)
    return pl.pallas_call(
        _adaln_kernel,
        grid=(depth, d6 // tn),
        in_specs=[pl.BlockSpec((SUBLANES, d), lambda l, j: (0, 0)),
                  pl.BlockSpec((1, d, tn), lambda l, j: (l, 0, j)),
                  pl.BlockSpec((1, 1, tn), lambda l, j: (l, 0, j))],
        out_specs=pl.BlockSpec((1, SUBLANES, tn), lambda l, j: (l, 0, j)),
        out_shape=jax.ShapeDtypeStruct((depth, SUBLANES, d6), F32),
        compiler_params=_params("arbitrary", "arbitrary"),
        name="adaln",
    )(cond8, ada_w, ada_b.reshape(depth, 1, d6))


def _normed(x_ref, g_ref):
    x = x_ref[...]
    return x * lax.rsqrt(jnp.mean(x * x, axis=-1, keepdims=True) + NORM_EPS) * g_ref[...]


def _norm_plain_kernel(sh, sc, x_ref, g_ref, mod_ref, o_ref):
    h = _normed(x_ref, g_ref) * (1.0 + mod_ref[0, sc:sc + 1, :]) + mod_ref[0, sh:sh + 1, :]
    o_ref[...] = h.astype(o_ref.dtype)


def _norm_gate_kernel(sh, sc, x_ref, g_ref, mod_ref, w_ref, b_ref, o_ref, gate_ref):
    h = _normed(x_ref, g_ref) * (1.0 + mod_ref[0, sc:sc + 1, :]) + mod_ref[0, sh:sh + 1, :]
    o_ref[...] = h.astype(o_ref.dtype)
    z = _dot3(h, w_ref[...]) + b_ref[...]
    z = M_GATE_CAP * jnp.tanh(z / M_GATE_CAP)
    is_forget = ((_iota(z.shape, 1) // M_HEADS) % 2) == 1
    gate_ref[...] = jnp.where(is_forget, _log_sigmoid(z), z)


def _mixer_residual_and_route(x_ref, mod_ref, mix, g2_ref, wr_ref, out_ref, hffn_ref, aff_ref):
    x = x_ref[...] + mod_ref[0, 2:3, :] * mix
    out_ref[...] = x
    h = x * lax.rsqrt(jnp.mean(x * x, axis=-1, keepdims=True) + NORM_EPS) * g2_ref[...]
    h = h * (1.0 + mod_ref[0, 4:5, :]) + mod_ref[0, 3:4, :]
    hffn_ref[...] = h
    z = _dot3(h, wr_ref[...])
    e = jnp.exp(z - jnp.max(z, axis=-1, keepdims=True))
    aff_ref[...] = e / jnp.sum(e, axis=-1, keepdims=True)


def _route_tail_specs(n, d, tm, norm2, w_router):
    ns = w_router.shape[1]
    in_specs = [pl.BlockSpec((1, d), lambda i: (0, 0)), pl.BlockSpec((d, ns), lambda i: (0, 0))]
    out_specs = [pl.BlockSpec((tm, d), lambda i: (i, 0)), pl.BlockSpec((tm, d), lambda i: (i, 0)),
                 pl.BlockSpec((tm, ns), lambda i: (i, 0))]
    out_shape = [jax.ShapeDtypeStruct((n, d), F32), jax.ShapeDtypeStruct((n, d), F32),
                 jax.ShapeDtypeStruct((n, ns), F32)]
    return in_specs, [norm2.reshape(1, d), w_router], out_specs, out_shape


def _norm_final_kernel(x_ref, g_ref, o_ref):
    o_ref[...] = _normed(x_ref, g_ref)


def _row_tile(n, t, want):
    tm = min(want, t)
    assert t % tm == 0 and n % tm == 0
    return tm


def _mod_spec(mod, tm, t):
    d = mod.shape[-1]
    if mod.shape[0] == 1:
        return pl.BlockSpec((1, 6, d), lambda i: (0, 0, 0))
    return pl.BlockSpec((1, 6, d), lambda i: ((i * tm) // t, 0, 0))


def _norm_call(kind, x, gain, mod, t, rows=(0, 1), w=None, b=None, out_dtype=F32):
    n, d = x.shape
    tm = _row_tile(n, t, 256)
    xs = pl.BlockSpec((tm, d), lambda i: (i, 0))
    gs = pl.BlockSpec((1, d), lambda i: (0, 0))
    ms = None if mod is None else _mod_spec(mod, tm, t)
    gain = gain.reshape(1, d)
    if kind == "final":
        return pl.pallas_call(_norm_final_kernel, grid=(n // tm,), in_specs=[xs, gs], out_specs=xs,
                              out_shape=jax.ShapeDtypeStruct((n, d), F32),
                              compiler_params=_params("arbitrary"), name="norm_final")(x, gain)
    sh, sc = rows
    if kind == "plain":
        return pl.pallas_call(functools.partial(_norm_plain_kernel, sh, sc), grid=(n // tm,),
                              in_specs=[xs, gs, ms], out_specs=xs,
                              out_shape=jax.ShapeDtypeStruct((n, d), out_dtype),
                              compiler_params=_params("arbitrary"), name="norm_plain")(x, gain, mod)
    ns = w.shape[1]
    ws = pl.BlockSpec((d, ns), lambda i: (0, 0))
    ps = pl.BlockSpec((tm, ns), lambda i: (i, 0))
    assert kind == "gate"
    return pl.pallas_call(
        functools.partial(_norm_gate_kernel, sh, sc), grid=(n // tm,),
        in_specs=[xs, gs, ms, ws, pl.BlockSpec((1, ns), lambda i: (0, 0))],
        out_specs=[xs, ps],
        out_shape=[jax.ShapeDtypeStruct((n, d), out_dtype), jax.ShapeDtypeStruct((n, ns), F32)],
        compiler_params=_params("arbitrary"), name="norm_gate")(x, gain, mod, w, b.reshape(1, ns))


def _mm_kernel(x_ref, w_ref, o_ref):
    o_ref[...] = _dot(x_ref[...], w_ref[...])


def _mm(x, w, tn):
    n, k = x.shape
    m = w.shape[1]
    tm = min(512, n)
    tn = min(tn, m)
    return pl.pallas_call(
        _mm_kernel, grid=(m // tn, n // tm),
        in_specs=[pl.BlockSpec((tm, k), lambda j, i: (i, 0)), pl.BlockSpec((k, tn), lambda j, i: (0, j))],
        out_specs=pl.BlockSpec((tm, tn), lambda j, i: (i, j)),
        out_shape=jax.ShapeDtypeStruct((n, m), F32),
        compiler_params=_params("arbitrary", "arbitrary"), name="mm")(x, w)


def _mlstm_kernel(has_init, emit_state, *refs):
    q_ref, k_ref, v_ref, g_ref, gt_ref = refs[:5]
    pos = 5
    if has_init:
        c0_ref, n0_ref, m0_ref = refs[pos:pos + 3]
        pos += 3
    h_ref = refs[pos]
    pos += 1
    if emit_state:
        cout_ref, nout_ref, mout_ref = refs[pos:pos + 3]
        pos += 3
    c_scr, n_scr, m_scr = refs[pos:pos + 3]

    d = pl.program_id(1)
    c = pl.program_id(2)
    nc = pl.num_programs(2)
    fwd = d == 0
    heads = M_HEADS
    length = q_ref.shape[0]

    @pl.when(c == 0)
    def _():
        if has_init:
            c_scr[...] = c0_ref[0, 0]
            n_scr[...] = n0_ref[0, 0]
            m_scr[...] = m0_ref[0, 0]
        else:
            c_scr[...] = jnp.zeros_like(c_scr)
            n_scr[...] = jnp.zeros_like(n_scr)
            m_scr[...] = jnp.zeros_like(m_scr)

    sgn = jnp.where(fwd, 1, -1)
    row = _iota((length, length), 0)
    col = _iota((length, length), 1)
    mask = ((row - col) * sgn) >= 0
    mask_t = ((col - row) * sgn) >= 0
    scale = M_DK ** -0.5
    g = g_ref[...]
    gt = gt_ref[0]

    tri = mask.astype(F32).astype(BF16)
    tri_t = mask_t.astype(F32).astype(BF16)
    g1, g2, g3 = _split3(g)
    cum_cols = _dot(tri, g1) + _dot(tri, g2) + _dot(tri, g3)
    t1, t2, t3 = _split3(gt)
    cum_rows = _dot(t1, tri_t) + _dot(t2, tri_t) + _dot(t3, tri_t)

    def pick_col(x, kind, h):
        i, j = kind * heads + h, (kind + 2) * heads + h
        return jnp.where(fwd, x[:, i:i + 1], x[:, j:j + 1])

    def pick_row(x, kind, h):
        i, j = kind * heads + h, (kind + 2) * heads + h
        return jnp.where(fwd, x[i:i + 1, :], x[j:j + 1, :])

    def each(fn, *lists):
        return [fn(*args) for args in zip(*lists)]

    hs = list(range(heads))
    ig_row = [pick_row(gt, 0, h) for h in hs]
    ig_col = [pick_col(g, 0, h) for h in hs]
    b_col = [pick_col(cum_cols, 1, h) for h in hs]
    b_row = [pick_row(cum_rows, 1, h) for h in hs]
    m_prev = [m_scr[h:h + 1, 0:1] for h in hs]
    c_prev = [c_scr[h] for h in hs]
    n_prev = [n_scr[h:h + 1, :] for h in hs]
    qh = [q_ref[:, h * M_DK:(h + 1) * M_DK] for h in hs]
    kh = [k_ref[:, h * M_DK:(h + 1) * M_DK] * scale for h in hs]
    qb = each(lambda x: x.astype(BF16), qh)
    kb = each(lambda x: x.astype(BF16), kh)
    vb = [v_ref[:, h * M_DV:(h + 1) * M_DV].astype(BF16) for h in hs]
    qk = each(_dot_nt, qb, kb)
    qc = each(lambda q_, c_: _dot(q_, c_.astype(BF16)), qb, c_prev)
    dmat = each(lambda bc, br, ir: jnp.where(mask, bc - br + ir, -jnp.inf), b_col, b_row, ig_row)
    a = each(lambda bc, mp: bc + mp, b_col, m_prev)
    m_t = each(lambda a_, d_: jnp.maximum(a_, jnp.max(d_, axis=1, keepdims=True)), a, dmat)
    w_inter = each(lambda a_, m_: jnp.exp(a_ - m_), a, m_t)
    s = each(lambda x_, d_, m_: x_ * jnp.exp(d_ - m_), qk, dmat, m_t)
    sv = each(lambda s_, v_: _dot(s_.astype(BF16), v_), s, vb)
    num = each(lambda w_, x_, y_: w_ * x_ + y_, w_inter, qc, sv)
    den = each(lambda w_, q_, n_, s_: w_ * jnp.sum(q_ * n_, axis=1, keepdims=True)
               + jnp.sum(s_, axis=1, keepdims=True), w_inter, qh, n_prev, s)
    for h in hs:
        h_ref[0, :, h * M_DV:(h + 1) * M_DV] = (
            num[h] / jnp.maximum(jnp.abs(den[h]), jnp.exp(-m_t[h]))).astype(h_ref.dtype)

    b_last = each(lambda bc: jnp.where(fwd, bc[length - 1:length, :], bc[0:1, :]), b_col)
    g_col = each(lambda bl, bc, ic: bl - bc + ic, b_last, b_col, ig_col)
    m_new = each(lambda bl, mp, gc: jnp.maximum(bl + mp, jnp.max(gc, axis=0, keepdims=True)), b_last, m_prev, g_col)
    decay = each(lambda bl, mp, mn: jnp.exp(bl + mp - mn), b_last, m_prev, m_new)
    wk = each(lambda gc, mn, k_: jnp.exp(gc - mn) * k_, g_col, m_new, kh)
    kv = each(lambda w_, v_: _dot_tn(w_.astype(BF16), v_), wk, vb)
    for h in hs:
        c_scr[h] = decay[h] * c_prev[h] + kv[h]
        n_scr[h:h + 1, :] = decay[h] * n_prev[h] + jnp.sum(wk[h], axis=0, keepdims=True)
        m_scr[h:h + 1, :] = jnp.broadcast_to(m_new[h], (1, LANES))

    if emit_state:
        @pl.when(c == nc - 1)
        def _():
            cout_ref[0, 0] = c_scr[...]
            nout_ref[0, 0] = n_scr[...]
            mout_ref[0, 0] = m_scr[...]


def _mlstm_scan(proj, gates, b, t, init, emit_state):
    n = proj.shape[0]
    heads, dk, dv = M_HEADS, M_DK, M_DV
    qk, mv = heads * dk, heads * dv
    length = min(MLSTM_CHUNK, t)
    nc = t // length
    gates_t = gates.reshape(n // length, length, 4 * heads).transpose(0, 2, 1)

    def tile(bi, di, ci):
        return bi * nc + ci + di * (nc - 1 - 2 * ci)

    in_specs = [pl.BlockSpec((length, qk), lambda bi, di, ci: (tile(bi, di, ci), 0)),
                pl.BlockSpec((length, qk), lambda bi, di, ci: (tile(bi, di, ci), 1)),
                pl.BlockSpec((length, mv), lambda bi, di, ci: (tile(bi, di, ci), 1)),
                pl.BlockSpec((length, 4 * heads), lambda bi, di, ci: (tile(bi, di, ci), 0)),
                pl.BlockSpec((1, 4 * heads, length), lambda bi, di, ci: (tile(bi, di, ci), 0, 0))]
    args = [proj, proj, proj, gates, gates_t]
    state_specs = [pl.BlockSpec((1, 1, heads, dk, dv), lambda bi, di, ci: (bi, di, 0, 0, 0)),
                   pl.BlockSpec((1, 1, heads, dk), lambda bi, di, ci: (bi, di, 0, 0)),
                   pl.BlockSpec((1, 1, heads, LANES), lambda bi, di, ci: (bi, di, 0, 0))]
    if init is not None:
        c0, n0, m0 = init
        in_specs += state_specs
        args += [c0, n0, jnp.broadcast_to(m0[..., None], m0.shape + (LANES,))]
    out_specs = [pl.BlockSpec((1, length, mv), lambda bi, di, ci: (di, tile(bi, di, ci), 0))]
    out_shape = [jax.ShapeDtypeStruct((2, n, mv), SCAN_OUT_DTYPE)]
    if emit_state:
        out_specs += state_specs
        out_shape += [jax.ShapeDtypeStruct((b, 2, heads, dk, dv), F32),
                      jax.ShapeDtypeStruct((b, 2, heads, dk), F32),
                      jax.ShapeDtypeStruct((b, 2, heads, LANES), F32)]
    return pl.pallas_call(
        functools.partial(_mlstm_kernel, init is not None, emit_state),
        grid=(b, 2, nc), in_specs=in_specs, out_specs=out_specs, out_shape=out_shape,
        scratch_shapes=[pltpu.VMEM((heads, dk, dv), F32), pltpu.VMEM((heads, dk), F32),
                        pltpu.VMEM((heads, LANES), F32)],
        compiler_params=_params("arbitrary", "arbitrary", "arbitrary"), name="mlstm_scan")(*args)


def _mlstm_out_kernel(hs_ref, o_ref, gain_ref, w_ref, x_ref, mod_ref, g2_ref, wr_ref, out_ref, hffn_ref, aff_ref):
    hs = hs_ref[0].astype(F32) + hs_ref[1].astype(F32)
    parts = []
    for h in range(M_HEADS):
        seg = hs[:, h * M_DV:(h + 1) * M_DV]
        parts.append(seg * lax.rsqrt(jnp.mean(seg * seg, axis=-1, keepdims=True) + NORM_EPS))
    hn = jnp.concatenate(parts, axis=1) * gain_ref[...]
    lhs = (jax.nn.sigmoid(o_ref[...]) * hn).astype(BF16)
    _mixer_residual_and_route(x_ref, mod_ref, _dot(lhs, w_ref[...]), g2_ref, wr_ref, out_ref, hffn_ref, aff_ref)


def _mlstm_out(hs2, proj, gain, w_out, x, mod, t, norm2, w_router):
    n, d = x.shape
    mv = M_HEADS * M_DV
    tm = _row_tile(n, t, 256)
    tail_in, tail_args, out_specs, out_shape = _route_tail_specs(n, d, tm, norm2, w_router)
    return pl.pallas_call(
        _mlstm_out_kernel, grid=(n // tm,),
        in_specs=[pl.BlockSpec((2, tm, mv), lambda i: (0, i, 0)),
                  pl.BlockSpec((tm, mv), lambda i: (i, 2)),
                  pl.BlockSpec((1, mv), lambda i: (0, 0)),
                  pl.BlockSpec((mv, d), lambda i: (0, 0)),
                  pl.BlockSpec((tm, d), lambda i: (i, 0)),
                  _mod_spec(mod, tm, t)] + tail_in,
        out_specs=out_specs, out_shape=out_shape,
        compiler_params=_params("arbitrary"), name="mlstm_out")(
            hs2, proj, gain.reshape(1, mv), w_out, x, mod, *tail_args)


def _attn_qkv_kernel(use_rope, *refs):
    if use_rope:
        x_ref, w_ref, qg_ref, kg_ref, cos_ref, sin_ref, q_ref, k_ref, v_ref = refs
    else:
        x_ref, w_ref, qg_ref, kg_ref, q_ref, k_ref, v_ref = refs
    acc = _dot(x_ref[...], w_ref[...])
    qd, kvd = A_HEADS * A_HD, A_KV_HEADS * A_HD

    def head_norm(xh, gain):
        xh = xh * lax.rsqrt(jnp.mean(xh * xh, axis=-1, keepdims=True) + NORM_EPS) * gain
        if use_rope:
            xh = xh * cos_ref[...] + pltpu.roll(xh, A_HD // 2, axis=1) * sin_ref[...]
        return xh

    for h in range(A_HEADS):
        qh = head_norm(acc[:, h * A_HD:(h + 1) * A_HD], qg_ref[...])
        q_ref[:, h * A_HD:(h + 1) * A_HD] = (qh * (A_HD ** -0.5)).astype(BF16)
    for h in range(A_KV_HEADS):
        k_ref[:, h * A_HD:(h + 1) * A_HD] = head_norm(acc[:, qd + h * A_HD:qd + (h + 1) * A_HD], kg_ref[...])
    v_ref[...] = acc[:, qd + kvd:]


def _attn_qkv(hb, w, qgain, kgain, t, rope):
    n, d = hb.shape
    qd, kvd = A_HEADS * A_HD, A_KV_HEADS * A_HD
    tm = _row_tile(n, t, 256)
    in_specs = [pl.BlockSpec((tm, d), lambda i: (i, 0)),
                pl.BlockSpec((d, qd + 2 * kvd), lambda i: (0, 0)),
                pl.BlockSpec((1, A_HD), lambda i: (0, 0)),
                pl.BlockSpec((1, A_HD), lambda i: (0, 0))]
    args = [hb, w, qgain.reshape(1, A_HD), kgain.reshape(1, A_HD)]
    if rope is not None:
        in_specs += [pl.BlockSpec((tm, A_HD), lambda i: (i % (t // tm), 0))] * 2
        args += list(rope)
    return pl.pallas_call(
        functools.partial(_attn_qkv_kernel, rope is not None), grid=(n // tm,),
        in_specs=in_specs,
        out_specs=[pl.BlockSpec((tm, qd), lambda i: (i, 0)),
                   pl.BlockSpec((tm, kvd), lambda i: (i, 0)),
                   pl.BlockSpec((tm, kvd), lambda i: (i, 0))],
        out_shape=[jax.ShapeDtypeStruct((n, qd), BF16), jax.ShapeDtypeStruct((n, kvd), F32),
                   jax.ShapeDtypeStruct((n, kvd), F32)],
        compiler_params=_params("arbitrary"), name="attn_qkv")(*args)


def _attn_kernel(has_cache, *refs):
    if has_cache:
        q_ref, kn_ref, vn_ref, kc_ref, vc_ref, o_ref, k_scr, v_scr = refs
    else:
        q_ref, kn_ref, vn_ref, o_ref, k_scr, v_scr = refs
    group = A_HEADS // A_KV_HEADS
    tq = q_ref.shape[0]
    t_new = kn_ref.shape[0]
    past = k_scr.shape[0] - t_new

    @pl.when(pl.program_id(2) == 0)
    def _():
        if has_cache:
            k_scr[0:past, :] = kc_ref[...].astype(BF16)
            v_scr[0:past, :] = vc_ref[...].astype(BF16)
        k_scr[past:, :] = kn_ref[...].astype(BF16)
        v_scr[past:, :] = vn_ref[...].astype(BF16)

    sub = min(ATTN_SUB_ROWS, tq)
    starts = list(range(0, tq, sub))
    q4 = [jnp.concatenate([q_ref[r0:r0 + sub, g * A_HD:(g + 1) * A_HD] for g in range(group)], axis=0)
          for r0 in starts]
    s = [_dot_nt(x, k_scr[...]) for x in q4]
    p = [jnp.exp(x - jnp.max(x, axis=-1, keepdims=True)) for x in s]
    pv = [_dot(x.astype(BF16), v_scr[...]) for x in p]
    o = [x / jnp.sum(y, axis=-1, keepdims=True) for x, y in zip(pv, p)]
    for r0, x in zip(starts, o):
        for g in range(group):
            o_ref[r0:r0 + sub, g * A_HD:(g + 1) * A_HD] = x[g * sub:(g + 1) * sub, :].astype(BF16)


def _attention(q, k, v, b, t, cache):
    n = q.shape[0]
    group = A_HEADS // A_KV_HEADS
    tq = min(ATTN_Q_ROWS, t)
    nq = t // tq
    past = 0 if cache is None else cache[0].shape[0] // b
    in_specs = [pl.BlockSpec((tq, group * A_HD), lambda bi, hi, qi: (bi * nq + qi, hi)),
                pl.BlockSpec((t, A_HD), lambda bi, hi, qi: (bi, hi)),
                pl.BlockSpec((t, A_HD), lambda bi, hi, qi: (bi, hi))]
    args = [q, k, v]
    if cache is not None:
        in_specs += [pl.BlockSpec((past, A_HD), lambda bi, hi, qi: (bi, hi))] * 2
        args += list(cache)
    return pl.pallas_call(
        functools.partial(_attn_kernel, cache is not None), grid=(b, A_KV_HEADS, nq),
        in_specs=in_specs,
        out_specs=pl.BlockSpec((tq, group * A_HD), lambda bi, hi, qi: (bi * nq + qi, hi)),
        out_shape=jax.ShapeDtypeStruct((n, A_HEADS * A_HD), BF16),
        scratch_shapes=[pltpu.VMEM((past + t, A_HD), BF16), pltpu.VMEM((past + t, A_HD), BF16)],
        compiler_params=_params("arbitrary", "arbitrary", "arbitrary"), name="attention")(*args)


def _resid_mm_kernel(lhs_ref, w_ref, x_ref, mod_ref, g2_ref, wr_ref, out_ref, hffn_ref, aff_ref):
    _mixer_residual_and_route(x_ref, mod_ref, _dot(lhs_ref[...], w_ref[...]), g2_ref, wr_ref, out_ref, hffn_ref,
                              aff_ref)


def _resid_mm(lhs, w, x, mod, t, norm2, w_router):
    n, d = x.shape
    k = lhs.shape[1]
    tm = _row_tile(n, t, 256)
    tail_in, tail_args, out_specs, out_shape = _route_tail_specs(n, d, tm, norm2, w_router)
    return pl.pallas_call(
        _resid_mm_kernel, grid=(n // tm,),
        in_specs=[pl.BlockSpec((tm, k), lambda i: (i, 0)),
                  pl.BlockSpec((k, d), lambda i: (0, 0)),
                  pl.BlockSpec((tm, d), lambda i: (i, 0)),
                  _mod_spec(mod, tm, t)] + tail_in,
        out_specs=out_specs, out_shape=out_shape,
        compiler_params=_params("arbitrary"), name="resid_mm")(lhs, w, x, mod, *tail_args)


def _token_shift_delta(h, hp_ref, hn_ref, i, tiles_per_seq):
    tm = h.shape[0]
    rowi = _iota((tm, 1), 0)
    ti = i % tiles_per_seq
    prev_row = jnp.where(ti == 0, 0.0, hp_ref[SUBLANES - 1:SUBLANES, :])
    next_row = jnp.where(ti == tiles_per_seq - 1, 0.0, hn_ref[0:1, :])
    h_prev = jnp.where(rowi == 0, prev_row, pltpu.roll(h, 1, axis=0))
    h_next = jnp.where(rowi == tm - 1, next_row, pltpu.roll(h, tm - 1, axis=0))
    return 0.5 * (h_prev + h_next) - h


def _shift_specs(tm, d, n, grid_rank):
    nb8 = n // SUBLANES
    per = tm // SUBLANES
    if grid_rank == 1:
        return [pl.BlockSpec((tm, d), lambda i: (i, 0)),
                pl.BlockSpec((SUBLANES, d), lambda i: (jnp.maximum(i * per - 1, 0), 0)),
                pl.BlockSpec((SUBLANES, d), lambda i: (jnp.minimum((i + 1) * per, nb8 - 1), 0))]
    return [pl.BlockSpec((tm, d), lambda j, i: (i, 0)),
            pl.BlockSpec((SUBLANES, d), lambda j, i: (jnp.maximum(i * per - 1, 0), 0)),
            pl.BlockSpec((SUBLANES, d), lambda j, i: (jnp.minimum((i + 1) * per, nb8 - 1), 0))]


def _rwkv_rkv_kernel(tiles_per_seq, h_ref, hp_ref, hn_ref, mix_ref, w_ref, o_ref):
    h = h_ref[...]
    xx = _token_shift_delta(h, hp_ref, hn_ref, pl.program_id(1), tiles_per_seq)
    o_ref[0] = _dot((h + xx * mix_ref[0]).astype(BF16), w_ref[0])


def _rwkv_rkv(h, mix3, w_rkv, t):
    n, d = h.shape
    tm = _row_tile(n, t, 256)
    return pl.pallas_call(
        functools.partial(_rwkv_rkv_kernel, t // tm), grid=(3, n // tm),
        in_specs=_shift_specs(tm, d, n, 2) + [pl.BlockSpec((1, 1, d), lambda j, i: (j, 0, 0)),
                                              pl.BlockSpec((1, d, d), lambda j, i: (j, 0, 0))],
        out_specs=pl.BlockSpec((1, tm, d), lambda j, i: (j, i, 0)),
        out_shape=jax.ShapeDtypeStruct((3, n, d), F32),
        compiler_params=_params("arbitrary", "arbitrary"), name="rwkv_rkv")(h, h, h, mix3, w_rkv)


def _rwkv_lora_kernel(tiles_per_seq, lora, h_ref, hp_ref, hn_ref, mix_ref, w1_ref, a1_ref, g1_ref,
                      w2_ref, a2_ref, g2_ref, w0_ref, a0_ref, lw_ref, asig_ref, g_ref):
    h = h_ref[...]
    xx = _token_shift_delta(h, hp_ref, hn_ref, pl.program_id(0), tiles_per_seq)
    xw = (h + xx * mix_ref[0:1, :]).astype(BF16)
    xa = (h + xx * mix_ref[1:2, :]).astype(BF16)
    xg = (h + xx * mix_ref[2:3, :]).astype(BF16)
    tw = jnp.tanh(_dot(xw, w1_ref[...])).astype(BF16)
    ta = _dot(xa, a1_ref[...]).astype(BF16)
    tg = jax.nn.sigmoid(_dot(xg, g1_ref[...])).astype(BF16)
    g_ref[...] = _dot(tg, g2_ref[...])
    for dr in range(2):
        z = -(w0_ref[dr] + _dot(tw[:, dr * lora:(dr + 1) * lora], w2_ref[dr]))
        softplus = jnp.maximum(z, 0.0) + jnp.log(1.0 + jnp.exp(-jnp.abs(z)))
        lw_ref[dr] = -jnp.exp(-softplus - 0.5)
        asig_ref[dr] = jax.nn.sigmoid(a0_ref[dr] + _dot(ta[:, dr * lora:(dr + 1) * lora], a2_ref[dr]))


def _rwkv_lora(h, mix3, w1c, a1c, g1, w2p, a2p, g2, w0, a0, t):
    n, d = h.shape
    tm = _row_tile(n, t, 256)
    lora = w2p.shape[1]
    gl = g1.shape[1]
    full2 = lambda shape: pl.BlockSpec(shape, lambda i: (0, 0))
    full3 = lambda shape: pl.BlockSpec(shape, lambda i: (0, 0, 0))
    return pl.pallas_call(
        functools.partial(_rwkv_lora_kernel, t // tm, lora), grid=(n // tm,),
        in_specs=_shift_specs(tm, d, n, 1) + [
            full2((3, d)), full2((d, 2 * lora)), full2((d, 2 * lora)), full2((d, gl)),
            full3((2, lora, d)), full3((2, lora, d)), full2((gl, d)), full3((2, 1, d)), full3((2, 1, d))],
        out_specs=[pl.BlockSpec((2, tm, d), lambda i: (0, i, 0)),
                   pl.BlockSpec((2, tm, d), lambda i: (0, i, 0)),
                   pl.BlockSpec((tm, d), lambda i: (i, 0))],
        out_shape=[jax.ShapeDtypeStruct((2, n, d), F32), jax.ShapeDtypeStruct((2, n, d), F32),
                   jax.ShapeDtypeStruct((n, d), F32)],
        compiler_params=_params("arbitrary"), name="rwkv_lora")(h, h, h, mix3, w1c, a1c, g1, w2p, a2p, g2, w0, a0)


def _rwkv_scan_kernel(has_init, emit_state, *refs):
    r_ref, k_ref, v_ref, lw_ref, a_ref, kk_ref, ka_ref, rk_ref = refs[:8]
    pos = 8
    if has_init:
        s0_ref = refs[pos]
        pos += 1
    y_ref, bonus_ref = refs[pos:pos + 2]
    pos += 2
    if emit_state:
        sout_ref = refs[pos]
        pos += 1
    s_scr, cum_scr = refs[pos:pos + 2]

    d = pl.program_id(1)
    c = pl.program_id(2)
    nc = pl.num_programs(2)
    fwd = d == 0
    sgn = jnp.where(fwd, 1, -1)
    length = r_ref.shape[1]
    npairs = s_scr.shape[0]
    hd = R_HD

    @pl.when(c == 0)
    def _():
        if has_init:
            s_scr[...] = s0_ref[0, 0]
        else:
            s_scr[...] = jnp.zeros_like(s_scr)

    ti = _iota((length, length), 0)
    si = _iota((length, length), 1)
    tri = (((ti - si) * sgn) >= 0).astype(BF16)
    l1, l2, l3 = _split3(lw_ref[0])
    cum_scr[...] = _dot(tri, l1) + _dot(tri, l2) + _dot(tri, l3)

    gl = 2 * length
    rr = _iota((gl, gl), 0)
    cc = _iota((gl, gl), 1)
    same = (rr // length) == (cc // length)
    e = ((rr % length) - (cc % length)) * sgn
    strict = same & (e > 0)
    incl = same & (e >= 0)
    eye = (rr == cc).astype(F32)
    lane = _iota((1, LANES), 1)
    m0 = (lane < hd).astype(F32)
    m1 = 1.0 - m0
    lr = _iota((LANES, LANES), 0)
    lc = _iota((LANES, LANES), 1)
    ones_blk = ((lr // hd) == (lc // hd)).astype(BF16)

    def segsum(x):
        hi, lo = _split2(x)
        return _dot(hi, ones_blk) + _dot(lo, ones_blk)

    def stack(x):
        return jnp.concatenate([x * m0, x * m1], axis=0)

    def smm(x, y, dims):
        return _mm_passes(x, y, dims, RWKV_SOLVE_PASSES)

    tr = rr % length
    tc = cc % length
    base = 4
    base_mask = (tr // base) == (tc // base)
    level_masks = []
    size = base
    while size < length:
        level_masks.append(((tr // (2 * size)) == (tc // (2 * size))) & ((tr // size) != (tc // size)))
        size *= 2

    def load_pair(p):
        ds = pl.ds(pl.multiple_of(p * LANES, LANES), LANES)
        return dict(ds=ds, p=p, r=r_ref[0, :, ds], k=k_ref[0, :, ds], v=v_ref[0, :, ds], lw=lw_ref[0, :, ds],
                    a=a_ref[0, :, ds], cum=cum_scr[:, ds], kkp=kk_ref[:, ds], kap=ka_ref[:, ds],
                    rkp=rk_ref[:, ds], st=s_scr[p])

    def each(fn, *lists):
        return [fn(*args) for args in zip(*lists)]

    def solve_group(xs):
        get = lambda name: [x[name] for x in xs]
        r, k, v, lw, a, cum, st = (get(n) for n in ("r", "k", "v", "lw", "a", "cum", "st"))
        kk = each(lambda k_, p_: k_ * p_, k, get("kkp"))
        ss = each(lambda q: segsum(q * q), kk)
        kk = each(lambda q, s_: q * lax.rsqrt(jnp.maximum(s_, 1e-24)), kk, ss)
        kd = each(lambda k_, a_, p_: k_ * (1.0 + (a_ - 1.0) * p_), k, a, get("kap"))
        bb = each(lambda q, a_: q * a_, kk, a)
        tot = each(lambda c_: jnp.where(fwd, c_[length - 1:length, :], c_[0:1, :]), cum)
        e_n = each(lambda c_: jnp.exp(-c_), cum)
        sa = each(lambda q, c_, l_: stack(-q * jnp.exp(c_ - l_)), kk, cum, lw)
        sr_full = each(lambda r_, c_: r_ * jnp.exp(c_), r, cum)
        sr = each(stack, sr_full)
        sb = each(lambda b_, e_: stack(b_ * e_), bb, e_n)
        sk = each(lambda k_, e_: stack(k_ * e_), kd, e_n)
        vexp = each(stack, v)
        m_ab = each(lambda x_, y_: jnp.where(strict, smm(x_, y_, _NT), 0.0), sa, sb)
        m_ak = each(lambda x_, y_: jnp.where(strict, smm(x_, y_, _NT), 0.0), sa, sk)
        rhs1 = each(lambda x_, s_: smm(x_, s_, _NT), sa, st)
        rhs2 = each(lambda m_, v_: smm(m_, v_, _NN), m_ak, vexp)
        rhs = each(lambda x_, y_: x_ + y_, rhs1, rhs2)
        d0 = each(lambda m_: jnp.where(base_mask, m_, 0.0), m_ab)
        sq = each(lambda d_: smm(d_, d_, _NN), d0)
        inv = each(lambda d_: eye + d_, d0)
        inv = each(lambda i_, q_: i_ + smm(i_, q_, _NN), inv, sq)
        for lm in level_masks:
            half = each(lambda i_, m_: smm(i_, jnp.where(lm, m_, 0.0), _NN), inv, m_ab)
            inv = each(lambda i_, h_: i_ + smm(h_, i_, _NN), inv, half)
        uexp = each(lambda i_, r_: smm(i_, r_, _NN), inv, rhs)
        t_rb = each(lambda x_, y_: jnp.where(incl, smm(x_, y_, _NT), 0.0), sr, sb)
        t_rk = each(lambda x_, y_: jnp.where(incl, smm(x_, y_, _NT), 0.0), sr, sk)
        fold = lambda t_: t_[0:length] + t_[length:gl]
        y1 = each(lambda r_, s_: smm(r_, s_, _NT), sr_full, st)
        uv = each(lambda u_, v_: jnp.concatenate([u_, v_], axis=0), uexp, vexp)
        y2 = each(lambda b_, k_, w_: smm(jnp.concatenate([fold(b_), fold(k_)], axis=1), w_, _NN), t_rb, t_rk, uv)
        y = each(lambda p_, q_: p_ + q_, y1, y2)
        bs = each(lambda r_, k_, p_: segsum(r_ * k_ * p_), r, kd, get("rkp"))
        bonus = each(lambda s_, v_: s_ * v_, bs, v)
        e_e = each(lambda t_, c_: jnp.exp(t_ - c_), tot, cum)
        bk = each(lambda b_, k_, e_: jnp.concatenate([stack(b_ * e_), stack(k_ * e_)], axis=0), bb, kd, e_e)
        s1 = each(lambda w_, x_: smm(w_, x_, _TN), uv, bk)
        s_new = each(lambda s_, t_, p_: s_ * jnp.exp(t_) + p_, st, tot, s1)
        return y, bonus, s_new

    npg = min(RWKV_PAIRS_PER_GROUP, npairs)

    def group(gi, carry):
        xs = [load_pair(gi * npg + u) for u in range(npg)]
        ys, bonuses, states = solve_group(xs)
        for x, y, bonus, s_new in zip(xs, ys, bonuses, states):
            y_ref[0, :, x["ds"]] = y.astype(y_ref.dtype)
            bonus_ref[0, :, x["ds"]] = bonus.astype(bonus_ref.dtype)
            s_scr[x["p"]] = s_new
        return carry

    lax.fori_loop(0, npairs // npg, group, 0)

    if emit_state:
        @pl.when(c == nc - 1)
        def _():
            sout_ref[0, 0] = s_scr[...]


def _rwkv_scan(rkv, lw, asig, kk, ka, rk, b, t, s0, emit_state):
    n, d = rkv.shape[1:]
    length = RWKV_CHUNK
    nc = t // length
    npairs = d // LANES

    def tile(bi, di, ci):
        return bi * nc + ci + di * (nc - 1 - 2 * ci)

    def plane(j):
        return pl.BlockSpec((1, length, d), lambda bi, di, ci: (j, tile(bi, di, ci), 0))

    dirspec = pl.BlockSpec((1, length, d), lambda bi, di, ci: (di, tile(bi, di, ci), 0))
    vec = pl.BlockSpec((1, d), lambda bi, di, ci: (0, 0))
    sspec = pl.BlockSpec((1, 1, npairs, LANES, LANES), lambda bi, di, ci: (bi, di, 0, 0, 0))
    in_specs = [plane(0), plane(1), plane(2), dirspec, dirspec, vec, vec, vec]
    args = [rkv, rkv, rkv, lw, asig, kk.reshape(1, d), ka.reshape(1, d), rk.reshape(1, d)]
    if s0 is not None:
        in_specs.append(sspec)
        args.append(s0)
    out_specs = [dirspec, dirspec]
    out_shape = [jax.ShapeDtypeStruct((2, n, d), SCAN_OUT_DTYPE), jax.ShapeDtypeStruct((2, n, d), SCAN_OUT_DTYPE)]
    if emit_state:
        out_specs.append(sspec)
        out_shape.append(jax.ShapeDtypeStruct((b, 2, npairs, LANES, LANES), F32))
    return pl.pallas_call(
        functools.partial(_rwkv_scan_kernel, s0 is not None, emit_state),
        grid=(b, 2, nc), in_specs=in_specs, out_specs=out_specs, out_shape=out_shape,
        scratch_shapes=[pltpu.VMEM((npairs, LANES, LANES), F32), pltpu.VMEM((length, d), F32)],
        compiler_params=_params("arbitrary", "arbitrary", "arbitrary"), name="rwkv_scan")(*args)


def _rwkv_out_kernel(y_ref, bonus_ref, g_ref, lnw_ref, lnb_ref, w_ref, x_ref, mod_ref, g2_ref, wr_ref, out_ref,
                     hffn_ref, aff_ref):
    y = y_ref[0].astype(F32) + y_ref[1].astype(F32)
    lr = _iota((LANES, LANES), 0)
    lc = _iota((LANES, LANES), 1)
    mean_blk = ((lr // R_HD) == (lc // R_HD)).astype(BF16)

    def segmean(x):
        hi, lo = _split2(x)
        return (_dot(hi, mean_blk) + _dot(lo, mean_blk)) * (1.0 / R_HD)

    tiles = [y[:, j * LANES:(j + 1) * LANES] for j in range(y.shape[1] // LANES)]
    means = [segmean(yj) for yj in tiles]
    cens = [yj - mj for yj, mj in zip(tiles, means)]
    variances = [segmean(cj * cj) for cj in cens]
    parts = [cj * lax.rsqrt(vj + R_LN_EPS) for cj, vj in zip(cens, variances)]
    yn = (jnp.concatenate(parts, axis=1) * lnw_ref[...] + lnb_ref[...]
          + bonus_ref[0].astype(F32) + bonus_ref[1].astype(F32))
    lhs = (yn * g_ref[...]).astype(BF16)
    _mixer_residual_and_route(x_ref, mod_ref, _dot(lhs, w_ref[...]), g2_ref, wr_ref, out_ref, hffn_ref, aff_ref)


def _rwkv_out(y2, bonus2, g, lnw, lnb, w_out, x, mod, t, norm2, w_router):
    n, d = x.shape
    tm = _row_tile(n, t, 256)
    two = pl.BlockSpec((2, tm, d), lambda i: (0, i, 0))
    rows = pl.BlockSpec((tm, d), lambda i: (i, 0))
    vec = pl.BlockSpec((1, d), lambda i: (0, 0))
    tail_in, tail_args, out_specs, out_shape = _route_tail_specs(n, d, tm, norm2, w_router)
    return pl.pallas_call(
        _rwkv_out_kernel, grid=(n // tm,),
        in_specs=[two, two, rows, vec, vec, pl.BlockSpec((d, d), lambda i: (0, 0)), rows,
                  _mod_spec(mod, tm, t)] + tail_in,
        out_specs=out_specs, out_shape=out_shape,
        compiler_params=_params("arbitrary"), name="rwkv_out")(
            y2, bonus2, g, lnw.reshape(1, d), lnb.reshape(1, d), w_out, x, mod, *tail_args)


def _count(m):
    x = jnp.sum(m.astype(F32), axis=1, keepdims=True)
    return jnp.sum(x, axis=0, keepdims=True)


def _route_search_kernel(cap, aff_ref, thr_ref, tie_ref):
    n_exp, nb, _ = aff_ref.shape
    experts = list(range(n_exp))
    bits = [lax.bitcast_convert_type(aff_ref[e], I32) for e in experts]
    tok = _iota((nb, LANES), 0) * LANES + _iota((nb, LANES), 1)

    def value_step(i, cur):
        bit = jnp.int32(1) << (30 - i)
        return tuple(jnp.where(_count(b >= (c | bit)) >= cap, c | bit, c) for b, c in zip(bits, cur))

    zeros = tuple(jnp.zeros((1, 1), I32) for _ in experts)
    thr = lax.fori_loop(0, 31, value_step, zeros)
    equal = [b == t for b, t in zip(bits, thr)]
    need = [cap - _count(b > t) for b, t in zip(bits, thr)]

    def index_step(i, cur):
        bit = jnp.int32(1) << (14 - i)
        return tuple(jnp.where(_count(q & (tok < (c + bit))) < n, c + bit, c) for q, n, c in zip(equal, need, cur))

    tie = lax.fori_loop(0, 15, index_step, zeros)
    for e in experts:
        thr_ref[e] = jnp.broadcast_to(thr[e], (SUBLANES, LANES))
        tie_ref[e] = jnp.broadcast_to(tie[e], (SUBLANES, LANES))


def _route_kernel(cap, aff_ref, thr_ref, tie_ref, idx_ref, gate_ref, pos_ref, tot_ref, off_ref):
    aff = aff_ref[0]
    nb = aff.shape[0]
    bits = lax.bitcast_convert_type(aff, I32)
    tok = _iota(aff.shape, 0) * LANES + _iota(aff.shape, 1)
    thr = thr_ref[0, 0:1, 0:1]
    sel = (bits > thr) | ((bits == thr) & (tok <= tie_ref[0, 0:1, 0:1]))
    self32 = sel.astype(F32)
    selb = self32.astype(BF16)

    li = _iota((LANES, LANES), 0)
    lj = _iota((LANES, LANES), 1)
    upper_incl = (li <= lj).astype(BF16)
    cl = _dot(selb, upper_incl)
    ones8 = jnp.ones((SUBLANES, LANES), BF16)
    tot_row = _dot_nt(ones8, selb)
    bi = _iota((nb, nb), 0)
    bj = _iota((nb, nb), 1)
    offi_row = _dot(tot_row.astype(BF16), (bi <= bj).astype(BF16))
    offx_row = offi_row - tot_row
    tot_col = jnp.broadcast_to(cl[:, LANES - 1:LANES], (nb, LANES)).astype(BF16)
    offx_col = _dot((bj < bi).astype(BF16), tot_col)
    pos_ref[0] = jnp.where(sel, offx_col + cl - 1.0, -1.0).astype(I32)
    tot_ref[0] = tot_row
    off_ref[0] = offx_row

    pcol = _iota((cap, 1), 0).astype(F32)
    before = offi_row[0:1, :] <= pcol
    blk = jnp.sum(before.astype(F32), axis=1, keepdims=True)
    base = jnp.sum(jnp.where(before, tot_row[0:1, :], 0.0), axis=1, keepdims=True)
    onehot = (_iota((cap, nb), 1).astype(F32) == blk).astype(BF16)
    rowcnt = _dot(onehot, cl.astype(BF16))
    rank = pcol - base
    lane = jnp.sum((rowcnt <= rank).astype(F32), axis=1, keepdims=True)
    idx_ref[0] = (blk * LANES + lane).astype(I32)
    a1, a2, a3 = _split3(aff)
    rowaff = _dot(onehot, a1) + _dot(onehot, a2) + _dot(onehot, a3)
    gate_ref[0] = jnp.sum(jnp.where(_iota((cap, LANES), 1).astype(F32) == lane, rowaff, 0.0),
                          axis=1, keepdims=True)


def _route(aff3, cap):
    e, nb, _ = aff3.shape
    word = pl.BlockSpec((1, SUBLANES, LANES), lambda i: (i, 0, 0))
    thr, tie = pl.pallas_call(
        functools.partial(_route_search_kernel, cap), grid=(1,),
        in_specs=[pl.BlockSpec((e, nb, LANES), lambda i: (0, 0, 0))],
        out_specs=[pl.BlockSpec((e, SUBLANES, LANES), lambda i: (0, 0, 0))] * 2,
        out_shape=[jax.ShapeDtypeStruct((e, SUBLANES, LANES), I32)] * 2,
        compiler_params=_params("arbitrary"), name="moe_route_search")(aff3)
    return pl.pallas_call(
        functools.partial(_route_kernel, cap), grid=(e,),
        in_specs=[pl.BlockSpec((1, nb, LANES), lambda i: (i, 0, 0)), word, word],
        out_specs=[pl.BlockSpec((1, cap, 1), lambda i: (i, 0, 0)),
                   pl.BlockSpec((1, cap, 1), lambda i: (i, 0, 0)),
                   pl.BlockSpec((1, nb, LANES), lambda i: (i, 0, 0)),
                   pl.BlockSpec((1, SUBLANES, nb), lambda i: (i, 0, 0)),
                   pl.BlockSpec((1, SUBLANES, nb), lambda i: (i, 0, 0))],
        out_shape=[jax.ShapeDtypeStruct((e, cap, 1), I32), jax.ShapeDtypeStruct((e, cap, 1), F32),
                   jax.ShapeDtypeStruct((e, nb, LANES), I32),
                   jax.ShapeDtypeStruct((e, SUBLANES, nb), F32), jax.ShapeDtypeStruct((e, SUBLANES, nb), F32)],
        compiler_params=_params("arbitrary"), name="moe_route")(aff3, thr, tie)


def _row_copy(x_hbm, xbuf, sem, slot, tok, r):
    return pltpu.make_async_copy(x_hbm.at[pl.ds(tok, 1)], xbuf.at[pl.ds(r, 1)], sem.at[slot])


def _expert_kernel(idx_ref, idx_next_ref, x_hbm, gate_ref, wg_ref, wu_ref, wd_ref, o_ref, xbuf0, xbuf1, sem):
    tm = xbuf0.shape[0]
    bufs = (xbuf0, xbuf1)
    step = pl.program_id(0) * pl.num_programs(1) + pl.program_id(1)
    last = pl.num_programs(0) * pl.num_programs(1) - 1

    def start_rows(ids_ref, slot):
        for r in range(tm):
            _row_copy(x_hbm, bufs[slot], sem, slot, ids_ref[0, 0, r], r).start()

    def wait_rows(slot):
        for r in range(tm):
            _row_copy(x_hbm, bufs[slot], sem, slot, 0, r).wait()

    @pl.when(step == 0)
    def _():
        start_rows(idx_ref, 0)

    def run(slot):
        start_rows(idx_next_ref, 1 - slot)
        wait_rows(slot)
        xb = bufs[slot][...].astype(BF16)
        hg = _dot(xb, wg_ref[0, 0])
        hu = _dot(xb, wu_ref[0, 0])
        hid = (hg * jax.nn.sigmoid(hg) * hu).astype(BF16)
        o_ref[...] = (_dot(hid, wd_ref[0, 0]) * gate_ref[...]).astype(o_ref.dtype)

        @pl.when(step == last)
        def _():
            wait_rows(1 - slot)

    for slot in range(2):
        @pl.when(step % 2 == slot)
        def _():
            run(slot)


def _experts(hffn, idx, gate, wg, wu, wd, layer, cap):
    n, d = hffn.shape
    _, e, _, f = wg.shape
    tm = min(256, cap)
    nt = cap // tm
    idx3 = idx.reshape(e * nt, 1, tm)
    last = e * nt - 1
    return pl.pallas_call(
        _expert_kernel, grid=(e, nt),
        in_specs=[pl.BlockSpec((1, 1, tm), lambda ei, ti: (ei * nt + ti, 0, 0), memory_space=pltpu.SMEM),
                  pl.BlockSpec((1, 1, tm), lambda ei, ti: (jnp.minimum(ei * nt + ti + 1, last), 0, 0),
                               memory_space=pltpu.SMEM),
                  pl.BlockSpec(memory_space=pl.ANY),
                  pl.BlockSpec((tm, 1), lambda ei, ti: (ei * nt + ti, 0)),
                  pl.BlockSpec((1, 1, d, f), lambda ei, ti: (layer, ei, 0, 0)),
                  pl.BlockSpec((1, 1, d, f), lambda ei, ti: (layer, ei, 0, 0)),
                  pl.BlockSpec((1, 1, f, d), lambda ei, ti: (layer, ei, 0, 0))],
        out_specs=pl.BlockSpec((tm, d), lambda ei, ti: (ei * nt + ti, 0)),
        out_shape=jax.ShapeDtypeStruct((e * cap, d), BF16),
        scratch_shapes=[pltpu.VMEM((tm, d), F32), pltpu.VMEM((tm, d), F32), pltpu.SemaphoreType.DMA((2,))],
        compiler_params=_params("arbitrary", "arbitrary"), name="moe_experts")(
            idx3, idx3, hffn, gate.reshape(e * cap, 1), wg, wu, wd)


def _slab_copy(ye_hbm, slab, sem, buf, src_row, slot):
    return pltpu.make_async_copy(ye_hbm.at[pl.ds(pl.multiple_of(src_row, PACKED_ROWS), SLAB_CHUNK)],
                                 slab.at[buf, pl.ds(pl.multiple_of(slot * SLAB_CHUNK, SLAB_CHUNK), SLAB_CHUNK)],
                                 sem.at[buf])


def _combine_kernel(n_exp, meta_ref, meta_next_ref, x_ref, mod_ref, pos_ref, ye_hbm, out_ref, slab, acc, sem):
    tm = x_ref.shape[0]
    i = pl.program_id(0)
    buf = i % 2
    total = meta_ref[0, 0, 0]
    total_next = jnp.where(i + 1 < pl.num_programs(0), meta_next_ref[0, 0, 0], 0)

    def start_chunks(ref, count, b):
        def body(g, carry):
            _slab_copy(ye_hbm, slab, sem, b, ref[0, 0, 1 + n_exp + g], g).start()
            return carry
        lax.fori_loop(0, count, body, 0)

    @pl.when(i == 0)
    def _():
        slab[...] = jnp.zeros_like(slab)
        start_chunks(meta_ref, total, 0)

    start_chunks(meta_next_ref, total_next, 1 - buf)

    def wait(g, carry):
        _slab_copy(ye_hbm, slab, sem, buf, 0, g).wait()
        return carry

    pos = pos_ref[...]
    sub_e = _iota((n_exp, 1), 0)
    shift = jnp.zeros((n_exp, 1), I32)
    for e in range(n_exp):
        shift = jnp.where(sub_e == e, meta_ref[0, 0, 1 + e], shift)
    target = jnp.where(pos >= 0, pos + shift, -1)
    acc[...] = jnp.zeros_like(acc)
    lax.fori_loop(0, total, wait, 0)

    kc_rows = 2 * LANES
    chunks_per_kc = kc_rows // SLAB_CHUNK

    def kbody(kc, carry):
        base = pl.multiple_of(kc * kc_rows, kc_rows)
        rows = slab[buf, pl.ds(base, kc_rows), :]
        slab_row = _iota((kc_rows, tm), 0) + base
        place_t = target[0:1, :] == slab_row
        for e in range(1, n_exp):
            place_t = place_t | (target[e:e + 1, :] == slab_row)
        acc[...] += _dot_tn(place_t.astype(F32).astype(BF16), rows)
        return carry

    lax.fori_loop(0, (total + chunks_per_kc - 1) // chunks_per_kc, kbody, 0)
    out_ref[...] = x_ref[...] + mod_ref[0, 5:6, :] * acc[...]


def _combine(x, mod, pos_t, meta, ye, t, max_chunks):
    n, d = x.shape
    tm = LANES
    n_exp = N_EXPERTS
    ntiles = n // tm
    kc_rows = 2 * LANES
    slab_rows = -(-(max_chunks * SLAB_CHUNK) // kc_rows) * kc_rows
    return pl.pallas_call(
        functools.partial(_combine_kernel, n_exp), grid=(ntiles,),
        in_specs=[pl.BlockSpec((1, 1, meta.shape[2]), lambda i: (i, 0, 0), memory_space=pltpu.SMEM),
                  pl.BlockSpec((1, 1, meta.shape[2]), lambda i: (jnp.minimum(i + 1, ntiles - 1), 0, 0),
                               memory_space=pltpu.SMEM),
                  pl.BlockSpec((tm, d), lambda i: (i, 0)),
                  _mod_spec(mod, tm, t),
                  pl.BlockSpec((n_exp, tm), lambda i: (0, i)),
                  pl.BlockSpec(memory_space=pl.ANY)],
        out_specs=pl.BlockSpec((tm, d), lambda i: (i, 0)),
        out_shape=jax.ShapeDtypeStruct((n, d), F32),
        scratch_shapes=[pltpu.VMEM((2, slab_rows, d), BF16), pltpu.VMEM((tm, d), F32),
                        pltpu.SemaphoreType.DMA((2,))],
        compiler_params=_params("arbitrary"), name="moe_combine")(meta, meta, x, mod, pos_t, ye)


def _moe(x, hffn, aff, mod, t, wg, wu, wd, layer):
    n, d = x.shape
    n_exp = N_EXPERTS
    cap = (EC_CAPACITY_FACTOR * n) // n_exp
    npad = ROUTE_BLOCKS * LANES
    assert n <= npad and n % LANES == 0 and cap % SLAB_CHUNK == 0
    aff_t = jnp.pad(aff.T, ((0, 0), (0, npad - n)), constant_values=-1.0)
    idx, gate, pos, tot, off = _route(aff_t.reshape(n_exp, ROUTE_BLOCKS, LANES), cap)
    ye = _experts(hffn, idx, gate, wg, wu, wd, layer, cap)
    ntiles = n // LANES
    span = min(LANES + PACKED_ROWS + SLAB_CHUNK - (LANES + PACKED_ROWS) % SLAB_CHUNK, cap)
    per_expert = span // SLAB_CHUNK
    max_chunks = n_exp * per_expert
    start = off[:, 0, :ntiles].T.astype(I32)
    cnt = tot[:, 0, :ntiles].T.astype(I32)
    start_al = jnp.minimum((start // PACKED_ROWS) * PACKED_ROWS, cap - span)
    nch = jnp.where(cnt > 0, (start - start_al + cnt + SLAB_CHUNK - 1) // SLAB_CHUNK, 0)
    cend = jnp.cumsum(nch, axis=1)
    cstart = cend - nch
    slots = jnp.arange(max_chunks, dtype=I32)
    owns = (cstart[:, None, :] <= slots[None, :, None]) & (slots[None, :, None] < cend[:, None, :])
    base = start_al + jnp.arange(n_exp, dtype=I32)[None, :] * cap
    src = jnp.sum(jnp.where(owns, (base - cstart * SLAB_CHUNK)[:, None, :], 0), axis=2) + jnp.where(
        jnp.any(owns, axis=2), slots[None, :] * SLAB_CHUNK, 0)
    meta = jnp.concatenate([cend[:, -1:], cstart * SLAB_CHUNK - start_al, src], axis=1)
    return _combine(x, mod, pos.reshape(n_exp, npad), meta.reshape(ntiles, 1, 1 + n_exp + max_chunks), ye, t,
                    max_chunks)


def _rope_tables(t):
    n_rows = t // GRID_W
    axis_dim = A_HD // 2
    row = jnp.repeat(jnp.arange(n_rows, dtype=F32), GRID_W)
    col = jnp.tile(jnp.arange(GRID_W, dtype=F32), n_rows)
    inv_freq = ROPE_THETA ** (-jnp.arange(0, axis_dim, 2, dtype=F32) / axis_dim)
    ang = jnp.concatenate([row[:, None] * inv_freq, col[:, None] * inv_freq], axis=-1)
    cos, sin = jnp.cos(ang), jnp.sin(ang)
    return jnp.concatenate([cos, cos], axis=-1), jnp.concatenate([-sin, sin], axis=-1)


def _pad_lora(w1, w2):
    r = w1.shape[2]
    rp = -(-r // LANES) * LANES
    w1p = jnp.pad(w1, ((0, 0), (0, 0), (0, rp - r)))
    w2p = jnp.pad(w2, ((0, 0), (0, rp - r), (0, 0)))
    return jnp.concatenate([w1p[0], w1p[1]], axis=1).astype(BF16), w2p.astype(BF16)


def _pair_state(s):
    b, _, heads, hd, _ = s.shape
    sp = s.reshape(b, 2, heads // 2, 2, hd, hd)
    out = jnp.zeros((b, 2, heads // 2, 2 * hd, 2 * hd), F32)
    out = out.at[:, :, :, :hd, :hd].set(sp[:, :, :, 0])
    return out.at[:, :, :, hd:, hd:].set(sp[:, :, :, 1])


def _unpair_state(sp):
    b, _, npairs, _, _ = sp.shape
    hd = R_HD
    s = jnp.stack([sp[:, :, :, :hd, :hd], sp[:, :, :, hd:, hd:]], axis=3)
    return s.reshape(b, 2, 2 * npairs, hd, hd)


def _trunk(x, mod_all, b, t, latent_states, p):
    n, d = x.shape
    latent = latent_states is not None
    produced = []
    for i in range(DEPTH):
        kind, j = i % N_MIXERS, i // N_MIXERS
        mod = mod_all[i]
        route_params = (p["norm2"][i], p["router"][i])
        if kind == 0:
            hb, gates = _norm_call("gate", x, p["norm1"][i], mod, t, w=p["m_gate_w"][j], b=p["m_gate_b"][j],
                                   out_dtype=BF16)
            proj = _mm(hb, p["m_qkvo"][j], 2048)
            res = _mlstm_scan(proj, gates, b, t, latent_states[i] if latent else None, not latent)
            if not latent:
                produced += [res[1], res[2], res[3][..., 0]]
            x, hffn, aff = _mlstm_out(res[0], proj, p["m_hnorm"][j], p["m_out"][j], x, mod, t, *route_params)
        elif kind == 1:
            hb = _norm_call("plain", x, p["norm1"][i], mod, t, out_dtype=BF16)
            q, k, v = _attn_qkv(hb, p["a_qkv"][j], p["a_qnorm"][j], p["a_knorm"][j], t,
                                _rope_tables(t) if latent else None)
            cache = None
            if latent:
                ck, cv = latent_states[i]
                kvd = A_KV_HEADS * A_HD
                cache = (ck.reshape(-1, kvd), cv.reshape(-1, kvd))
            else:
                produced += [k.reshape(b, t, A_KV_HEADS, A_HD), v.reshape(b, t, A_KV_HEADS, A_HD)]
            o = _attention(q, k, v, b, t, cache)
            x, hffn, aff = _resid_mm(o, p["a_out"][j], x, mod, t, *route_params)
        else:
            h = _norm_call("plain", x, p["norm1"][i], mod, t)
            mix = p["r_mix"][j]
            rkv = _rwkv_rkv(h, mix[jnp.array([0, 2, 3])].reshape(3, 1, d), p["r_rkv"][j], t)
            lw, asig, g = _rwkv_lora(h, mix[jnp.array([1, 4, 5])], p["r_w1c"][j], p["r_a1c"][j], p["r_g1"][j],
                                     p["r_w2p"][j], p["r_a2p"][j], p["r_g2"][j],
                                     p["r_w0"][j].reshape(2, 1, d), p["r_a0"][j].reshape(2, 1, d), t)
            s0 = _pair_state(latent_states[i][0]) if latent else None
            res = _rwkv_scan(rkv, lw, asig, p["r_kk"][j], p["r_ka"][j], p["r_rk"][j].reshape(-1), b, t, s0,
                             not latent)
            if not latent:
                produced.append(_unpair_state(res[2]))
            x, hffn, aff = _rwkv_out(res[0], res[1], g, p["r_lnx_w"][j], p["r_lnx_b"][j], p["r_out"][j], x, mod, t,
                                     *route_params)
        x = _moe(x, hffn, aff, mod, t, p["exp_gate"], p["exp_up"], p["exp_down"], i)
    return _norm_call("final", x, p["final_norm"], None, t), produced


def kernel(x_prompt, x_sample, state_l0_C, state_l0_n, state_l0_m, cache_l1_k, cache_l1_v, state_l2_S,
           state_l3_C, state_l3_n, state_l3_m, c, c_ctx, ada_w, ada_b, norm1, norm2, router, exp_gate,
           exp_up, exp_down, m_qkvo, m_gate_w, m_gate_b, m_hnorm, m_out, a_qkv, a_qnorm, a_knorm, a_out,
           r_mix, r_rkv, r_w0, r_w1, r_w2, r_a0, r_a1, r_a2, r_g1, r_g2, r_kk, r_ka, r_rk, r_lnx_w,
           r_lnx_b, r_out, final_norm):
    bc, tc, d = x_prompt.shape
    bl, tl, _ = x_sample.shape
    depth = ada_w.shape[0]
    assert bl + 1 <= SUBLANES

    cond8 = jnp.zeros((SUBLANES, d), F32).at[0].set(c_ctx).at[1:1 + bl].set(c)
    mod = _adaln(cond8, ada_w, ada_b).reshape(depth, SUBLANES, 6, d)
    mod_ctx = mod[:, 0:1]
    mod_lat = mod[:, 1:1 + bl]

    w1c, w2p, a1c, a2p = [], [], [], []
    for j in range(r_w1.shape[0]):
        w1, w2 = _pad_lora(r_w1[j], r_w2[j])
        a1, a2 = _pad_lora(r_a1[j], r_a2[j])
        w1c.append(w1), w2p.append(w2), a1c.append(a1), a2p.append(a2)
    p = dict(norm1=norm1, norm2=norm2, router=router, final_norm=final_norm,
             exp_gate=exp_gate.astype(BF16), exp_up=exp_up.astype(BF16), exp_down=exp_down.astype(BF16),
             m_qkvo=m_qkvo.astype(BF16), m_gate_w=m_gate_w, m_gate_b=m_gate_b, m_hnorm=m_hnorm,
             m_out=m_out.astype(BF16), a_qkv=a_qkv.astype(BF16), a_qnorm=a_qnorm, a_knorm=a_knorm,
             a_out=a_out.astype(BF16), r_mix=r_mix, r_rkv=r_rkv.astype(BF16), r_w0=r_w0, r_a0=r_a0,
             r_w1c=w1c, r_w2p=w2p, r_a1c=a1c, r_a2p=a2p, r_g1=r_g1.astype(BF16), r_g2=r_g2.astype(BF16),
             r_kk=r_kk, r_ka=r_ka, r_rk=r_rk, r_lnx_w=r_lnx_w, r_lnx_b=r_lnx_b, r_out=r_out.astype(BF16))

    y_prompt, new_state = _trunk(x_prompt.reshape(bc * tc, d), mod_ctx, bc, tc, None, p)
    lat_states = [(state_l0_C, state_l0_n, state_l0_m), (cache_l1_k, cache_l1_v), (state_l2_S,),
                  (state_l3_C, state_l3_n, state_l3_m)]
    y_sample, _ = _trunk(x_sample.reshape(bl * tl, d), mod_lat, bl, tl, lat_states, p)
    return (y_prompt.reshape(bc, tc, d), y_sample.reshape(bl, tl, d), *new_state)
```

```python
import functools

import jax
import jax.numpy as jnp
from jax import lax
from jax.experimental import pallas as pl
from jax.experimental.pallas import tpu as pltpu

F32 = jnp.float32
BF16 = jnp.bfloat16
I32 = jnp.int32

D_MODEL = 2048
BATCH = 32
SEQ = 256
DEPTH = 4
DEC_BATCH = 4
DEC_SEQ = 4096
PAST_LEN = 256
GRID_W = 64
N_MIXERS = 3
NORM_EPS = 1e-6
M_HEADS = 8
M_DK = D_MODEL // 2 // M_HEADS
M_DV = D_MODEL // M_HEADS
M_GATE_CAP = 15.0
A_HEADS = 16
A_KV_HEADS = 4
A_HD = 128
ROPE_THETA = 10000.0
R_HD = 64
R_LN_EPS = 64e-5
N_EXPERTS = 16
EC_CAPACITY_FACTOR = 2
D_EXPERT = 1024

LANES = 128
SUBLANES = 8
PACKED_ROWS = 16
VMEM_LIMIT_BYTES = 56 * 2 ** 20

MLSTM_CHUNK = 256
ATTN_Q_ROWS = 128
ATTN_SUB_ROWS = 64
RWKV_CHUNK = 64
RWKV_PAIRS_PER_GROUP = 16
RWKV_SOLVE_PASSES = 1
ROUTE_BLOCKS = 128
SLAB_CHUNK = 32


def _params(*sem):
    return pltpu.CompilerParams(dimension_semantics=sem, vmem_limit_bytes=VMEM_LIMIT_BYTES)


def _dot(a, b):
    return jnp.dot(a, b, preferred_element_type=F32)


def _dot_nt(a, b):
    return lax.dot_general(a, b, (((1,), (1,)), ((), ())), preferred_element_type=F32)


def _dot_tn(a, b):
    return lax.dot_general(a, b, (((0,), (0,)), ((), ())), preferred_element_type=F32)


def _split2(x):
    hi = x.astype(BF16)
    lo = (x - hi.astype(F32)).astype(BF16)
    return hi, lo


def _split3(x):
    h1 = x.astype(BF16)
    r1 = x - h1.astype(F32)
    h2 = r1.astype(BF16)
    h3 = (r1 - h2.astype(F32)).astype(BF16)
    return h1, h2, h3


def _dot3(a, w):
    ah, al = _split2(a)
    wh, wl = _split2(w)
    return _dot(ah, wh) + _dot(al, wh) + _dot(ah, wl)


_NN = (((1,), (0,)), ((), ()))
_NT = (((1,), (1,)), ((), ()))
_TN = (((0,), (0,)), ((), ()))


def _mm_passes(a, b, dims, passes):
    def f(x, y):
        return lax.dot_general(x, y, dims, preferred_element_type=F32)
    if passes == 1:
        return f(a.astype(BF16), b.astype(BF16))
    ah, al = _split2(a)
    bh, bl = _split2(b)
    return f(ah, bh) + f(al, bh) + f(ah, bl)


def _iota(shape, dim):
    return lax.broadcasted_iota(I32, shape, dim)


def _log_sigmoid(z):
    return jnp.minimum(z, 0.0) - jnp.log(1.0 + jnp.exp(-jnp.abs(z)))


def _adaln_kernel(c_ref, w_ref, b_ref, o_ref):
    c = c_ref[...]
    s = c * jax.nn.sigmoid(c)
    hi, lo = _split2(s)
    w = w_ref[0].astype(BF16)
    o_ref[0] = _dot(hi, w) + _dot(lo, w) + b_ref[0]


def _adaln(cond8, ada_w, ada_b):
    depth, d, d6 = ada_w.shape
    tn = min(d, 1024)
    return pl.pallas_call(
        _adaln_kernel,
        grid=(depth, d6 // tn),
        in_specs=[pl.BlockSpec((SUBLANES, d), lambda l, j: (0, 0)),
                  pl.BlockSpec((1, d, tn), lambda l, j: (l, 0, j)),
                  pl.BlockSpec((1, 1, tn), lambda l, j: (l, 0, j))],
        out_specs=pl.BlockSpec((1, SUBLANES, tn), lambda l, j: (l, 0, j)),
        out_shape=jax.ShapeDtypeStruct((depth, SUBLANES, d6), F32),
        compiler_params=_params("arbitrary", "arbitrary"),
        name="adaln",
    )(cond8, ada_w, ada_b.reshape(depth, 1, d6))


def _normed(x_ref, g_ref):
    x = x_ref[...]
    return x * lax.rsqrt(jnp.mean(x * x, axis=-1, keepdims=True) + NORM_EPS) * g_ref[...]


def _norm_gate_kernel(x_ref, g_ref, mod_ref, w_ref, b_ref, o_ref, gate_ref):
    h = _normed(x_ref, g_ref) * (1.0 + mod_ref[0, 1:2, :]) + mod_ref[0, 0:1, :]
    o_ref[...] = h.astype(o_ref.dtype)
    z = _dot3(h, w_ref[...]) + b_ref[...]
    z = M_GATE_CAP * jnp.tanh(z / M_GATE_CAP)
    is_forget = ((_iota(z.shape, 1) // M_HEADS) % 2) == 1
    gate_ref[...] = jnp.where(is_forget, _log_sigmoid(z), z)


def _mixer_residual_and_route(x_ref, mod_ref, mix, g2_ref, wr_ref, out_ref, hffn_ref, aff_ref):
    x = x_ref[...] + mod_ref[0, 2:3, :] * mix
    out_ref[...] = x
    h = x * lax.rsqrt(jnp.mean(x * x, axis=-1, keepdims=True) + NORM_EPS) * g2_ref[...]
    h = h * (1.0 + mod_ref[0, 4:5, :]) + mod_ref[0, 3:4, :]
    hffn_ref[...] = h
    z = _dot3(h, wr_ref[...])
    e = jnp.exp(z - jnp.max(z, axis=-1, keepdims=True))
    aff_ref[...] = e / jnp.sum(e, axis=-1, keepdims=True)


def _route_tail_specs(n, d, tm, norm2, w_router):
    ns = w_router.shape[1]
    in_specs = [pl.BlockSpec((1, d), lambda i: (0, 0)), pl.BlockSpec((d, ns), lambda i: (0, 0))]
    out_specs = [pl.BlockSpec((tm, d), lambda i: (i, 0)), pl.BlockSpec((tm, d), lambda i: (i, 0)),
                 pl.BlockSpec((tm, ns), lambda i: (i, 0))]
    out_shape = [jax.ShapeDtypeStruct((n, d), F32), jax.ShapeDtypeStruct((n, d), F32),
                 jax.ShapeDtypeStruct((n, ns), F32)]
    return in_specs, [norm2.reshape(1, d), w_router], out_specs, out_shape


def _row_tile(n, t, want):
    tm = min(want, t)
    assert t % tm == 0 and n % tm == 0
    return tm


def _mod_spec(mod, tm, t):
    d = mod.shape[-1]
    if mod.shape[0] == 1:
        return pl.BlockSpec((1, 6, d), lambda i: (0, 0, 0))
    return pl.BlockSpec((1, 6, d), lambda i: ((i * tm) // t, 0, 0))


def _norm_gate(x, gain, mod, t, w, b):
    n, d = x.shape
    tm = _row_tile(n, t, 256)
    ns = w.shape[1]
    xs = pl.BlockSpec((tm, d), lambda i: (i, 0))
    return pl.pallas_call(
        _norm_gate_kernel, grid=(n // tm,),
        in_specs=[xs, pl.BlockSpec((1, d), lambda i: (0, 0)), _mod_spec(mod, tm, t),
                  pl.BlockSpec((d, ns), lambda i: (0, 0)), pl.BlockSpec((1, ns), lambda i: (0, 0))],
        out_specs=[xs, pl.BlockSpec((tm, ns), lambda i: (i, 0))],
        out_shape=[jax.ShapeDtypeStruct((n, d), BF16), jax.ShapeDtypeStruct((n, ns), F32)],
        compiler_params=_params("arbitrary"), name="norm_gate")(x, gain.reshape(1, d), mod, w, b.reshape(1, ns))


def _mm_kernel(x_ref, w_ref, o_ref):
    o_ref[...] = _dot(x_ref[...], w_ref[...])


def _mm(x, w, tn):
    n, k = x.shape
    m = w.shape[1]
    tm = min(512, n)
    tn = min(tn, m)
    return pl.pallas_call(
        _mm_kernel, grid=(m // tn, n // tm),
        in_specs=[pl.BlockSpec((tm, k), lambda j, i: (i, 0)), pl.BlockSpec((k, tn), lambda j, i: (0, j))],
        out_specs=pl.BlockSpec((tm, tn), lambda j, i: (i, j)),
        out_shape=jax.ShapeDtypeStruct((n, m), F32),
        compiler_params=_params("arbitrary", "arbitrary"), name="mm")(x, w)


def _mlstm_kernel(has_init, emit_state, *refs):
    q_ref, k_ref, v_ref, g_ref, gt_ref = refs[:5]
    pos = 5
    if has_init:
        c0_ref, n0_ref, m0_ref = refs[pos:pos + 3]
        pos += 3
    h_ref = refs[pos]
    pos += 1
    if emit_state:
        cout_ref, nout_ref, mout_ref = refs[pos:pos + 3]
        pos += 3
    c_scr, n_scr, m_scr = refs[pos:pos + 3]

    d = pl.program_id(1)
    c = pl.program_id(2)
    nc = pl.num_programs(2)
    fwd = d == 0
    heads = M_HEADS
    length = q_ref.shape[0]

    @pl.when(c == 0)
    def _():
        if has_init:
            c_scr[...] = c0_ref[0, 0]
            n_scr[...] = n0_ref[0, 0]
            m_scr[...] = m0_ref[0, 0]
        else:
            c_scr[...] = jnp.zeros_like(c_scr)
            n_scr[...] = jnp.zeros_like(n_scr)
            m_scr[...] = jnp.zeros_like(m_scr)

    sgn = jnp.where(fwd, 1, -1)
    row = _iota((length, length), 0)
    col = _iota((length, length), 1)
    mask = ((row - col) * sgn) >= 0
    mask_t = ((col - row) * sgn) >= 0
    scale = M_DK ** -0.5
    g = g_ref[...]
    gt = gt_ref[0]

    tri = mask.astype(F32).astype(BF16)
    tri_t = mask_t.astype(F32).astype(BF16)
    g1, g2, g3 = _split3(g)
    cum_cols = _dot(tri, g1) + _dot(tri, g2) + _dot(tri, g3)
    t1, t2, t3 = _split3(gt)
    cum_rows = _dot(t1, tri_t) + _dot(t2, tri_t) + _dot(t3, tri_t)

    def pick_col(x, kind, h):
        i, j = kind * heads + h, (kind + 2) * heads + h
        return jnp.where(fwd, x[:, i:i + 1], x[:, j:j + 1])

    def pick_row(x, kind, h):
        i, j = kind * heads + h, (kind + 2) * heads + h
        return jnp.where(fwd, x[i:i + 1, :], x[j:j + 1, :])

    def each(fn, *lists):
        return [fn(*args) for args in zip(*lists)]

    hs = list(range(heads))
    ig_row = [pick_row(gt, 0, h) for h in hs]
    ig_col = [pick_col(g, 0, h) for h in hs]
    b_col = [pick_col(cum_cols, 1, h) for h in hs]
    b_row = [pick_row(cum_rows, 1, h) for h in hs]
    m_prev = [m_scr[h:h + 1, 0:1] for h in hs]
    c_prev = [c_scr[h] for h in hs]
    n_prev = [n_scr[h:h + 1, :] for h in hs]
    qh = [q_ref[:, h * M_DK:(h + 1) * M_DK] for h in hs]
    kh = [k_ref[:, h * M_DK:(h + 1) * M_DK] * scale for h in hs]
    qb = each(lambda x: x.astype(BF16), qh)
    kb = each(lambda x: x.astype(BF16), kh)
    vb = [v_ref[:, h * M_DV:(h + 1) * M_DV].astype(BF16) for h in hs]
    qk = each(_dot_nt, qb, kb)
    qc = each(lambda q_, c_: _dot(q_, c_.astype(BF16)), qb, c_prev)
    dmat = each(lambda bc, br, ir: jnp.where(mask, bc - br + ir, -jnp.inf), b_col, b_row, ig_row)
    a = each(lambda bc, mp: bc + mp, b_col, m_prev)
    m_t = each(lambda a_, d_: jnp.maximum(a_, jnp.max(d_, axis=1, keepdims=True)), a, dmat)
    w_inter = each(lambda a_, m_: jnp.exp(a_ - m_), a, m_t)
    s = each(lambda x_, d_, m_: x_ * jnp.exp(d_ - m_), qk, dmat, m_t)
    sv = each(lambda s_, v_: _dot(s_.astype(BF16), v_), s, vb)
    num = each(lambda w_, x_, y_: w_ * x_ + y_, w_inter, qc, sv)
    den = each(lambda w_, q_, n_, s_: w_ * jnp.sum(q_ * n_, axis=1, keepdims=True)
               + jnp.sum(s_, axis=1, keepdims=True), w_inter, qh, n_prev, s)
    for h in hs:
        h_ref[0, :, h * M_DV:(h + 1) * M_DV] = num[h] / jnp.maximum(jnp.abs(den[h]), jnp.exp(-m_t[h]))

    b_last = each(lambda bc: jnp.where(fwd, bc[length - 1:length, :], bc[0:1, :]), b_col)
    g_col = each(lambda bl, bc, ic: bl - bc + ic, b_last, b_col, ig_col)
    m_new = each(lambda bl, mp, gc: jnp.maximum(bl + mp, jnp.max(gc, axis=0, keepdims=True)), b_last, m_prev, g_col)
    decay = each(lambda bl, mp, mn: jnp.exp(bl + mp - mn), b_last, m_prev, m_new)
    wk = each(lambda gc, mn, k_: jnp.exp(gc - mn) * k_, g_col, m_new, kh)
    kv = each(lambda w_, v_: _dot_tn(w_.astype(BF16), v_), wk, vb)
    for h in hs:
        c_scr[h] = decay[h] * c_prev[h] + kv[h]
        n_scr[h:h + 1, :] = decay[h] * n_prev[h] + jnp.sum(wk[h], axis=0, keepdims=True)
        m_scr[h:h + 1, :] = jnp.broadcast_to(m_new[h], (1, LANES))

    if emit_state:
        @pl.when(c == nc - 1)
        def _():
            cout_ref[0, 0] = c_scr[...]
            nout_ref[0, 0] = n_scr[...]
            mout_ref[0, 0] = m_scr[...]


def _mlstm_scan(proj, gates, b, t, init, emit_state):
    n = proj.shape[0]
    heads, dk, dv = M_HEADS, M_DK, M_DV
    qk, mv = heads * dk, heads * dv
    length = min(MLSTM_CHUNK, t)
    nc = t // length
    gates_t = gates.reshape(n // length, length, 4 * heads).transpose(0, 2, 1)

    def tile(bi, di, ci):
        return bi * nc + ci + di * (nc - 1 - 2 * ci)

    in_specs = [pl.BlockSpec((length, qk), lambda bi, di, ci: (tile(bi, di, ci), 0)),
                pl.BlockSpec((length, qk), lambda bi, di, ci: (tile(bi, di, ci), 1)),
                pl.BlockSpec((length, mv), lambda bi, di, ci: (tile(bi, di, ci), 1)),
                pl.BlockSpec((length, 4 * heads), lambda bi, di, ci: (tile(bi, di, ci), 0)),
                pl.BlockSpec((1, 4 * heads, length), lambda bi, di, ci: (tile(bi, di, ci), 0, 0))]
    args = [proj, proj, proj, gates, gates_t]
    state_specs = [pl.BlockSpec((1, 1, heads, dk, dv), lambda bi, di, ci: (bi, di, 0, 0, 0)),
                   pl.BlockSpec((1, 1, heads, dk), lambda bi, di, ci: (bi, di, 0, 0)),
                   pl.BlockSpec((1, 1, heads, LANES), lambda bi, di, ci: (bi, di, 0, 0))]
    if init is not None:
        c0, n0, m0 = init
        in_specs += state_specs
        args += [c0, n0, jnp.broadcast_to(m0[..., None], m0.shape + (LANES,))]
    out_specs = [pl.BlockSpec((1, length, mv), lambda bi, di, ci: (di, tile(bi, di, ci), 0))]
    out_shape = [jax.ShapeDtypeStruct((2, n, mv), F32)]
    if emit_state:
        out_specs += state_specs
        out_shape += [jax.ShapeDtypeStruct((b, 2, heads, dk, dv), F32),
                      jax.ShapeDtypeStruct((b, 2, heads, dk), F32),
                      jax.ShapeDtypeStruct((b, 2, heads, LANES), F32)]
    return pl.pallas_call(
        functools.partial(_mlstm_kernel, init is not None, emit_state),
        grid=(b, 2, nc), in_specs=in_specs, out_specs=out_specs, out_shape=out_shape,
        scratch_shapes=[pltpu.VMEM((heads, dk, dv), F32), pltpu.VMEM((heads, dk), F32),
                        pltpu.VMEM((heads, LANES), F32)],
        compiler_params=_params("arbitrary", "arbitrary", "arbitrary"), name="mlstm_scan")(*args)


def _mlstm_out_kernel(hs_ref, o_ref, gain_ref, w_ref, x_ref, mod_ref, g2_ref, wr_ref, out_ref, hffn_ref, aff_ref):
    hs = hs_ref[0] + hs_ref[1]
    parts = []
    for h in range(M_HEADS):
        seg = hs[:, h * M_DV:(h + 1) * M_DV]
        parts.append(seg * lax.rsqrt(jnp.mean(seg * seg, axis=-1, keepdims=True) + NORM_EPS))
    hn = jnp.concatenate(parts, axis=1) * gain_ref[...]
    lhs = (jax.nn.sigmoid(o_ref[...]) * hn).astype(BF16)
    _mixer_residual_and_route(x_ref, mod_ref, _dot(lhs, w_ref[...]), g2_ref, wr_ref, out_ref, hffn_ref, aff_ref)


def _mlstm_out(hs2, proj, gain, w_out, x, mod, t, norm2, w_router):
    n, d = x.shape
    mv = M_HEADS * M_DV
    tm = _row_tile(n, t, 256)
    tail_in, tail_args, out_specs, out_shape = _route_tail_specs(n, d, tm, norm2, w_router)
    return pl.pallas_call(
        _mlstm_out_kernel, grid=(n // tm,),
        in_specs=[pl.BlockSpec((2, tm, mv), lambda i: (0, i, 0)),
                  pl.BlockSpec((tm, mv), lambda i: (i, 2)),
                  pl.BlockSpec((1, mv), lambda i: (0, 0)),
                  pl.BlockSpec((mv, d), lambda i: (0, 0)),
                  pl.BlockSpec((tm, d), lambda i: (i, 0)),
                  _mod_spec(mod, tm, t)] + tail_in,
        out_specs=out_specs, out_shape=out_shape,
        compiler_params=_params("arbitrary"), name="mlstm_out")(
            hs2, proj, gain.reshape(1, mv), w_out, x, mod, *tail_args)


def _attn_qkv_kernel(use_rope, *refs):
    if use_rope:
        x_ref, w_ref, qg_ref, kg_ref, cos_ref, sin_ref, q_ref, k_ref, v_ref = refs
    else:
        x_ref, w_ref, qg_ref, kg_ref, q_ref, k_ref, v_ref = refs
    acc = _dot(x_ref[...], w_ref[...])
    qd, kvd = A_HEADS * A_HD, A_KV_HEADS * A_HD

    def head_norm(xh, gain):
        xh = xh * lax.rsqrt(jnp.mean(xh * xh, axis=-1, keepdims=True) + NORM_EPS) * gain
        if use_rope:
            xh = xh * cos_ref[...] + pltpu.roll(xh, A_HD // 2, axis=1) * sin_ref[...]
        return xh

    for h in range(A_HEADS):
        qh = head_norm(acc[:, h * A_HD:(h + 1) * A_HD], qg_ref[...])
        q_ref[:, h * A_HD:(h + 1) * A_HD] = (qh * (A_HD ** -0.5)).astype(BF16)
    for h in range(A_KV_HEADS):
        k_ref[:, h * A_HD:(h + 1) * A_HD] = head_norm(acc[:, qd + h * A_HD:qd + (h + 1) * A_HD], kg_ref[...])
    v_ref[...] = acc[:, qd + kvd:]


def _attn_qkv(hb, w, qgain, kgain, t, rope):
    n, d = hb.shape
    qd, kvd = A_HEADS * A_HD, A_KV_HEADS * A_HD
    tm = _row_tile(n, t, 256)
    in_specs = [pl.BlockSpec((tm, d), lambda i: (i, 0)),
                pl.BlockSpec((d, qd + 2 * kvd), lambda i: (0, 0)),
                pl.BlockSpec((1, A_HD), lambda i: (0, 0)),
                pl.BlockSpec((1, A_HD), lambda i: (0, 0))]
    args = [hb, w, qgain.reshape(1, A_HD), kgain.reshape(1, A_HD)]
    if rope is not None:
        in_specs += [pl.BlockSpec((tm, A_HD), lambda i: (i % (t // tm), 0))] * 2
        args += list(rope)
    return pl.pallas_call(
        functools.partial(_attn_qkv_kernel, rope is not None), grid=(n // tm,),
        in_specs=in_specs,
        out_specs=[pl.BlockSpec((tm, qd), lambda i: (i, 0)),
                   pl.BlockSpec((tm, kvd), lambda i: (i, 0)),
                   pl.BlockSpec((tm, kvd), lambda i: (i, 0))],
        out_shape=[jax.ShapeDtypeStruct((n, qd), BF16), jax.ShapeDtypeStruct((n, kvd), F32),
                   jax.ShapeDtypeStruct((n, kvd), F32)],
        compiler_params=_params("arbitrary"), name="attn_qkv")(*args)


def _attn_kernel(has_cache, *refs):
    if has_cache:
        q_ref, kn_ref, vn_ref, kc_ref, vc_ref, o_ref, k_scr, v_scr = refs
    else:
        q_ref, kn_ref, vn_ref, o_ref, k_scr, v_scr = refs
    group = A_HEADS // A_KV_HEADS
    tq = q_ref.shape[0]
    t_new = kn_ref.shape[0]
    past = k_scr.shape[0] - t_new

    @pl.when(pl.program_id(2) == 0)
    def _():
        if has_cache:
            k_scr[0:past, :] = kc_ref[...].astype(BF16)
            v_scr[0:past, :] = vc_ref[...].astype(BF16)
        k_scr[past:, :] = kn_ref[...].astype(BF16)
        v_scr[past:, :] = vn_ref[...].astype(BF16)

    sub = min(ATTN_SUB_ROWS, tq)
    starts = list(range(0, tq, sub))
    q4 = [jnp.concatenate([q_ref[r0:r0 + sub, g * A_HD:(g + 1) * A_HD] for g in range(group)], axis=0)
          for r0 in starts]
    s = [_dot_nt(x, k_scr[...]) for x in q4]
    p = [jnp.exp(x - jnp.max(x, axis=-1, keepdims=True)) for x in s]
    pv = [_dot(x.astype(BF16), v_scr[...]) for x in p]
    o = [x / jnp.sum(y, axis=-1, keepdims=True) for x, y in zip(pv, p)]
    for r0, x in zip(starts, o):
        for g in range(group):
            o_ref[r0:r0 + sub, g * A_HD:(g + 1) * A_HD] = x[g * sub:(g + 1) * sub, :].astype(BF16)


def _attention(q, k, v, b, t, cache):
    n = q.shape[0]
    group = A_HEADS // A_KV_HEADS
    tq = min(ATTN_Q_ROWS, t)
    nq = t // tq
    past = 0 if cache is None else cache[0].shape[0] // b
    in_specs = [pl.BlockSpec((tq, group * A_HD), lambda bi, hi, qi: (bi * nq + qi, hi)),
                pl.BlockSpec((t, A_HD), lambda bi, hi, qi: (bi, hi)),
                pl.BlockSpec((t, A_HD), lambda bi, hi, qi: (bi, hi))]
    args = [q, k, v]
    if cache is not None:
        in_specs += [pl.BlockSpec((past, A_HD), lambda bi, hi, qi: (bi, hi))] * 2
        args += list(cache)
    return pl.pallas_call(
        functools.partial(_attn_kernel, cache is not None), grid=(b, A_KV_HEADS, nq),
        in_specs=in_specs,
        out_specs=pl.BlockSpec((tq, group * A_HD), lambda bi, hi, qi: (bi * nq + qi, hi)),
        out_shape=jax.ShapeDtypeStruct((n, A_HEADS * A_HD), BF16),
        scratch_shapes=[pltpu.VMEM((past + t, A_HD), BF16), pltpu.VMEM((past + t, A_HD), BF16)],
        compiler_params=_params("arbitrary", "arbitrary", "arbitrary"), name="attention")(*args)


def _resid_mm_kernel(lhs_ref, w_ref, x_ref, mod_ref, g2_ref, wr_ref, out_ref, hffn_ref, aff_ref):
    _mixer_residual_and_route(x_ref, mod_ref, _dot(lhs_ref[...], w_ref[...]), g2_ref, wr_ref, out_ref, hffn_ref,
                              aff_ref)


def _resid_mm(lhs, w, x, mod, t, norm2, w_router):
    n, d = x.shape
    k = lhs.shape[1]
    tm = _row_tile(n, t, 256)
    tail_in, tail_args, out_specs, out_shape = _route_tail_specs(n, d, tm, norm2, w_router)
    return pl.pallas_call(
        _resid_mm_kernel, grid=(n // tm,),
        in_specs=[pl.BlockSpec((tm, k), lambda i: (i, 0)),
                  pl.BlockSpec((k, d), lambda i: (0, 0)),
                  pl.BlockSpec((tm, d), lambda i: (i, 0)),
                  _mod_spec(mod, tm, t)] + tail_in,
        out_specs=out_specs, out_shape=out_shape,
        compiler_params=_params("arbitrary"), name="resid_mm")(lhs, w, x, mod, *tail_args)


def _token_shift_delta(h, hp_ref, hn_ref, i, tiles_per_seq):
    tm = h.shape[0]
    rowi = _iota((tm, 1), 0)
    ti = i % tiles_per_seq
    prev_row = jnp.where(ti == 0, 0.0, hp_ref[SUBLANES - 1:SUBLANES, :])
    next_row = jnp.where(ti == tiles_per_seq - 1, 0.0, hn_ref[0:1, :])
    h_prev = jnp.where(rowi == 0, prev_row, pltpu.roll(h, 1, axis=0))
    h_next = jnp.where(rowi == tm - 1, next_row, pltpu.roll(h, tm - 1, axis=0))
    return 0.5 * (h_prev + h_next) - h


def _shift_specs(tm, d, n, grid_rank):
    nb8 = n // SUBLANES
    per = tm // SUBLANES
    if grid_rank == 1:
        return [pl.BlockSpec((tm, d), lambda i: (i, 0)),
                pl.BlockSpec((SUBLANES, d), lambda i: (jnp.maximum(i * per - 1, 0), 0)),
                pl.BlockSpec((SUBLANES, d), lambda i: (jnp.minimum((i + 1) * per, nb8 - 1), 0))]
    return [pl.BlockSpec((tm, d), lambda j, i: (i, 0)),
            pl.BlockSpec((SUBLANES, d), lambda j, i: (jnp.maximum(i * per - 1, 0), 0)),
            pl.BlockSpec((SUBLANES, d), lambda j, i: (jnp.minimum((i + 1) * per, nb8 - 1), 0))]


def _rwkv_rkv_kernel(tiles_per_seq, h_ref, hp_ref, hn_ref, mix_ref, w_ref, o_ref):
    h = h_ref[...]
    xx = _token_shift_delta(h, hp_ref, hn_ref, pl.program_id(1), tiles_per_seq)
    o_ref[0] = _dot((h + xx * mix_ref[0]).astype(BF16), w_ref[0])


def _rwkv_rkv(h, mix3, w_rkv, t):
    n, d = h.shape
    tm = _row_tile(n, t, 256)
    return pl.pallas_call(
        functools.partial(_rwkv_rkv_kernel, t // tm), grid=(3, n // tm),
        in_specs=_shift_specs(tm, d, n, 2) + [pl.BlockSpec((1, 1, d), lambda j, i: (j, 0, 0)),
                                              pl.BlockSpec((1, d, d), lambda j, i: (j, 0, 0))],
        out_specs=pl.BlockSpec((1, tm, d), lambda j, i: (j, i, 0)),
        out_shape=jax.ShapeDtypeStruct((3, n, d), F32),
        compiler_params=_params("arbitrary", "arbitrary"), name="rwkv_rkv")(h, h, h, mix3, w_rkv)


def _rwkv_lora_kernel(tiles_per_seq, lora, h_ref, hp_ref, hn_ref, mix_ref, w1_ref, a1_ref, g1_ref,
                      w2_ref, a2_ref, g2_ref, w0_ref, a0_ref, lw_ref, asig_ref, g_ref):
    h = h_ref[...]
    xx = _token_shift_delta(h, hp_ref, hn_ref, pl.program_id(0), tiles_per_seq)
    xw = (h + xx * mix_ref[0:1, :]).astype(BF16)
    xa = (h + xx * mix_ref[1:2, :]).astype(BF16)
    xg = (h + xx * mix_ref[2:3, :]).astype(BF16)
    tw = jnp.tanh(_dot(xw, w1_ref[...])).astype(BF16)
    ta = _dot(xa, a1_ref[...]).astype(BF16)
    tg = jax.nn.sigmoid(_dot(xg, g1_ref[...])).astype(BF16)
    g_ref[...] = _dot(tg, g2_ref[...])
    for dr in range(2):
        z = -(w0_ref[dr] + _dot(tw[:, dr * lora:(dr + 1) * lora], w2_ref[dr]))
        softplus = jnp.maximum(z, 0.0) + jnp.log(1.0 + jnp.exp(-jnp.abs(z)))
        lw_ref[dr] = -jnp.exp(-softplus - 0.5)
        asig_ref[dr] = jax.nn.sigmoid(a0_ref[dr] + _dot(ta[:, dr * lora:(dr + 1) * lora], a2_ref[dr]))


def _rwkv_lora(h, mix3, w1c, a1c, g1, w2p, a2p, g2, w0, a0, t):
    n, d = h.shape
    tm = _row_tile(n, t, 256)
    lora = w2p.shape[1]
    gl = g1.shape[1]
    full2 = lambda shape: pl.BlockSpec(shape, lambda i: (0, 0))
    full3 = lambda shape: pl.BlockSpec(shape, lambda i: (0, 0, 0))
    return pl.pallas_call(
        functools.partial(_rwkv_lora_kernel, t // tm, lora), grid=(n // tm,),
        in_specs=_shift_specs(tm, d, n, 1) + [
            full2((3, d)), full2((d, 2 * lora)), full2((d, 2 * lora)), full2((d, gl)),
            full3((2, lora, d)), full3((2, lora, d)), full2((gl, d)), full3((2, 1, d)), full3((2, 1, d))],
        out_specs=[pl.BlockSpec((2, tm, d), lambda i: (0, i, 0)),
                   pl.BlockSpec((2, tm, d), lambda i: (0, i, 0)),
                   pl.BlockSpec((tm, d), lambda i: (i, 0))],
        out_shape=[jax.ShapeDtypeStruct((2, n, d), F32), jax.ShapeDtypeStruct((2, n, d), F32),
                   jax.ShapeDtypeStruct((n, d), F32)],
        compiler_params=_params("arbitrary"), name="rwkv_lora")(h, h, h, mix3, w1c, a1c, g1, w2p, a2p, g2, w0, a0)


def _rwkv_scan_kernel(has_init, emit_state, *refs):
    r_ref, k_ref, v_ref, lw_ref, a_ref, kk_ref, ka_ref, rk_ref = refs[:8]
    pos = 8
    if has_init:
        s0_ref = refs[pos]
        pos += 1
    y_ref, bonus_ref = refs[pos:pos + 2]
    pos += 2
    if emit_state:
        sout_ref = refs[pos]
        pos += 1
    s_scr, cum_scr = refs[pos:pos + 2]

    d = pl.program_id(1)
    c = pl.program_id(2)
    nc = pl.num_programs(2)
    fwd = d == 0
    sgn = jnp.where(fwd, 1, -1)
    length = r_ref.shape[1]
    npairs = s_scr.shape[0]
    hd = R_HD

    @pl.when(c == 0)
    def _():
        if has_init:
            s_scr[...] = s0_ref[0, 0]
        else:
            s_scr[...] = jnp.zeros_like(s_scr)

    ti = _iota((length, length), 0)
    si = _iota((length, length), 1)
    tri = (((ti - si) * sgn) >= 0).astype(BF16)
    l1, l2, l3 = _split3(lw_ref[0])
    cum_scr[...] = _dot(tri, l1) + _dot(tri, l2) + _dot(tri, l3)

    gl = 2 * length
    rr = _iota((gl, gl), 0)
    cc = _iota((gl, gl), 1)
    same = (rr // length) == (cc // length)
    e = ((rr % length) - (cc % length)) * sgn
    strict = same & (e > 0)
    incl = same & (e >= 0)
    eye = (rr == cc).astype(F32)
    lane = _iota((1, LANES), 1)
    m0 = (lane < hd).astype(F32)
    m1 = 1.0 - m0
    lr = _iota((LANES, LANES), 0)
    lc = _iota((LANES, LANES), 1)
    ones_blk = ((lr // hd) == (lc // hd)).astype(BF16)

    def segsum(x):
        hi, lo = _split2(x)
        return _dot(hi, ones_blk) + _dot(lo, ones_blk)

    def stack(x):
        return jnp.concatenate([x * m0, x * m1], axis=0)

    def smm(x, y, dims):
        return _mm_passes(x, y, dims, RWKV_SOLVE_PASSES)

    tr = rr % length
    tc = cc % length
    base = 4
    base_mask = (tr // base) == (tc // base)
    level_masks = []
    size = base
    while size < length:
        level_masks.append(((tr // (2 * size)) == (tc // (2 * size))) & ((tr // size) != (tc // size)))
        size *= 2

    def load_pair(p):
        ds = pl.ds(pl.multiple_of(p * LANES, LANES), LANES)
        return dict(ds=ds, p=p, r=r_ref[0, :, ds], k=k_ref[0, :, ds], v=v_ref[0, :, ds], lw=lw_ref[0, :, ds],
                    a=a_ref[0, :, ds], cum=cum_scr[:, ds], kkp=kk_ref[:, ds], kap=ka_ref[:, ds],
                    rkp=rk_ref[:, ds], st=s_scr[p])

    def each(fn, *lists):
        return [fn(*args) for args in zip(*lists)]

    def solve_group(xs):
        get = lambda name: [x[name] for x in xs]
        r, k, v, lw, a, cum, st = (get(n) for n in ("r", "k", "v", "lw", "a", "cum", "st"))
        kk = each(lambda k_, p_: k_ * p_, k, get("kkp"))
        ss = each(lambda q: segsum(q * q), kk)
        kk = each(lambda q, s_: q * lax.rsqrt(jnp.maximum(s_, 1e-24)), kk, ss)
        kd = each(lambda k_, a_, p_: k_ * (1.0 + (a_ - 1.0) * p_), k, a, get("kap"))
        bb = each(lambda q, a_: q * a_, kk, a)
        tot = each(lambda c_: jnp.where(fwd, c_[length - 1:length, :], c_[0:1, :]), cum)
        e_n = each(lambda c_: jnp.exp(-c_), cum)
        sa = each(lambda q, c_, l_: stack(-q * jnp.exp(c_ - l_)), kk, cum, lw)
        sr_full = each(lambda r_, c_: r_ * jnp.exp(c_), r, cum)
        sr = each(stack, sr_full)
        sb = each(lambda b_, e_: stack(b_ * e_), bb, e_n)
        sk = each(lambda k_, e_: stack(k_ * e_), kd, e_n)
        vexp = each(stack, v)
        m_ab = each(lambda x_, y_: jnp.where(strict, smm(x_, y_, _NT), 0.0), sa, sb)
        m_ak = each(lambda x_, y_: jnp.where(strict, smm(x_, y_, _NT), 0.0), sa, sk)
        rhs1 = each(lambda x_, s_: smm(x_, s_, _NT), sa, st)
        rhs2 = each(lambda m_, v_: smm(m_, v_, _NN), m_ak, vexp)
        rhs = each(lambda x_, y_: x_ + y_, rhs1, rhs2)
        d0 = each(lambda m_: jnp.where(base_mask, m_, 0.0), m_ab)
        sq = each(lambda d_: smm(d_, d_, _NN), d0)
        inv = each(lambda d_: eye + d_, d0)
        inv = each(lambda i_, q_: i_ + smm(i_, q_, _NN), inv, sq)
        for lm in level_masks:
            half = each(lambda i_, m_: smm(i_, jnp.where(lm, m_, 0.0), _NN), inv, m_ab)
            inv = each(lambda i_, h_: i_ + smm(h_, i_, _NN), inv, half)
        uexp = each(lambda i_, r_: smm(i_, r_, _NN), inv, rhs)
        t_rb = each(lambda x_, y_: jnp.where(incl, smm(x_, y_, _NT), 0.0), sr, sb)
        t_rk = each(lambda x_, y_: jnp.where(incl, smm(x_, y_, _NT), 0.0), sr, sk)
        fold = lambda t_: t_[0:length] + t_[length:gl]
        y1 = each(lambda r_, s_: smm(r_, s_, _NT), sr_full, st)
        uv = each(lambda u_, v_: jnp.concatenate([u_, v_], axis=0), uexp, vexp)
        y2 = each(lambda b_, k_, w_: smm(jnp.concatenate([fold(b_), fold(k_)], axis=1), w_, _NN), t_rb, t_rk, uv)
        y = each(lambda p_, q_: p_ + q_, y1, y2)
        bs = each(lambda r_, k_, p_: segsum(r_ * k_ * p_), r, kd, get("rkp"))
        bonus = each(lambda s_, v_: s_ * v_, bs, v)
        e_e = each(lambda t_, c_: jnp.exp(t_ - c_), tot, cum)
        bk = each(lambda b_, k_, e_: jnp.concatenate([stack(b_ * e_), stack(k_ * e_)], axis=0), bb, kd, e_e)
        s1 = each(lambda w_, x_: smm(w_, x_, _TN), uv, bk)
        s_new = each(lambda s_, t_, p_: s_ * jnp.exp(t_) + p_, st, tot, s1)
        return y, bonus, s_new

    npg = min(RWKV_PAIRS_PER_GROUP, npairs)

    def group(gi, carry):
        xs = [load_pair(gi * npg + u) for u in range(npg)]
        ys, bonuses, states = solve_group(xs)
        for x, y, bonus, s_new in zip(xs, ys, bonuses, states):
            y_ref[0, :, x["ds"]] = y
            bonus_ref[0, :, x["ds"]] = bonus
            s_scr[x["p"]] = s_new
        return carry

    lax.fori_loop(0, npairs // npg, group, 0)

    if emit_state:
        @pl.when(c == nc - 1)
        def _():
            sout_ref[0, 0] = s_scr[...]


def _rwkv_scan(rkv, lw, asig, kk, ka, rk, b, t, s0, emit_state):
    n, d = rkv.shape[1:]
    length = RWKV_CHUNK
    nc = t // length
    npairs = d // LANES

    def tile(bi, di, ci):
        return bi * nc + ci + di * (nc - 1 - 2 * ci)

    def plane(j):
        return pl.BlockSpec((1, length, d), lambda bi, di, ci: (j, tile(bi, di, ci), 0))

    dirspec = pl.BlockSpec((1, length, d), lambda bi, di, ci: (di, tile(bi, di, ci), 0))
    vec = pl.BlockSpec((1, d), lambda bi, di, ci: (0, 0))
    sspec = pl.BlockSpec((1, 1, npairs, LANES, LANES), lambda bi, di, ci: (bi, di, 0, 0, 0))
    in_specs = [plane(0), plane(1), plane(2), dirspec, dirspec, vec, vec, vec]
    args = [rkv, rkv, rkv, lw, asig, kk.reshape(1, d), ka.reshape(1, d), rk.reshape(1, d)]
    if s0 is not None:
        in_specs.append(sspec)
        args.append(s0)
    out_specs = [dirspec, dirspec]
    out_shape = [jax.ShapeDtypeStruct((2, n, d), F32), jax.ShapeDtypeStruct((2, n, d), F32)]
    if emit_state:
        out_specs.append(sspec)
        out_shape.append(jax.ShapeDtypeStruct((b, 2, npairs, LANES, LANES), F32))
    return pl.pallas_call(
        functools.partial(_rwkv_scan_kernel, s0 is not None, emit_state),
        grid=(b, 2, nc), in_specs=in_specs, out_specs=out_specs, out_shape=out_shape,
        scratch_shapes=[pltpu.VMEM((npairs, LANES, LANES), F32), pltpu.VMEM((length, d), F32)],
        compiler_params=_params("arbitrary", "arbitrary", "arbitrary"), name="rwkv_scan")(*args)


def _rwkv_out_kernel(y_ref, bonus_ref, g_ref, lnw_ref, lnb_ref, w_ref, x_ref, mod_ref, g2_ref, wr_ref, out_ref,
                     hffn_ref, aff_ref):
    y = y_ref[0] + y_ref[1]
    lr = _iota((LANES, LANES), 0)
    lc = _iota((LANES, LANES), 1)
    mean_blk = ((lr // R_HD) == (lc // R_HD)).astype(BF16)

    def segmean(x):
        hi, lo = _split2(x)
        return (_dot(hi, mean_blk) + _dot(lo, mean_blk)) * (1.0 / R_HD)

    tiles = [y[:, j * LANES:(j + 1) * LANES] for j in range(y.shape[1] // LANES)]
    means = [segmean(yj) for yj in tiles]
    cens = [yj - mj for yj, mj in zip(tiles, means)]
    variances = [segmean(cj * cj) for cj in cens]
    parts = [cj * lax.rsqrt(vj + R_LN_EPS) for cj, vj in zip(cens, variances)]
    yn = jnp.concatenate(parts, axis=1) * lnw_ref[...] + lnb_ref[...] + bonus_ref[0] + bonus_ref[1]
    lhs = (yn * g_ref[...]).astype(BF16)
    _mixer_residual_and_route(x_ref, mod_ref, _dot(lhs, w_ref[...]), g2_ref, wr_ref, out_ref, hffn_ref, aff_ref)


def _rwkv_out(y2, bonus2, g, lnw, lnb, w_out, x, mod, t, norm2, w_router):
    n, d = x.shape
    tm = _row_tile(n, t, 256)
    two = pl.BlockSpec((2, tm, d), lambda i: (0, i, 0))
    rows = pl.BlockSpec((tm, d), lambda i: (i, 0))
    vec = pl.BlockSpec((1, d), lambda i: (0, 0))
    tail_in, tail_args, out_specs, out_shape = _route_tail_specs(n, d, tm, norm2, w_router)
    return pl.pallas_call(
        _rwkv_out_kernel, grid=(n // tm,),
        in_specs=[two, two, rows, vec, vec, pl.BlockSpec((d, d), lambda i: (0, 0)), rows,
                  _mod_spec(mod, tm, t)] + tail_in,
        out_specs=out_specs, out_shape=out_shape,
        compiler_params=_params("arbitrary"), name="rwkv_out")(
            y2, bonus2, g, lnw.reshape(1, d), lnb.reshape(1, d), w_out, x, mod, *tail_args)


def _count(m):
    x = jnp.sum(m.astype(F32), axis=1, keepdims=True)
    return jnp.sum(x, axis=0, keepdims=True)


def _route_search_kernel(cap, aff_ref, thr_ref, tie_ref):
    n_exp, nb, _ = aff_ref.shape
    experts = list(range(n_exp))
    bits = [lax.bitcast_convert_type(aff_ref[e], I32) for e in experts]
    tok = _iota((nb, LANES), 0) * LANES + _iota((nb, LANES), 1)

    def value_step(i, cur):
        bit = jnp.int32(1) << (30 - i)
        return tuple(jnp.where(_count(b >= (c | bit)) >= cap, c | bit, c) for b, c in zip(bits, cur))

    zeros = tuple(jnp.zeros((1, 1), I32) for _ in experts)
    thr = lax.fori_loop(0, 31, value_step, zeros)
    equal = [b == t for b, t in zip(bits, thr)]
    need = [cap - _count(b > t) for b, t in zip(bits, thr)]

    def index_step(i, cur):
        bit = jnp.int32(1) << (14 - i)
        return tuple(jnp.where(_count(q & (tok < (c + bit))) < n, c + bit, c) for q, n, c in zip(equal, need, cur))

    tie = lax.fori_loop(0, 15, index_step, zeros)
    for e in experts:
        thr_ref[e] = jnp.broadcast_to(thr[e], (SUBLANES, LANES))
        tie_ref[e] = jnp.broadcast_to(tie[e], (SUBLANES, LANES))


def _route_kernel(cap, aff_ref, thr_ref, tie_ref, idx_ref, gate_ref, pos_ref, tot_ref, off_ref):
    aff = aff_ref[0]
    nb = aff.shape[0]
    bits = lax.bitcast_convert_type(aff, I32)
    tok = _iota(aff.shape, 0) * LANES + _iota(aff.shape, 1)
    thr = thr_ref[0, 0:1, 0:1]
    sel = (bits > thr) | ((bits == thr) & (tok <= tie_ref[0, 0:1, 0:1]))
    self32 = sel.astype(F32)
    selb = self32.astype(BF16)

    li = _iota((LANES, LANES), 0)
    lj = _iota((LANES, LANES), 1)
    upper_incl = (li <= lj).astype(BF16)
    cl = _dot(selb, upper_incl)
    ones8 = jnp.ones((SUBLANES, LANES), BF16)
    tot_row = _dot_nt(ones8, selb)
    bi = _iota((nb, nb), 0)
    bj = _iota((nb, nb), 1)
    offi_row = _dot(tot_row.astype(BF16), (bi <= bj).astype(BF16))
    offx_row = offi_row - tot_row
    tot_col = jnp.broadcast_to(cl[:, LANES - 1:LANES], (nb, LANES)).astype(BF16)
    offx_col = _dot((bj < bi).astype(BF16), tot_col)
    pos_ref[0] = jnp.where(sel, offx_col + cl - 1.0, -1.0).astype(I32)
    tot_ref[0] = tot_row
    off_ref[0] = offx_row

    pcol = _iota((cap, 1), 0).astype(F32)
    before = offi_row[0:1, :] <= pcol
    blk = jnp.sum(before.astype(F32), axis=1, keepdims=True)
    base = jnp.sum(jnp.where(before, tot_row[0:1, :], 0.0), axis=1, keepdims=True)
    onehot = (_iota((cap, nb), 1).astype(F32) == blk).astype(BF16)
    rowcnt = _dot(onehot, cl.astype(BF16))
    rank = pcol - base
    lane = jnp.sum((rowcnt <= rank).astype(F32), axis=1, keepdims=True)
    idx_ref[0] = (blk * LANES + lane).astype(I32)
    a1, a2, a3 = _split3(aff)
    rowaff = _dot(onehot, a1) + _dot(onehot, a2) + _dot(onehot, a3)
    gate_ref[0] = jnp.sum(jnp.where(_iota((cap, LANES), 1).astype(F32) == lane, rowaff, 0.0),
                          axis=1, keepdims=True)


def _route(aff3, cap):
    e, nb, _ = aff3.shape
    word = pl.BlockSpec((1, SUBLANES, LANES), lambda i: (i, 0, 0))
    thr, tie = pl.pallas_call(
        functools.partial(_route_search_kernel, cap), grid=(1,),
        in_specs=[pl.BlockSpec((e, nb, LANES), lambda i: (0, 0, 0))],
        out_specs=[pl.BlockSpec((e, SUBLANES, LANES), lambda i: (0, 0, 0))] * 2,
        out_shape=[jax.ShapeDtypeStruct((e, SUBLANES, LANES), I32)] * 2,
        compiler_params=_params("arbitrary"), name="moe_route_search")(aff3)
    return pl.pallas_call(
        functools.partial(_route_kernel, cap), grid=(e,),
        in_specs=[pl.BlockSpec((1, nb, LANES), lambda i: (i, 0, 0)), word, word],
        out_specs=[pl.BlockSpec((1, cap, 1), lambda i: (i, 0, 0)),
                   pl.BlockSpec((1, cap, 1), lambda i: (i, 0, 0)),
                   pl.BlockSpec((1, nb, LANES), lambda i: (i, 0, 0)),
                   pl.BlockSpec((1, SUBLANES, nb), lambda i: (i, 0, 0)),
                   pl.BlockSpec((1, SUBLANES, nb), lambda i: (i, 0, 0))],
        out_shape=[jax.ShapeDtypeStruct((e, cap, 1), I32), jax.ShapeDtypeStruct((e, cap, 1), F32),
                   jax.ShapeDtypeStruct((e, nb, LANES), I32),
                   jax.ShapeDtypeStruct((e, SUBLANES, nb), F32), jax.ShapeDtypeStruct((e, SUBLANES, nb), F32)],
        compiler_params=_params("arbitrary"), name="moe_route")(aff3, thr, tie)


def _row_copy(x_hbm, xbuf, sem, slot, tok, r):
    return pltpu.make_async_copy(x_hbm.at[pl.ds(tok, 1)], xbuf.at[pl.ds(r, 1)], sem.at[slot])


def _expert_kernel(idx_ref, idx_next_ref, x_hbm, gate_ref, wg_ref, wu_ref, wd_ref, o_ref, xbuf0, xbuf1, sem):
    tm = xbuf0.shape[0]
    bufs = (xbuf0, xbuf1)
    step = pl.program_id(0) * pl.num_programs(1) + pl.program_id(1)
    last = pl.num_programs(0) * pl.num_programs(1) - 1

    def start_rows(ids_ref, slot):
        for r in range(tm):
            _row_copy(x_hbm, bufs[slot], sem, slot, ids_ref[0, 0, r], r).start()

    def wait_rows(slot):
        for r in range(tm):
            _row_copy(x_hbm, bufs[slot], sem, slot, 0, r).wait()

    @pl.when(step == 0)
    def _():
        start_rows(idx_ref, 0)

    def run(slot):
        start_rows(idx_next_ref, 1 - slot)
        wait_rows(slot)
        xb = bufs[slot][...].astype(BF16)
        hg = _dot(xb, wg_ref[0, 0])
        hu = _dot(xb, wu_ref[0, 0])
        hid = (hg * jax.nn.sigmoid(hg) * hu).astype(BF16)
        o_ref[...] = (_dot(hid, wd_ref[0, 0]) * gate_ref[...]).astype(o_ref.dtype)

        @pl.when(step == last)
        def _():
            wait_rows(1 - slot)

    for slot in range(2):
        @pl.when(step % 2 == slot)
        def _():
            run(slot)


def _experts(hffn, idx, gate, wg, wu, wd, layer, cap):
    n, d = hffn.shape
    _, e, _, f = wg.shape
    tm = min(256, cap)
    nt = cap // tm
    idx3 = idx.reshape(e * nt, 1, tm)
    last = e * nt - 1
    return pl.pallas_call(
        _expert_kernel, grid=(e, nt),
        in_specs=[pl.BlockSpec((1, 1, tm), lambda ei, ti: (ei * nt + ti, 0, 0), memory_space=pltpu.SMEM),
                  pl.BlockSpec((1, 1, tm), lambda ei, ti: (jnp.minimum(ei * nt + ti + 1, last), 0, 0),
                               memory_space=pltpu.SMEM),
                  pl.BlockSpec(memory_space=pl.ANY),
                  pl.BlockSpec((tm, 1), lambda ei, ti: (ei * nt + ti, 0)),
                  pl.BlockSpec((1, 1, d, f), lambda ei, ti: (layer, ei, 0, 0)),
                  pl.BlockSpec((1, 1, d, f), lambda ei, ti: (layer, ei, 0, 0)),
                  pl.BlockSpec((1, 1, f, d), lambda ei, ti: (layer, ei, 0, 0))],
        out_specs=pl.BlockSpec((tm, d), lambda ei, ti: (ei * nt + ti, 0)),
        out_shape=jax.ShapeDtypeStruct((e * cap, d), BF16),
        scratch_shapes=[pltpu.VMEM((tm, d), F32), pltpu.VMEM((tm, d), F32), pltpu.SemaphoreType.DMA((2,))],
        compiler_params=_params("arbitrary", "arbitrary"), name="moe_experts")(
            idx3, idx3, hffn, gate.reshape(e * cap, 1), wg, wu, wd)


def _slab_copy(ye_hbm, slab, sem, buf, src_row, slot):
    return pltpu.make_async_copy(ye_hbm.at[pl.ds(pl.multiple_of(src_row, PACKED_ROWS), SLAB_CHUNK)],
                                 slab.at[buf, pl.ds(pl.multiple_of(slot * SLAB_CHUNK, SLAB_CHUNK), SLAB_CHUNK)],
                                 sem.at[buf])


def _combine_kernel(n_exp, post, meta_ref, meta_next_ref, x_ref, mod_ref, pos_ref, ye_hbm, *rest):
    slab, acc, sem = rest[-3:]
    if post == "next":
        gain_ref, modn_ref, out_ref, h_ref = rest[:-3]
    elif post == "final":
        gain_ref, out_ref = rest[:-3]
    else:
        (out_ref,) = rest[:-3]
    tm = x_ref.shape[0]
    i = pl.program_id(0)
    buf = i % 2
    total = meta_ref[0, 0, 0]
    total_next = jnp.where(i + 1 < pl.num_programs(0), meta_next_ref[0, 0, 0], 0)

    def start_chunks(ref, count, b):
        def body(g, carry):
            _slab_copy(ye_hbm, slab, sem, b, ref[0, 0, 1 + n_exp + g], g).start()
            return carry
        lax.fori_loop(0, count, body, 0)

    @pl.when(i == 0)
    def _():
        slab[...] = jnp.zeros_like(slab)
        start_chunks(meta_ref, total, 0)

    start_chunks(meta_next_ref, total_next, 1 - buf)

    def wait(g, carry):
        _slab_copy(ye_hbm, slab, sem, buf, 0, g).wait()
        return carry

    pos = pos_ref[...]
    sub_e = _iota((n_exp, 1), 0)
    shift = jnp.zeros((n_exp, 1), I32)
    for e in range(n_exp):
        shift = jnp.where(sub_e == e, meta_ref[0, 0, 1 + e], shift)
    target = jnp.where(pos >= 0, pos + shift, -1)
    acc[...] = jnp.zeros_like(acc)
    lax.fori_loop(0, total, wait, 0)

    kc_rows = 2 * LANES
    chunks_per_kc = kc_rows // SLAB_CHUNK

    def kbody(kc, carry):
        base = pl.multiple_of(kc * kc_rows, kc_rows)
        rows = slab[buf, pl.ds(base, kc_rows), :]
        slab_row = _iota((kc_rows, tm), 0) + base
        place_t = target[0:1, :] == slab_row
        for e in range(1, n_exp):
            place_t = place_t | (target[e:e + 1, :] == slab_row)
        acc[...] += _dot_tn(place_t.astype(F32).astype(BF16), rows)
        return carry

    lax.fori_loop(0, (total + chunks_per_kc - 1) // chunks_per_kc, kbody, 0)
    x_new = x_ref[...] + mod_ref[0, 5:6, :] * acc[...]
    if post is None:
        out_ref[...] = x_new
        return
    normed = x_new * lax.rsqrt(jnp.mean(x_new * x_new, axis=-1, keepdims=True) + NORM_EPS) * gain_ref[...]
    if post == "final":
        out_ref[...] = normed
    else:
        out_ref[...] = x_new
        h_ref[...] = (normed * (1.0 + modn_ref[0, 1:2, :]) + modn_ref[0, 0:1, :]).astype(h_ref.dtype)


def _combine(x, mod, pos_t, meta, ye, t, max_chunks, post=None):
    n, d = x.shape
    tm = LANES
    n_exp = N_EXPERTS
    ntiles = n // tm
    kc_rows = 2 * LANES
    slab_rows = -(-(max_chunks * SLAB_CHUNK) // kc_rows) * kc_rows
    rows = pl.BlockSpec((tm, d), lambda i: (i, 0))
    in_specs = [pl.BlockSpec((1, 1, meta.shape[2]), lambda i: (i, 0, 0), memory_space=pltpu.SMEM),
                pl.BlockSpec((1, 1, meta.shape[2]), lambda i: (jnp.minimum(i + 1, ntiles - 1), 0, 0),
                             memory_space=pltpu.SMEM),
                rows, _mod_spec(mod, tm, t), pl.BlockSpec((n_exp, tm), lambda i: (0, i)),
                pl.BlockSpec(memory_space=pl.ANY)]
    args = [meta, meta, x, mod, pos_t, ye]
    out_specs, out_shape = rows, jax.ShapeDtypeStruct((n, d), F32)
    mode = None if post is None else post[0]
    if mode is not None:
        in_specs.append(pl.BlockSpec((1, d), lambda i: (0, 0)))
        args.append(post[1].reshape(1, d))
    if mode == "next":
        in_specs.append(_mod_spec(post[2], tm, t))
        args.append(post[2])
        out_specs, out_shape = [rows, rows], [out_shape, jax.ShapeDtypeStruct((n, d), post[3])]
    return pl.pallas_call(
        functools.partial(_combine_kernel, n_exp, mode), grid=(ntiles,),
        in_specs=in_specs, out_specs=out_specs, out_shape=out_shape,
        scratch_shapes=[pltpu.VMEM((2, slab_rows, d), BF16), pltpu.VMEM((tm, d), F32),
                        pltpu.SemaphoreType.DMA((2,))],
        compiler_params=_params("arbitrary"), name="moe_combine")(*args)


def _moe(x, hffn, aff, mod, t, wg, wu, wd, layer, post):
    n, d = x.shape
    n_exp = N_EXPERTS
    cap = (EC_CAPACITY_FACTOR * n) // n_exp
    npad = ROUTE_BLOCKS * LANES
    assert n <= npad and n % LANES == 0 and cap % SLAB_CHUNK == 0
    aff_t = jnp.pad(aff.T, ((0, 0), (0, npad - n)), constant_values=-1.0)
    idx, gate, pos, tot, off = _route(aff_t.reshape(n_exp, ROUTE_BLOCKS, LANES), cap)
    ye = _experts(hffn, idx, gate, wg, wu, wd, layer, cap)
    ntiles = n // LANES
    span = min(LANES + PACKED_ROWS + SLAB_CHUNK - (LANES + PACKED_ROWS) % SLAB_CHUNK, cap)
    per_expert = span // SLAB_CHUNK
    max_chunks = n_exp * per_expert
    start = off[:, 0, :ntiles].T.astype(I32)
    cnt = tot[:, 0, :ntiles].T.astype(I32)
    start_al = jnp.minimum((start // PACKED_ROWS) * PACKED_ROWS, cap - span)
    nch = jnp.where(cnt > 0, (start - start_al + cnt + SLAB_CHUNK - 1) // SLAB_CHUNK, 0)
    cend = jnp.cumsum(nch, axis=1)
    cstart = cend - nch
    slots = jnp.arange(max_chunks, dtype=I32)
    owns = (cstart[:, None, :] <= slots[None, :, None]) & (slots[None, :, None] < cend[:, None, :])
    base = start_al + jnp.arange(n_exp, dtype=I32)[None, :] * cap
    src = jnp.sum(jnp.where(owns, (base - cstart * SLAB_CHUNK)[:, None, :], 0), axis=2) + jnp.where(
        jnp.any(owns, axis=2), slots[None, :] * SLAB_CHUNK, 0)
    meta = jnp.concatenate([cend[:, -1:], cstart * SLAB_CHUNK - start_al, src], axis=1)
    return _combine(x, mod, pos.reshape(n_exp, npad), meta.reshape(ntiles, 1, 1 + n_exp + max_chunks), ye, t,
                    max_chunks, post)


def _rope_tables(t):
    n_rows = t // GRID_W
    axis_dim = A_HD // 2
    row = jnp.repeat(jnp.arange(n_rows, dtype=F32), GRID_W)
    col = jnp.tile(jnp.arange(GRID_W, dtype=F32), n_rows)
    inv_freq = ROPE_THETA ** (-jnp.arange(0, axis_dim, 2, dtype=F32) / axis_dim)
    ang = jnp.concatenate([row[:, None] * inv_freq, col[:, None] * inv_freq], axis=-1)
    cos, sin = jnp.cos(ang), jnp.sin(ang)
    return jnp.concatenate([cos, cos], axis=-1), jnp.concatenate([-sin, sin], axis=-1)


def _pad_lora(w1, w2):
    r = w1.shape[2]
    rp = -(-r // LANES) * LANES
    w1p = jnp.pad(w1, ((0, 0), (0, 0), (0, rp - r)))
    w2p = jnp.pad(w2, ((0, 0), (0, rp - r), (0, 0)))
    return jnp.concatenate([w1p[0], w1p[1]], axis=1).astype(BF16), w2p.astype(BF16)


def _pair_state(s):
    b, _, heads, hd, _ = s.shape
    sp = s.reshape(b, 2, heads // 2, 2, hd, hd)
    out = jnp.zeros((b, 2, heads // 2, 2 * hd, 2 * hd), F32)
    out = out.at[:, :, :, :hd, :hd].set(sp[:, :, :, 0])
    return out.at[:, :, :, hd:, hd:].set(sp[:, :, :, 1])


def _unpair_state(sp):
    b, _, npairs, _, _ = sp.shape
    hd = R_HD
    s = jnp.stack([sp[:, :, :, :hd, :hd], sp[:, :, :, hd:, hd:]], axis=3)
    return s.reshape(b, 2, 2 * npairs, hd, hd)


def _trunk(x, mod_all, b, t, latent_states, p):
    n, d = x.shape
    latent = latent_states is not None
    produced = []
    h_next = None
    for i in range(DEPTH):
        kind, j = i % N_MIXERS, i // N_MIXERS
        mod = mod_all[i]
        route_params = (p["norm2"][i], p["router"][i])
        if i == DEPTH - 1:
            post = ("final", p["final_norm"])
        elif (i + 1) % N_MIXERS == 1:
            post = ("next", p["norm1"][i + 1], mod_all[i + 1], BF16)
        elif (i + 1) % N_MIXERS == 2:
            post = ("next", p["norm1"][i + 1], mod_all[i + 1], F32)
        else:
            post = None
        if kind == 0:
            hb, gates = _norm_gate(x, p["norm1"][i], mod, t, p["m_gate_w"][j], p["m_gate_b"][j])
            proj = _mm(hb, p["m_qkvo"][j], 2048)
            res = _mlstm_scan(proj, gates, b, t, latent_states[i] if latent else None, not latent)
            if not latent:
                produced += [res[1], res[2], res[3][..., 0]]
            x, hffn, aff = _mlstm_out(res[0], proj, p["m_hnorm"][j], p["m_out"][j], x, mod, t, *route_params)
        elif kind == 1:
            assert h_next is not None and h_next.dtype == BF16
            q, k, v = _attn_qkv(h_next, p["a_qkv"][j], p["a_qnorm"][j], p["a_knorm"][j], t,
                                _rope_tables(t) if latent else None)
            cache = None
            if latent:
                ck, cv = latent_states[i]
                kvd = A_KV_HEADS * A_HD
                cache = (ck.reshape(-1, kvd), cv.reshape(-1, kvd))
            else:
                produced += [k.reshape(b, t, A_KV_HEADS, A_HD), v.reshape(b, t, A_KV_HEADS, A_HD)]
            o = _attention(q, k, v, b, t, cache)
            x, hffn, aff = _resid_mm(o, p["a_out"][j], x, mod, t, *route_params)
        else:
            assert h_next is not None and h_next.dtype == F32
            h = h_next
            mix = p["r_mix"][j]
            rkv = _rwkv_rkv(h, mix[jnp.array([0, 2, 3])].reshape(3, 1, d), p["r_rkv"][j], t)
            lw, asig, g = _rwkv_lora(h, mix[jnp.array([1, 4, 5])], p["r_w1c"][j], p["r_a1c"][j], p["r_g1"][j],
                                     p["r_w2p"][j], p["r_a2p"][j], p["r_g2"][j],
                                     p["r_w0"][j].reshape(2, 1, d), p["r_a0"][j].reshape(2, 1, d), t)
            s0 = _pair_state(latent_states[i][0]) if latent else None
            res = _rwkv_scan(rkv, lw, asig, p["r_kk"][j], p["r_ka"][j], p["r_rk"][j].reshape(-1), b, t, s0,
                             not latent)
            if not latent:
                produced.append(_unpair_state(res[2]))
            x, hffn, aff = _rwkv_out(res[0], res[1], g, p["r_lnx_w"][j], p["r_lnx_b"][j], p["r_out"][j], x, mod, t,
                                     *route_params)
        res = _moe(x, hffn, aff, mod, t, p["exp_gate"], p["exp_up"], p["exp_down"], i, post)
        x, h_next = res if post is not None and post[0] == "next" else (res, None)
    return x, produced


def kernel(x_prompt, x_sample, state_l0_C, state_l0_n, state_l0_m, cache_l1_k, cache_l1_v, state_l2_S,
           state_l3_C, state_l3_n, state_l3_m, c, c_ctx, ada_w, ada_b, norm1, norm2, router, exp_gate,
           exp_up, exp_down, m_qkvo, m_gate_w, m_gate_b, m_hnorm, m_out, a_qkv, a_qnorm, a_knorm, a_out,
           r_mix, r_rkv, r_w0, r_w1, r_w2, r_a0, r_a1, r_a2, r_g1, r_g2, r_kk, r_ka, r_rk, r_lnx_w,
           r_lnx_b, r_out, final_norm):
    bc, tc, d = x_prompt.shape
    bl, tl, _ = x_sample.shape
    depth = ada_w.shape[0]
    assert bl + 1 <= SUBLANES

    cond8 = jnp.zeros((SUBLANES, d), F32).at[0].set(c_ctx).at[1:1 + bl].set(c)
    mod = _adaln(cond8, ada_w, ada_b).reshape(depth, SUBLANES, 6, d)
    mod_ctx = mod[:, 0:1]
    mod_lat = mod[:, 1:1 + bl]

    w1c, w2p, a1c, a2p = [], [], [], []
    for j in range(r_w1.shape[0]):
        w1, w2 = _pad_lora(r_w1[j], r_w2[j])
        a1, a2 = _pad_lora(r_a1[j], r_a2[j])
        w1c.append(w1), w2p.append(w2), a1c.append(a1), a2p.append(a2)
    p = dict(norm1=norm1, norm2=norm2, router=router, final_norm=final_norm,
             exp_gate=exp_gate.astype(BF16), exp_up=exp_up.astype(BF16), exp_down=exp_down.astype(BF16),
             m_qkvo=m_qkvo.astype(BF16), m_gate_w=m_gate_w, m_gate_b=m_gate_b, m_hnorm=m_hnorm,
             m_out=m_out.astype(BF16), a_qkv=a_qkv.astype(BF16), a_qnorm=a_qnorm, a_knorm=a_knorm,
             a_out=a_out.astype(BF16), r_mix=r_mix, r_rkv=r_rkv.astype(BF16), r_w0=r_w0, r_a0=r_a0,
             r_w1c=w1c, r_w2p=w2p, r_a1c=a1c, r_a2p=a2p, r_g1=r_g1.astype(BF16), r_g2=r_g2.astype(BF16),
             r_kk=r_kk, r_ka=r_ka, r_rk=r_rk, r_lnx_w=r_lnx_w, r_lnx_b=r_lnx_b, r_out=r_out.astype(BF16))

    y_prompt, new_state = _trunk(x_prompt.reshape(bc * tc, d), mod_ctx, bc, tc, None, p)
    lat_states = [(state_l0_C, state_l0_n, state_l0_m), (cache_l1_k, cache_l1_v), (state_l2_S,),
                  (state_l3_C, state_l3_n, state_l3_m)]
    y_sample, _ = _trunk(x_sample.reshape(bl * tl, d), mod_lat, bl, tl, lat_states, p)
    return (y_prompt.reshape(bc, tc, d), y_sample.reshape(bl, tl, d), *new_state)
```

```python
import functools

import jax
import jax.numpy as jnp
from jax import lax
from jax.experimental import pallas as pl
from jax.experimental.pallas import tpu as pltpu

F32 = jnp.float32
BF16 = jnp.bfloat16
I32 = jnp.int32

D_MODEL = 2048
BATCH = 32
SEQ = 256
DEPTH = 4
DEC_BATCH = 4
DEC_SEQ = 4096
PAST_LEN = 256
GRID_W = 64
N_MIXERS = 3
NORM_EPS = 1e-6
M_HEADS = 8
M_DK = D_MODEL // 2 // M_HEADS
M_DV = D_MODEL // M_HEADS
M_GATE_CAP = 15.0
A_HEADS = 16
A_KV_HEADS = 4
A_HD = 128
ROPE_THETA = 10000.0
R_HD = 64
R_LN_EPS = 64e-5
N_EXPERTS = 16
EC_CAPACITY_FACTOR = 2
D_EXPERT = 1024

LANES = 128
SUBLANES = 8
PACKED_ROWS = 16
VMEM_LIMIT_BYTES = 56 * 2 ** 20

MLSTM_CHUNK = 256
ATTN_Q_ROWS = 128
ATTN_SUB_ROWS = 64
RWKV_CHUNK = 64
RWKV_PAIRS_PER_GROUP = 16
RWKV_SOLVE_PASSES = 1
ROUTE_BLOCKS = 128
SLAB_CHUNK = 32


def _params(*sem):
    return pltpu.CompilerParams(dimension_semantics=sem, vmem_limit_bytes=VMEM_LIMIT_BYTES)


def _dot(a, b):
    return jnp.dot(a, b, preferred_element_type=F32)


def _dot_nt(a, b):
    return lax.dot_general(a, b, (((1,), (1,)), ((), ())), preferred_element_type=F32)


def _dot_tn(a, b):
    return lax.dot_general(a, b, (((0,), (0,)), ((), ())), preferred_element_type=F32)


def _split2(x):
    hi = x.astype(BF16)
    lo = (x - hi.astype(F32)).astype(BF16)
    return hi, lo


def _split3(x):
    h1 = x.astype(BF16)
    r1 = x - h1.astype(F32)
    h2 = r1.astype(BF16)
    h3 = (r1 - h2.astype(F32)).astype(BF16)
    return h1, h2, h3


def _dot3(a, w):
    ah, al = _split2(a)
    wh, wl = _split2(w)
    return _dot(ah, wh) + _dot(al, wh) + _dot(ah, wl)


_NN = (((1,), (0,)), ((), ()))
_NT = (((1,), (1,)), ((), ()))
_TN = (((0,), (0,)), ((), ()))


def _mm_passes(a, b, dims, passes):
    def f(x, y):
        return lax.dot_general(x, y, dims, preferred_element_type=F32)
    if passes == 1:
        return f(a.astype(BF16), b.astype(BF16))
    ah, al = _split2(a)
    bh, bl = _split2(b)
    return f(ah, bh) + f(al, bh) + f(ah, bl)


def _iota(shape, dim):
    return lax.broadcasted_iota(I32, shape, dim)


def _log_sigmoid(z):
    return jnp.minimum(z, 0.0) - jnp.log(1.0 + jnp.exp(-jnp.abs(z)))


def _adaln_kernel(c_ref, w_ref, b_ref, o_ref):
    c = c_ref[...]
    s = c * jax.nn.sigmoid(c)
    hi, lo = _split2(s)
    w = w_ref[0].astype(BF16)
    o_ref[0] = _dot(hi, w) + _dot(lo, w) + b_ref[0]


def _adaln(cond8, ada_w, ada_b):
    depth, d, d6 = ada_w.shape
    tn = min(d, 1024)
    return pl.pallas_call(
        _adaln_kernel,
        grid=(depth, d6 // tn),
        in_specs=[pl.BlockSpec((SUBLANES, d), lambda l, j: (0, 0)),
                  pl.BlockSpec((1, d, tn), lambda l, j: (l, 0, j)),
                  pl.BlockSpec((1, 1, tn), lambda l, j: (l, 0, j))],
        out_specs=pl.BlockSpec((1, SUBLANES, tn), lambda l, j: (l, 0, j)),
        out_shape=jax.ShapeDtypeStruct((depth, SUBLANES, d6), F32),
        compiler_params=_params("arbitrary", "arbitrary"),
        name="adaln",
    )(cond8, ada_w, ada_b.reshape(depth, 1, d6))


def _normed(x_ref, g_ref):
    x = x_ref[...]
    return x * lax.rsqrt(jnp.mean(x * x, axis=-1, keepdims=True) + NORM_EPS) * g_ref[...]


def _norm_gate_kernel(x_ref, g_ref, mod_ref, w_ref, b_ref, o_ref, gate_ref):
    h = _normed(x_ref, g_ref) * (1.0 + mod_ref[0, 1:2, :]) + mod_ref[0, 0:1, :]
    o_ref[...] = h.astype(o_ref.dtype)
    z = _dot3(h, w_ref[...]) + b_ref[...]
    z = M_GATE_CAP * jnp.tanh(z / M_GATE_CAP)
    is_forget = ((_iota(z.shape, 1) // M_HEADS) % 2) == 1
    gate_ref[...] = jnp.where(is_forget, _log_sigmoid(z), z)


def _mixer_residual_and_route(x_ref, mod_ref, mix, g2_ref, wr_ref, out_ref, hffn_ref, aff_ref):
    x = x_ref[...] + mod_ref[0, 2:3, :] * mix
    out_ref[...] = x
    h = x * lax.rsqrt(jnp.mean(x * x, axis=-1, keepdims=True) + NORM_EPS) * g2_ref[...]
    h = h * (1.0 + mod_ref[0, 4:5, :]) + mod_ref[0, 3:4, :]
    hffn_ref[...] = h
    z = _dot3(h, wr_ref[...])
    e = jnp.exp(z - jnp.max(z, axis=-1, keepdims=True))
    aff_ref[...] = e / jnp.sum(e, axis=-1, keepdims=True)


def _route_tail_specs(n, d, tm, norm2, w_router):
    ns = w_router.shape[1]
    in_specs = [pl.BlockSpec((1, d), lambda i: (0, 0)), pl.BlockSpec((d, ns), lambda i: (0, 0))]
    out_specs = [pl.BlockSpec((tm, d), lambda i: (i, 0)), pl.BlockSpec((tm, d), lambda i: (i, 0)),
                 pl.BlockSpec((tm, ns), lambda i: (i, 0))]
    out_shape = [jax.ShapeDtypeStruct((n, d), F32), jax.ShapeDtypeStruct((n, d), F32),
                 jax.ShapeDtypeStruct((n, ns), F32)]
    return in_specs, [norm2.reshape(1, d), w_router], out_specs, out_shape


def _row_tile(n, t, want):
    tm = min(want, t)
    assert t % tm == 0 and n % tm == 0
    return tm


def _mod_spec(mod, tm, t):
    d = mod.shape[-1]
    if mod.shape[0] == 1:
        return pl.BlockSpec((1, 6, d), lambda i: (0, 0, 0))
    return pl.BlockSpec((1, 6, d), lambda i: ((i * tm) // t, 0, 0))


def _norm_gate(x, gain, mod, t, w, b):
    n, d = x.shape
    tm = _row_tile(n, t, 256)
    ns = w.shape[1]
    xs = pl.BlockSpec((tm, d), lambda i: (i, 0))
    return pl.pallas_call(
        _norm_gate_kernel, grid=(n // tm,),
        in_specs=[xs, pl.BlockSpec((1, d), lambda i: (0, 0)), _mod_spec(mod, tm, t),
                  pl.BlockSpec((d, ns), lambda i: (0, 0)), pl.BlockSpec((1, ns), lambda i: (0, 0))],
        out_specs=[xs, pl.BlockSpec((tm, ns), lambda i: (i, 0))],
        out_shape=[jax.ShapeDtypeStruct((n, d), BF16), jax.ShapeDtypeStruct((n, ns), F32)],
        compiler_params=_params("arbitrary"), name="norm_gate")(x, gain.reshape(1, d), mod, w, b.reshape(1, ns))


def _mm_kernel(x_ref, w_ref, o_ref):
    o_ref[...] = _dot(x_ref[...], w_ref[...])


def _mm(x, w, tn):
    n, k = x.shape
    m = w.shape[1]
    tm = min(512, n)
    tn = min(tn, m)
    return pl.pallas_call(
        _mm_kernel, grid=(m // tn, n // tm),
        in_specs=[pl.BlockSpec((tm, k), lambda j, i: (i, 0)), pl.BlockSpec((k, tn), lambda j, i: (0, j))],
        out_specs=pl.BlockSpec((tm, tn), lambda j, i: (i, j)),
        out_shape=jax.ShapeDtypeStruct((n, m), F32),
        compiler_params=_params("arbitrary", "arbitrary"), name="mm")(x, w)


def _mlstm_kernel(has_init, emit_state, *refs):
    q_ref, k_ref, v_ref, g_ref, gt_ref = refs[:5]
    pos = 5
    if has_init:
        c0_ref, n0_ref, m0_ref = refs[pos:pos + 3]
        pos += 3
    h_ref = refs[pos]
    pos += 1
    if emit_state:
        cout_ref, nout_ref, mout_ref = refs[pos:pos + 3]
        pos += 3
    c_scr, n_scr, m_scr = refs[pos:pos + 3]

    d = pl.program_id(1)
    c = pl.program_id(2)
    nc = pl.num_programs(2)
    fwd = d == 0
    heads = M_HEADS
    length = q_ref.shape[0]

    @pl.when(c == 0)
    def _():
        if has_init:
            c_scr[...] = c0_ref[0, 0]
            n_scr[...] = n0_ref[0, 0]
            m_scr[...] = m0_ref[0, 0]
        else:
            c_scr[...] = jnp.zeros_like(c_scr)
            n_scr[...] = jnp.zeros_like(n_scr)
            m_scr[...] = jnp.zeros_like(m_scr)

    sgn = jnp.where(fwd, 1, -1)
    row = _iota((length, length), 0)
    col = _iota((length, length), 1)
    mask = ((row - col) * sgn) >= 0
    mask_t = ((col - row) * sgn) >= 0
    scale = M_DK ** -0.5
    g = g_ref[...]
    gt = gt_ref[0]

    tri = mask.astype(F32).astype(BF16)
    tri_t = mask_t.astype(F32).astype(BF16)
    g1, g2, g3 = _split3(g)
    cum_cols = _dot(tri, g1) + _dot(tri, g2) + _dot(tri, g3)
    t1, t2, t3 = _split3(gt)
    cum_rows = _dot(t1, tri_t) + _dot(t2, tri_t) + _dot(t3, tri_t)

    def pick_col(x, kind, h):
        i, j = kind * heads + h, (kind + 2) * heads + h
        return jnp.where(fwd, x[:, i:i + 1], x[:, j:j + 1])

    def pick_row(x, kind, h):
        i, j = kind * heads + h, (kind + 2) * heads + h
        return jnp.where(fwd, x[i:i + 1, :], x[j:j + 1, :])

    def each(fn, *lists):
        return [fn(*args) for args in zip(*lists)]

    hs = list(range(heads))
    ig_row = [pick_row(gt, 0, h) for h in hs]
    ig_col = [pick_col(g, 0, h) for h in hs]
    b_col = [pick_col(cum_cols, 1, h) for h in hs]
    b_row = [pick_row(cum_rows, 1, h) for h in hs]
    m_prev = [m_scr[h:h + 1, 0:1] for h in hs]
    c_prev = [c_scr[h] for h in hs]
    n_prev = [n_scr[h:h + 1, :] for h in hs]
    qh = [q_ref[:, h * M_DK:(h + 1) * M_DK] for h in hs]
    kh = [k_ref[:, h * M_DK:(h + 1) * M_DK] * scale for h in hs]
    qb = each(lambda x: x.astype(BF16), qh)
    kb = each(lambda x: x.astype(BF16), kh)
    vb = [v_ref[:, h * M_DV:(h + 1) * M_DV].astype(BF16) for h in hs]
    qk = each(_dot_nt, qb, kb)
    qc = each(lambda q_, c_: _dot(q_, c_.astype(BF16)), qb, c_prev)
    dmat = each(lambda bc, br, ir: jnp.where(mask, bc - br + ir, -jnp.inf), b_col, b_row, ig_row)
    a = each(lambda bc, mp: bc + mp, b_col, m_prev)
    m_t = each(lambda a_, d_: jnp.maximum(a_, jnp.max(d_, axis=1, keepdims=True)), a, dmat)
    w_inter = each(lambda a_, m_: jnp.exp(a_ - m_), a, m_t)
    s = each(lambda x_, d_, m_: x_ * jnp.exp(d_ - m_), qk, dmat, m_t)
    sv = each(lambda s_, v_: _dot(s_.astype(BF16), v_), s, vb)
    num = each(lambda w_, x_, y_: w_ * x_ + y_, w_inter, qc, sv)
    den = each(lambda w_, q_, n_, s_: w_ * jnp.sum(q_ * n_, axis=1, keepdims=True)
               + jnp.sum(s_, axis=1, keepdims=True), w_inter, qh, n_prev, s)
    for h in hs:
        h_ref[0, :, h * M_DV:(h + 1) * M_DV] = num[h] / jnp.maximum(jnp.abs(den[h]), jnp.exp(-m_t[h]))

    b_last = each(lambda bc: jnp.where(fwd, bc[length - 1:length, :], bc[0:1, :]), b_col)
    g_col = each(lambda bl, bc, ic: bl - bc + ic, b_last, b_col, ig_col)
    m_new = each(lambda bl, mp, gc: jnp.maximum(bl + mp, jnp.max(gc, axis=0, keepdims=True)), b_last, m_prev, g_col)
    decay = each(lambda bl, mp, mn: jnp.exp(bl + mp - mn), b_last, m_prev, m_new)
    wk = each(lambda gc, mn, k_: jnp.exp(gc - mn) * k_, g_col, m_new, kh)
    kv = each(lambda w_, v_: _dot_tn(w_.astype(BF16), v_), wk, vb)
    for h in hs:
        c_scr[h] = decay[h] * c_prev[h] + kv[h]
        n_scr[h:h + 1, :] = decay[h] * n_prev[h] + jnp.sum(wk[h], axis=0, keepdims=True)
        m_scr[h:h + 1, :] = jnp.broadcast_to(m_new[h], (1, LANES))

    if emit_state:
        @pl.when(c == nc - 1)
        def _():
            cout_ref[0, 0] = c_scr[...]
            nout_ref[0, 0] = n_scr[...]
            mout_ref[0, 0] = m_scr[...]


def _mlstm_scan(proj, gates, b, t, init, emit_state):
    n = proj.shape[0]
    heads, dk, dv = M_HEADS, M_DK, M_DV
    qk, mv = heads * dk, heads * dv
    length = min(MLSTM_CHUNK, t)
    nc = t // length
    gates_t = gates.reshape(n // length, length, 4 * heads).transpose(0, 2, 1)

    def tile(bi, di, ci):
        return bi * nc + ci + di * (nc - 1 - 2 * ci)

    in_specs = [pl.BlockSpec((length, qk), lambda bi, di, ci: (tile(bi, di, ci), 0)),
                pl.BlockSpec((length, qk), lambda bi, di, ci: (tile(bi, di, ci), 1)),
                pl.BlockSpec((length, mv), lambda bi, di, ci: (tile(bi, di, ci), 1)),
                pl.BlockSpec((length, 4 * heads), lambda bi, di, ci: (tile(bi, di, ci), 0)),
                pl.BlockSpec((1, 4 * heads, length), lambda bi, di, ci: (tile(bi, di, ci), 0, 0))]
    args = [proj, proj, proj, gates, gates_t]
    state_specs = [pl.BlockSpec((1, 1, heads, dk, dv), lambda bi, di, ci: (bi, di, 0, 0, 0)),
                   pl.BlockSpec((1, 1, heads, dk), lambda bi, di, ci: (bi, di, 0, 0)),
                   pl.BlockSpec((1, 1, heads, LANES), lambda bi, di, ci: (bi, di, 0, 0))]
    if init is not None:
        c0, n0, m0 = init
        in_specs += state_specs
        args += [c0, n0, jnp.broadcast_to(m0[..., None], m0.shape + (LANES,))]
    out_specs = [pl.BlockSpec((1, length, mv), lambda bi, di, ci: (di, tile(bi, di, ci), 0))]
    out_shape = [jax.ShapeDtypeStruct((2, n, mv), F32)]
    if emit_state:
        out_specs += state_specs
        out_shape += [jax.ShapeDtypeStruct((b, 2, heads, dk, dv), F32),
                      jax.ShapeDtypeStruct((b, 2, heads, dk), F32),
                      jax.ShapeDtypeStruct((b, 2, heads, LANES), F32)]
    return pl.pallas_call(
        functools.partial(_mlstm_kernel, init is not None, emit_state),
        grid=(b, 2, nc), in_specs=in_specs, out_specs=out_specs, out_shape=out_shape,
        scratch_shapes=[pltpu.VMEM((heads, dk, dv), F32), pltpu.VMEM((heads, dk), F32),
                        pltpu.VMEM((heads, LANES), F32)],
        compiler_params=_params("arbitrary", "arbitrary", "arbitrary"), name="mlstm_scan")(*args)


def _mlstm_out_kernel(hs_ref, o_ref, gain_ref, w_ref, x_ref, mod_ref, g2_ref, wr_ref, out_ref, hffn_ref, aff_ref):
    hs = hs_ref[0] + hs_ref[1]
    parts = []
    for h in range(M_HEADS):
        seg = hs[:, h * M_DV:(h + 1) * M_DV]
        parts.append(seg * lax.rsqrt(jnp.mean(seg * seg, axis=-1, keepdims=True) + NORM_EPS))
    hn = jnp.concatenate(parts, axis=1) * gain_ref[...]
    lhs = (jax.nn.sigmoid(o_ref[...]) * hn).astype(BF16)
    _mixer_residual_and_route(x_ref, mod_ref, _dot(lhs, w_ref[...]), g2_ref, wr_ref, out_ref, hffn_ref, aff_ref)


def _mlstm_out(hs2, proj, gain, w_out, x, mod, t, norm2, w_router):
    n, d = x.shape
    mv = M_HEADS * M_DV
    tm = _row_tile(n, t, 256)
    tail_in, tail_args, out_specs, out_shape = _route_tail_specs(n, d, tm, norm2, w_router)
    return pl.pallas_call(
        _mlstm_out_kernel, grid=(n // tm,),
        in_specs=[pl.BlockSpec((2, tm, mv), lambda i: (0, i, 0)),
                  pl.BlockSpec((tm, mv), lambda i: (i, 2)),
                  pl.BlockSpec((1, mv), lambda i: (0, 0)),
                  pl.BlockSpec((mv, d), lambda i: (0, 0)),
                  pl.BlockSpec((tm, d), lambda i: (i, 0)),
                  _mod_spec(mod, tm, t)] + tail_in,
        out_specs=out_specs, out_shape=out_shape,
        compiler_params=_params("arbitrary"), name="mlstm_out")(
            hs2, proj, gain.reshape(1, mv), w_out, x, mod, *tail_args)


def _attn_qkv_kernel(use_rope, *refs):
    if use_rope:
        x_ref, w_ref, qg_ref, kg_ref, cos_ref, sin_ref, q_ref, k_ref, v_ref = refs
    else:
        x_ref, w_ref, qg_ref, kg_ref, q_ref, k_ref, v_ref = refs
    acc = _dot(x_ref[...], w_ref[...])
    qd, kvd = A_HEADS * A_HD, A_KV_HEADS * A_HD

    def head_norm(xh, gain):
        xh = xh * lax.rsqrt(jnp.mean(xh * xh, axis=-1, keepdims=True) + NORM_EPS) * gain
        if use_rope:
            xh = xh * cos_ref[...] + pltpu.roll(xh, A_HD // 2, axis=1) * sin_ref[...]
        return xh

    for h in range(A_HEADS):
        qh = head_norm(acc[:, h * A_HD:(h + 1) * A_HD], qg_ref[...])
        q_ref[:, h * A_HD:(h + 1) * A_HD] = (qh * (A_HD ** -0.5)).astype(BF16)
    for h in range(A_KV_HEADS):
        k_ref[:, h * A_HD:(h + 1) * A_HD] = head_norm(acc[:, qd + h * A_HD:qd + (h + 1) * A_HD], kg_ref[...])
    v_ref[...] = acc[:, qd + kvd:]


def _attn_qkv(hb, w, qgain, kgain, t, rope):
    n, d = hb.shape
    qd, kvd = A_HEADS * A_HD, A_KV_HEADS * A_HD
    tm = _row_tile(n, t, 256)
    in_specs = [pl.BlockSpec((tm, d), lambda i: (i, 0)),
                pl.BlockSpec((d, qd + 2 * kvd), lambda i: (0, 0)),
                pl.BlockSpec((1, A_HD), lambda i: (0, 0)),
                pl.BlockSpec((1, A_HD), lambda i: (0, 0))]
    args = [hb, w, qgain.reshape(1, A_HD), kgain.reshape(1, A_HD)]
    if rope is not None:
        in_specs += [pl.BlockSpec((tm, A_HD), lambda i: (i % (t // tm), 0))] * 2
        args += list(rope)
    return pl.pallas_call(
        functools.partial(_attn_qkv_kernel, rope is not None), grid=(n // tm,),
        in_specs=in_specs,
        out_specs=[pl.BlockSpec((tm, qd), lambda i: (i, 0)),
                   pl.BlockSpec((tm, kvd), lambda i: (i, 0)),
                   pl.BlockSpec((tm, kvd), lambda i: (i, 0))],
        out_shape=[jax.ShapeDtypeStruct((n, qd), BF16), jax.ShapeDtypeStruct((n, kvd), F32),
                   jax.ShapeDtypeStruct((n, kvd), F32)],
        compiler_params=_params("arbitrary"), name="attn_qkv")(*args)


def _attn_kernel(has_cache, *refs):
    if has_cache:
        q_ref, kn_ref, vn_ref, kc_ref, vc_ref, o_ref, k_scr, v_scr = refs
    else:
        q_ref, kn_ref, vn_ref, o_ref, k_scr, v_scr = refs
    group = A_HEADS // A_KV_HEADS
    tq = q_ref.shape[0]
    t_new = kn_ref.shape[0]
    past = k_scr.shape[0] - t_new

    @pl.when(pl.program_id(2) == 0)
    def _():
        if has_cache:
            k_scr[0:past, :] = kc_ref[...].astype(BF16)
            v_scr[0:past, :] = vc_ref[...].astype(BF16)
        k_scr[past:, :] = kn_ref[...].astype(BF16)
        v_scr[past:, :] = vn_ref[...].astype(BF16)

    sub = min(ATTN_SUB_ROWS, tq)
    starts = list(range(0, tq, sub))
    q4 = [jnp.concatenate([q_ref[r0:r0 + sub, g * A_HD:(g + 1) * A_HD] for g in range(group)], axis=0)
          for r0 in starts]
    s = [_dot_nt(x, k_scr[...]) for x in q4]
    p = [jnp.exp(x - jnp.max(x, axis=-1, keepdims=True)) for x in s]
    pv = [_dot(x.astype(BF16), v_scr[...]) for x in p]
    o = [x / jnp.sum(y, axis=-1, keepdims=True) for x, y in zip(pv, p)]
    for r0, x in zip(starts, o):
        for g in range(group):
            o_ref[r0:r0 + sub, g * A_HD:(g + 1) * A_HD] = x[g * sub:(g + 1) * sub, :].astype(BF16)


def _attention(q, k, v, b, t, cache):
    n = q.shape[0]
    group = A_HEADS // A_KV_HEADS
    tq = min(ATTN_Q_ROWS, t)
    nq = t // tq
    past = 0 if cache is None else cache[0].shape[0] // b
    in_specs = [pl.BlockSpec((tq, group * A_HD), lambda bi, hi, qi: (bi * nq + qi, hi)),
                pl.BlockSpec((t, A_HD), lambda bi, hi, qi: (bi, hi)),
                pl.BlockSpec((t, A_HD), lambda bi, hi, qi: (bi, hi))]
    args = [q, k, v]
    if cache is not None:
        in_specs += [pl.BlockSpec((past, A_HD), lambda bi, hi, qi: (bi, hi))] * 2
        args += list(cache)
    return pl.pallas_call(
        functools.partial(_attn_kernel, cache is not None), grid=(b, A_KV_HEADS, nq),
        in_specs=in_specs,
        out_specs=pl.BlockSpec((tq, group * A_HD), lambda bi, hi, qi: (bi * nq + qi, hi)),
        out_shape=jax.ShapeDtypeStruct((n, A_HEADS * A_HD), BF16),
        scratch_shapes=[pltpu.VMEM((past + t, A_HD), BF16), pltpu.VMEM((past + t, A_HD), BF16)],
        compiler_params=_params("arbitrary", "arbitrary", "arbitrary"), name="attention")(*args)


def _resid_mm_kernel(lhs_ref, w_ref, x_ref, mod_ref, g2_ref, wr_ref, out_ref, hffn_ref, aff_ref):
    _mixer_residual_and_route(x_ref, mod_ref, _dot(lhs_ref[...], w_ref[...]), g2_ref, wr_ref, out_ref, hffn_ref,
                              aff_ref)


def _resid_mm(lhs, w, x, mod, t, norm2, w_router):
    n, d = x.shape
    k = lhs.shape[1]
    tm = _row_tile(n, t, 256)
    tail_in, tail_args, out_specs, out_shape = _route_tail_specs(n, d, tm, norm2, w_router)
    return pl.pallas_call(
        _resid_mm_kernel, grid=(n // tm,),
        in_specs=[pl.BlockSpec((tm, k), lambda i: (i, 0)),
                  pl.BlockSpec((k, d), lambda i: (0, 0)),
                  pl.BlockSpec((tm, d), lambda i: (i, 0)),
                  _mod_spec(mod, tm, t)] + tail_in,
        out_specs=out_specs, out_shape=out_shape,
        compiler_params=_params("arbitrary"), name="resid_mm")(lhs, w, x, mod, *tail_args)


def _token_shift_delta(h, hp_ref, hn_ref, i, tiles_per_seq):
    tm = h.shape[0]
    rowi = _iota((tm, 1), 0)
    ti = i % tiles_per_seq
    prev_row = jnp.where(ti == 0, 0.0, hp_ref[SUBLANES - 1:SUBLANES, :])
    next_row = jnp.where(ti == tiles_per_seq - 1, 0.0, hn_ref[0:1, :])
    h_prev = jnp.where(rowi == 0, prev_row, pltpu.roll(h, 1, axis=0))
    h_next = jnp.where(rowi == tm - 1, next_row, pltpu.roll(h, tm - 1, axis=0))
    return 0.5 * (h_prev + h_next) - h


def _shift_specs(tm, d, n, grid_rank):
    nb8 = n // SUBLANES
    per = tm // SUBLANES
    if grid_rank == 1:
        return [pl.BlockSpec((tm, d), lambda i: (i, 0)),
                pl.BlockSpec((SUBLANES, d), lambda i: (jnp.maximum(i * per - 1, 0), 0)),
                pl.BlockSpec((SUBLANES, d), lambda i: (jnp.minimum((i + 1) * per, nb8 - 1), 0))]
    return [pl.BlockSpec((tm, d), lambda j, i: (i, 0)),
            pl.BlockSpec((SUBLANES, d), lambda j, i: (jnp.maximum(i * per - 1, 0), 0)),
            pl.BlockSpec((SUBLANES, d), lambda j, i: (jnp.minimum((i + 1) * per, nb8 - 1), 0))]


def _rwkv_rkv_kernel(tiles_per_seq, h_ref, hp_ref, hn_ref, mix_ref, w_ref, o_ref):
    h = h_ref[...]
    xx = _token_shift_delta(h, hp_ref, hn_ref, pl.program_id(1), tiles_per_seq)
    o_ref[0] = _dot((h + xx * mix_ref[0]).astype(BF16), w_ref[0])


def _rwkv_rkv(h, mix3, w_rkv, t):
    n, d = h.shape
    tm = _row_tile(n, t, 256)
    return pl.pallas_call(
        functools.partial(_rwkv_rkv_kernel, t // tm), grid=(3, n // tm),
        in_specs=_shift_specs(tm, d, n, 2) + [pl.BlockSpec((1, 1, d), lambda j, i: (j, 0, 0)),
                                              pl.BlockSpec((1, d, d), lambda j, i: (j, 0, 0))],
        out_specs=pl.BlockSpec((1, tm, d), lambda j, i: (j, i, 0)),
        out_shape=jax.ShapeDtypeStruct((3, n, d), F32),
        compiler_params=_params("arbitrary", "arbitrary"), name="rwkv_rkv")(h, h, h, mix3, w_rkv)


def _rwkv_lora_kernel(tiles_per_seq, lora, h_ref, hp_ref, hn_ref, mix_ref, w1_ref, a1_ref, g1_ref,
                      w2_ref, a2_ref, g2_ref, w0_ref, a0_ref, lw_ref, asig_ref, g_ref):
    h = h_ref[...]
    xx = _token_shift_delta(h, hp_ref, hn_ref, pl.program_id(0), tiles_per_seq)
    xw = (h + xx * mix_ref[0:1, :]).astype(BF16)
    xa = (h + xx * mix_ref[1:2, :]).astype(BF16)
    xg = (h + xx * mix_ref[2:3, :]).astype(BF16)
    tw = jnp.tanh(_dot(xw, w1_ref[...])).astype(BF16)
    ta = _dot(xa, a1_ref[...]).astype(BF16)
    tg = jax.nn.sigmoid(_dot(xg, g1_ref[...])).astype(BF16)
    g_ref[...] = _dot(tg, g2_ref[...])
    for dr in range(2):
        z = -(w0_ref[dr] + _dot(tw[:, dr * lora:(dr + 1) * lora], w2_ref[dr]))
        softplus = jnp.maximum(z, 0.0) + jnp.log(1.0 + jnp.exp(-jnp.abs(z)))
        lw_ref[dr] = -jnp.exp(-softplus - 0.5)
        asig_ref[dr] = jax.nn.sigmoid(a0_ref[dr] + _dot(ta[:, dr * lora:(dr + 1) * lora], a2_ref[dr]))


def _rwkv_lora(h, mix3, w1c, a1c, g1, w2p, a2p, g2, w0, a0, t):
    n, d = h.shape
    tm = _row_tile(n, t, 256)
    lora = w2p.shape[1]
    gl = g1.shape[1]
    full2 = lambda shape: pl.BlockSpec(shape, lambda i: (0, 0))
    full3 = lambda shape: pl.BlockSpec(shape, lambda i: (0, 0, 0))
    return pl.pallas_call(
        functools.partial(_rwkv_lora_kernel, t // tm, lora), grid=(n // tm,),
        in_specs=_shift_specs(tm, d, n, 1) + [
            full2((3, d)), full2((d, 2 * lora)), full2((d, 2 * lora)), full2((d, gl)),
            full3((2, lora, d)), full3((2, lora, d)), full2((gl, d)), full3((2, 1, d)), full3((2, 1, d))],
        out_specs=[pl.BlockSpec((2, tm, d), lambda i: (0, i, 0)),
                   pl.BlockSpec((2, tm, d), lambda i: (0, i, 0)),
                   pl.BlockSpec((tm, d), lambda i: (i, 0))],
        out_shape=[jax.ShapeDtypeStruct((2, n, d), F32), jax.ShapeDtypeStruct((2, n, d), F32),
                   jax.ShapeDtypeStruct((n, d), F32)],
        compiler_params=_params("arbitrary"), name="rwkv_lora")(h, h, h, mix3, w1c, a1c, g1, w2p, a2p, g2, w0, a0)


def _rwkv_scan_kernel(has_init, emit_state, *refs):
    r_ref, k_ref, v_ref, lw_ref, a_ref, kk_ref, ka_ref, rk_ref = refs[:8]
    pos = 8
    if has_init:
        s0_ref = refs[pos]
        pos += 1
    y_ref, bonus_ref = refs[pos:pos + 2]
    pos += 2
    if emit_state:
        sout_ref = refs[pos]
        pos += 1
    s_scr, cum_scr = refs[pos:pos + 2]

    d = pl.program_id(1)
    c = pl.program_id(2)
    nc = pl.num_programs(2)
    fwd = d == 0
    sgn = jnp.where(fwd, 1, -1)
    length = r_ref.shape[1]
    npairs = s_scr.shape[0]
    hd = R_HD

    @pl.when(c == 0)
    def _():
        if has_init:
            s_scr[...] = s0_ref[0, 0]
        else:
            s_scr[...] = jnp.zeros_like(s_scr)

    ti = _iota((length, length), 0)
    si = _iota((length, length), 1)
    tri = (((ti - si) * sgn) >= 0).astype(BF16)
    l1, l2, l3 = _split3(lw_ref[0])
    cum_scr[...] = _dot(tri, l1) + _dot(tri, l2) + _dot(tri, l3)

    gl = 2 * length
    rr = _iota((gl, gl), 0)
    cc = _iota((gl, gl), 1)
    same = (rr // length) == (cc // length)
    e = ((rr % length) - (cc % length)) * sgn
    strict = same & (e > 0)
    incl = same & (e >= 0)
    eye = (rr == cc).astype(F32)
    lane = _iota((1, LANES), 1)
    m0 = (lane < hd).astype(F32)
    m1 = 1.0 - m0
    lr = _iota((LANES, LANES), 0)
    lc = _iota((LANES, LANES), 1)
    ones_blk = ((lr // hd) == (lc // hd)).astype(BF16)

    def segsum(x):
        hi, lo = _split2(x)
        return _dot(hi, ones_blk) + _dot(lo, ones_blk)

    def stack(x):
        return jnp.concatenate([x * m0, x * m1], axis=0)

    def smm(x, y, dims):
        return _mm_passes(x, y, dims, RWKV_SOLVE_PASSES)

    tr = rr % length
    tc = cc % length
    base = 4
    base_mask = (tr // base) == (tc // base)
    level_masks = []
    size = base
    while size < length:
        level_masks.append(((tr // (2 * size)) == (tc // (2 * size))) & ((tr // size) != (tc // size)))
        size *= 2

    def load_pair(p):
        ds = pl.ds(pl.multiple_of(p * LANES, LANES), LANES)
        return dict(ds=ds, p=p, r=r_ref[0, :, ds], k=k_ref[0, :, ds], v=v_ref[0, :, ds], lw=lw_ref[0, :, ds],
                    a=a_ref[0, :, ds], cum=cum_scr[:, ds], kkp=kk_ref[:, ds], kap=ka_ref[:, ds],
                    rkp=rk_ref[:, ds], st=s_scr[p])

    def each(fn, *lists):
        return [fn(*args) for args in zip(*lists)]

    def solve_group(xs):
        get = lambda name: [x[name] for x in xs]
        r, k, v, lw, a, cum, st = (get(n) for n in ("r", "k", "v", "lw", "a", "cum", "st"))
        kk = each(lambda k_, p_: k_ * p_, k, get("kkp"))
        ss = each(lambda q: segsum(q * q), kk)
        kk = each(lambda q, s_: q * lax.rsqrt(jnp.maximum(s_, 1e-24)), kk, ss)
        kd = each(lambda k_, a_, p_: k_ * (1.0 + (a_ - 1.0) * p_), k, a, get("kap"))
        bb = each(lambda q, a_: q * a_, kk, a)
        tot = each(lambda c_: jnp.where(fwd, c_[length - 1:length, :], c_[0:1, :]), cum)
        e_n = each(lambda c_: jnp.exp(-c_), cum)
        sa = each(lambda q, c_, l_: stack(-q * jnp.exp(c_ - l_)), kk, cum, lw)
        sr_full = each(lambda r_, c_: r_ * jnp.exp(c_), r, cum)
        sr = each(stack, sr_full)
        sb = each(lambda b_, e_: stack(b_ * e_), bb, e_n)
        sk = each(lambda k_, e_: stack(k_ * e_), kd, e_n)
        vexp = each(stack, v)
        m_ab = each(lambda x_, y_: jnp.where(strict, smm(x_, y_, _NT), 0.0), sa, sb)
        m_ak = each(lambda x_, y_: jnp.where(strict, smm(x_, y_, _NT), 0.0), sa, sk)
        rhs1 = each(lambda x_, s_: smm(x_, s_, _NT), sa, st)
        rhs2 = each(lambda m_, v_: smm(m_, v_, _NN), m_ak, vexp)
        rhs = each(lambda x_, y_: x_ + y_, rhs1, rhs2)
        d0 = each(lambda m_: jnp.where(base_mask, m_, 0.0), m_ab)
        sq = each(lambda d_: smm(d_, d_, _NN), d0)
        inv = each(lambda d_: eye + d_, d0)
        inv = each(lambda i_, q_: i_ + smm(i_, q_, _NN), inv, sq)
        for lm in level_masks:
            half = each(lambda i_, m_: smm(i_, jnp.where(lm, m_, 0.0), _NN), inv, m_ab)
            inv = each(lambda i_, h_: i_ + smm(h_, i_, _NN), inv, half)
        uexp = each(lambda i_, r_: smm(i_, r_, _NN), inv, rhs)
        t_rb = each(lambda x_, y_: jnp.where(incl, smm(x_, y_, _NT), 0.0), sr, sb)
        t_rk = each(lambda x_, y_: jnp.where(incl, smm(x_, y_, _NT), 0.0), sr, sk)
        fold = lambda t_: t_[0:length] + t_[length:gl]
        y1 = each(lambda r_, s_: smm(r_, s_, _NT), sr_full, st)
        uv = each(lambda u_, v_: jnp.concatenate([u_, v_], axis=0), uexp, vexp)
        y2 = each(lambda b_, k_, w_: smm(jnp.concatenate([fold(b_), fold(k_)], axis=1), w_, _NN), t_rb, t_rk, uv)
        y = each(lambda p_, q_: p_ + q_, y1, y2)
        bs = each(lambda r_, k_, p_: segsum(r_ * k_ * p_), r, kd, get("rkp"))
        bonus = each(lambda s_, v_: s_ * v_, bs, v)
        e_e = each(lambda t_, c_: jnp.exp(t_ - c_), tot, cum)
        bk = each(lambda b_, k_, e_: jnp.concatenate([stack(b_ * e_), stack(k_ * e_)], axis=0), bb, kd, e_e)
        s1 = each(lambda w_, x_: smm(w_, x_, _TN), uv, bk)
        s_new = each(lambda s_, t_, p_: s_ * jnp.exp(t_) + p_, st, tot, s1)
        return y, bonus, s_new

    npg = min(RWKV_PAIRS_PER_GROUP, npairs)

    def group(gi, carry):
        xs = [load_pair(gi * npg + u) for u in range(npg)]
        ys, bonuses, states = solve_group(xs)
        for x, y, bonus, s_new in zip(xs, ys, bonuses, states):
            y_ref[0, :, x["ds"]] = y
            bonus_ref[0, :, x["ds"]] = bonus
            s_scr[x["p"]] = s_new
        return carry

    lax.fori_loop(0, npairs // npg, group, 0)

    if emit_state:
        @pl.when(c == nc - 1)
        def _():
            sout_ref[0, 0] = s_scr[...]


def _rwkv_scan(rkv, lw, asig, kk, ka, rk, b, t, s0, emit_state):
    n, d = rkv.shape[1:]
    length = RWKV_CHUNK
    nc = t // length
    npairs = d // LANES

    def tile(bi, di, ci):
        return bi * nc + ci + di * (nc - 1 - 2 * ci)

    def plane(j):
        return pl.BlockSpec((1, length, d), lambda bi, di, ci: (j, tile(bi, di, ci), 0))

    dirspec = pl.BlockSpec((1, length, d), lambda bi, di, ci: (di, tile(bi, di, ci), 0))
    vec = pl.BlockSpec((1, d), lambda bi, di, ci: (0, 0))
    sspec = pl.BlockSpec((1, 1, npairs, LANES, LANES), lambda bi, di, ci: (bi, di, 0, 0, 0))
    in_specs = [plane(0), plane(1), plane(2), dirspec, dirspec, vec, vec, vec]
    args = [rkv, rkv, rkv, lw, asig, kk.reshape(1, d), ka.reshape(1, d), rk.reshape(1, d)]
    if s0 is not None:
        in_specs.append(sspec)
        args.append(s0)
    out_specs = [dirspec, dirspec]
    out_shape = [jax.ShapeDtypeStruct((2, n, d), F32), jax.ShapeDtypeStruct((2, n, d), F32)]
    if emit_state:
        out_specs.append(sspec)
        out_shape.append(jax.ShapeDtypeStruct((b, 2, npairs, LANES, LANES), F32))
    return pl.pallas_call(
        functools.partial(_rwkv_scan_kernel, s0 is not None, emit_state),
        grid=(b, 2, nc), in_specs=in_specs, out_specs=out_specs, out_shape=out_shape,
        scratch_shapes=[pltpu.VMEM((npairs, LANES, LANES), F32), pltpu.VMEM((length, d), F32)],
        compiler_params=_params("arbitrary", "arbitrary", "arbitrary"), name="rwkv_scan")(*args)


def _rwkv_out_kernel(y_ref, bonus_ref, g_ref, lnw_ref, lnb_ref, w_ref, x_ref, mod_ref, g2_ref, wr_ref, out_ref,
                     hffn_ref, aff_ref):
    y = y_ref[0] + y_ref[1]
    lr = _iota((LANES, LANES), 0)
    lc = _iota((LANES, LANES), 1)
    mean_blk = ((lr // R_HD) == (lc // R_HD)).astype(BF16)

    def segmean(x):
        hi, lo = _split2(x)
        return (_dot(hi, mean_blk) + _dot(lo, mean_blk)) * (1.0 / R_HD)

    tiles = [y[:, j * LANES:(j + 1) * LANES] for j in range(y.shape[1] // LANES)]
    means = [segmean(yj) for yj in tiles]
    cens = [yj - mj for yj, mj in zip(tiles, means)]
    variances = [segmean(cj * cj) for cj in cens]
    parts = [cj * lax.rsqrt(vj + R_LN_EPS) for cj, vj in zip(cens, variances)]
    yn = jnp.concatenate(parts, axis=1) * lnw_ref[...] + lnb_ref[...] + bonus_ref[0] + bonus_ref[1]
    lhs = (yn * g_ref[...]).astype(BF16)
    _mixer_residual_and_route(x_ref, mod_ref, _dot(lhs, w_ref[...]), g2_ref, wr_ref, out_ref, hffn_ref, aff_ref)


def _rwkv_out(y2, bonus2, g, lnw, lnb, w_out, x, mod, t, norm2, w_router):
    n, d = x.shape
    tm = _row_tile(n, t, 256)
    two = pl.BlockSpec((2, tm, d), lambda i: (0, i, 0))
    rows = pl.BlockSpec((tm, d), lambda i: (i, 0))
    vec = pl.BlockSpec((1, d), lambda i: (0, 0))
    tail_in, tail_args, out_specs, out_shape = _route_tail_specs(n, d, tm, norm2, w_router)
    return pl.pallas_call(
        _rwkv_out_kernel, grid=(n // tm,),
        in_specs=[two, two, rows, vec, vec, pl.BlockSpec((d, d), lambda i: (0, 0)), rows,
                  _mod_spec(mod, tm, t)] + tail_in,
        out_specs=out_specs, out_shape=out_shape,
        compiler_params=_params("arbitrary"), name="rwkv_out")(
            y2, bonus2, g, lnw.reshape(1, d), lnb.reshape(1, d), w_out, x, mod, *tail_args)


def _count(m):
    x = jnp.sum(m.astype(F32), axis=1, keepdims=True)
    return jnp.sum(x, axis=0, keepdims=True)


def _route_search_kernel(cap, aff_ref, thr_ref, tie_ref):
    n_exp, nb, _ = aff_ref.shape
    experts = list(range(n_exp))
    bits = [lax.bitcast_convert_type(aff_ref[e], I32) for e in experts]
    tok = _iota((nb, LANES), 0) * LANES + _iota((nb, LANES), 1)

    def value_step(i, cur):
        bit = jnp.int32(1) << (30 - i)
        return tuple(jnp.where(_count(b >= (c | bit)) >= cap, c | bit, c) for b, c in zip(bits, cur))

    zeros = tuple(jnp.zeros((1, 1), I32) for _ in experts)
    thr = lax.fori_loop(0, 31, value_step, zeros)
    equal = [b == t for b, t in zip(bits, thr)]
    need = [cap - _count(b > t) for b, t in zip(bits, thr)]

    def index_step(i, cur):
        bit = jnp.int32(1) << (14 - i)
        return tuple(jnp.where(_count(q & (tok < (c + bit))) < n, c + bit, c) for q, n, c in zip(equal, need, cur))

    tie = lax.fori_loop(0, 15, index_step, zeros)
    for e in experts:
        thr_ref[e] = jnp.broadcast_to(thr[e], (SUBLANES, LANES))
        tie_ref[e] = jnp.broadcast_to(tie[e], (SUBLANES, LANES))


def _route_kernel(cap, aff_ref, thr_ref, tie_ref, idx_ref, gate_ref, pos_ref, tot_ref, off_ref):
    aff = aff_ref[0]
    nb = aff.shape[0]
    bits = lax.bitcast_convert_type(aff, I32)
    tok = _iota(aff.shape, 0) * LANES + _iota(aff.shape, 1)
    thr = thr_ref[0, 0:1, 0:1]
    sel = (bits > thr) | ((bits == thr) & (tok <= tie_ref[0, 0:1, 0:1]))
    self32 = sel.astype(F32)
    selb = self32.astype(BF16)

    li = _iota((LANES, LANES), 0)
    lj = _iota((LANES, LANES), 1)
    upper_incl = (li <= lj).astype(BF16)
    cl = _dot(selb, upper_incl)
    ones8 = jnp.ones((SUBLANES, LANES), BF16)
    tot_row = _dot_nt(ones8, selb)
    bi = _iota((nb, nb), 0)
    bj = _iota((nb, nb), 1)
    offi_row = _dot(tot_row.astype(BF16), (bi <= bj).astype(BF16))
    offx_row = offi_row - tot_row
    tot_col = jnp.broadcast_to(cl[:, LANES - 1:LANES], (nb, LANES)).astype(BF16)
    offx_col = _dot((bj < bi).astype(BF16), tot_col)
    pos_ref[0] = jnp.where(sel, offx_col + cl - 1.0, -1.0).astype(I32)
    tot_ref[0] = tot_row
    off_ref[0] = offx_row

    pcol = _iota((cap, 1), 0).astype(F32)
    before = offi_row[0:1, :] <= pcol
    blk = jnp.sum(before.astype(F32), axis=1, keepdims=True)
    base = jnp.sum(jnp.where(before, tot_row[0:1, :], 0.0), axis=1, keepdims=True)
    onehot = (_iota((cap, nb), 1).astype(F32) == blk).astype(BF16)
    rowcnt = _dot(onehot, cl.astype(BF16))
    rank = pcol - base
    lane = jnp.sum((rowcnt <= rank).astype(F32), axis=1, keepdims=True)
    idx_ref[0] = (blk * LANES + lane).astype(I32)
    a1, a2, a3 = _split3(aff)
    rowaff = _dot(onehot, a1) + _dot(onehot, a2) + _dot(onehot, a3)
    gate_ref[0] = jnp.sum(jnp.where(_iota((cap, LANES), 1).astype(F32) == lane, rowaff, 0.0),
                          axis=1, keepdims=True)


def _route(aff3, cap):
    e, nb, _ = aff3.shape
    word = pl.BlockSpec((1, SUBLANES, LANES), lambda i: (i, 0, 0))
    thr, tie = pl.pallas_call(
        functools.partial(_route_search_kernel, cap), grid=(1,),
        in_specs=[pl.BlockSpec((e, nb, LANES), lambda i: (0, 0, 0))],
        out_specs=[pl.BlockSpec((e, SUBLANES, LANES), lambda i: (0, 0, 0))] * 2,
        out_shape=[jax.ShapeDtypeStruct((e, SUBLANES, LANES), I32)] * 2,
        compiler_params=_params("arbitrary"), name="moe_route_search")(aff3)
    return pl.pallas_call(
        functools.partial(_route_kernel, cap), grid=(e,),
        in_specs=[pl.BlockSpec((1, nb, LANES), lambda i: (i, 0, 0)), word, word],
        out_specs=[pl.BlockSpec((1, cap, 1), lambda i: (i, 0, 0)),
                   pl.BlockSpec((1, cap, 1), lambda i: (i, 0, 0)),
                   pl.BlockSpec((1, nb, LANES), lambda i: (i, 0, 0)),
                   pl.BlockSpec((1, SUBLANES, nb), lambda i: (i, 0, 0)),
                   pl.BlockSpec((1, SUBLANES, nb), lambda i: (i, 0, 0))],
        out_shape=[jax.ShapeDtypeStruct((e, cap, 1), I32), jax.ShapeDtypeStruct((e, cap, 1), F32),
                   jax.ShapeDtypeStruct((e, nb, LANES), I32),
                   jax.ShapeDtypeStruct((e, SUBLANES, nb), F32), jax.ShapeDtypeStruct((e, SUBLANES, nb), F32)],
        compiler_params=_params("arbitrary"), name="moe_route")(aff3, thr, tie)


def _row_copy(x_hbm, xbuf, sem, slot, tok, r):
    return pltpu.make_async_copy(x_hbm.at[pl.ds(tok, 1)], xbuf.at[pl.ds(r, 1)], sem.at[slot])


def _expert_kernel(idx_ref, idx_next_ref, x_hbm, gate_ref, wg_ref, wu_ref, wd_ref, o_ref, xbuf0, xbuf1, sem):
    tm = xbuf0.shape[0]
    bufs = (xbuf0, xbuf1)
    step = pl.program_id(0) * pl.num_programs(1) + pl.program_id(1)
    last = pl.num_programs(0) * pl.num_programs(1) - 1

    def start_rows(ids_ref, slot):
        for r in range(tm):
            _row_copy(x_hbm, bufs[slot], sem, slot, ids_ref[0, 0, r], r).start()

    def wait_rows(slot):
        for r in range(tm):
            _row_copy(x_hbm, bufs[slot], sem, slot, 0, r).wait()

    @pl.when(step == 0)
    def _():
        start_rows(idx_ref, 0)

    def run(slot):
        wait_rows(slot)
        start_rows(idx_next_ref, 1 - slot)
        xb = bufs[slot][...].astype(BF16)
        hg = _dot(xb, wg_ref[0, 0])
        hu = _dot(xb, wu_ref[0, 0])
        hid = (hg * jax.nn.sigmoid(hg) * hu).astype(BF16)
        o_ref[...] = (_dot(hid, wd_ref[0, 0]) * gate_ref[...]).astype(o_ref.dtype)

        @pl.when(step == last)
        def _():
            wait_rows(1 - slot)

    for slot in range(2):
        @pl.when(step % 2 == slot)
        def _():
            run(slot)


def _experts(hffn, idx, gate, wg, wu, wd, layer, cap):
    n, d = hffn.shape
    _, e, _, f = wg.shape
    tm = min(256, cap)
    nt = cap // tm
    idx3 = idx.reshape(e * nt, 1, tm)
    last = e * nt - 1
    return pl.pallas_call(
        _expert_kernel, grid=(e, nt),
        in_specs=[pl.BlockSpec((1, 1, tm), lambda ei, ti: (ei * nt + ti, 0, 0), memory_space=pltpu.SMEM),
                  pl.BlockSpec((1, 1, tm), lambda ei, ti: (jnp.minimum(ei * nt + ti + 1, last), 0, 0),
                               memory_space=pltpu.SMEM),
                  pl.BlockSpec(memory_space=pl.ANY),
                  pl.BlockSpec((tm, 1), lambda ei, ti: (ei * nt + ti, 0)),
                  pl.BlockSpec((1, 1, d, f), lambda ei, ti: (layer, ei, 0, 0)),
                  pl.BlockSpec((1, 1, d, f), lambda ei, ti: (layer, ei, 0, 0)),
                  pl.BlockSpec((1, 1, f, d), lambda ei, ti: (layer, ei, 0, 0))],
        out_specs=pl.BlockSpec((tm, d), lambda ei, ti: (ei * nt + ti, 0)),
        out_shape=jax.ShapeDtypeStruct((e * cap, d), BF16),
        scratch_shapes=[pltpu.VMEM((tm, d), F32), pltpu.VMEM((tm, d), F32), pltpu.SemaphoreType.DMA((2,))],
        compiler_params=_params("arbitrary", "arbitrary"), name="moe_experts")(
            idx3, idx3, hffn, gate.reshape(e * cap, 1), wg, wu, wd)


def _slab_copy(ye_hbm, slab, sem, buf, src_row, slot):
    return pltpu.make_async_copy(ye_hbm.at[pl.ds(pl.multiple_of(src_row, PACKED_ROWS), SLAB_CHUNK)],
                                 slab.at[buf, pl.ds(pl.multiple_of(slot * SLAB_CHUNK, SLAB_CHUNK), SLAB_CHUNK)],
                                 sem.at[buf])


def _combine_kernel(n_exp, post, meta_ref, meta_next_ref, x_ref, mod_ref, pos_ref, ye_hbm, *rest):
    slab, acc, sem = rest[-3:]
    if post == "next":
        gain_ref, modn_ref, out_ref, h_ref = rest[:-3]
    elif post == "final":
        gain_ref, out_ref = rest[:-3]
    else:
        (out_ref,) = rest[:-3]
    tm = x_ref.shape[0]
    i = pl.program_id(0)
    buf = i % 2
    total = meta_ref[0, 0, 0]
    total_next = jnp.where(i + 1 < pl.num_programs(0), meta_next_ref[0, 0, 0], 0)

    def start_chunks(ref, count, b):
        def body(g, carry):
            _slab_copy(ye_hbm, slab, sem, b, ref[0, 0, 1 + n_exp + g], g).start()
            return carry
        lax.fori_loop(0, count, body, 0)

    @pl.when(i == 0)
    def _():
        slab[...] = jnp.zeros_like(slab)
        start_chunks(meta_ref, total, 0)

    start_chunks(meta_next_ref, total_next, 1 - buf)

    def wait(g, carry):
        _slab_copy(ye_hbm, slab, sem, buf, 0, g).wait()
        return carry

    pos = pos_ref[...]
    sub_e = _iota((n_exp, 1), 0)
    shift = jnp.zeros((n_exp, 1), I32)
    for e in range(n_exp):
        shift = jnp.where(sub_e == e, meta_ref[0, 0, 1 + e], shift)
    target = jnp.where(pos >= 0, pos + shift, -1)
    acc[...] = jnp.zeros_like(acc)
    lax.fori_loop(0, total, wait, 0)

    kc_rows = 2 * LANES
    chunks_per_kc = kc_rows // SLAB_CHUNK

    def kbody(kc, carry):
        base = pl.multiple_of(kc * kc_rows, kc_rows)
        rows = slab[buf, pl.ds(base, kc_rows), :]
        slab_row = _iota((kc_rows, tm), 0) + base
        place_t = target[0:1, :] == slab_row
        for e in range(1, n_exp):
            place_t = place_t | (target[e:e + 1, :] == slab_row)
        acc[...] += _dot_tn(place_t.astype(F32).astype(BF16), rows)
        return carry

    lax.fori_loop(0, (total + chunks_per_kc - 1) // chunks_per_kc, kbody, 0)
    x_new = x_ref[...] + mod_ref[0, 5:6, :] * acc[...]
    if post is None:
        out_ref[...] = x_new
        return
    normed = x_new * lax.rsqrt(jnp.mean(x_new * x_new, axis=-1, keepdims=True) + NORM_EPS) * gain_ref[...]
    if post == "final":
        out_ref[...] = normed
    else:
        out_ref[...] = x_new
        h_ref[...] = (normed * (1.0 + modn_ref[0, 1:2, :]) + modn_ref[0, 0:1, :]).astype(h_ref.dtype)


def _combine(x, mod, pos_t, meta, ye, t, max_chunks, post=None):
    n, d = x.shape
    tm = LANES
    n_exp = N_EXPERTS
    ntiles = n // tm
    kc_rows = 2 * LANES
    slab_rows = -(-(max_chunks * SLAB_CHUNK) // kc_rows) * kc_rows
    rows = pl.BlockSpec((tm, d), lambda i: (i, 0))
    in_specs = [pl.BlockSpec((1, 1, meta.shape[2]), lambda i: (i, 0, 0), memory_space=pltpu.SMEM),
                pl.BlockSpec((1, 1, meta.shape[2]), lambda i: (jnp.minimum(i + 1, ntiles - 1), 0, 0),
                             memory_space=pltpu.SMEM),
                rows, _mod_spec(mod, tm, t), pl.BlockSpec((n_exp, tm), lambda i: (0, i)),
                pl.BlockSpec(memory_space=pl.ANY)]
    args = [meta, meta, x, mod, pos_t, ye]
    out_specs, out_shape = rows, jax.ShapeDtypeStruct((n, d), F32)
    mode = None if post is None else post[0]
    if mode is not None:
        in_specs.append(pl.BlockSpec((1, d), lambda i: (0, 0)))
        args.append(post[1].reshape(1, d))
    if mode == "next":
        in_specs.append(_mod_spec(post[2], tm, t))
        args.append(post[2])
        out_specs, out_shape = [rows, rows], [out_shape, jax.ShapeDtypeStruct((n, d), post[3])]
    return pl.pallas_call(
        functools.partial(_combine_kernel, n_exp, mode), grid=(ntiles,),
        in_specs=in_specs, out_specs=out_specs, out_shape=out_shape,
        scratch_shapes=[pltpu.VMEM((2, slab_rows, d), BF16), pltpu.VMEM((tm, d), F32),
                        pltpu.SemaphoreType.DMA((2,))],
        compiler_params=_params("arbitrary"), name="moe_combine")(*args)


def _moe(x, hffn, aff, mod, t, wg, wu, wd, layer, post):
    n, d = x.shape
    n_exp = N_EXPERTS
    cap = (EC_CAPACITY_FACTOR * n) // n_exp
    npad = ROUTE_BLOCKS * LANES
    assert n <= npad and n % LANES == 0 and cap % SLAB_CHUNK == 0
    aff_t = jnp.pad(aff.T, ((0, 0), (0, npad - n)), constant_values=-1.0)
    idx, gate, pos, tot, off = _route(aff_t.reshape(n_exp, ROUTE_BLOCKS, LANES), cap)
    ye = _experts(hffn, idx, gate, wg, wu, wd, layer, cap)
    ntiles = n // LANES
    span = min(LANES + PACKED_ROWS + SLAB_CHUNK - (LANES + PACKED_ROWS) % SLAB_CHUNK, cap)
    per_expert = span // SLAB_CHUNK
    max_chunks = n_exp * per_expert
    start = off[:, 0, :ntiles].T.astype(I32)
    cnt = tot[:, 0, :ntiles].T.astype(I32)
    start_al = jnp.minimum((start // PACKED_ROWS) * PACKED_ROWS, cap - span)
    nch = jnp.where(cnt > 0, (start - start_al + cnt + SLAB_CHUNK - 1) // SLAB_CHUNK, 0)
    cend = jnp.cumsum(nch, axis=1)
    cstart = cend - nch
    slots = jnp.arange(max_chunks, dtype=I32)
    owns = (cstart[:, None, :] <= slots[None, :, None]) & (slots[None, :, None] < cend[:, None, :])
    base = start_al + jnp.arange(n_exp, dtype=I32)[None, :] * cap
    src = jnp.sum(jnp.where(owns, (base - cstart * SLAB_CHUNK)[:, None, :], 0), axis=2) + jnp.where(
        jnp.any(owns, axis=2), slots[None, :] * SLAB_CHUNK, 0)
    meta = jnp.concatenate([cend[:, -1:], cstart * SLAB_CHUNK - start_al, src], axis=1)
    return _combine(x, mod, pos.reshape(n_exp, npad), meta.reshape(ntiles, 1, 1 + n_exp + max_chunks), ye, t,
                    max_chunks, post)


def _rope_tables(t):
    n_rows = t // GRID_W
    axis_dim = A_HD // 2
    row = jnp.repeat(jnp.arange(n_rows, dtype=F32), GRID_W)
    col = jnp.tile(jnp.arange(GRID_W, dtype=F32), n_rows)
    inv_freq = ROPE_THETA ** (-jnp.arange(0, axis_dim, 2, dtype=F32) / axis_dim)
    ang = jnp.concatenate([row[:, None] * inv_freq, col[:, None] * inv_freq], axis=-1)
    cos, sin = jnp.cos(ang), jnp.sin(ang)
    return jnp.concatenate([cos, cos], axis=-1), jnp.concatenate([-sin, sin], axis=-1)


def _pad_lora(w1, w2):
    r = w1.shape[2]
    rp = -(-r // LANES) * LANES
    w1p = jnp.pad(w1, ((0, 0), (0, 0), (0, rp - r)))
    w2p = jnp.pad(w2, ((0, 0), (0, rp - r), (0, 0)))
    return jnp.concatenate([w1p[0], w1p[1]], axis=1).astype(BF16), w2p.astype(BF16)


def _pair_state(s):
    b, _, heads, hd, _ = s.shape
    sp = s.reshape(b, 2, heads // 2, 2, hd, hd)
    out = jnp.zeros((b, 2, heads // 2, 2 * hd, 2 * hd), F32)
    out = out.at[:, :, :, :hd, :hd].set(sp[:, :, :, 0])
    return out.at[:, :, :, hd:, hd:].set(sp[:, :, :, 1])


def _unpair_state(sp):
    b, _, npairs, _, _ = sp.shape
    hd = R_HD
    s = jnp.stack([sp[:, :, :, :hd, :hd], sp[:, :, :, hd:, hd:]], axis=3)
    return s.reshape(b, 2, 2 * npairs, hd, hd)


def _trunk(x, mod_all, b, t, latent_states, p):
    n, d = x.shape
    latent = latent_states is not None
    produced = []
    h_next = None
    for i in range(DEPTH):
        kind, j = i % N_MIXERS, i // N_MIXERS
        mod = mod_all[i]
        route_params = (p["norm2"][i], p["router"][i])
        if i == DEPTH - 1:
            post = ("final", p["final_norm"])
        elif (i + 1) % N_MIXERS == 1:
            post = ("next", p["norm1"][i + 1], mod_all[i + 1], BF16)
        elif (i + 1) % N_MIXERS == 2:
            post = ("next", p["norm1"][i + 1], mod_all[i + 1], F32)
        else:
            post = None
        if kind == 0:
            hb, gates = _norm_gate(x, p["norm1"][i], mod, t, p["m_gate_w"][j], p["m_gate_b"][j])
            proj = _mm(hb, p["m_qkvo"][j], 2048)
            res = _mlstm_scan(proj, gates, b, t, latent_states[i] if latent else None, not latent)
            if not latent:
                produced += [res[1], res[2], res[3][..., 0]]
            x, hffn, aff = _mlstm_out(res[0], proj, p["m_hnorm"][j], p["m_out"][j], x, mod, t, *route_params)
        elif kind == 1:
            assert h_next is not None and h_next.dtype == BF16
            q, k, v = _attn_qkv(h_next, p["a_qkv"][j], p["a_qnorm"][j], p["a_knorm"][j], t,
                                _rope_tables(t) if latent else None)
            cache = None
            if latent:
                ck, cv = latent_states[i]
                kvd = A_KV_HEADS * A_HD
                cache = (ck.reshape(-1, kvd), cv.reshape(-1, kvd))
            else:
                produced += [k.reshape(b, t, A_KV_HEADS, A_HD), v.reshape(b, t, A_KV_HEADS, A_HD)]
            o = _attention(q, k, v, b, t, cache)
            x, hffn, aff = _resid_mm(o, p["a_out"][j], x, mod, t, *route_params)
        else:
            assert h_next is not None and h_next.dtype == F32
            h = h_next
            mix = p["r_mix"][j]
            rkv = _rwkv_rkv(h, mix[jnp.array([0, 2, 3])].reshape(3, 1, d), p["r_rkv"][j], t)
            lw, asig, g = _rwkv_lora(h, mix[jnp.array([1, 4, 5])], p["r_w1c"][j], p["r_a1c"][j], p["r_g1"][j],
                                     p["r_w2p"][j], p["r_a2p"][j], p["r_g2"][j],
                                     p["r_w0"][j].reshape(2, 1, d), p["r_a0"][j].reshape(2, 1, d), t)
            s0 = _pair_state(latent_states[i][0]) if latent else None
            res = _rwkv_scan(rkv, lw, asig, p["r_kk"][j], p["r_ka"][j], p["r_rk"][j].reshape(-1), b, t, s0,
                             not latent)
            if not latent:
                produced.append(_unpair_state(res[2]))
            x, hffn, aff = _rwkv_out(res[0], res[1], g, p["r_lnx_w"][j], p["r_lnx_b"][j], p["r_out"][j], x, mod, t,
                                     *route_params)
        res = _moe(x, hffn, aff, mod, t, p["exp_gate"], p["exp_up"], p["exp_down"], i, post)
        x, h_next = res if post is not None and post[0] == "next" else (res, None)
    return x, produced


def kernel(x_prompt, x_sample, state_l0_C, state_l0_n, state_l0_m, cache_l1_k, cache_l1_v, state_l2_S,
           state_l3_C, state_l3_n, state_l3_m, c, c_ctx, ada_w, ada_b, norm1, norm2, router, exp_gate,
           exp_up, exp_down, m_qkvo, m_gate_w, m_gate_b, m_hnorm, m_out, a_qkv, a_qnorm, a_knorm, a_out,
           r_mix, r_rkv, r_w0, r_w1, r_w2, r_a0, r_a1, r_a2, r_g1, r_g2, r_kk, r_ka, r_rk, r_lnx_w,
           r_lnx_b, r_out, final_norm):
    bc, tc, d = x_prompt.shape
    bl, tl, _ = x_sample.shape
    depth = ada_w.shape[0]
    assert bl + 1 <= SUBLANES

    cond8 = jnp.zeros((SUBLANES, d), F32).at[0].set(c_ctx).at[1:1 + bl].set(c)
    mod = _adaln(cond8, ada_w, ada_b).reshape(depth, SUBLANES, 6, d)
    mod_ctx = mod[:, 0:1]
    mod_lat = mod[:, 1:1 + bl]

    w1c, w2p, a1c, a2p = [], [], [], []
    for j in range(r_w1.shape[0]):
        w1, w2 = _pad_lora(r_w1[j], r_w2[j])
        a1, a2 = _pad_lora(r_a1[j], r_a2[j])
        w1c.append(w1), w2p.append(w2), a1c.append(a1), a2p.append(a2)
    p = dict(norm1=norm1, norm2=norm2, router=router, final_norm=final_norm,
             exp_gate=exp_gate.astype(BF16), exp_up=exp_up.astype(BF16), exp_down=exp_down.astype(BF16),
             m_qkvo=m_qkvo.astype(BF16), m_gate_w=m_gate_w, m_gate_b=m_gate_b, m_hnorm=m_hnorm,
             m_out=m_out.astype(BF16), a_qkv=a_qkv.astype(BF16), a_qnorm=a_qnorm, a_knorm=a_knorm,
             a_out=a_out.astype(BF16), r_mix=r_mix, r_rkv=r_rkv.astype(BF16), r_w0=r_w0, r_a0=r_a0,
             r_w1c=w1c, r_w2p=w2p, r_a1c=a1c, r_a2p=a2p, r_g1=r_g1.astype(BF16), r_g2=r_g2.astype(BF16),
             r_kk=r_kk, r_ka=r_ka, r_rk=r_rk, r_lnx_w=r_lnx_w, r_lnx_b=r_lnx_b, r_out=r_out.astype(BF16))

    y_prompt, new_state = _trunk(x_prompt.reshape(bc * tc, d), mod_ctx, bc, tc, None, p)
    lat_states = [(state_l0_C, state_l0_n, state_l0_m), (cache_l1_k, cache_l1_v), (state_l2_S,),
                  (state_l3_C, state_l3_n, state_l3_m)]
    y_sample, _ = _trunk(x_sample.reshape(bl * tl, d), mod_lat, bl, tl, lat_states, p)
    return (y_prompt.reshape(bc, tc, d), y_sample.reshape(bl, tl, d), *new_state)
```

```python
import functools

import jax
import jax.numpy as jnp
from jax import lax
from jax.experimental import pallas as pl
from jax.experimental.pallas import tpu as pltpu

F32 = jnp.float32
BF16 = jnp.bfloat16
I32 = jnp.int32

D_MODEL = 2048
BATCH = 32
SEQ = 256
DEPTH = 4
DEC_BATCH = 4
DEC_SEQ = 4096
PAST_LEN = 256
GRID_W = 64
N_MIXERS = 3
NORM_EPS = 1e-6
M_HEADS = 8
M_DK = D_MODEL // 2 // M_HEADS
M_DV = D_MODEL // M_HEADS
M_GATE_CAP = 15.0
A_HEADS = 16
A_KV_HEADS = 4
A_HD = 128
ROPE_THETA = 10000.0
R_HD = 64
R_LN_EPS = 64e-5
N_EXPERTS = 16
EC_CAPACITY_FACTOR = 2
D_EXPERT = 1024

LANES = 128
SUBLANES = 8
PACKED_ROWS = 16
VMEM_LIMIT_BYTES = 56 * 2 ** 20

MLSTM_CHUNK = 256
ATTN_Q_ROWS = 256
ATTN_SUB_ROWS = 64
RWKV_CHUNK = 64
RWKV_PAIRS_PER_GROUP = 16
RWKV_SOLVE_PASSES = 1
ROUTE_BLOCKS = 128
SLAB_CHUNK = 32


def _params(*sem):
    return pltpu.CompilerParams(dimension_semantics=sem, vmem_limit_bytes=VMEM_LIMIT_BYTES)


def _dot(a, b):
    return jnp.dot(a, b, preferred_element_type=F32)


def _dot_nt(a, b):
    return lax.dot_general(a, b, (((1,), (1,)), ((), ())), preferred_element_type=F32)


def _dot_tn(a, b):
    return lax.dot_general(a, b, (((0,), (0,)), ((), ())), preferred_element_type=F32)


def _split2(x):
    hi = x.astype(BF16)
    lo = (x - hi.astype(F32)).astype(BF16)
    return hi, lo


def _split3(x):
    h1 = x.astype(BF16)
    r1 = x - h1.astype(F32)
    h2 = r1.astype(BF16)
    h3 = (r1 - h2.astype(F32)).astype(BF16)
    return h1, h2, h3


def _dot3(a, w):
    ah, al = _split2(a)
    wh, wl = _split2(w)
    return _dot(ah, wh) + _dot(al, wh) + _dot(ah, wl)


_NN = (((1,), (0,)), ((), ()))
_NT = (((1,), (1,)), ((), ()))
_TN = (((0,), (0,)), ((), ()))


def _mm_passes(a, b, dims, passes):
    def f(x, y):
        return lax.dot_general(x, y, dims, preferred_element_type=F32)
    if passes == 1:
        return f(a.astype(BF16), b.astype(BF16))
    ah, al = _split2(a)
    bh, bl = _split2(b)
    return f(ah, bh) + f(al, bh) + f(ah, bl)


def _iota(shape, dim):
    return lax.broadcasted_iota(I32, shape, dim)


def _log_sigmoid(z):
    return jnp.minimum(z, 0.0) - jnp.log(1.0 + jnp.exp(-jnp.abs(z)))


def _adaln_kernel(c_ref, w_ref, b_ref, o_ref):
    c = c_ref[...]
    s = c * jax.nn.sigmoid(c)
    hi, lo = _split2(s)
    w = w_ref[0].astype(BF16)
    o_ref[0] = _dot(hi, w) + _dot(lo, w) + b_ref[0]


def _adaln(cond8, ada_w, ada_b):
    depth, d, d6 = ada_w.shape
    tn = min(d, 1024)
    return pl.pallas_call(
        _adaln_kernel,
        grid=(depth, d6 // tn),
        in_specs=[pl.BlockSpec((SUBLANES, d), lambda l, j: (0, 0)),
                  pl.BlockSpec((1, d, tn), lambda l, j: (l, 0, j)),
                  pl.BlockSpec((1, 1, tn), lambda l, j: (l, 0, j))],
        out_specs=pl.BlockSpec((1, SUBLANES, tn), lambda l, j: (l, 0, j)),
        out_shape=jax.ShapeDtypeStruct((depth, SUBLANES, d6), F32),
        compiler_params=_params("arbitrary", "arbitrary"),
        name="adaln",
    )(cond8, ada_w, ada_b.reshape(depth, 1, d6))


def _normed(x_ref, g_ref):
    x = x_ref[...]
    return x * lax.rsqrt(jnp.mean(x * x, axis=-1, keepdims=True) + NORM_EPS) * g_ref[...]


def _norm_gate_kernel(x_ref, g_ref, mod_ref, w_ref, b_ref, o_ref, gate_ref):
    h = _normed(x_ref, g_ref) * (1.0 + mod_ref[0, 1:2, :]) + mod_ref[0, 0:1, :]
    o_ref[...] = h.astype(o_ref.dtype)
    z = _dot3(h, w_ref[...]) + b_ref[...]
    z = M_GATE_CAP * jnp.tanh(z / M_GATE_CAP)
    is_forget = ((_iota(z.shape, 1) // M_HEADS) % 2) == 1
    gate_ref[...] = jnp.where(is_forget, _log_sigmoid(z), z)


def _mixer_residual_and_route(x_ref, mod_ref, mix, g2_ref, wr_ref, out_ref, hffn_ref, aff_ref):
    x = x_ref[...] + mod_ref[0, 2:3, :] * mix
    out_ref[...] = x
    h = x * lax.rsqrt(jnp.mean(x * x, axis=-1, keepdims=True) + NORM_EPS) * g2_ref[...]
    h = h * (1.0 + mod_ref[0, 4:5, :]) + mod_ref[0, 3:4, :]
    hffn_ref[...] = h
    z = _dot3(h, wr_ref[...])
    e = jnp.exp(z - jnp.max(z, axis=-1, keepdims=True))
    aff_ref[...] = e / jnp.sum(e, axis=-1, keepdims=True)


def _route_tail_specs(n, d, tm, norm2, w_router):
    ns = w_router.shape[1]
    in_specs = [pl.BlockSpec((1, d), lambda i: (0, 0)), pl.BlockSpec((d, ns), lambda i: (0, 0))]
    out_specs = [pl.BlockSpec((tm, d), lambda i: (i, 0)), pl.BlockSpec((tm, d), lambda i: (i, 0)),
                 pl.BlockSpec((tm, ns), lambda i: (i, 0))]
    out_shape = [jax.ShapeDtypeStruct((n, d), F32), jax.ShapeDtypeStruct((n, d), F32),
                 jax.ShapeDtypeStruct((n, ns), F32)]
    return in_specs, [norm2.reshape(1, d), w_router], out_specs, out_shape


def _row_tile(n, t, want):
    tm = min(want, t)
    assert t % tm == 0 and n % tm == 0
    return tm


def _mod_spec(mod, tm, t):
    d = mod.shape[-1]
    if mod.shape[0] == 1:
        return pl.BlockSpec((1, 6, d), lambda i: (0, 0, 0))
    return pl.BlockSpec((1, 6, d), lambda i: ((i * tm) // t, 0, 0))


def _norm_gate(x, gain, mod, t, w, b):
    n, d = x.shape
    tm = _row_tile(n, t, 256)
    ns = w.shape[1]
    xs = pl.BlockSpec((tm, d), lambda i: (i, 0))
    return pl.pallas_call(
        _norm_gate_kernel, grid=(n // tm,),
        in_specs=[xs, pl.BlockSpec((1, d), lambda i: (0, 0)), _mod_spec(mod, tm, t),
                  pl.BlockSpec((d, ns), lambda i: (0, 0)), pl.BlockSpec((1, ns), lambda i: (0, 0))],
        out_specs=[xs, pl.BlockSpec((tm, ns), lambda i: (i, 0))],
        out_shape=[jax.ShapeDtypeStruct((n, d), BF16), jax.ShapeDtypeStruct((n, ns), F32)],
        compiler_params=_params("arbitrary"), name="norm_gate")(x, gain.reshape(1, d), mod, w, b.reshape(1, ns))


def _mm_kernel(x_ref, w_ref, o_ref):
    o_ref[...] = _dot(x_ref[...], w_ref[...])


def _mm(x, w, tn):
    n, k = x.shape
    m = w.shape[1]
    tm = min(512, n)
    tn = min(tn, m)
    return pl.pallas_call(
        _mm_kernel, grid=(m // tn, n // tm),
        in_specs=[pl.BlockSpec((tm, k), lambda j, i: (i, 0)), pl.BlockSpec((k, tn), lambda j, i: (0, j))],
        out_specs=pl.BlockSpec((tm, tn), lambda j, i: (i, j)),
        out_shape=jax.ShapeDtypeStruct((n, m), F32),
        compiler_params=_params("arbitrary", "arbitrary"), name="mm")(x, w)


def _mlstm_kernel(has_init, emit_state, *refs):
    q_ref, k_ref, v_ref, g_ref, gt_ref = refs[:5]
    pos = 5
    if has_init:
        c0_ref, n0_ref, m0_ref = refs[pos:pos + 3]
        pos += 3
    h_ref = refs[pos]
    pos += 1
    if emit_state:
        cout_ref, nout_ref, mout_ref = refs[pos:pos + 3]
        pos += 3
    c_scr, n_scr, m_scr = refs[pos:pos + 3]

    d = pl.program_id(1)
    c = pl.program_id(2)
    nc = pl.num_programs(2)
    fwd = d == 0
    heads = M_HEADS
    length = q_ref.shape[0]

    @pl.when(c == 0)
    def _():
        if has_init:
            c_scr[...] = c0_ref[0, 0]
            n_scr[...] = n0_ref[0, 0]
            m_scr[...] = m0_ref[0, 0]
        else:
            c_scr[...] = jnp.zeros_like(c_scr)
            n_scr[...] = jnp.zeros_like(n_scr)
            m_scr[...] = jnp.zeros_like(m_scr)

    sgn = jnp.where(fwd, 1, -1)
    row = _iota((length, length), 0)
    col = _iota((length, length), 1)
    mask = ((row - col) * sgn) >= 0
    mask_t = ((col - row) * sgn) >= 0
    scale = M_DK ** -0.5
    g = g_ref[...]
    gt = gt_ref[0]

    tri = mask.astype(F32).astype(BF16)
    tri_t = mask_t.astype(F32).astype(BF16)
    g1, g2, g3 = _split3(g)
    cum_cols = _dot(tri, g1) + _dot(tri, g2) + _dot(tri, g3)
    t1, t2, t3 = _split3(gt)
    cum_rows = _dot(t1, tri_t) + _dot(t2, tri_t) + _dot(t3, tri_t)

    def pick_col(x, kind, h):
        i, j = kind * heads + h, (kind + 2) * heads + h
        return jnp.where(fwd, x[:, i:i + 1], x[:, j:j + 1])

    def pick_row(x, kind, h):
        i, j = kind * heads + h, (kind + 2) * heads + h
        return jnp.where(fwd, x[i:i + 1, :], x[j:j + 1, :])

    def each(fn, *lists):
        return [fn(*args) for args in zip(*lists)]

    hs = list(range(heads))
    ig_row = [pick_row(gt, 0, h) for h in hs]
    ig_col = [pick_col(g, 0, h) for h in hs]
    b_col = [pick_col(cum_cols, 1, h) for h in hs]
    b_row = [pick_row(cum_rows, 1, h) for h in hs]
    m_prev = [m_scr[h:h + 1, 0:1] for h in hs]
    c_prev = [c_scr[h] for h in hs]
    n_prev = [n_scr[h:h + 1, :] for h in hs]
    qh = [q_ref[:, h * M_DK:(h + 1) * M_DK] for h in hs]
    kh = [k_ref[:, h * M_DK:(h + 1) * M_DK] * scale for h in hs]
    qb = each(lambda x: x.astype(BF16), qh)
    kb = each(lambda x: x.astype(BF16), kh)
    vb = [v_ref[:, h * M_DV:(h + 1) * M_DV].astype(BF16) for h in hs]
    qk = each(_dot_nt, qb, kb)
    qc = each(lambda q_, c_: _dot(q_, c_.astype(BF16)), qb, c_prev)
    dmat = each(lambda bc, br, ir: jnp.where(mask, bc - br + ir, -jnp.inf), b_col, b_row, ig_row)
    a = each(lambda bc, mp: bc + mp, b_col, m_prev)
    m_t = each(lambda a_, d_: jnp.maximum(a_, jnp.max(d_, axis=1, keepdims=True)), a, dmat)
    w_inter = each(lambda a_, m_: jnp.exp(a_ - m_), a, m_t)
    s = each(lambda x_, d_, m_: x_ * jnp.exp(d_ - m_), qk, dmat, m_t)
    sv = each(lambda s_, v_: _dot(s_.astype(BF16), v_), s, vb)
    num = each(lambda w_, x_, y_: w_ * x_ + y_, w_inter, qc, sv)
    den = each(lambda w_, q_, n_, s_: w_ * jnp.sum(q_ * n_, axis=1, keepdims=True)
               + jnp.sum(s_, axis=1, keepdims=True), w_inter, qh, n_prev, s)
    for h in hs:
        h_ref[0, :, h * M_DV:(h + 1) * M_DV] = num[h] / jnp.maximum(jnp.abs(den[h]), jnp.exp(-m_t[h]))

    b_last = each(lambda bc: jnp.where(fwd, bc[length - 1:length, :], bc[0:1, :]), b_col)
    g_col = each(lambda bl, bc, ic: bl - bc + ic, b_last, b_col, ig_col)
    m_new = each(lambda bl, mp, gc: jnp.maximum(bl + mp, jnp.max(gc, axis=0, keepdims=True)), b_last, m_prev, g_col)
    decay = each(lambda bl, mp, mn: jnp.exp(bl + mp - mn), b_last, m_prev, m_new)
    wk = each(lambda gc, mn, k_: jnp.exp(gc - mn) * k_, g_col, m_new, kh)
    kv = each(lambda w_, v_: _dot_tn(w_.astype(BF16), v_), wk, vb)
    for h in hs:
        c_scr[h] = decay[h] * c_prev[h] + kv[h]
        n_scr[h:h + 1, :] = decay[h] * n_prev[h] + jnp.sum(wk[h], axis=0, keepdims=True)
        m_scr[h:h + 1, :] = jnp.broadcast_to(m_new[h], (1, LANES))

    if emit_state:
        @pl.when(c == nc - 1)
        def _():
            cout_ref[0, 0] = c_scr[...]
            nout_ref[0, 0] = n_scr[...]
            mout_ref[0, 0] = m_scr[...]


def _mlstm_scan(proj, gates, b, t, init, emit_state):
    n = proj.shape[0]
    heads, dk, dv = M_HEADS, M_DK, M_DV
    qk, mv = heads * dk, heads * dv
    length = min(MLSTM_CHUNK, t)
    nc = t // length
    gates_t = gates.reshape(n // length, length, 4 * heads).transpose(0, 2, 1)

    def tile(bi, di, ci):
        return bi * nc + ci + di * (nc - 1 - 2 * ci)

    in_specs = [pl.BlockSpec((length, qk), lambda bi, di, ci: (tile(bi, di, ci), 0)),
                pl.BlockSpec((length, qk), lambda bi, di, ci: (tile(bi, di, ci), 1)),
                pl.BlockSpec((length, mv), lambda bi, di, ci: (tile(bi, di, ci), 1)),
                pl.BlockSpec((length, 4 * heads), lambda bi, di, ci: (tile(bi, di, ci), 0)),
                pl.BlockSpec((1, 4 * heads, length), lambda bi, di, ci: (tile(bi, di, ci), 0, 0))]
    args = [proj, proj, proj, gates, gates_t]
    state_specs = [pl.BlockSpec((1, 1, heads, dk, dv), lambda bi, di, ci: (bi, di, 0, 0, 0)),
                   pl.BlockSpec((1, 1, heads, dk), lambda bi, di, ci: (bi, di, 0, 0)),
                   pl.BlockSpec((1, 1, heads, LANES), lambda bi, di, ci: (bi, di, 0, 0))]
    if init is not None:
        c0, n0, m0 = init
        in_specs += state_specs
        args += [c0, n0, jnp.broadcast_to(m0[..., None], m0.shape + (LANES,))]
    out_specs = [pl.BlockSpec((1, length, mv), lambda bi, di, ci: (di, tile(bi, di, ci), 0))]
    out_shape = [jax.ShapeDtypeStruct((2, n, mv), F32)]
    if emit_state:
        out_specs += state_specs
        out_shape += [jax.ShapeDtypeStruct((b, 2, heads, dk, dv), F32),
                      jax.ShapeDtypeStruct((b, 2, heads, dk), F32),
                      jax.ShapeDtypeStruct((b, 2, heads, LANES), F32)]
    return pl.pallas_call(
        functools.partial(_mlstm_kernel, init is not None, emit_state),
        grid=(b, 2, nc), in_specs=in_specs, out_specs=out_specs, out_shape=out_shape,
        scratch_shapes=[pltpu.VMEM((heads, dk, dv), F32), pltpu.VMEM((heads, dk), F32),
                        pltpu.VMEM((heads, LANES), F32)],
        compiler_params=_params("arbitrary", "arbitrary", "arbitrary"), name="mlstm_scan")(*args)


def _mlstm_out_kernel(hs_ref, o_ref, gain_ref, w_ref, x_ref, mod_ref, g2_ref, wr_ref, out_ref, hffn_ref, aff_ref):
    hs = hs_ref[0] + hs_ref[1]
    parts = []
    for h in range(M_HEADS):
        seg = hs[:, h * M_DV:(h + 1) * M_DV]
        parts.append(seg * lax.rsqrt(jnp.mean(seg * seg, axis=-1, keepdims=True) + NORM_EPS))
    hn = jnp.concatenate(parts, axis=1) * gain_ref[...]
    lhs = (jax.nn.sigmoid(o_ref[...]) * hn).astype(BF16)
    _mixer_residual_and_route(x_ref, mod_ref, _dot(lhs, w_ref[...]), g2_ref, wr_ref, out_ref, hffn_ref, aff_ref)


def _mlstm_out(hs2, proj, gain, w_out, x, mod, t, norm2, w_router):
    n, d = x.shape
    mv = M_HEADS * M_DV
    tm = _row_tile(n, t, 256)
    tail_in, tail_args, out_specs, out_shape = _route_tail_specs(n, d, tm, norm2, w_router)
    return pl.pallas_call(
        _mlstm_out_kernel, grid=(n // tm,),
        in_specs=[pl.BlockSpec((2, tm, mv), lambda i: (0, i, 0)),
                  pl.BlockSpec((tm, mv), lambda i: (i, 2)),
                  pl.BlockSpec((1, mv), lambda i: (0, 0)),
                  pl.BlockSpec((mv, d), lambda i: (0, 0)),
                  pl.BlockSpec((tm, d), lambda i: (i, 0)),
                  _mod_spec(mod, tm, t)] + tail_in,
        out_specs=out_specs, out_shape=out_shape,
        compiler_params=_params("arbitrary"), name="mlstm_out")(
            hs2, proj, gain.reshape(1, mv), w_out, x, mod, *tail_args)


def _attn_qkv_kernel(use_rope, *refs):
    if use_rope:
        x_ref, w_ref, qg_ref, kg_ref, cos_ref, sin_ref, q_ref, k_ref, v_ref = refs
    else:
        x_ref, w_ref, qg_ref, kg_ref, q_ref, k_ref, v_ref = refs
    acc = _dot(x_ref[...], w_ref[...])
    qd, kvd = A_HEADS * A_HD, A_KV_HEADS * A_HD

    def head_norm(xh, gain):
        xh = xh * lax.rsqrt(jnp.mean(xh * xh, axis=-1, keepdims=True) + NORM_EPS) * gain
        if use_rope:
            xh = xh * cos_ref[...] + pltpu.roll(xh, A_HD // 2, axis=1) * sin_ref[...]
        return xh

    for h in range(A_HEADS):
        qh = head_norm(acc[:, h * A_HD:(h + 1) * A_HD], qg_ref[...])
        q_ref[:, h * A_HD:(h + 1) * A_HD] = (qh * (A_HD ** -0.5)).astype(BF16)
    for h in range(A_KV_HEADS):
        k_ref[:, h * A_HD:(h + 1) * A_HD] = head_norm(acc[:, qd + h * A_HD:qd + (h + 1) * A_HD], kg_ref[...])
    v_ref[...] = acc[:, qd + kvd:]


def _attn_qkv(hb, w, qgain, kgain, t, rope):
    n, d = hb.shape
    qd, kvd = A_HEADS * A_HD, A_KV_HEADS * A_HD
    tm = _row_tile(n, t, 256)
    in_specs = [pl.BlockSpec((tm, d), lambda i: (i, 0)),
                pl.BlockSpec((d, qd + 2 * kvd), lambda i: (0, 0)),
                pl.BlockSpec((1, A_HD), lambda i: (0, 0)),
                pl.BlockSpec((1, A_HD), lambda i: (0, 0))]
    args = [hb, w, qgain.reshape(1, A_HD), kgain.reshape(1, A_HD)]
    if rope is not None:
        in_specs += [pl.BlockSpec((tm, A_HD), lambda i: (i % (t // tm), 0))] * 2
        args += list(rope)
    return pl.pallas_call(
        functools.partial(_attn_qkv_kernel, rope is not None), grid=(n // tm,),
        in_specs=in_specs,
        out_specs=[pl.BlockSpec((tm, qd), lambda i: (i, 0)),
                   pl.BlockSpec((tm, kvd), lambda i: (i, 0)),
                   pl.BlockSpec((tm, kvd), lambda i: (i, 0))],
        out_shape=[jax.ShapeDtypeStruct((n, qd), BF16), jax.ShapeDtypeStruct((n, kvd), F32),
                   jax.ShapeDtypeStruct((n, kvd), F32)],
        compiler_params=_params("arbitrary"), name="attn_qkv")(*args)


def _attn_kernel(has_cache, *refs):
    if has_cache:
        q_ref, kn_ref, vn_ref, kc_ref, vc_ref, o_ref, k_scr, v_scr = refs
    else:
        q_ref, kn_ref, vn_ref, o_ref, k_scr, v_scr = refs
    group = A_HEADS // A_KV_HEADS
    tq = q_ref.shape[0]
    t_new = kn_ref.shape[0]
    past = k_scr.shape[0] - t_new

    @pl.when(pl.program_id(2) == 0)
    def _():
        if has_cache:
            k_scr[0:past, :] = kc_ref[...].astype(BF16)
            v_scr[0:past, :] = vc_ref[...].astype(BF16)
        k_scr[past:, :] = kn_ref[...].astype(BF16)
        v_scr[past:, :] = vn_ref[...].astype(BF16)

    sub = min(ATTN_SUB_ROWS, tq)
    starts = list(range(0, tq, sub))
    q4 = [jnp.concatenate([q_ref[r0:r0 + sub, g * A_HD:(g + 1) * A_HD] for g in range(group)], axis=0)
          for r0 in starts]
    s = [_dot_nt(x, k_scr[...]) for x in q4]
    p = [jnp.exp(x - jnp.max(x, axis=-1, keepdims=True)) for x in s]
    pv = [_dot(x.astype(BF16), v_scr[...]) for x in p]
    o = [x / jnp.sum(y, axis=-1, keepdims=True) for x, y in zip(pv, p)]
    for r0, x in zip(starts, o):
        for g in range(group):
            o_ref[r0:r0 + sub, g * A_HD:(g + 1) * A_HD] = x[g * sub:(g + 1) * sub, :].astype(BF16)


def _attention(q, k, v, b, t, cache):
    n = q.shape[0]
    group = A_HEADS // A_KV_HEADS
    tq = min(ATTN_Q_ROWS, t)
    nq = t // tq
    past = 0 if cache is None else cache[0].shape[0] // b
    in_specs = [pl.BlockSpec((tq, group * A_HD), lambda bi, hi, qi: (bi * nq + qi, hi)),
                pl.BlockSpec((t, A_HD), lambda bi, hi, qi: (bi, hi)),
                pl.BlockSpec((t, A_HD), lambda bi, hi, qi: (bi, hi))]
    args = [q, k, v]
    if cache is not None:
        in_specs += [pl.BlockSpec((past, A_HD), lambda bi, hi, qi: (bi, hi))] * 2
        args += list(cache)
    return pl.pallas_call(
        functools.partial(_attn_kernel, cache is not None), grid=(b, A_KV_HEADS, nq),
        in_specs=in_specs,
        out_specs=pl.BlockSpec((tq, group * A_HD), lambda bi, hi, qi: (bi * nq + qi, hi)),
        out_shape=jax.ShapeDtypeStruct((n, A_HEADS * A_HD), BF16),
        scratch_shapes=[pltpu.VMEM((past + t, A_HD), BF16), pltpu.VMEM((past + t, A_HD), BF16)],
        compiler_params=_params("arbitrary", "arbitrary", "arbitrary"), name="attention")(*args)


def _resid_mm_kernel(lhs_ref, w_ref, x_ref, mod_ref, g2_ref, wr_ref, out_ref, hffn_ref, aff_ref):
    _mixer_residual_and_route(x_ref, mod_ref, _dot(lhs_ref[...], w_ref[...]), g2_ref, wr_ref, out_ref, hffn_ref,
                              aff_ref)


def _resid_mm(lhs, w, x, mod, t, norm2, w_router):
    n, d = x.shape
    k = lhs.shape[1]
    tm = _row_tile(n, t, 256)
    tail_in, tail_args, out_specs, out_shape = _route_tail_specs(n, d, tm, norm2, w_router)
    return pl.pallas_call(
        _resid_mm_kernel, grid=(n // tm,),
        in_specs=[pl.BlockSpec((tm, k), lambda i: (i, 0)),
                  pl.BlockSpec((k, d), lambda i: (0, 0)),
                  pl.BlockSpec((tm, d), lambda i: (i, 0)),
                  _mod_spec(mod, tm, t)] + tail_in,
        out_specs=out_specs, out_shape=out_shape,
        compiler_params=_params("arbitrary"), name="resid_mm")(lhs, w, x, mod, *tail_args)


def _token_shift_delta(h, hp_ref, hn_ref, i, tiles_per_seq):
    tm = h.shape[0]
    rowi = _iota((tm, 1), 0)
    ti = i % tiles_per_seq
    prev_row = jnp.where(ti == 0, 0.0, hp_ref[SUBLANES - 1:SUBLANES, :])
    next_row = jnp.where(ti == tiles_per_seq - 1, 0.0, hn_ref[0:1, :])
    h_prev = jnp.where(rowi == 0, prev_row, pltpu.roll(h, 1, axis=0))
    h_next = jnp.where(rowi == tm - 1, next_row, pltpu.roll(h, tm - 1, axis=0))
    return 0.5 * (h_prev + h_next) - h


def _shift_specs(tm, d, n, grid_rank):
    nb8 = n // SUBLANES
    per = tm // SUBLANES
    if grid_rank == 1:
        return [pl.BlockSpec((tm, d), lambda i: (i, 0)),
                pl.BlockSpec((SUBLANES, d), lambda i: (jnp.maximum(i * per - 1, 0), 0)),
                pl.BlockSpec((SUBLANES, d), lambda i: (jnp.minimum((i + 1) * per, nb8 - 1), 0))]
    return [pl.BlockSpec((tm, d), lambda j, i: (i, 0)),
            pl.BlockSpec((SUBLANES, d), lambda j, i: (jnp.maximum(i * per - 1, 0), 0)),
            pl.BlockSpec((SUBLANES, d), lambda j, i: (jnp.minimum((i + 1) * per, nb8 - 1), 0))]


def _rwkv_rkv_kernel(tiles_per_seq, h_ref, hp_ref, hn_ref, mix_ref, w_ref, o_ref):
    h = h_ref[...]
    xx = _token_shift_delta(h, hp_ref, hn_ref, pl.program_id(1), tiles_per_seq)
    o_ref[0] = _dot((h + xx * mix_ref[0]).astype(BF16), w_ref[0])


def _rwkv_rkv(h, mix3, w_rkv, t):
    n, d = h.shape
    tm = _row_tile(n, t, 256)
    return pl.pallas_call(
        functools.partial(_rwkv_rkv_kernel, t // tm), grid=(3, n // tm),
        in_specs=_shift_specs(tm, d, n, 2) + [pl.BlockSpec((1, 1, d), lambda j, i: (j, 0, 0)),
                                              pl.BlockSpec((1, d, d), lambda j, i: (j, 0, 0))],
        out_specs=pl.BlockSpec((1, tm, d), lambda j, i: (j, i, 0)),
        out_shape=jax.ShapeDtypeStruct((3, n, d), F32),
        compiler_params=_params("arbitrary", "arbitrary"), name="rwkv_rkv")(h, h, h, mix3, w_rkv)


def _rwkv_lora_kernel(tiles_per_seq, lora, h_ref, hp_ref, hn_ref, mix_ref, w1_ref, a1_ref, g1_ref,
                      w2_ref, a2_ref, g2_ref, w0_ref, a0_ref, lw_ref, asig_ref, g_ref):
    h = h_ref[...]
    xx = _token_shift_delta(h, hp_ref, hn_ref, pl.program_id(0), tiles_per_seq)
    xw = (h + xx * mix_ref[0:1, :]).astype(BF16)
    xa = (h + xx * mix_ref[1:2, :]).astype(BF16)
    xg = (h + xx * mix_ref[2:3, :]).astype(BF16)
    tw = jnp.tanh(_dot(xw, w1_ref[...])).astype(BF16)
    ta = _dot(xa, a1_ref[...]).astype(BF16)
    tg = jax.nn.sigmoid(_dot(xg, g1_ref[...])).astype(BF16)
    g_ref[...] = _dot(tg, g2_ref[...])
    for dr in range(2):
        z = -(w0_ref[dr] + _dot(tw[:, dr * lora:(dr + 1) * lora], w2_ref[dr]))
        softplus = jnp.maximum(z, 0.0) + jnp.log(1.0 + jnp.exp(-jnp.abs(z)))
        lw_ref[dr] = -jnp.exp(-softplus - 0.5)
        asig_ref[dr] = jax.nn.sigmoid(a0_ref[dr] + _dot(ta[:, dr * lora:(dr + 1) * lora], a2_ref[dr]))


def _rwkv_lora(h, mix3, w1c, a1c, g1, w2p, a2p, g2, w0, a0, t):
    n, d = h.shape
    tm = _row_tile(n, t, 256)
    lora = w2p.shape[1]
    gl = g1.shape[1]
    full2 = lambda shape: pl.BlockSpec(shape, lambda i: (0, 0))
    full3 = lambda shape: pl.BlockSpec(shape, lambda i: (0, 0, 0))
    return pl.pallas_call(
        functools.partial(_rwkv_lora_kernel, t // tm, lora), grid=(n // tm,),
        in_specs=_shift_specs(tm, d, n, 1) + [
            full2((3, d)), full2((d, 2 * lora)), full2((d, 2 * lora)), full2((d, gl)),
            full3((2, lora, d)), full3((2, lora, d)), full2((gl, d)), full3((2, 1, d)), full3((2, 1, d))],
        out_specs=[pl.BlockSpec((2, tm, d), lambda i: (0, i, 0)),
                   pl.BlockSpec((2, tm, d), lambda i: (0, i, 0)),
                   pl.BlockSpec((tm, d), lambda i: (i, 0))],
        out_shape=[jax.ShapeDtypeStruct((2, n, d), F32), jax.ShapeDtypeStruct((2, n, d), F32),
                   jax.ShapeDtypeStruct((n, d), F32)],
        compiler_params=_params("arbitrary"), name="rwkv_lora")(h, h, h, mix3, w1c, a1c, g1, w2p, a2p, g2, w0, a0)


def _rwkv_scan_kernel(has_init, emit_state, *refs):
    r_ref, k_ref, v_ref, lw_ref, a_ref, kk_ref, ka_ref, rk_ref = refs[:8]
    pos = 8
    if has_init:
        s0_ref = refs[pos]
        pos += 1
    y_ref, bonus_ref = refs[pos:pos + 2]
    pos += 2
    if emit_state:
        sout_ref = refs[pos]
        pos += 1
    s_scr, cum_scr = refs[pos:pos + 2]

    d = pl.program_id(1)
    c = pl.program_id(2)
    nc = pl.num_programs(2)
    fwd = d == 0
    sgn = jnp.where(fwd, 1, -1)
    length = r_ref.shape[1]
    npairs = s_scr.shape[0]
    hd = R_HD

    @pl.when(c == 0)
    def _():
        if has_init:
            s_scr[...] = s0_ref[0, 0]
        else:
            s_scr[...] = jnp.zeros_like(s_scr)

    ti = _iota((length, length), 0)
    si = _iota((length, length), 1)
    tri = (((ti - si) * sgn) >= 0).astype(BF16)
    l1, l2, l3 = _split3(lw_ref[0])
    cum_scr[...] = _dot(tri, l1) + _dot(tri, l2) + _dot(tri, l3)

    gl = 2 * length
    rr = _iota((gl, gl), 0)
    cc = _iota((gl, gl), 1)
    same = (rr // length) == (cc // length)
    e = ((rr % length) - (cc % length)) * sgn
    strict = same & (e > 0)
    incl = same & (e >= 0)
    eye = (rr == cc).astype(F32)
    lane = _iota((1, LANES), 1)
    m0 = (lane < hd).astype(F32)
    m1 = 1.0 - m0
    lr = _iota((LANES, LANES), 0)
    lc = _iota((LANES, LANES), 1)
    ones_blk = ((lr // hd) == (lc // hd)).astype(BF16)

    def segsum(x):
        hi, lo = _split2(x)
        return _dot(hi, ones_blk) + _dot(lo, ones_blk)

    def stack(x):
        return jnp.concatenate([x * m0, x * m1], axis=0)

    def smm(x, y, dims):
        return _mm_passes(x, y, dims, RWKV_SOLVE_PASSES)

    tr = rr % length
    tc = cc % length
    base = 4
    base_mask = (tr // base) == (tc // base)
    level_masks = []
    size = base
    while size < length:
        level_masks.append(((tr // (2 * size)) == (tc // (2 * size))) & ((tr // size) != (tc // size)))
        size *= 2

    def load_pair(p):
        ds = pl.ds(pl.multiple_of(p * LANES, LANES), LANES)
        return dict(ds=ds, p=p, r=r_ref[0, :, ds], k=k_ref[0, :, ds], v=v_ref[0, :, ds], lw=lw_ref[0, :, ds],
                    a=a_ref[0, :, ds], cum=cum_scr[:, ds], kkp=kk_ref[:, ds], kap=ka_ref[:, ds],
                    rkp=rk_ref[:, ds], st=s_scr[p])

    def each(fn, *lists):
        return [fn(*args) for args in zip(*lists)]

    def solve_group(xs):
        get = lambda name: [x[name] for x in xs]
        r, k, v, lw, a, cum, st = (get(n) for n in ("r", "k", "v", "lw", "a", "cum", "st"))
        kk = each(lambda k_, p_: k_ * p_, k, get("kkp"))
        ss = each(lambda q: segsum(q * q), kk)
        kk = each(lambda q, s_: q * lax.rsqrt(jnp.maximum(s_, 1e-24)), kk, ss)
        kd = each(lambda k_, a_, p_: k_ * (1.0 + (a_ - 1.0) * p_), k, a, get("kap"))
        bb = each(lambda q, a_: q * a_, kk, a)
        tot = each(lambda c_: jnp.where(fwd, c_[length - 1:length, :], c_[0:1, :]), cum)
        e_n = each(lambda c_: jnp.exp(-c_), cum)
        sa = each(lambda q, c_, l_: stack(-q * jnp.exp(c_ - l_)), kk, cum, lw)
        sr_full = each(lambda r_, c_: r_ * jnp.exp(c_), r, cum)
        sr = each(stack, sr_full)
        sb = each(lambda b_, e_: stack(b_ * e_), bb, e_n)
        sk = each(lambda k_, e_: stack(k_ * e_), kd, e_n)
        vexp = each(stack, v)
        m_ab = each(lambda x_, y_: jnp.where(strict, smm(x_, y_, _NT), 0.0), sa, sb)
        m_ak = each(lambda x_, y_: jnp.where(strict, smm(x_, y_, _NT), 0.0), sa, sk)
        rhs1 = each(lambda x_, s_: smm(x_, s_, _NT), sa, st)
        rhs2 = each(lambda m_, v_: smm(m_, v_, _NN), m_ak, vexp)
        rhs = each(lambda x_, y_: x_ + y_, rhs1, rhs2)
        d0 = each(lambda m_: jnp.where(base_mask, m_, 0.0), m_ab)
        sq = each(lambda d_: smm(d_, d_, _NN), d0)
        inv = each(lambda d_: eye + d_, d0)
        inv = each(lambda i_, q_: i_ + smm(i_, q_, _NN), inv, sq)
        for lm in level_masks:
            half = each(lambda i_, m_: smm(i_, jnp.where(lm, m_, 0.0), _NN), inv, m_ab)
            inv = each(lambda i_, h_: i_ + smm(h_, i_, _NN), inv, half)
        uexp = each(lambda i_, r_: smm(i_, r_, _NN), inv, rhs)
        t_rb = each(lambda x_, y_: jnp.where(incl, smm(x_, y_, _NT), 0.0), sr, sb)
        t_rk = each(lambda x_, y_: jnp.where(incl, smm(x_, y_, _NT), 0.0), sr, sk)
        fold = lambda t_: t_[0:length] + t_[length:gl]
        y1 = each(lambda r_, s_: smm(r_, s_, _NT), sr_full, st)
        uv = each(lambda u_, v_: jnp.concatenate([u_, v_], axis=0), uexp, vexp)
        y2 = each(lambda b_, k_, w_: smm(jnp.concatenate([fold(b_), fold(k_)], axis=1), w_, _NN), t_rb, t_rk, uv)
        y = each(lambda p_, q_: p_ + q_, y1, y2)
        bs = each(lambda r_, k_, p_: segsum(r_ * k_ * p_), r, kd, get("rkp"))
        bonus = each(lambda s_, v_: s_ * v_, bs, v)
        e_e = each(lambda t_, c_: jnp.exp(t_ - c_), tot, cum)
        bk = each(lambda b_, k_, e_: jnp.concatenate([stack(b_ * e_), stack(k_ * e_)], axis=0), bb, kd, e_e)
        s1 = each(lambda w_, x_: smm(w_, x_, _TN), uv, bk)
        s_new = each(lambda s_, t_, p_: s_ * jnp.exp(t_) + p_, st, tot, s1)
        return y, bonus, s_new

    npg = min(RWKV_PAIRS_PER_GROUP, npairs)

    def group(gi, carry):
        xs = [load_pair(gi * npg + u) for u in range(npg)]
        ys, bonuses, states = solve_group(xs)
        for x, y, bonus, s_new in zip(xs, ys, bonuses, states):
            y_ref[0, :, x["ds"]] = y
            bonus_ref[0, :, x["ds"]] = bonus
            s_scr[x["p"]] = s_new
        return carry

    lax.fori_loop(0, npairs // npg, group, 0)

    if emit_state:
        @pl.when(c == nc - 1)
        def _():
            sout_ref[0, 0] = s_scr[...]


def _rwkv_scan(rkv, lw, asig, kk, ka, rk, b, t, s0, emit_state):
    n, d = rkv.shape[1:]
    length = RWKV_CHUNK
    nc = t // length
    npairs = d // LANES

    def tile(bi, di, ci):
        return bi * nc + ci + di * (nc - 1 - 2 * ci)

    def plane(j):
        return pl.BlockSpec((1, length, d), lambda bi, di, ci: (j, tile(bi, di, ci), 0))

    dirspec = pl.BlockSpec((1, length, d), lambda bi, di, ci: (di, tile(bi, di, ci), 0))
    vec = pl.BlockSpec((1, d), lambda bi, di, ci: (0, 0))
    sspec = pl.BlockSpec((1, 1, npairs, LANES, LANES), lambda bi, di, ci: (bi, di, 0, 0, 0))
    in_specs = [plane(0), plane(1), plane(2), dirspec, dirspec, vec, vec, vec]
    args = [rkv, rkv, rkv, lw, asig, kk.reshape(1, d), ka.reshape(1, d), rk.reshape(1, d)]
    if s0 is not None:
        in_specs.append(sspec)
        args.append(s0)
    out_specs = [dirspec, dirspec]
    out_shape = [jax.ShapeDtypeStruct((2, n, d), F32), jax.ShapeDtypeStruct((2, n, d), F32)]
    if emit_state:
        out_specs.append(sspec)
        out_shape.append(jax.ShapeDtypeStruct((b, 2, npairs, LANES, LANES), F32))
    return pl.pallas_call(
        functools.partial(_rwkv_scan_kernel, s0 is not None, emit_state),
        grid=(b, 2, nc), in_specs=in_specs, out_specs=out_specs, out_shape=out_shape,
        scratch_shapes=[pltpu.VMEM((npairs, LANES, LANES), F32), pltpu.VMEM((length, d), F32)],
        compiler_params=_params("arbitrary", "arbitrary", "arbitrary"), name="rwkv_scan")(*args)


def _rwkv_out_kernel(y_ref, bonus_ref, g_ref, lnw_ref, lnb_ref, w_ref, x_ref, mod_ref, g2_ref, wr_ref, out_ref,
                     hffn_ref, aff_ref):
    y = y_ref[0] + y_ref[1]
    lr = _iota((LANES, LANES), 0)
    lc = _iota((LANES, LANES), 1)
    mean_blk = ((lr // R_HD) == (lc // R_HD)).astype(BF16)

    def segmean(x):
        hi, lo = _split2(x)
        return (_dot(hi, mean_blk) + _dot(lo, mean_blk)) * (1.0 / R_HD)

    tiles = [y[:, j * LANES:(j + 1) * LANES] for j in range(y.shape[1] // LANES)]
    means = [segmean(yj) for yj in tiles]
    cens = [yj - mj for yj, mj in zip(tiles, means)]
    variances = [segmean(cj * cj) for cj in cens]
    parts = [cj * lax.rsqrt(vj + R_LN_EPS) for cj, vj in zip(cens, variances)]
    yn = jnp.concatenate(parts, axis=1) * lnw_ref[...] + lnb_ref[...] + bonus_ref[0] + bonus_ref[1]
    lhs = (yn * g_ref[...]).astype(BF16)
    _mixer_residual_and_route(x_ref, mod_ref, _dot(lhs, w_ref[...]), g2_ref, wr_ref, out_ref, hffn_ref, aff_ref)


def _rwkv_out(y2, bonus2, g, lnw, lnb, w_out, x, mod, t, norm2, w_router):
    n, d = x.shape
    tm = _row_tile(n, t, 256)
    two = pl.BlockSpec((2, tm, d), lambda i: (0, i, 0))
    rows = pl.BlockSpec((tm, d), lambda i: (i, 0))
    vec = pl.BlockSpec((1, d), lambda i: (0, 0))
    tail_in, tail_args, out_specs, out_shape = _route_tail_specs(n, d, tm, norm2, w_router)
    return pl.pallas_call(
        _rwkv_out_kernel, grid=(n // tm,),
        in_specs=[two, two, rows, vec, vec, pl.BlockSpec((d, d), lambda i: (0, 0)), rows,
                  _mod_spec(mod, tm, t)] + tail_in,
        out_specs=out_specs, out_shape=out_shape,
        compiler_params=_params("arbitrary"), name="rwkv_out")(
            y2, bonus2, g, lnw.reshape(1, d), lnb.reshape(1, d), w_out, x, mod, *tail_args)


def _count(m):
    x = jnp.sum(m.astype(F32), axis=1, keepdims=True)
    return jnp.sum(x, axis=0, keepdims=True)


def _route_search_kernel(cap, aff_ref, thr_ref, tie_ref):
    n_exp, nb, _ = aff_ref.shape
    experts = list(range(n_exp))
    bits = [lax.bitcast_convert_type(aff_ref[e], I32) for e in experts]
    tok = _iota((nb, LANES), 0) * LANES + _iota((nb, LANES), 1)

    def value_step(i, cur):
        bit = jnp.int32(1) << (30 - i)
        return tuple(jnp.where(_count(b >= (c | bit)) >= cap, c | bit, c) for b, c in zip(bits, cur))

    zeros = tuple(jnp.zeros((1, 1), I32) for _ in experts)
    thr = lax.fori_loop(0, 31, value_step, zeros)
    equal = [b == t for b, t in zip(bits, thr)]
    need = [cap - _count(b > t) for b, t in zip(bits, thr)]

    def index_step(i, cur):
        bit = jnp.int32(1) << (14 - i)
        return tuple(jnp.where(_count(q & (tok < (c + bit))) < n, c + bit, c) for q, n, c in zip(equal, need, cur))

    tie = lax.fori_loop(0, 15, index_step, zeros)
    for e in experts:
        thr_ref[e] = jnp.broadcast_to(thr[e], (SUBLANES, LANES))
        tie_ref[e] = jnp.broadcast_to(tie[e], (SUBLANES, LANES))


def _route_kernel(cap, aff_ref, thr_ref, tie_ref, idx_ref, gate_ref, pos_ref, tot_ref, off_ref):
    aff = aff_ref[0]
    nb = aff.shape[0]
    bits = lax.bitcast_convert_type(aff, I32)
    tok = _iota(aff.shape, 0) * LANES + _iota(aff.shape, 1)
    thr = thr_ref[0, 0:1, 0:1]
    sel = (bits > thr) | ((bits == thr) & (tok <= tie_ref[0, 0:1, 0:1]))
    self32 = sel.astype(F32)
    selb = self32.astype(BF16)

    li = _iota((LANES, LANES), 0)
    lj = _iota((LANES, LANES), 1)
    upper_incl = (li <= lj).astype(BF16)
    cl = _dot(selb, upper_incl)
    ones8 = jnp.ones((SUBLANES, LANES), BF16)
    tot_row = _dot_nt(ones8, selb)
    bi = _iota((nb, nb), 0)
    bj = _iota((nb, nb), 1)
    offi_row = _dot(tot_row.astype(BF16), (bi <= bj).astype(BF16))
    offx_row = offi_row - tot_row
    tot_col = jnp.broadcast_to(cl[:, LANES - 1:LANES], (nb, LANES)).astype(BF16)
    offx_col = _dot((bj < bi).astype(BF16), tot_col)
    pos_ref[0] = jnp.where(sel, offx_col + cl - 1.0, -1.0).astype(I32)
    tot_ref[0] = tot_row
    off_ref[0] = offx_row

    pcol = _iota((cap, 1), 0).astype(F32)
    before = offi_row[0:1, :] <= pcol
    blk = jnp.sum(before.astype(F32), axis=1, keepdims=True)
    base = jnp.sum(jnp.where(before, tot_row[0:1, :], 0.0), axis=1, keepdims=True)
    onehot = (_iota((cap, nb), 1).astype(F32) == blk).astype(BF16)
    rowcnt = _dot(onehot, cl.astype(BF16))
    rank = pcol - base
    lane = jnp.sum((rowcnt <= rank).astype(F32), axis=1, keepdims=True)
    idx_ref[0] = (blk * LANES + lane).astype(I32)
    a1, a2, a3 = _split3(aff)
    rowaff = _dot(onehot, a1) + _dot(onehot, a2) + _dot(onehot, a3)
    gate_ref[0] = jnp.sum(jnp.where(_iota((cap, LANES), 1).astype(F32) == lane, rowaff, 0.0),
                          axis=1, keepdims=True)


def _route(aff3, cap):
    e, nb, _ = aff3.shape
    word = pl.BlockSpec((1, SUBLANES, LANES), lambda i: (i, 0, 0))
    thr, tie = pl.pallas_call(
        functools.partial(_route_search_kernel, cap), grid=(1,),
        in_specs=[pl.BlockSpec((e, nb, LANES), lambda i: (0, 0, 0))],
        out_specs=[pl.BlockSpec((e, SUBLANES, LANES), lambda i: (0, 0, 0))] * 2,
        out_shape=[jax.ShapeDtypeStruct((e, SUBLANES, LANES), I32)] * 2,
        compiler_params=_params("arbitrary"), name="moe_route_search")(aff3)
    return pl.pallas_call(
        functools.partial(_route_kernel, cap), grid=(e,),
        in_specs=[pl.BlockSpec((1, nb, LANES), lambda i: (i, 0, 0)), word, word],
        out_specs=[pl.BlockSpec((1, cap, 1), lambda i: (i, 0, 0)),
                   pl.BlockSpec((1, cap, 1), lambda i: (i, 0, 0)),
                   pl.BlockSpec((1, nb, LANES), lambda i: (i, 0, 0)),
                   pl.BlockSpec((1, SUBLANES, nb), lambda i: (i, 0, 0)),
                   pl.BlockSpec((1, SUBLANES, nb), lambda i: (i, 0, 0))],
        out_shape=[jax.ShapeDtypeStruct((e, cap, 1), I32), jax.ShapeDtypeStruct((e, cap, 1), F32),
                   jax.ShapeDtypeStruct((e, nb, LANES), I32),
                   jax.ShapeDtypeStruct((e, SUBLANES, nb), F32), jax.ShapeDtypeStruct((e, SUBLANES, nb), F32)],
        compiler_params=_params("arbitrary"), name="moe_route")(aff3, thr, tie)


def _row_copy(x_hbm, xbuf, sem, slot, tok, r):
    return pltpu.make_async_copy(x_hbm.at[pl.ds(tok, 1)], xbuf.at[pl.ds(r, 1)], sem.at[slot])


def _expert_kernel(idx_ref, idx_next_ref, x_hbm, gate_ref, wg_ref, wu_ref, wd_ref, o_ref, xbuf0, xbuf1, sem):
    tm = xbuf0.shape[0]
    bufs = (xbuf0, xbuf1)
    step = pl.program_id(0) * pl.num_programs(1) + pl.program_id(1)
    last = pl.num_programs(0) * pl.num_programs(1) - 1

    def start_rows(ids_ref, slot):
        for r in range(tm):
            _row_copy(x_hbm, bufs[slot], sem, slot, ids_ref[0, 0, r], r).start()

    def wait_rows(slot):
        for r in range(tm):
            _row_copy(x_hbm, bufs[slot], sem, slot, 0, r).wait()

    @pl.when(step == 0)
    def _():
        start_rows(idx_ref, 0)

    def run(slot):
        start_rows(idx_next_ref, 1 - slot)
        wait_rows(slot)
        xb = bufs[slot][...].astype(BF16)
        hg = _dot(xb, wg_ref[0, 0])
        hu = _dot(xb, wu_ref[0, 0])
        hid = (hg * jax.nn.sigmoid(hg) * hu).astype(BF16)
        o_ref[...] = (_dot(hid, wd_ref[0, 0]) * gate_ref[...]).astype(o_ref.dtype)

        @pl.when(step == last)
        def _():
            wait_rows(1 - slot)

    for slot in range(2):
        @pl.when(step % 2 == slot)
        def _():
            run(slot)


def _experts(hffn, idx, gate, wg, wu, wd, layer, cap):
    n, d = hffn.shape
    _, e, _, f = wg.shape
    tm = min(256, cap)
    nt = cap // tm
    idx3 = idx.reshape(e * nt, 1, tm)
    last = e * nt - 1
    return pl.pallas_call(
        _expert_kernel, grid=(e, nt),
        in_specs=[pl.BlockSpec((1, 1, tm), lambda ei, ti: (ei * nt + ti, 0, 0), memory_space=pltpu.SMEM),
                  pl.BlockSpec((1, 1, tm), lambda ei, ti: (jnp.minimum(ei * nt + ti + 1, last), 0, 0),
                               memory_space=pltpu.SMEM),
                  pl.BlockSpec(memory_space=pl.ANY),
                  pl.BlockSpec((tm, 1), lambda ei, ti: (ei * nt + ti, 0)),
                  pl.BlockSpec((1, 1, d, f), lambda ei, ti: (layer, ei, 0, 0)),
                  pl.BlockSpec((1, 1, d, f), lambda ei, ti: (layer, ei, 0, 0)),
                  pl.BlockSpec((1, 1, f, d), lambda ei, ti: (layer, ei, 0, 0))],
        out_specs=pl.BlockSpec((tm, d), lambda ei, ti: (ei * nt + ti, 0)),
        out_shape=jax.ShapeDtypeStruct((e * cap, d), BF16),
        scratch_shapes=[pltpu.VMEM((tm, d), F32), pltpu.VMEM((tm, d), F32), pltpu.SemaphoreType.DMA((2,))],
        compiler_params=_params("arbitrary", "arbitrary"), name="moe_experts")(
            idx3, idx3, hffn, gate.reshape(e * cap, 1), wg, wu, wd)


def _slab_copy(ye_hbm, slab, sem, buf, src_row, slot):
    return pltpu.make_async_copy(ye_hbm.at[pl.ds(pl.multiple_of(src_row, PACKED_ROWS), SLAB_CHUNK)],
                                 slab.at[buf, pl.ds(pl.multiple_of(slot * SLAB_CHUNK, SLAB_CHUNK), SLAB_CHUNK)],
                                 sem.at[buf])


def _combine_kernel(n_exp, post, meta_ref, meta_next_ref, x_ref, mod_ref, pos_ref, ye_hbm, *rest):
    slab, acc, sem = rest[-3:]
    if post == "next":
        gain_ref, modn_ref, out_ref, h_ref = rest[:-3]
    elif post == "final":
        gain_ref, out_ref = rest[:-3]
    else:
        (out_ref,) = rest[:-3]
    tm = x_ref.shape[0]
    i = pl.program_id(0)
    buf = i % 2
    total = meta_ref[0, 0, 0]
    total_next = jnp.where(i + 1 < pl.num_programs(0), meta_next_ref[0, 0, 0], 0)

    def start_chunks(ref, count, b):
        def body(g, carry):
            _slab_copy(ye_hbm, slab, sem, b, ref[0, 0, 1 + n_exp + g], g).start()
            return carry
        lax.fori_loop(0, count, body, 0)

    @pl.when(i == 0)
    def _():
        slab[...] = jnp.zeros_like(slab)
        start_chunks(meta_ref, total, 0)

    start_chunks(meta_next_ref, total_next, 1 - buf)

    def wait(g, carry):
        _slab_copy(ye_hbm, slab, sem, buf, 0, g).wait()
        return carry

    pos = pos_ref[...]
    sub_e = _iota((n_exp, 1), 0)
    shift = jnp.zeros((n_exp, 1), I32)
    for e in range(n_exp):
        shift = jnp.where(sub_e == e, meta_ref[0, 0, 1 + e], shift)
    target = jnp.where(pos >= 0, pos + shift, -1)
    acc[...] = jnp.zeros_like(acc)
    lax.fori_loop(0, total, wait, 0)

    kc_rows = 2 * LANES
    chunks_per_kc = kc_rows // SLAB_CHUNK

    def kbody(kc, carry):
        base = pl.multiple_of(kc * kc_rows, kc_rows)
        rows = slab[buf, pl.ds(base, kc_rows), :]
        slab_row = _iota((kc_rows, tm), 0) + base
        place_t = target[0:1, :] == slab_row
        for e in range(1, n_exp):
            place_t = place_t | (target[e:e + 1, :] == slab_row)
        acc[...] += _dot_tn(place_t.astype(F32).astype(BF16), rows)
        return carry

    lax.fori_loop(0, (total + chunks_per_kc - 1) // chunks_per_kc, kbody, 0)
    x_new = x_ref[...] + mod_ref[0, 5:6, :] * acc[...]
    if post is None:
        out_ref[...] = x_new
        return
    normed = x_new * lax.rsqrt(jnp.mean(x_new * x_new, axis=-1, keepdims=True) + NORM_EPS) * gain_ref[...]
    if post == "final":
        out_ref[...] = normed
    else:
        out_ref[...] = x_new
        h_ref[...] = (normed * (1.0 + modn_ref[0, 1:2, :]) + modn_ref[0, 0:1, :]).astype(h_ref.dtype)


def _combine(x, mod, pos_t, meta, ye, t, max_chunks, post=None):
    n, d = x.shape
    tm = LANES
    n_exp = N_EXPERTS
    ntiles = n // tm
    kc_rows = 2 * LANES
    slab_rows = -(-(max_chunks * SLAB_CHUNK) // kc_rows) * kc_rows
    rows = pl.BlockSpec((tm, d), lambda i: (i, 0))
    in_specs = [pl.BlockSpec((1, 1, meta.shape[2]), lambda i: (i, 0, 0), memory_space=pltpu.SMEM),
                pl.BlockSpec((1, 1, meta.shape[2]), lambda i: (jnp.minimum(i + 1, ntiles - 1), 0, 0),
                             memory_space=pltpu.SMEM),
                rows, _mod_spec(mod, tm, t), pl.BlockSpec((n_exp, tm), lambda i: (0, i)),
                pl.BlockSpec(memory_space=pl.ANY)]
    args = [meta, meta, x, mod, pos_t, ye]
    out_specs, out_shape = rows, jax.ShapeDtypeStruct((n, d), F32)
    mode = None if post is None else post[0]
    if mode is not None:
        in_specs.append(pl.BlockSpec((1, d), lambda i: (0, 0)))
        args.append(post[1].reshape(1, d))
    if mode == "next":
        in_specs.append(_mod_spec(post[2], tm, t))
        args.append(post[2])
        out_specs, out_shape = [rows, rows], [out_shape, jax.ShapeDtypeStruct((n, d), post[3])]
    return pl.pallas_call(
        functools.partial(_combine_kernel, n_exp, mode), grid=(ntiles,),
        in_specs=in_specs, out_specs=out_specs, out_shape=out_shape,
        scratch_shapes=[pltpu.VMEM((2, slab_rows, d), BF16), pltpu.VMEM((tm, d), F32),
                        pltpu.SemaphoreType.DMA((2,))],
        compiler_params=_params("arbitrary"), name="moe_combine")(*args)


def _moe(x, hffn, aff, mod, t, wg, wu, wd, layer, post):
    n, d = x.shape
    n_exp = N_EXPERTS
    cap = (EC_CAPACITY_FACTOR * n) // n_exp
    npad = ROUTE_BLOCKS * LANES
    assert n <= npad and n % LANES == 0 and cap % SLAB_CHUNK == 0
    aff_t = jnp.pad(aff.T, ((0, 0), (0, npad - n)), constant_values=-1.0)
    idx, gate, pos, tot, off = _route(aff_t.reshape(n_exp, ROUTE_BLOCKS, LANES), cap)
    ye = _experts(hffn, idx, gate, wg, wu, wd, layer, cap)
    ntiles = n // LANES
    span = min(LANES + PACKED_ROWS + SLAB_CHUNK - (LANES + PACKED_ROWS) % SLAB_CHUNK, cap)
    per_expert = span // SLAB_CHUNK
    max_chunks = n_exp * per_expert
    start = off[:, 0, :ntiles].T.astype(I32)
    cnt = tot[:, 0, :ntiles].T.astype(I32)
    start_al = jnp.minimum((start // PACKED_ROWS) * PACKED_ROWS, cap - span)
    nch = jnp.where(cnt > 0, (start - start_al + cnt + SLAB_CHUNK - 1) // SLAB_CHUNK, 0)
    cend = jnp.cumsum(nch, axis=1)
    cstart = cend - nch
    slots = jnp.arange(max_chunks, dtype=I32)
    owns = (cstart[:, None, :] <= slots[None, :, None]) & (slots[None, :, None] < cend[:, None, :])
    base = start_al + jnp.arange(n_exp, dtype=I32)[None, :] * cap
    src = jnp.sum(jnp.where(owns, (base - cstart * SLAB_CHUNK)[:, None, :], 0), axis=2) + jnp.where(
        jnp.any(owns, axis=2), slots[None, :] * SLAB_CHUNK, 0)
    meta = jnp.concatenate([cend[:, -1:], cstart * SLAB_CHUNK - start_al, src], axis=1)
    return _combine(x, mod, pos.reshape(n_exp, npad), meta.reshape(ntiles, 1, 1 + n_exp + max_chunks), ye, t,
                    max_chunks, post)


def _rope_tables(t):
    n_rows = t // GRID_W
    axis_dim = A_HD // 2
    row = jnp.repeat(jnp.arange(n_rows, dtype=F32), GRID_W)
    col = jnp.tile(jnp.arange(GRID_W, dtype=F32), n_rows)
    inv_freq = ROPE_THETA ** (-jnp.arange(0, axis_dim, 2, dtype=F32) / axis_dim)
    ang = jnp.concatenate([row[:, None] * inv_freq, col[:, None] * inv_freq], axis=-1)
    cos, sin = jnp.cos(ang), jnp.sin(ang)
    return jnp.concatenate([cos, cos], axis=-1), jnp.concatenate([-sin, sin], axis=-1)


def _pad_lora(w1, w2):
    r = w1.shape[2]
    rp = -(-r // LANES) * LANES
    w1p = jnp.pad(w1, ((0, 0), (0, 0), (0, rp - r)))
    w2p = jnp.pad(w2, ((0, 0), (0, rp - r), (0, 0)))
    return jnp.concatenate([w1p[0], w1p[1]], axis=1).astype(BF16), w2p.astype(BF16)


def _pair_state(s):
    b, _, heads, hd, _ = s.shape
    sp = s.reshape(b, 2, heads // 2, 2, hd, hd)
    out = jnp.zeros((b, 2, heads // 2, 2 * hd, 2 * hd), F32)
    out = out.at[:, :, :, :hd, :hd].set(sp[:, :, :, 0])
    return out.at[:, :, :, hd:, hd:].set(sp[:, :, :, 1])


def _unpair_state(sp):
    b, _, npairs, _, _ = sp.shape
    hd = R_HD
    s = jnp.stack([sp[:, :, :, :hd, :hd], sp[:, :, :, hd:, hd:]], axis=3)
    return s.reshape(b, 2, 2 * npairs, hd, hd)


def _trunk(x, mod_all, b, t, latent_states, p):
    n, d = x.shape
    latent = latent_states is not None
    produced = []
    h_next = None
    for i in range(DEPTH):
        kind, j = i % N_MIXERS, i // N_MIXERS
        mod = mod_all[i]
        route_params = (p["norm2"][i], p["router"][i])
        if i == DEPTH - 1:
            post = ("final", p["final_norm"])
        elif (i + 1) % N_MIXERS == 1:
            post = ("next", p["norm1"][i + 1], mod_all[i + 1], BF16)
        elif (i + 1) % N_MIXERS == 2:
            post = ("next", p["norm1"][i + 1], mod_all[i + 1], F32)
        else:
            post = None
        if kind == 0:
            hb, gates = _norm_gate(x, p["norm1"][i], mod, t, p["m_gate_w"][j], p["m_gate_b"][j])
            proj = _mm(hb, p["m_qkvo"][j], 2048)
            res = _mlstm_scan(proj, gates, b, t, latent_states[i] if latent else None, not latent)
            if not latent:
                produced += [res[1], res[2], res[3][..., 0]]
            x, hffn, aff = _mlstm_out(res[0], proj, p["m_hnorm"][j], p["m_out"][j], x, mod, t, *route_params)
        elif kind == 1:
            assert h_next is not None and h_next.dtype == BF16
            q, k, v = _attn_qkv(h_next, p["a_qkv"][j], p["a_qnorm"][j], p["a_knorm"][j], t,
                                _rope_tables(t) if latent else None)
            cache = None
            if latent:
                ck, cv = latent_states[i]
                kvd = A_KV_HEADS * A_HD
                cache = (ck.reshape(-1, kvd), cv.reshape(-1, kvd))
            else:
                produced += [k.reshape(b, t, A_KV_HEADS, A_HD), v.reshape(b, t, A_KV_HEADS, A_HD)]
            o = _attention(q, k, v, b, t, cache)
            x, hffn, aff = _resid_mm(o, p["a_out"][j], x, mod, t, *route_params)
        else:
            assert h_next is not None and h_next.dtype == F32
            h = h_next
            mix = p["r_mix"][j]
            rkv = _rwkv_rkv(h, mix[jnp.array([0, 2, 3])].reshape(3, 1, d), p["r_rkv"][j], t)
            lw, asig, g = _rwkv_lora(h, mix[jnp.array([1, 4, 5])], p["r_w1c"][j], p["r_a1c"][j], p["r_g1"][j],
                                     p["r_w2p"][j], p["r_a2p"][j], p["r_g2"][j],
                                     p["r_w0"][j].reshape(2, 1, d), p["r_a0"][j].reshape(2, 1, d), t)
            s0 = _pair_state(latent_states[i][0]) if latent else None
            res = _rwkv_scan(rkv, lw, asig, p["r_kk"][j], p["r_ka"][j], p["r_rk"][j].reshape(-1), b, t, s0,
                             not latent)
            if not latent:
                produced.append(_unpair_state(res[2]))
            x, hffn, aff = _rwkv_out(res[0], res[1], g, p["r_lnx_w"][j], p["r_lnx_b"][j], p["r_out"][j], x, mod, t,
                                     *route_params)
        res = _moe(x, hffn, aff, mod, t, p["exp_gate"], p["exp_up"], p["exp_down"], i, post)
        x, h_next = res if post is not None and post[0] == "next" else (res, None)
    return x, produced


def kernel(x_prompt, x_sample, state_l0_C, state_l0_n, state_l0_m, cache_l1_k, cache_l1_v, state_l2_S,
           state_l3_C, state_l3_n, state_l3_m, c, c_ctx, ada_w, ada_b, norm1, norm2, router, exp_gate,
           exp_up, exp_down, m_qkvo, m_gate_w, m_gate_b, m_hnorm, m_out, a_qkv, a_qnorm, a_knorm, a_out,
           r_mix, r_rkv, r_w0, r_w1, r_w2, r_a0, r_a1, r_a2, r_g1, r_g2, r_kk, r_ka, r_rk, r_lnx_w,
           r_lnx_b, r_out, final_norm):
    bc, tc, d = x_prompt.shape
    bl, tl, _ = x_sample.shape
    depth = ada_w.shape[0]
    assert bl + 1 <= SUBLANES

    cond8 = jnp.zeros((SUBLANES, d), F32).at[0].set(c_ctx).at[1:1 + bl].set(c)
    mod = _adaln(cond8, ada_w, ada_b).reshape(depth, SUBLANES, 6, d)
    mod_ctx = mod[:, 0:1]
    mod_lat = mod[:, 1:1 + bl]

    w1c, w2p, a1c, a2p = [], [], [], []
    for j in range(r_w1.shape[0]):
        w1, w2 = _pad_lora(r_w1[j], r_w2[j])
        a1, a2 = _pad_lora(r_a1[j], r_a2[j])
        w1c.append(w1), w2p.append(w2), a1c.append(a1), a2p.append(a2)
    p = dict(norm1=norm1, norm2=norm2, router=router, final_norm=final_norm,
             exp_gate=exp_gate.astype(BF16), exp_up=exp_up.astype(BF16), exp_down=exp_down.astype(BF16),
             m_qkvo=m_qkvo.astype(BF16), m_gate_w=m_gate_w, m_gate_b=m_gate_b, m_hnorm=m_hnorm,
             m_out=m_out.astype(BF16), a_qkv=a_qkv.astype(BF16), a_qnorm=a_qnorm, a_knorm=a_knorm,
             a_out=a_out.astype(BF16), r_mix=r_mix, r_rkv=r_rkv.astype(BF16), r_w0=r_w0, r_a0=r_a0,
             r_w1c=w1c, r_w2p=w2p, r_a1c=a1c, r_a2p=a2p, r_g1=r_g1.astype(BF16), r_g2=r_g2.astype(BF16),
             r_kk=r_kk, r_ka=r_ka, r_rk=r_rk, r_lnx_w=r_lnx_w, r_lnx_b=r_lnx_b, r_out=r_out.astype(BF16))

    y_prompt, new_state = _trunk(x_prompt.reshape(bc * tc, d), mod_ctx, bc, tc, None, p)
    lat_states = [(state_l0_C, state_l0_n, state_l0_m), (cache_l1_k, cache_l1_v), (state_l2_S,),
                  (state_l3_C, state_l3_n, state_l3_m)]
    y_sample, _ = _trunk(x_sample.reshape(bl * tl, d), mod_lat, bl, tl, lat_states, p)
    return (y_prompt.reshape(bc, tc, d), y_sample.reshape(bl, tl, d), *new_state)
```
